```python
import jax, jax.numpy as jnp
from jax import lax
import numpy as np

D_MODEL = 1024
BATCH = 1
SEQ = 16384
DEPTH = 1
DEC_BATCH = 128
DEC_SEQ = 8
PAST_LEN = 16384
PAGE_SIZE = 128

ATTN_WIDTH = D_MODEL // 2
POOL_WIDTH = D_MODEL - ATTN_WIDTH
HEAD_DIM = 64
N_HEADS = ATTN_WIDTH // HEAD_DIM
N_KV_HEADS = 2
GROUP = N_HEADS // N_KV_HEADS
WINDOW = 128
BLOCK = 128
ROPE_THETA = 10000.0
POOL_WINDOWS = (2, 4, 8, 16)
N_POOL_GROUPS = len(POOL_WINDOWS)
POOL_CG = POOL_WIDTH // N_POOL_GROUPS
POOL_BUF = max(POOL_WINDOWS) - 1
D_FF = 4 * D_MODEL
PLE_DIM = 256
EPS = 1e-6
Q_COLS = N_HEADS * HEAD_DIM
KV_COLS = N_KV_HEADS * HEAD_DIM
IN_COLS = Q_COLS + 2 * KV_COLS + POOL_WIDTH
NEG = -1e30

kernel_name = "hymba_pool_swa_sink_decoder_step"


def rmsnorm(x, g):
    xf = x.astype(jnp.float32)
    y = xf * lax.rsqrt(jnp.mean(xf * xf, axis=-1, keepdims=True) + EPS)
    return (y * g.astype(jnp.float32)).astype(x.dtype)


def rope(x, pos):
    half = HEAD_DIM // 2
    inv = ROPE_THETA ** (-jnp.arange(half, dtype=jnp.float32) / half)
    ang = pos.astype(jnp.float32)[:, None] * inv[None, :]
    cos = jnp.cos(ang)[:, None, :]
    sin = jnp.sin(ang)[:, None, :]
    xf = x.astype(jnp.float32)
    x1, x2 = xf[..., :half], xf[..., half:]
    return jnp.concatenate([x1 * cos - x2 * sin, x2 * cos + x1 * sin], axis=-1).astype(x.dtype)


def window_mask(qpos, kpos):
    diff = qpos[..., :, None] - kpos[..., None, :]
    return (diff >= 0) & (diff < WINDOW) & (kpos[..., None, :] >= 0)


def sink_attention(q, k, v, mask, sinks):
    s = jnp.einsum('...qhgd,...shd->...hgqs', q.astype(jnp.float32), k.astype(jnp.float32)) * (HEAD_DIM ** -0.5)
    s = jnp.where(mask, s, NEG)
    sink = sinks.astype(jnp.float32).reshape(N_KV_HEADS, GROUP, 1, 1)
    m = jnp.maximum(jnp.max(s, axis=-1, keepdims=True), sink)
    e = jnp.exp(s - m)
    pr = e / (jnp.sum(e, axis=-1, keepdims=True) + jnp.exp(sink - m))
    return jnp.einsum('...hgqs,...shd->...qhgd', pr.astype(v.dtype), v)


def prompt_attention(q, k, v, sinks):
    B, T = q.shape[0], q.shape[1]
    nb = T // BLOCK
    qb = q.reshape(B, nb, BLOCK, N_KV_HEADS, GROUP, HEAD_DIM)
    kb = k.reshape(B, nb, BLOCK, N_KV_HEADS, HEAD_DIM)
    vb = v.reshape(B, nb, BLOCK, N_KV_HEADS, HEAD_DIM)
    pad = ((0, 0), (1, 0), (0, 0), (0, 0), (0, 0))
    kk = jnp.concatenate([jnp.pad(kb[:, :-1], pad), kb], axis=2)
    vv = jnp.concatenate([jnp.pad(vb[:, :-1], pad), vb], axis=2)
    blk = jnp.arange(nb, dtype=jnp.int32)[:, None]
    qpos = blk * BLOCK + jnp.arange(BLOCK, dtype=jnp.int32)[None, :]
    kpos = (blk - 1) * BLOCK + jnp.arange(2 * BLOCK, dtype=jnp.int32)[None, :]
    mask = window_mask(qpos, kpos)[:, None, None]
    out = sink_attention(qb, kk, vv, mask, sinks)
    return out.reshape(B, T, Q_COLS)


def sample_attention(q, k_all, v_all, pos, sinks):
    Bd, T = q.shape[0], q.shape[1]
    qg = q.reshape(Bd, T, N_KV_HEADS, GROUP, HEAD_DIM)
    kpos = pos[0] - WINDOW + jnp.arange(WINDOW + T, dtype=jnp.int32)
    mask = window_mask(pos, kpos)
    out = sink_attention(qg, k_all, v_all, mask, sinks)
    return out.reshape(Bd, T, Q_COLS)


def pool_mix(u_ext, pos, pool_w, pool_scale):
    T = pos.shape[0]
    P = POOL_BUF
    uf = u_ext.astype(jnp.float32)
    cs = jnp.concatenate([jnp.zeros_like(uf[:, :1]), jnp.cumsum(uf, axis=1)], axis=1)
    tok = uf[:, P:]
    outs = []
    for g, w in enumerate(POOL_WINDOWS):
        sl = slice(g * POOL_CG, (g + 1) * POOL_CG)
        win = cs[:, P + 1:, sl] - cs[:, P + 1 - w:P + 1 - w + T, sl]
        cnt = jnp.minimum(pos + 1, w).astype(jnp.float32)[None, :, None]
        r = win / cnt - tok[..., sl]
        outs.append(jnp.einsum('btc,cd->btd', r, pool_w[g].astype(jnp.float32)))
    return (jnp.concatenate(outs, axis=-1) * pool_scale.astype(jnp.float32)).astype(u_ext.dtype)


def trunk_layer(x, pe, pos, k_buf, v_buf, u_buf, w_in, sinks, pool_w, pool_scale, g_attn, g_pool, w_out,
                g_pre_mix, g_post_mix, g_pre_mlp, g_post_mlp, w_up, w_down, w_ple, w_ple_gate, b_ple_gate):
    B, T, _ = x.shape
    h = rmsnorm(x, g_pre_mix)
    proj = h @ w_in
    q = proj[..., :Q_COLS].reshape(B, T, N_HEADS, HEAD_DIM)
    k = proj[..., Q_COLS:Q_COLS + KV_COLS].reshape(B, T, N_KV_HEADS, HEAD_DIM)
    v = proj[..., Q_COLS + KV_COLS:Q_COLS + 2 * KV_COLS].reshape(B, T, N_KV_HEADS, HEAD_DIM)
    u = proj[..., Q_COLS + 2 * KV_COLS:]
    q = rope(q, pos)
    k = rope(k, pos)
    if k_buf is None:
        attn = prompt_attention(q, k, v, sinks)
        k_all, v_all = k, v
        u_ext = jnp.concatenate([jnp.zeros((B, POOL_BUF, POOL_WIDTH), u.dtype), u], axis=1)
    else:
        k_all = jnp.concatenate([k_buf.astype(k.dtype), k], axis=1)
        v_all = jnp.concatenate([v_buf.astype(v.dtype), v], axis=1)
        attn = sample_attention(q, k_all, v_all, pos, sinks)
        u_ext = jnp.concatenate([u_buf.astype(u.dtype), u], axis=1)
    pooled = pool_mix(u_ext, pos, pool_w, pool_scale)
    mix = jnp.concatenate([rmsnorm(attn, g_attn), rmsnorm(pooled, g_pool)], axis=-1) @ w_out
    x = x + rmsnorm(mix, g_post_mix)
    hf = rmsnorm(x, g_pre_mlp)
    ff = jnp.square(jax.nn.relu(hf @ w_up)) @ w_down
    x = x + rmsnorm(ff, g_post_mlp)
    gate = jax.nn.sigmoid(x @ w_ple_gate + b_ple_gate)
    x = x + gate * (pe @ w_ple)
    return x, k_all[:, -WINDOW:], v_all[:, -WINDOW:], u_ext[:, -POOL_BUF:]


def setup_inputs(seed: int = 0) -> dict:
    key = jax.random.key(seed)
    ks = jax.random.split(key, 24)
    f32 = jnp.float32
    nrm = lambda k, s: jax.random.normal(k, s, f32)
    gain = lambda k, n: 1.0 + 0.1 * nrm(k, (DEPTH, n))
    return {
        "x_prompt": nrm(ks[0], (BATCH, SEQ, D_MODEL)),
        "x_sample": nrm(ks[1], (DEC_BATCH, DEC_SEQ, D_MODEL)),
        "state_k": nrm(ks[2], (DEPTH, DEC_BATCH, WINDOW, N_KV_HEADS, HEAD_DIM)),
        "state_v": nrm(ks[3], (DEPTH, DEC_BATCH, WINDOW, N_KV_HEADS, HEAD_DIM)),
        "state_pool": nrm(ks[4], (DEPTH, DEC_BATCH, POOL_BUF, POOL_WIDTH)),
        "p_prompt": nrm(ks[5], (DEPTH, BATCH, SEQ, PLE_DIM)),
        "p_sample": nrm(ks[6], (DEPTH, DEC_BATCH, DEC_SEQ, PLE_DIM)),
        "w_in": nrm(ks[7], (DEPTH, D_MODEL, IN_COLS)) * D_MODEL ** -0.5,
        "attn_sinks": 0.5 * nrm(ks[8], (DEPTH, N_HEADS)),
        "pool_w": nrm(ks[9], (DEPTH, N_POOL_GROUPS, POOL_CG, POOL_CG)) * POOL_CG ** -0.5,
        "pool_scale": gain(ks[10], POOL_WIDTH),
        "g_attn_out": gain(ks[11], ATTN_WIDTH),
        "g_pool_out": gain(ks[12], POOL_WIDTH),
        "w_out": nrm(ks[13], (DEPTH, ATTN_WIDTH + POOL_WIDTH, D_MODEL)) * (ATTN_WIDTH + POOL_WIDTH) ** -0.5,
        "g_pre_mix": gain(ks[14], D_MODEL),
        "g_post_mix": gain(ks[15], D_MODEL),
        "g_pre_mlp": gain(ks[16], D_MODEL),
        "g_post_mlp": gain(ks[17], D_MODEL),
        "w_up": nrm(ks[18], (DEPTH, D_MODEL, D_FF)) * D_MODEL ** -0.5,
        "w_down": nrm(ks[19], (DEPTH, D_FF, D_MODEL)) * D_FF ** -0.5,
        "w_ple": nrm(ks[20], (DEPTH, PLE_DIM, D_MODEL)) * PLE_DIM ** -0.5,
        "w_ple_gate": nrm(ks[21], (DEPTH, D_MODEL, D_MODEL)) * D_MODEL ** -0.5,
        "b_ple_gate": 0.01 * nrm(ks[22], (DEPTH, D_MODEL)),
    }


def reference(x_prompt, x_sample, state_k, state_v, state_pool, p_prompt, p_sample, w_in, attn_sinks, pool_w,
              pool_scale, g_attn_out, g_pool_out, w_out, g_pre_mix, g_post_mix, g_pre_mlp, g_post_mlp, w_up, w_down,
              w_ple, w_ple_gate, b_ple_gate):
    pos_prompt = jnp.arange(x_prompt.shape[1], dtype=jnp.int32)
    pos_sample = PAST_LEN + jnp.arange(x_sample.shape[1], dtype=jnp.int32)
    xp, xs = x_prompt, x_sample
    kp_l, vp_l, up_l, ks_l, vs_l, us_l = [], [], [], [], [], []
    for i in range(DEPTH):
        w = (w_in[i], attn_sinks[i], pool_w[i], pool_scale[i], g_attn_out[i], g_pool_out[i], w_out[i],
             g_pre_mix[i], g_post_mix[i], g_pre_mlp[i], g_post_mlp[i], w_up[i], w_down[i], w_ple[i],
             w_ple_gate[i], b_ple_gate[i])
        xp, kp, vp, up = trunk_layer(xp, p_prompt[i], pos_prompt, None, None, None, *w)
        xs, kn, vn, un = trunk_layer(xs, p_sample[i], pos_sample, state_k[i], state_v[i], state_pool[i], *w)
        kp_l.append(kp); vp_l.append(vp); up_l.append(up)
        ks_l.append(kn); vs_l.append(vn); us_l.append(un)
    new_k_prompt = jnp.stack(kp_l)
    new_v_prompt = jnp.stack(vp_l)
    new_pool_prompt = jnp.stack(up_l)
    new_k_sample = jnp.stack(ks_l)
    new_v_sample = jnp.stack(vs_l)
    new_pool_sample = jnp.stack(us_l)
    return (xp, xs, new_k_prompt, new_v_prompt, new_pool_prompt, new_k_sample, new_v_sample, new_pool_sample)
```

```python
import functools

import jax
import jax.numpy as jnp
import numpy as np
from jax import lax
from jax.experimental import pallas as pl
from jax.experimental.pallas import tpu as pltpu

D_MODEL = 1024
HEAD_DIM = 64
N_HEADS = 8
N_KV_HEADS = 2
GROUP = N_HEADS // N_KV_HEADS
WINDOW = 128
Q_COLS = N_HEADS * HEAD_DIM
KV_COLS = N_KV_HEADS * HEAD_DIM
POOL_WIDTH = 512
POOL_WINDOWS = (2, 4, 8, 16)
POOL_CG = POOL_WIDTH // len(POOL_WINDOWS)
POOL_BUF = max(POOL_WINDOWS) - 1
POOL_PAD = POOL_BUF + 1
IN_COLS = Q_COLS + 2 * KV_COLS + POOL_WIDTH
D_FF = 4 * D_MODEL
FF_CHUNK = 1024
PLE_DIM = 256
ROPE_THETA = 10000.0
PAST_LEN = 16384
EPS = 1e-6
NEG = -1e30
LANES = 128

PROMPT_TILE = 512
SAMPLE_SEQS = 16
VMEM_LIMIT = 56 * 1024 * 1024

BF16 = jnp.bfloat16
F32 = jnp.float32


def _rms(x, g):
    y = x * lax.rsqrt(jnp.mean(x * x, axis=-1, keepdims=True) + EPS)
    return y * g


def _mm(a, w):
    return jnp.dot(a, w, preferred_element_type=F32)


def _rope_tables(cos_a, sin_a, cos_b, sin_b, sign):
    cos_t = cos_a * cos_b - sin_a * sin_b
    sin_t = (sin_a * cos_b + cos_a * sin_b) * sign
    return cos_t, sin_t


def _rope(x, cos_t, sin_t, first_half):
    n = x.shape[1] // LANES
    width = x.shape[1]
    partner = jnp.where(first_half, pltpu.roll(x, width - HEAD_DIM // 2, axis=1), pltpu.roll(x, HEAD_DIM // 2, axis=1))
    if n > 1:
        cos_t = jnp.concatenate([cos_t] * n, axis=1)
        sin_t = jnp.concatenate([sin_t] * n, axis=1)
    return x * cos_t + partner * sin_t


def _first_half_mask(rows, width):
    lane = lax.broadcasted_iota(jnp.int32, (rows, width), 1)
    return (lane % HEAD_DIM) < (HEAD_DIM // 2)


def _project(x, g_pre, w_in, tables):
    rows = x.shape[0]
    h = _rms(x, g_pre).astype(BF16)
    proj = _mm(h, w_in)
    cos_t, sin_t = tables
    q = _rope(proj[:, :Q_COLS], cos_t, sin_t, _first_half_mask(rows, Q_COLS)) * (HEAD_DIM ** -0.5)
    k = _rope(proj[:, Q_COLS:Q_COLS + KV_COLS], cos_t, sin_t, _first_half_mask(rows, KV_COLS))
    v = proj[:, Q_COLS + KV_COLS:Q_COLS + 2 * KV_COLS]
    u = proj[:, Q_COLS + 2 * KV_COLS:]
    return q, k, v, u


def _window_sums(ext):
    outs = []
    for g, w in enumerate(POOL_WINDOWS):
        a = ext[:, g * POOL_CG:(g + 1) * POOL_CG]
        step = 1
        while step < w:
            a = a + pltpu.roll(a, step, axis=0)
            step *= 2
        outs.append(a)
    return outs


def _pool_out(win, tok, inv_cnt, pool_w_ref, pool_scale):
    outs = []
    for g in range(len(POOL_WINDOWS)):
        r = win[g] * inv_cnt[g] - tok[:, g * POOL_CG:(g + 1) * POOL_CG]
        outs.append(_mm(r.astype(BF16), pool_w_ref[g]))
    return jnp.concatenate(outs, axis=1) * pool_scale


def _mix_out(x, attn, pooled, g_attn, g_pool, w_out, g_post):
    cat = jnp.concatenate([_rms(attn, g_attn), _rms(pooled, g_pool)], axis=1).astype(BF16)
    return x + _rms(_mm(cat, w_out), g_post)


def _softmax_parts(s_list, sink):
    m = sink
    for s in s_list:
        m = jnp.maximum(m, jnp.max(s, axis=-1, keepdims=True))
    e_list = [jnp.exp(s - m) for s in s_list]
    den = jnp.exp(sink - m)
    for e in e_list:
        den = den + jnp.sum(e, axis=-1, keepdims=True)
    return e_list, den


def _prompt_mixer_kernel(x_ref, cos_a_ref, sin_a_ref, cos_b_ref, sin_b_ref, sign_ref, g_pre_ref, w_in_ref,
                         sinks_ref, pool_w_ref, pool_scale_ref, g_attn_ref, g_pool_ref, w_out_ref, g_post_ref,
                         xmid_ref, knew_ref, vnew_ref, unew_ref, kbuf, vbuf, ubuf):
    tm = x_ref.shape[0]
    nblk = tm // WINDOW
    step = pl.program_id(0)
    base = step * tm

    @pl.when(step == 0)
    def _():
        kbuf[0:WINDOW, :] = jnp.zeros((WINDOW, KV_COLS), BF16)
        vbuf[0:WINDOW, :] = jnp.zeros((WINDOW, KV_COLS), BF16)
        ubuf[0:POOL_PAD, :] = jnp.zeros((POOL_PAD, POOL_WIDTH), F32)

    x = x_ref[...]
    tables = _rope_tables(cos_a_ref[0], sin_a_ref[0], cos_b_ref[...], sin_b_ref[...], sign_ref[...])
    q, k, v, u = _project(x, g_pre_ref[...], w_in_ref[...], tables)

    knew_ref[...] = k[tm - WINDOW:, :]
    vnew_ref[...] = v[tm - WINDOW:, :]
    kbuf[WINDOW:, :] = k.astype(BF16)
    vbuf[WINDOW:, :] = v.astype(BF16)
    ubuf[POOL_PAD:, :] = u
    qb = q.astype(BF16)

    r_i = lax.broadcasted_iota(jnp.int32, (WINDOW, 2 * WINDOW), 0)
    c_i = lax.broadcasted_iota(jnp.int32, (WINDOW, 2 * WINDOW), 1)
    d_i = c_i - r_i
    band = (d_i >= 1) & (d_i <= WINDOW)
    band0 = band & (c_i >= WINDOW - base)

    blocks = []
    for j in range(nblk):
        mask = band0 if j == 0 else band
        heads = [None] * N_HEADS
        for h in range(N_KV_HEADS):
            q4 = jnp.concatenate(
                [qb[j * WINDOW:(j + 1) * WINDOW, (GROUP * h + g) * HEAD_DIM:(GROUP * h + g + 1) * HEAD_DIM]
                 for g in range(GROUP)], axis=0)
            kk = kbuf[j * WINDOW:(j + 2) * WINDOW, h * HEAD_DIM:(h + 1) * HEAD_DIM]
            vv = vbuf[j * WINDOW:(j + 2) * WINDOW, h * HEAD_DIM:(h + 1) * HEAD_DIM]
            s = lax.dot_general(q4, kk, (((1,), (1,)), ((), ())), preferred_element_type=F32)
            s = jnp.where(mask[None], s.reshape(GROUP, WINDOW, 2 * WINDOW), NEG)
            sink = jnp.concatenate(
                [jnp.full((1, WINDOW, 1), sinks_ref[GROUP * h + g], F32) for g in range(GROUP)], axis=0)
            (e,), den = _softmax_parts([s], sink)
            o = _mm(e.reshape(GROUP * WINDOW, 2 * WINDOW).astype(BF16), vv)
            o = o.reshape(GROUP, WINDOW, HEAD_DIM) / den
            for g in range(GROUP):
                heads[GROUP * h + g] = o[g]
        blocks.append(jnp.concatenate(heads, axis=1))
    attn = jnp.concatenate(blocks, axis=0)

    win = [a[POOL_PAD:, :] for a in _window_sums(ubuf[...])]
    pos = base + lax.broadcasted_iota(jnp.int32, (tm, 1), 0)
    inv_cnt = [1.0 / jnp.minimum(pos + 1, w).astype(F32) for w in POOL_WINDOWS]
    pooled = _pool_out(win, u, inv_cnt, pool_w_ref, pool_scale_ref[...])

    xmid_ref[...] = _mix_out(x, attn, pooled, g_attn_ref[...], g_pool_ref[...], w_out_ref[...], g_post_ref[...])

    unew_ref[...] = ubuf[tm + 1:tm + POOL_PAD, :]
    kbuf[0:WINDOW, :] = kbuf[tm:tm + WINDOW, :]
    vbuf[0:WINDOW, :] = vbuf[tm:tm + WINDOW, :]
    ubuf[0:POOL_PAD, :] = ubuf[tm:tm + POOL_PAD, :]


def _sample_mixer_kernel(x_ref, cos_ref, sin_ref, sk_ref, sv_ref, sp_ref, g_pre_ref, w_in_ref,
                         sinks_ref, pool_w_ref, pool_scale_ref, g_attn_ref, g_pool_ref, w_out_ref, g_post_ref,
                         xmid_ref, knew_ref, vnew_ref, unew_ref, ubuf):
    rows = x_ref.shape[0]
    nseq = sk_ref.shape[0]
    t = rows // nseq
    x = x_ref[...]
    q, k, v, u = _project(x, g_pre_ref[...], w_in_ref[...], (cos_ref[...], sin_ref[...]))

    k3 = k.reshape(nseq, t, KV_COLS)
    v3 = v.reshape(nseq, t, KV_COLS)
    knew_ref[:, 0:WINDOW - t, :] = sk_ref[:, t:, :]
    knew_ref[:, WINDOW - t:, :] = k3
    vnew_ref[:, 0:WINDOW - t, :] = sv_ref[:, t:, :]
    vnew_ref[:, WINDOW - t:, :] = v3

    r_i = lax.broadcasted_iota(jnp.int32, (GROUP * t, WINDOW), 0) % t
    c_i = lax.broadcasted_iota(jnp.int32, (GROUP * t, WINDOW), 1)
    mask_old = (c_i > r_i)[None]
    r_n = lax.broadcasted_iota(jnp.int32, (GROUP * t, t), 0) % t
    c_n = lax.broadcasted_iota(jnp.int32, (GROUP * t, t), 1)
    mask_new = (c_n <= r_n)[None]

    qb = q.astype(BF16)
    kb = k3.astype(BF16)
    vb = v3.astype(BF16)
    heads = [None] * N_HEADS
    for h in range(N_KV_HEADS):
        lanes = slice(h * HEAD_DIM, (h + 1) * HEAD_DIM)
        q4 = jnp.concatenate(
            [qb[:, (GROUP * h + g) * HEAD_DIM:(GROUP * h + g + 1) * HEAD_DIM].reshape(nseq, t, HEAD_DIM)
             for g in range(GROUP)], axis=1)
        k_old = sk_ref[:, :, lanes].astype(BF16)
        v_old = sv_ref[:, :, lanes].astype(BF16)
        s_old = jnp.einsum('bqd,bkd->bqk', q4, k_old, preferred_element_type=F32)
        s_new = jnp.einsum('bqd,bkd->bqk', q4, kb[:, :, lanes], preferred_element_type=F32)
        s_old = jnp.where(mask_old, s_old, NEG)
        s_new = jnp.where(mask_new, s_new, NEG)
        sink = jnp.concatenate(
            [jnp.full((1, t, 1), sinks_ref[GROUP * h + g], F32) for g in range(GROUP)], axis=1)
        (e_old, e_new), den = _softmax_parts([s_old, s_new], sink)
        o = jnp.einsum('bqk,bkd->bqd', e_old.astype(BF16), v_old, preferred_element_type=F32)
        o = o + jnp.einsum('bqk,bkd->bqd', e_new.astype(BF16), vb[:, :, lanes], preferred_element_type=F32)
        o = o / den
        for g in range(GROUP):
            heads[GROUP * h + g] = o[:, g * t:(g + 1) * t, :].reshape(rows, HEAD_DIM)
    attn = jnp.concatenate(heads, axis=1)

    ext = POOL_PAD + t
    ubuf[:, 0:1, :] = jnp.zeros((nseq, 1, POOL_WIDTH), F32)
    ubuf[:, 1:POOL_PAD, :] = sp_ref[...]
    ubuf[:, POOL_PAD:, :] = u.reshape(nseq, t, POOL_WIDTH)
    unew_ref[...] = ubuf[:, t + 1:ext, :]
    win = [a.reshape(nseq, ext, POOL_CG)[:, POOL_PAD:, :].reshape(rows, POOL_CG)
           for a in _window_sums(ubuf[...].reshape(nseq * ext, POOL_WIDTH))]
    inv_cnt = [1.0 / w for w in POOL_WINDOWS]
    pooled = _pool_out(win, u, inv_cnt, pool_w_ref, pool_scale_ref[...])

    xmid_ref[...] = _mix_out(x, attn, pooled, g_attn_ref[...], g_pool_ref[...], w_out_ref[...], g_post_ref[...])


def _ffn_kernel(x_ref, p_ref, g_pre_ref, w_up_ref, w_down_ref, g_post_ref, w_gate_ref, b_gate_ref, w_ple_ref, y_ref):
    x = x_ref[...]
    hf = _rms(x, g_pre_ref[...]).astype(BF16)
    ff = jnp.zeros(x.shape, F32)
    for c in range(D_FF // FF_CHUNK):
        up = _mm(hf, w_up_ref[:, c * FF_CHUNK:(c + 1) * FF_CHUNK])
        act = jnp.square(jnp.maximum(up, 0.0)).astype(BF16)
        ff = ff + _mm(act, w_down_ref[c * FF_CHUNK:(c + 1) * FF_CHUNK, :])
    x = x + _rms(ff, g_post_ref[...])
    gate = jax.nn.sigmoid(_mm(x.astype(BF16), w_gate_ref[...]) + b_gate_ref[...])
    y_ref[...] = x + gate * _mm(p_ref[...].astype(BF16), w_ple_ref[...])


def _const_spec(shape):
    zeros = (0,) * len(shape)
    return pl.BlockSpec(shape, lambda i: zeros, pipeline_mode=pl.Buffered(1))


def _rope_lane_tables(pos):
    half = HEAD_DIM // 2
    inv = ROPE_THETA ** (-np.arange(half, dtype=np.float64) / half)
    ang = pos[:, None] * inv[None, :]
    reps = LANES // half
    return (np.tile(np.cos(ang), (1, reps)).astype(np.float32), np.tile(np.sin(ang), (1, reps)).astype(np.float32))


def _rope_sign():
    lane = np.arange(LANES)
    return np.where((lane % HEAD_DIM) < HEAD_DIM // 2, -1.0, 1.0).astype(np.float32)[None, :]


def _params(dims):
    return pltpu.CompilerParams(dimension_semantics=dims, vmem_limit_bytes=VMEM_LIMIT)


def _mixer_weight_specs():
    return [
        _const_spec((1, D_MODEL)),
        _const_spec((D_MODEL, IN_COLS)),
        pl.BlockSpec(memory_space=pltpu.SMEM),
        _const_spec((len(POOL_WINDOWS), POOL_CG, POOL_CG)),
        _const_spec((1, POOL_WIDTH)),
        _const_spec((1, Q_COLS)),
        _const_spec((1, POOL_WIDTH)),
        _const_spec((D_MODEL, D_MODEL)),
        _const_spec((1, D_MODEL)),
    ]


def _prompt_mixer(x, mixer_w):
    seq = x.shape[0]
    tm = PROMPT_TILE
    steps = seq // tm
    cos_a, sin_a = _rope_lane_tables(np.arange(steps, dtype=np.float64) * tm)
    cos_b, sin_b = _rope_lane_tables(np.arange(tm, dtype=np.float64))
    row = lambda i: (i, 0)
    fixed = lambda i: (0, 0)
    return pl.pallas_call(
        _prompt_mixer_kernel,
        grid=(steps,),
        in_specs=[
            pl.BlockSpec((tm, D_MODEL), row),
            pl.BlockSpec((1, 1, LANES), lambda i: (i, 0, 0)),
            pl.BlockSpec((1, 1, LANES), lambda i: (i, 0, 0)),
            _const_spec((tm, LANES)),
            _const_spec((tm, LANES)),
            _const_spec((1, LANES)),
        ] + _mixer_weight_specs(),
        out_specs=[
            pl.BlockSpec((tm, D_MODEL), row),
            pl.BlockSpec((WINDOW, KV_COLS), fixed),
            pl.BlockSpec((WINDOW, KV_COLS), fixed),
            pl.BlockSpec((POOL_BUF, POOL_WIDTH), fixed),
        ],
        out_shape=[
            jax.ShapeDtypeStruct((seq, D_MODEL), F32),
            jax.ShapeDtypeStruct((WINDOW, KV_COLS), F32),
            jax.ShapeDtypeStruct((WINDOW, KV_COLS), F32),
            jax.ShapeDtypeStruct((POOL_BUF, POOL_WIDTH), F32),
        ],
        scratch_shapes=[
            pltpu.VMEM((WINDOW + tm, KV_COLS), BF16),
            pltpu.VMEM((WINDOW + tm, KV_COLS), BF16),
            pltpu.VMEM((POOL_PAD + tm, POOL_WIDTH), F32),
        ],
        compiler_params=_params(("arbitrary",)),
        name="prompt_mixer",
    )(x, jnp.asarray(cos_a)[:, None, :], jnp.asarray(sin_a)[:, None, :], jnp.asarray(cos_b), jnp.asarray(sin_b),
      jnp.asarray(_rope_sign()), *mixer_w)


def _sample_mixer(x, state_k, state_v, state_pool, past_len, mixer_w):
    nseq_all, t = state_k.shape[0], x.shape[0] // state_k.shape[0]
    nseq = SAMPLE_SEQS
    rows = nseq * t
    steps = nseq_all // nseq
    cos, sin = _rope_lane_tables(past_len + np.arange(rows, dtype=np.float64) % t)
    sin = sin * _rope_sign()
    row = lambda i: (i, 0)
    seq3 = lambda i: (i, 0, 0)
    return pl.pallas_call(
        _sample_mixer_kernel,
        grid=(steps,),
        in_specs=[
            pl.BlockSpec((rows, D_MODEL), row),
            _const_spec((rows, LANES)),
            _const_spec((rows, LANES)),
            pl.BlockSpec((nseq, WINDOW, KV_COLS), seq3),
            pl.BlockSpec((nseq, WINDOW, KV_COLS), seq3),
            pl.BlockSpec((nseq, POOL_BUF, POOL_WIDTH), seq3),
        ] + _mixer_weight_specs(),
        out_specs=[
            pl.BlockSpec((rows, D_MODEL), row),
            pl.BlockSpec((nseq, WINDOW, KV_COLS), seq3),
            pl.BlockSpec((nseq, WINDOW, KV_COLS), seq3),
            pl.BlockSpec((nseq, POOL_BUF, POOL_WIDTH), seq3),
        ],
        out_shape=[
            jax.ShapeDtypeStruct((nseq_all * t, D_MODEL), F32),
            jax.ShapeDtypeStruct((nseq_all, WINDOW, KV_COLS), F32),
            jax.ShapeDtypeStruct((nseq_all, WINDOW, KV_COLS), F32),
            jax.ShapeDtypeStruct((nseq_all, POOL_BUF, POOL_WIDTH), F32),
        ],
        scratch_shapes=[pltpu.VMEM((nseq, POOL_PAD + t, POOL_WIDTH), F32)],
        compiler_params=_params(("arbitrary",)),
        name="sample_mixer",
    )(x, jnp.asarray(cos), jnp.asarray(sin), state_k, state_v, state_pool, *mixer_w)


def _ffn(x, p, ffn_w, tm):
    n = x.shape[0]
    row = lambda i: (i, 0)
    return pl.pallas_call(
        _ffn_kernel,
        grid=(n // tm,),
        in_specs=[
            pl.BlockSpec((tm, D_MODEL), row),
            pl.BlockSpec((tm, PLE_DIM), row),
            _const_spec((1, D_MODEL)),
            _const_spec((D_MODEL, D_FF)),
            _const_spec((D_FF, D_MODEL)),
            _const_spec((1, D_MODEL)),
            _const_spec((D_MODEL, D_MODEL)),
            _const_spec((1, D_MODEL)),
            _const_spec((PLE_DIM, D_MODEL)),
        ],
        out_specs=pl.BlockSpec((tm, D_MODEL), row),
        out_shape=jax.ShapeDtypeStruct((n, D_MODEL), F32),
        compiler_params=_params(("arbitrary",)),
        name="ffn",
    )(x, p, *ffn_w)


def kernel(x_prompt, x_sample, state_k, state_v, state_pool, p_prompt, p_sample, w_in, attn_sinks, pool_w,
           pool_scale, g_attn_out, g_pool_out, w_out, g_pre_mix, g_post_mix, g_pre_mlp, g_post_mlp, w_up, w_down,
           w_ple, w_ple_gate, b_ple_gate):
    depth = w_in.shape[0]
    batch, seq, _ = x_prompt.shape
    dec_batch, dec_seq, _ = x_sample.shape
    assert depth == 1 and batch == 1

    xp = x_prompt.reshape(seq, D_MODEL)
    xs = x_sample.reshape(dec_batch * dec_seq, D_MODEL)
    i = 0
    mixer_w = (
        g_pre_mix[i][None, :], w_in[i].astype(BF16), attn_sinks[i],
        pool_w[i].astype(BF16), pool_scale[i][None, :], g_attn_out[i][None, :], g_pool_out[i][None, :],
        w_out[i].astype(BF16), g_post_mix[i][None, :],
    )
    ffn_w = (
        g_pre_mlp[i][None, :], w_up[i].astype(BF16), w_down[i].astype(BF16), g_post_mlp[i][None, :],
        w_ple_gate[i].astype(BF16), b_ple_gate[i][None, :], w_ple[i].astype(BF16),
    )

    xp_mid, kp, vp, up = _prompt_mixer(xp, mixer_w)
    yp = _ffn(xp_mid, p_prompt[i, 0], ffn_w, PROMPT_TILE)

    xs_mid, ks, vs, us = _sample_mixer(
        xs, state_k[i].reshape(dec_batch, WINDOW, KV_COLS), state_v[i].reshape(dec_batch, WINDOW, KV_COLS),
        state_pool[i], float(PAST_LEN), mixer_w)
    ys = _ffn(xs_mid, p_sample[i].reshape(dec_batch * dec_seq, PLE_DIM), ffn_w, PROMPT_TILE)

    return (
        yp.reshape(batch, seq, D_MODEL),
        ys.reshape(dec_batch, dec_seq, D_MODEL),
        kp.reshape(depth, batch, WINDOW, N_KV_HEADS, HEAD_DIM),
        vp.reshape(depth, batch, WINDOW, N_KV_HEADS, HEAD_DIM),
        up.reshape(depth, batch, POOL_BUF, POOL_WIDTH),
        ks.reshape(depth, dec_batch, WINDOW, N_KV_HEADS, HEAD_DIM),
        vs.reshape(depth, dec_batch, WINDOW, N_KV_HEADS, HEAD_DIM),
        us.reshape(depth, dec_batch, POOL_BUF, POOL_WIDTH),
    )
```

```python
import functools

import jax
import jax.numpy as jnp
import numpy as np
from jax import lax
from jax.experimental import pallas as pl
from jax.experimental.pallas import tpu as pltpu

D_MODEL = 1024
HEAD_DIM = 64
N_HEADS = 8
N_KV_HEADS = 2
GROUP = N_HEADS // N_KV_HEADS
WINDOW = 128
Q_COLS = N_HEADS * HEAD_DIM
KV_COLS = N_KV_HEADS * HEAD_DIM
POOL_WIDTH = 512
POOL_WINDOWS = (2, 4, 8, 16)
POOL_CG = POOL_WIDTH // len(POOL_WINDOWS)
POOL_BUF = max(POOL_WINDOWS) - 1
POOL_PAD = POOL_BUF + 1
IN_COLS = Q_COLS + 2 * KV_COLS + POOL_WIDTH
D_FF = 4 * D_MODEL
FF_CHUNK = 1024
PLE_DIM = 256
ROPE_THETA = 10000.0
PAST_LEN = 16384
EPS = 1e-6
NEG = -1e30
LANES = 128

PROMPT_TILE = 512
SAMPLE_SEQS = 16
VMEM_LIMIT = 56 * 1024 * 1024

BF16 = jnp.bfloat16
F32 = jnp.float32


def _rms(x, g):
    y = x * lax.rsqrt(jnp.mean(x * x, axis=-1, keepdims=True) + EPS)
    return y * g


def _mm(a, w):
    return jnp.dot(a, w, preferred_element_type=F32)


def _rope_tables(cos_a, sin_a, cos_b, sin_b, sign):
    cos_t = cos_a * cos_b - sin_a * sin_b
    sin_t = (sin_a * cos_b + cos_a * sin_b) * sign
    return cos_t, sin_t


def _rope(x, cos_t, sin_t, first_half):
    n = x.shape[1] // LANES
    width = x.shape[1]
    partner = jnp.where(first_half, pltpu.roll(x, width - HEAD_DIM // 2, axis=1), pltpu.roll(x, HEAD_DIM // 2, axis=1))
    if n > 1:
        cos_t = jnp.concatenate([cos_t] * n, axis=1)
        sin_t = jnp.concatenate([sin_t] * n, axis=1)
    return x * cos_t + partner * sin_t


def _first_half_mask(rows, width):
    lane = lax.broadcasted_iota(jnp.int32, (rows, width), 1)
    return (lane % HEAD_DIM) < (HEAD_DIM // 2)


def _project(x, g_pre, w_in, tables):
    rows = x.shape[0]
    h = _rms(x, g_pre).astype(BF16)
    proj = _mm(h, w_in)
    cos_t, sin_t = tables
    q = _rope(proj[:, :Q_COLS], cos_t, sin_t, _first_half_mask(rows, Q_COLS)) * (HEAD_DIM ** -0.5)
    k = _rope(proj[:, Q_COLS:Q_COLS + KV_COLS], cos_t, sin_t, _first_half_mask(rows, KV_COLS))
    v = proj[:, Q_COLS + KV_COLS:Q_COLS + 2 * KV_COLS]
    u = proj[:, Q_COLS + 2 * KV_COLS:]
    return q, k, v, u


def _window_sums(ext):
    outs = []
    for g, w in enumerate(POOL_WINDOWS):
        a = ext[:, g * POOL_CG:(g + 1) * POOL_CG]
        step = 1
        while step < w:
            a = a + pltpu.roll(a, step, axis=0)
            step *= 2
        outs.append(a)
    return outs


def _pool_out(win, tok, inv_cnt, pool_w_ref, pool_scale):
    outs = []
    for g in range(len(POOL_WINDOWS)):
        r = win[g] * inv_cnt[g] - tok[:, g * POOL_CG:(g + 1) * POOL_CG]
        outs.append(_mm(r.astype(BF16), pool_w_ref[g]))
    return jnp.concatenate(outs, axis=1) * pool_scale


def _mix_out(x, attn, pooled, g_attn, g_pool, w_out, g_post):
    cat = jnp.concatenate([_rms(attn, g_attn), _rms(pooled, g_pool)], axis=1).astype(BF16)
    return x + _rms(_mm(cat, w_out), g_post)


def _softmax_parts(s_list, sink):
    m = sink
    for s in s_list:
        m = jnp.maximum(m, jnp.max(s, axis=-1, keepdims=True))
    e_list = [jnp.exp(s - m) for s in s_list]
    den = jnp.exp(sink - m)
    for e in e_list:
        den = den + jnp.sum(e, axis=-1, keepdims=True)
    return e_list, den


def _prompt_mixer_kernel(x_ref, cos_a_ref, sin_a_ref, cos_b_ref, sin_b_ref, sign_ref, g_pre_ref, w_in_ref,
                         sinks_ref, pool_w_ref, pool_scale_ref, g_attn_ref, g_pool_ref, w_out_ref, g_post_ref,
                         xmid_ref, knew_ref, vnew_ref, unew_ref, qbuf, kbuf, vtbuf, ubuf, abuf):
    tm = x_ref.shape[0]
    nblk = tm // WINDOW
    half_lanes = LANES // 2
    step = pl.program_id(0)
    base = step * tm

    @pl.when(step == 0)
    def _():
        kbuf[:, 0:WINDOW, :] = jnp.zeros((2 * N_KV_HEADS, WINDOW, LANES), BF16)
        vtbuf[:, 0:WINDOW] = jnp.zeros((KV_COLS, WINDOW), BF16)
        ubuf[0:POOL_PAD, :] = jnp.zeros((POOL_PAD, POOL_WIDTH), F32)

    x = x_ref[...]
    tables = _rope_tables(cos_a_ref[0], sin_a_ref[0], cos_b_ref[...], sin_b_ref[...], sign_ref[...])
    q, k, v, u = _project(x, g_pre_ref[...], w_in_ref[...], tables)

    knew_ref[...] = k[tm - WINDOW:, :]
    vnew_ref[...] = v[tm - WINDOW:, :]
    qbuf[...] = q.astype(BF16)
    low = lax.broadcasted_iota(jnp.int32, (tm, LANES), 1) < half_lanes
    k_swapped = pltpu.roll(k, half_lanes, axis=1)
    kbuf[0, WINDOW:, :] = jnp.where(low, k, 0.0).astype(BF16)
    kbuf[1, WINDOW:, :] = jnp.where(low, 0.0, k_swapped).astype(BF16)
    kbuf[2, WINDOW:, :] = jnp.where(low, k_swapped, 0.0).astype(BF16)
    kbuf[3, WINDOW:, :] = jnp.where(low, 0.0, k).astype(BF16)
    vtbuf[:, WINDOW:] = v.T.astype(BF16)
    ubuf[POOL_PAD:, :] = u

    c_i = lax.broadcasted_iota(jnp.int32, (2 * WINDOW, 2 * WINDOW), 0)
    r_i = lax.broadcasted_iota(jnp.int32, (2 * WINDOW, 2 * WINDOW), 1) % WINDOW
    d_i = c_i - r_i
    band = (d_i >= 1) & (d_i <= WINDOW)
    band0 = band & (c_i >= WINDOW - base)

    for j in range(nblk):
        mask = band0 if j == 0 else band
        rows = slice(j * WINDOW, (j + 1) * WINDOW)
        keys = slice(j * WINDOW, (j + 2) * WINDOW)
        heads = [None] * N_HEADS
        for h in range(N_KV_HEADS):
            q2 = jnp.concatenate([qbuf[rows, (2 * h) * LANES:(2 * h + 1) * LANES],
                                  qbuf[rows, (2 * h + 1) * LANES:(2 * h + 2) * LANES]], axis=0)
            vt = vtbuf[h * HEAD_DIM:(h + 1) * HEAD_DIM, keys]
            for par in range(2):
                st = lax.dot_general(kbuf[2 * h + par, keys, :], q2, (((1,), (1,)), ((), ())),
                                     preferred_element_type=F32)
                st = jnp.where(mask, st, NEG)
                sink = jnp.concatenate([jnp.full((1, WINDOW), sinks_ref[GROUP * h + par], F32),
                                        jnp.full((1, WINDOW), sinks_ref[GROUP * h + 2 + par], F32)], axis=1)
                m = jnp.maximum(jnp.max(st, axis=0, keepdims=True), sink)
                e = jnp.exp(st - m)
                den = jnp.sum(e, axis=0, keepdims=True) + jnp.exp(sink - m)
                ot = _mm(vt, e.astype(BF16)) / den
                heads[GROUP * h + par] = ot[:, :WINDOW]
                heads[GROUP * h + 2 + par] = ot[:, WINDOW:]
        abuf[rows, :] = jnp.concatenate(heads, axis=0).T

    win = [a[POOL_PAD:, :] for a in _window_sums(ubuf[...])]
    pos = base + lax.broadcasted_iota(jnp.int32, (tm, 1), 0)
    inv_cnt = [1.0 / jnp.minimum(pos + 1, w).astype(F32) for w in POOL_WINDOWS]
    pooled = _pool_out(win, u, inv_cnt, pool_w_ref, pool_scale_ref[...])

    xmid_ref[...] = _mix_out(x, abuf[...], pooled, g_attn_ref[...], g_pool_ref[...], w_out_ref[...], g_post_ref[...])

    unew_ref[...] = ubuf[tm + 1:tm + POOL_PAD, :]
    kbuf[:, 0:WINDOW, :] = kbuf[:, tm:tm + WINDOW, :]
    vtbuf[:, 0:WINDOW] = vtbuf[:, tm:tm + WINDOW]
    ubuf[0:POOL_PAD, :] = ubuf[tm:tm + POOL_PAD, :]


def _sample_mixer_kernel(x_ref, cos_ref, sin_ref, sk_ref, sv_ref, sp_ref, g_pre_ref, w_in_ref,
                         sinks_ref, pool_w_ref, pool_scale_ref, g_attn_ref, g_pool_ref, w_out_ref, g_post_ref,
                         xmid_ref, knew_ref, vnew_ref, unew_ref, ubuf):
    rows = x_ref.shape[0]
    nseq = sk_ref.shape[0]
    t = rows // nseq
    x = x_ref[...]
    q, k, v, u = _project(x, g_pre_ref[...], w_in_ref[...], (cos_ref[...], sin_ref[...]))

    k3 = k.reshape(nseq, t, KV_COLS)
    v3 = v.reshape(nseq, t, KV_COLS)
    knew_ref[:, 0:WINDOW - t, :] = sk_ref[:, t:, :]
    knew_ref[:, WINDOW - t:, :] = k3
    vnew_ref[:, 0:WINDOW - t, :] = sv_ref[:, t:, :]
    vnew_ref[:, WINDOW - t:, :] = v3

    r_i = lax.broadcasted_iota(jnp.int32, (GROUP * t, WINDOW), 0) % t
    c_i = lax.broadcasted_iota(jnp.int32, (GROUP * t, WINDOW), 1)
    mask_old = (c_i > r_i)[None]
    r_n = lax.broadcasted_iota(jnp.int32, (GROUP * t, t), 0) % t
    c_n = lax.broadcasted_iota(jnp.int32, (GROUP * t, t), 1)
    mask_new = (c_n <= r_n)[None]

    qb = q.astype(BF16)
    kb = k3.astype(BF16)
    vb = v3.astype(BF16)
    heads = [None] * N_HEADS
    for h in range(N_KV_HEADS):
        lanes = slice(h * HEAD_DIM, (h + 1) * HEAD_DIM)
        q4 = jnp.concatenate(
            [qb[:, (GROUP * h + g) * HEAD_DIM:(GROUP * h + g + 1) * HEAD_DIM].reshape(nseq, t, HEAD_DIM)
             for g in range(GROUP)], axis=1)
        k_old = sk_ref[:, :, lanes].astype(BF16)
        v_old = sv_ref[:, :, lanes].astype(BF16)
        s_old = jnp.einsum('bqd,bkd->bqk', q4, k_old, preferred_element_type=F32)
        s_new = jnp.einsum('bqd,bkd->bqk', q4, kb[:, :, lanes], preferred_element_type=F32)
        s_old = jnp.where(mask_old, s_old, NEG)
        s_new = jnp.where(mask_new, s_new, NEG)
        sink = jnp.concatenate(
            [jnp.full((1, t, 1), sinks_ref[GROUP * h + g], F32) for g in range(GROUP)], axis=1)
        (e_old, e_new), den = _softmax_parts([s_old, s_new], sink)
        o = jnp.einsum('bqk,bkd->bqd', e_old.astype(BF16), v_old, preferred_element_type=F32)
        o = o + jnp.einsum('bqk,bkd->bqd', e_new.astype(BF16), vb[:, :, lanes], preferred_element_type=F32)
        o = o / den
        for g in range(GROUP):
            heads[GROUP * h + g] = o[:, g * t:(g + 1) * t, :].reshape(rows, HEAD_DIM)
    attn = jnp.concatenate(heads, axis=1)

    ext = POOL_PAD + t
    ubuf[:, 0:1, :] = jnp.zeros((nseq, 1, POOL_WIDTH), F32)
    ubuf[:, 1:POOL_PAD, :] = sp_ref[...]
    ubuf[:, POOL_PAD:, :] = u.reshape(nseq, t, POOL_WIDTH)
    unew_ref[...] = ubuf[:, t + 1:ext, :]
    win = [a.reshape(nseq, ext, POOL_CG)[:, POOL_PAD:, :].reshape(rows, POOL_CG)
           for a in _window_sums(ubuf[...].reshape(nseq * ext, POOL_WIDTH))]
    inv_cnt = [1.0 / w for w in POOL_WINDOWS]
    pooled = _pool_out(win, u, inv_cnt, pool_w_ref, pool_scale_ref[...])

    xmid_ref[...] = _mix_out(x, attn, pooled, g_attn_ref[...], g_pool_ref[...], w_out_ref[...], g_post_ref[...])


def _ffn_kernel(x_ref, p_ref, g_pre_ref, w_up_ref, w_down_ref, g_post_ref, w_gate_ref, b_gate_ref, w_ple_ref, y_ref):
    x = x_ref[...]
    hf = _rms(x, g_pre_ref[...]).astype(BF16)
    ff = jnp.zeros(x.shape, F32)
    for c in range(D_FF // FF_CHUNK):
        up = _mm(hf, w_up_ref[:, c * FF_CHUNK:(c + 1) * FF_CHUNK])
        act = jnp.square(jnp.maximum(up, 0.0)).astype(BF16)
        ff = ff + _mm(act, w_down_ref[c * FF_CHUNK:(c + 1) * FF_CHUNK, :])
    x = x + _rms(ff, g_post_ref[...])
    gate = jax.nn.sigmoid(_mm(x.astype(BF16), w_gate_ref[...]) + b_gate_ref[...])
    y_ref[...] = x + gate * _mm(p_ref[...].astype(BF16), w_ple_ref[...])


def _const_spec(shape):
    zeros = (0,) * len(shape)
    return pl.BlockSpec(shape, lambda i: zeros, pipeline_mode=pl.Buffered(1))


def _rope_lane_tables(pos):
    half = HEAD_DIM // 2
    inv = ROPE_THETA ** (-np.arange(half, dtype=np.float64) / half)
    ang = pos[:, None] * inv[None, :]
    reps = LANES // half
    return (np.tile(np.cos(ang), (1, reps)).astype(np.float32), np.tile(np.sin(ang), (1, reps)).astype(np.float32))


def _rope_sign():
    lane = np.arange(LANES)
    return np.where((lane % HEAD_DIM) < HEAD_DIM // 2, -1.0, 1.0).astype(np.float32)[None, :]


def _params(dims):
    return pltpu.CompilerParams(dimension_semantics=dims, vmem_limit_bytes=VMEM_LIMIT)


def _mixer_weight_specs():
    return [
        _const_spec((1, D_MODEL)),
        _const_spec((D_MODEL, IN_COLS)),
        pl.BlockSpec(memory_space=pltpu.SMEM),
        _const_spec((len(POOL_WINDOWS), POOL_CG, POOL_CG)),
        _const_spec((1, POOL_WIDTH)),
        _const_spec((1, Q_COLS)),
        _const_spec((1, POOL_WIDTH)),
        _const_spec((D_MODEL, D_MODEL)),
        _const_spec((1, D_MODEL)),
    ]


def _prompt_mixer(x, mixer_w):
    seq = x.shape[0]
    tm = PROMPT_TILE
    steps = seq // tm
    cos_a, sin_a = _rope_lane_tables(np.arange(steps, dtype=np.float64) * tm)
    cos_b, sin_b = _rope_lane_tables(np.arange(tm, dtype=np.float64))
    row = lambda i: (i, 0)
    fixed = lambda i: (0, 0)
    return pl.pallas_call(
        _prompt_mixer_kernel,
        grid=(steps,),
        in_specs=[
            pl.BlockSpec((tm, D_MODEL), row),
            pl.BlockSpec((1, 1, LANES), lambda i: (i, 0, 0)),
            pl.BlockSpec((1, 1, LANES), lambda i: (i, 0, 0)),
            _const_spec((tm, LANES)),
            _const_spec((tm, LANES)),
            _const_spec((1, LANES)),
        ] + _mixer_weight_specs(),
        out_specs=[
            pl.BlockSpec((tm, D_MODEL), row),
            pl.BlockSpec((WINDOW, KV_COLS), fixed),
            pl.BlockSpec((WINDOW, KV_COLS), fixed),
            pl.BlockSpec((POOL_BUF, POOL_WIDTH), fixed),
        ],
        out_shape=[
            jax.ShapeDtypeStruct((seq, D_MODEL), F32),
            jax.ShapeDtypeStruct((WINDOW, KV_COLS), F32),
            jax.ShapeDtypeStruct((WINDOW, KV_COLS), F32),
            jax.ShapeDtypeStruct((POOL_BUF, POOL_WIDTH), F32),
        ],
        scratch_shapes=[
            pltpu.VMEM((tm, Q_COLS), BF16),
            pltpu.VMEM((2 * N_KV_HEADS, WINDOW + tm, LANES), BF16),
            pltpu.VMEM((KV_COLS, WINDOW + tm), BF16),
            pltpu.VMEM((POOL_PAD + tm, POOL_WIDTH), F32),
            pltpu.VMEM((tm, Q_COLS), F32),
        ],
        compiler_params=_params(("arbitrary",)),
        name="prompt_mixer",
    )(x, jnp.asarray(cos_a)[:, None, :], jnp.asarray(sin_a)[:, None, :], jnp.asarray(cos_b), jnp.asarray(sin_b),
      jnp.asarray(_rope_sign()), *mixer_w)


def _sample_mixer(x, state_k, state_v, state_pool, past_len, mixer_w):
    nseq_all, t = state_k.shape[0], x.shape[0] // state_k.shape[0]
    nseq = SAMPLE_SEQS
    rows = nseq * t
    steps = nseq_all // nseq
    cos, sin = _rope_lane_tables(past_len + np.arange(rows, dtype=np.float64) % t)
    sin = sin * _rope_sign()
    row = lambda i: (i, 0)
    seq3 = lambda i: (i, 0, 0)
    return pl.pallas_call(
        _sample_mixer_kernel,
        grid=(steps,),
        in_specs=[
            pl.BlockSpec((rows, D_MODEL), row),
            _const_spec((rows, LANES)),
            _const_spec((rows, LANES)),
            pl.BlockSpec((nseq, WINDOW, KV_COLS), seq3),
            pl.BlockSpec((nseq, WINDOW, KV_COLS), seq3),
            pl.BlockSpec((nseq, POOL_BUF, POOL_WIDTH), seq3),
        ] + _mixer_weight_specs(),
        out_specs=[
            pl.BlockSpec((rows, D_MODEL), row),
            pl.BlockSpec((nseq, WINDOW, KV_COLS), seq3),
            pl.BlockSpec((nseq, WINDOW, KV_COLS), seq3),
            pl.BlockSpec((nseq, POOL_BUF, POOL_WIDTH), seq3),
        ],
        out_shape=[
            jax.ShapeDtypeStruct((nseq_all * t, D_MODEL), F32),
            jax.ShapeDtypeStruct((nseq_all, WINDOW, KV_COLS), F32),
            jax.ShapeDtypeStruct((nseq_all, WINDOW, KV_COLS), F32),
            jax.ShapeDtypeStruct((nseq_all, POOL_BUF, POOL_WIDTH), F32),
        ],
        scratch_shapes=[pltpu.VMEM((nseq, POOL_PAD + t, POOL_WIDTH), F32)],
        compiler_params=_params(("arbitrary",)),
        name="sample_mixer",
    )(x, jnp.asarray(cos), jnp.asarray(sin), state_k, state_v, state_pool, *mixer_w)


def _ffn(x, p, ffn_w, tm):
    n = x.shape[0]
    row = lambda i: (i, 0)
    return pl.pallas_call(
        _ffn_kernel,
        grid=(n // tm,),
        in_specs=[
            pl.BlockSpec((tm, D_MODEL), row),
            pl.BlockSpec((tm, PLE_DIM), row),
            _const_spec((1, D_MODEL)),
            _const_spec((D_MODEL, D_FF)),
            _const_spec((D_FF, D_MODEL)),
            _const_spec((1, D_MODEL)),
            _const_spec((D_MODEL, D_MODEL)),
            _const_spec((1, D_MODEL)),
            _const_spec((PLE_DIM, D_MODEL)),
        ],
        out_specs=pl.BlockSpec((tm, D_MODEL), row),
        out_shape=jax.ShapeDtypeStruct((n, D_MODEL), F32),
        compiler_params=_params(("arbitrary",)),
        name="ffn",
    )(x, p, *ffn_w)


def kernel(x_prompt, x_sample, state_k, state_v, state_pool, p_prompt, p_sample, w_in, attn_sinks, pool_w,
           pool_scale, g_attn_out, g_pool_out, w_out, g_pre_mix, g_post_mix, g_pre_mlp, g_post_mlp, w_up, w_down,
           w_ple, w_ple_gate, b_ple_gate):
    depth = w_in.shape[0]
    batch, seq, _ = x_prompt.shape
    dec_batch, dec_seq, _ = x_sample.shape
    assert depth == 1 and batch == 1

    xp = x_prompt.reshape(seq, D_MODEL)
    xs = x_sample.reshape(dec_batch * dec_seq, D_MODEL)
    i = 0
    mixer_w = (
        g_pre_mix[i][None, :], w_in[i].astype(BF16), attn_sinks[i],
        pool_w[i].astype(BF16), pool_scale[i][None, :], g_attn_out[i][None, :], g_pool_out[i][None, :],
        w_out[i].astype(BF16), g_post_mix[i][None, :],
    )
    ffn_w = (
        g_pre_mlp[i][None, :], w_up[i].astype(BF16), w_down[i].astype(BF16), g_post_mlp[i][None, :],
        w_ple_gate[i].astype(BF16), b_ple_gate[i][None, :], w_ple[i].astype(BF16),
    )

    xp_mid, kp, vp, up = _prompt_mixer(xp, mixer_w)
    yp = _ffn(xp_mid, p_prompt[i, 0], ffn_w, PROMPT_TILE)

    xs_mid, ks, vs, us = _sample_mixer(
        xs, state_k[i].reshape(dec_batch, WINDOW, KV_COLS), state_v[i].reshape(dec_batch, WINDOW, KV_COLS),
        state_pool[i], float(PAST_LEN), mixer_w)
    ys = _ffn(xs_mid, p_sample[i].reshape(dec_batch * dec_seq, PLE_DIM), ffn_w, PROMPT_TILE)

    return (
        yp.reshape(batch, seq, D_MODEL),
        ys.reshape(dec_batch, dec_seq, D_MODEL),
        kp.reshape(depth, batch, WINDOW, N_KV_HEADS, HEAD_DIM),
        vp.reshape(depth, batch, WINDOW, N_KV_HEADS, HEAD_DIM),
        up.reshape(depth, batch, POOL_BUF, POOL_WIDTH),
        ks.reshape(depth, dec_batch, WINDOW, N_KV_HEADS, HEAD_DIM),
        vs.reshape(depth, dec_batch, WINDOW, N_KV_HEADS, HEAD_DIM),
        us.reshape(depth, dec_batch, POOL_BUF, POOL_WIDTH),
    )
```

```python
import jax
import jax.numpy as jnp
import numpy as np
from jax import lax
from jax.experimental import pallas as pl
from jax.experimental.pallas import tpu as pltpu

D_MODEL = 1024
HEAD_DIM = 64
N_HEADS = 8
N_KV_HEADS = 2
GROUP = N_HEADS // N_KV_HEADS
WINDOW = 128
Q_COLS = N_HEADS * HEAD_DIM
KV_COLS = N_KV_HEADS * HEAD_DIM
POOL_WIDTH = 512
POOL_WINDOWS = (2, 4, 8, 16)
POOL_CG = POOL_WIDTH // len(POOL_WINDOWS)
POOL_BUF = max(POOL_WINDOWS) - 1
POOL_PAD = POOL_BUF + 1
IN_COLS = Q_COLS + 2 * KV_COLS + POOL_WIDTH
D_FF = 4 * D_MODEL
FF_CHUNK = 1024
PLE_DIM = 256
ROPE_THETA = 10000.0
PAST_LEN = 16384
EPS = 1e-6
NEG = -1e30
LANES = 128

PROMPT_TILE = 512
SAMPLE_SEQS = 16
VMEM_LIMIT = 56 * 1024 * 1024

BF16 = jnp.bfloat16
F32 = jnp.float32


def _rms(x, g):
    y = x * lax.rsqrt(jnp.mean(x * x, axis=-1, keepdims=True) + EPS)
    return y * g


def _mm(a, w):
    return jnp.dot(a, w, preferred_element_type=F32)


def _rope_tables(cos_a, sin_a, cos_b, sin_b, sign):
    cos_t = cos_a * cos_b - sin_a * sin_b
    sin_t = (sin_a * cos_b + cos_a * sin_b) * sign
    return cos_t, sin_t


def _rope(x, cos_t, sin_t, first_half):
    n = x.shape[1] // LANES
    width = x.shape[1]
    partner = jnp.where(first_half, pltpu.roll(x, width - HEAD_DIM // 2, axis=1), pltpu.roll(x, HEAD_DIM // 2, axis=1))
    if n > 1:
        cos_t = jnp.concatenate([cos_t] * n, axis=1)
        sin_t = jnp.concatenate([sin_t] * n, axis=1)
    return x * cos_t + partner * sin_t


def _first_half_mask(rows, width):
    lane = lax.broadcasted_iota(jnp.int32, (rows, width), 1)
    return (lane % HEAD_DIM) < (HEAD_DIM // 2)


def _project(x, g_pre, w_in, tables):
    rows = x.shape[0]
    h = _rms(x, g_pre).astype(BF16)
    proj = _mm(h, w_in)
    cos_t, sin_t = tables
    q = _rope(proj[:, :Q_COLS], cos_t, sin_t, _first_half_mask(rows, Q_COLS)) * (HEAD_DIM ** -0.5)
    k = _rope(proj[:, Q_COLS:Q_COLS + KV_COLS], cos_t, sin_t, _first_half_mask(rows, KV_COLS))
    v = proj[:, Q_COLS + KV_COLS:Q_COLS + 2 * KV_COLS]
    u = proj[:, Q_COLS + 2 * KV_COLS:]
    return q, k, v, u


def _window_sums(ext):
    outs = []
    for g, w in enumerate(POOL_WINDOWS):
        a = ext[:, g * POOL_CG:(g + 1) * POOL_CG]
        step = 1
        while step < w:
            a = a + pltpu.roll(a, step, axis=0)
            step *= 2
        outs.append(a)
    return outs


def _pool_out(win, tok, inv_cnt, pool_w_ref, pool_scale):
    outs = []
    for g in range(len(POOL_WINDOWS)):
        r = win[g] * inv_cnt[g] - tok[:, g * POOL_CG:(g + 1) * POOL_CG]
        outs.append(_mm(r.astype(BF16), pool_w_ref[g]))
    return jnp.concatenate(outs, axis=1) * pool_scale


def _mix_out(x, attn, pooled, g_attn, g_pool, w_out, g_post):
    cat = jnp.concatenate([_rms(attn, g_attn), _rms(pooled, g_pool)], axis=1).astype(BF16)
    return x + _rms(_mm(cat, w_out), g_post)


def _softmax_parts(s_list, sink):
    m = sink
    for s in s_list:
        m = jnp.maximum(m, jnp.max(s, axis=-1, keepdims=True))
    e_list = [jnp.exp(s - m) for s in s_list]
    den = jnp.exp(sink - m)
    for e in e_list:
        den = den + jnp.sum(e, axis=-1, keepdims=True)
    return e_list, den


def _prompt_mixer_body(tile, x_ref, cos_a_ref, sin_a_ref, cos_b_ref, sin_b_ref, sign_ref, g_pre_ref, w_in_ref,
                       sinks_ref, pool_w_ref, pool_scale_ref, g_attn_ref, g_pool_ref, w_out_ref, g_post_ref,
                       knew_ref, vnew_ref, unew_ref, qbuf, kbuf, vtbuf, ubuf, abuf):
    tm = x_ref.shape[0]
    nblk = tm // WINDOW
    half_lanes = LANES // 2
    base = tile * tm

    x = x_ref[...]
    tables = _rope_tables(cos_a_ref[0], sin_a_ref[0], cos_b_ref[...], sin_b_ref[...], sign_ref[...])
    q, k, v, u = _project(x, g_pre_ref[...], w_in_ref[...], tables)

    knew_ref[...] = k[tm - WINDOW:, :]
    vnew_ref[...] = v[tm - WINDOW:, :]
    qbuf[...] = q.astype(BF16)
    low = lax.broadcasted_iota(jnp.int32, (tm, LANES), 1) < half_lanes
    k_swapped = pltpu.roll(k, half_lanes, axis=1)
    kbuf[0, WINDOW:, :] = jnp.where(low, k, 0.0).astype(BF16)
    kbuf[1, WINDOW:, :] = jnp.where(low, 0.0, k_swapped).astype(BF16)
    kbuf[2, WINDOW:, :] = jnp.where(low, k_swapped, 0.0).astype(BF16)
    kbuf[3, WINDOW:, :] = jnp.where(low, 0.0, k).astype(BF16)
    vtbuf[:, WINDOW:] = v.T.astype(BF16)
    ubuf[POOL_PAD:, :] = u

    c_i = lax.broadcasted_iota(jnp.int32, (2 * WINDOW, 2 * WINDOW), 0)
    r_i = lax.broadcasted_iota(jnp.int32, (2 * WINDOW, 2 * WINDOW), 1) % WINDOW
    d_i = c_i - r_i
    band = (d_i >= 1) & (d_i <= WINDOW)
    band0 = band & (c_i >= WINDOW - base)

    def scores(j, h, par):
        rows = slice(j * WINDOW, (j + 1) * WINDOW)
        keys = slice(j * WINDOW, (j + 2) * WINDOW)
        q2 = jnp.concatenate([qbuf[rows, (2 * h) * LANES:(2 * h + 1) * LANES],
                              qbuf[rows, (2 * h + 1) * LANES:(2 * h + 2) * LANES]], axis=0)
        return lax.dot_general(kbuf[2 * h + par, keys, :], q2, (((1,), (1,)), ((), ())),
                               preferred_element_type=F32)

    units = [(j, h, par) for j in range(nblk) for h in range(N_KV_HEADS) for par in range(2)]
    st_next = scores(*units[0])
    heads = [None] * N_HEADS
    for n, (j, h, par) in enumerate(units):
        st = st_next
        if n + 1 < len(units):
            st_next = scores(*units[n + 1])
        mask = band0 if j == 0 else band
        st = jnp.where(mask, st, NEG)
        sink = jnp.concatenate([jnp.full((1, WINDOW), sinks_ref[GROUP * h + par], F32),
                                jnp.full((1, WINDOW), sinks_ref[GROUP * h + 2 + par], F32)], axis=1)
        m = jnp.maximum(jnp.max(st, axis=0, keepdims=True), sink)
        e = jnp.exp(st - m)
        den = jnp.sum(e, axis=0, keepdims=True) + jnp.exp(sink - m)
        vt = vtbuf[h * HEAD_DIM:(h + 1) * HEAD_DIM, j * WINDOW:(j + 2) * WINDOW]
        ot = _mm(vt, e.astype(BF16)) / den
        heads[GROUP * h + par] = ot[:, :WINDOW]
        heads[GROUP * h + 2 + par] = ot[:, WINDOW:]
        if h == N_KV_HEADS - 1 and par == 1:
            abuf[j * WINDOW:(j + 1) * WINDOW, :] = jnp.concatenate(heads, axis=0).T

    win = [a[POOL_PAD:, :] for a in _window_sums(ubuf[...])]
    pos = base + lax.broadcasted_iota(jnp.int32, (tm, 1), 0)
    inv_cnt = [1.0 / jnp.minimum(pos + 1, w).astype(F32) for w in POOL_WINDOWS]
    pooled = _pool_out(win, u, inv_cnt, pool_w_ref, pool_scale_ref[...])

    x_mid = _mix_out(x, abuf[...], pooled, g_attn_ref[...], g_pool_ref[...], w_out_ref[...], g_post_ref[...])

    unew_ref[...] = ubuf[tm + 1:tm + POOL_PAD, :]
    kbuf[:, 0:WINDOW, :] = kbuf[:, tm:tm + WINDOW, :]
    vtbuf[:, 0:WINDOW] = vtbuf[:, tm:tm + WINDOW]
    ubuf[0:POOL_PAD, :] = ubuf[tm:tm + POOL_PAD, :]
    return x_mid


def _sample_mixer_kernel(x_ref, cos_ref, sin_ref, sk_ref, sv_ref, sp_ref, g_pre_ref, w_in_ref,
                         sinks_ref, pool_w_ref, pool_scale_ref, g_attn_ref, g_pool_ref, w_out_ref, g_post_ref,
                         xmid_ref, knew_ref, vnew_ref, unew_ref, ubuf):
    rows = x_ref.shape[0]
    nseq = sk_ref.shape[0]
    t = rows // nseq
    x = x_ref[...]
    q, k, v, u = _project(x, g_pre_ref[...], w_in_ref[...], (cos_ref[...], sin_ref[...]))

    k3 = k.reshape(nseq, t, KV_COLS)
    v3 = v.reshape(nseq, t, KV_COLS)
    knew_ref[:, 0:WINDOW - t, :] = sk_ref[:, t:, :]
    knew_ref[:, WINDOW - t:, :] = k3
    vnew_ref[:, 0:WINDOW - t, :] = sv_ref[:, t:, :]
    vnew_ref[:, WINDOW - t:, :] = v3

    r_i = lax.broadcasted_iota(jnp.int32, (GROUP * t, WINDOW), 0) % t
    c_i = lax.broadcasted_iota(jnp.int32, (GROUP * t, WINDOW), 1)
    mask_old = (c_i > r_i)[None]
    r_n = lax.broadcasted_iota(jnp.int32, (GROUP * t, t), 0) % t
    c_n = lax.broadcasted_iota(jnp.int32, (GROUP * t, t), 1)
    mask_new = (c_n <= r_n)[None]

    qb = q.astype(BF16)
    kb = k3.astype(BF16)
    vb = v3.astype(BF16)
    heads = [None] * N_HEADS
    for h in range(N_KV_HEADS):
        lanes = slice(h * HEAD_DIM, (h + 1) * HEAD_DIM)
        q4 = jnp.concatenate(
            [qb[:, (GROUP * h + g) * HEAD_DIM:(GROUP * h + g + 1) * HEAD_DIM].reshape(nseq, t, HEAD_DIM)
             for g in range(GROUP)], axis=1)
        k_old = sk_ref[:, :, lanes].astype(BF16)
        v_old = sv_ref[:, :, lanes].astype(BF16)
        s_old = jnp.einsum('bqd,bkd->bqk', q4, k_old, preferred_element_type=F32)
        s_new = jnp.einsum('bqd,bkd->bqk', q4, kb[:, :, lanes], preferred_element_type=F32)
        s_old = jnp.where(mask_old, s_old, NEG)
        s_new = jnp.where(mask_new, s_new, NEG)
        sink = jnp.concatenate(
            [jnp.full((1, t, 1), sinks_ref[GROUP * h + g], F32) for g in range(GROUP)], axis=1)
        (e_old, e_new), den = _softmax_parts([s_old, s_new], sink)
        o = jnp.einsum('bqk,bkd->bqd', e_old.astype(BF16), v_old, preferred_element_type=F32)
        o = o + jnp.einsum('bqk,bkd->bqd', e_new.astype(BF16), vb[:, :, lanes], preferred_element_type=F32)
        o = o / den
        for g in range(GROUP):
            heads[GROUP * h + g] = o[:, g * t:(g + 1) * t, :].reshape(rows, HEAD_DIM)
    attn = jnp.concatenate(heads, axis=1)

    ext = POOL_PAD + t
    ubuf[:, 0:1, :] = jnp.zeros((nseq, 1, POOL_WIDTH), F32)
    ubuf[:, 1:POOL_PAD, :] = sp_ref[...]
    ubuf[:, POOL_PAD:, :] = u.reshape(nseq, t, POOL_WIDTH)
    unew_ref[...] = ubuf[:, t + 1:ext, :]
    win = [a.reshape(nseq, ext, POOL_CG)[:, POOL_PAD:, :].reshape(rows, POOL_CG)
           for a in _window_sums(ubuf[...].reshape(nseq * ext, POOL_WIDTH))]
    inv_cnt = [1.0 / w for w in POOL_WINDOWS]
    pooled = _pool_out(win, u, inv_cnt, pool_w_ref, pool_scale_ref[...])

    xmid_ref[...] = _mix_out(x, attn, pooled, g_attn_ref[...], g_pool_ref[...], w_out_ref[...], g_post_ref[...])


def _ffn_body(x, p, g_pre_ref, w_up_ref, w_down_ref, g_post_ref, w_gate_ref, b_gate_ref, w_ple_ref):
    hf = _rms(x, g_pre_ref[...]).astype(BF16)
    ff = jnp.zeros(x.shape, F32)
    for c in range(D_FF // FF_CHUNK):
        up = _mm(hf, w_up_ref[:, c * FF_CHUNK:(c + 1) * FF_CHUNK])
        act = jnp.square(jnp.maximum(up, 0.0)).astype(BF16)
        ff = ff + _mm(act, w_down_ref[c * FF_CHUNK:(c + 1) * FF_CHUNK, :])
    x = x + _rms(ff, g_post_ref[...])
    gate = jax.nn.sigmoid(_mm(x.astype(BF16), w_gate_ref[...]) + b_gate_ref[...])
    return x + gate * _mm(p.astype(BF16), w_ple_ref[...])


def _ffn_kernel(x_ref, p_ref, *rest):
    *w_refs, y_ref = rest
    y_ref[...] = _ffn_body(x_ref[...], p_ref[...], *w_refs)


N_MIXER_IN = 15
N_FFN_W = 7


def _prompt_layer_kernel(*refs):
    mixer_in = refs[:N_MIXER_IN]
    p_ref = refs[N_MIXER_IN]
    ffn_w = refs[N_MIXER_IN + 1:N_MIXER_IN + 1 + N_FFN_W]
    y_ref, knew_ref, vnew_ref, unew_ref, qbuf, kbuf, vtbuf, ubuf, abuf, xmid = refs[N_MIXER_IN + 1 + N_FFN_W:]
    step = pl.program_id(0)
    last_tile = pl.num_programs(0) - 2

    @pl.when(step == 0)
    def _():
        kbuf[:, 0:WINDOW, :] = jnp.zeros((2 * N_KV_HEADS, WINDOW, LANES), BF16)
        vtbuf[:, 0:WINDOW] = jnp.zeros((KV_COLS, WINDOW), BF16)
        ubuf[0:POOL_PAD, :] = jnp.zeros((POOL_PAD, POOL_WIDTH), F32)
        xmid[...] = jnp.zeros(xmid.shape, F32)

    y_ref[...] = _ffn_body(xmid[...], p_ref[...], *ffn_w)
    xmid[...] = _prompt_mixer_body(jnp.minimum(step, last_tile), *mixer_in, knew_ref, vnew_ref, unew_ref,
                                   qbuf, kbuf, vtbuf, ubuf, abuf)


def _const_spec(shape):
    zeros = (0,) * len(shape)
    return pl.BlockSpec(shape, lambda i: zeros, pipeline_mode=pl.Buffered(1))


def _rope_lane_tables(pos):
    half = HEAD_DIM // 2
    inv = ROPE_THETA ** (-np.arange(half, dtype=np.float64) / half)
    ang = pos[:, None] * inv[None, :]
    reps = LANES // half
    return (np.tile(np.cos(ang), (1, reps)).astype(np.float32), np.tile(np.sin(ang), (1, reps)).astype(np.float32))


def _rope_sign():
    lane = np.arange(LANES)
    return np.where((lane % HEAD_DIM) < HEAD_DIM // 2, -1.0, 1.0).astype(np.float32)[None, :]


def _params(dims):
    return pltpu.CompilerParams(dimension_semantics=dims, vmem_limit_bytes=VMEM_LIMIT)


def _mixer_weight_specs():
    return [
        _const_spec((1, D_MODEL)),
        _const_spec((D_MODEL, IN_COLS)),
        pl.BlockSpec(memory_space=pltpu.SMEM),
        _const_spec((len(POOL_WINDOWS), POOL_CG, POOL_CG)),
        _const_spec((1, POOL_WIDTH)),
        _const_spec((1, Q_COLS)),
        _const_spec((1, POOL_WIDTH)),
        _const_spec((D_MODEL, D_MODEL)),
        _const_spec((1, D_MODEL)),
    ]


def _ffn_weight_specs():
    return [
        _const_spec((1, D_MODEL)),
        _const_spec((D_MODEL, D_FF)),
        _const_spec((D_FF, D_MODEL)),
        _const_spec((1, D_MODEL)),
        _const_spec((D_MODEL, D_MODEL)),
        _const_spec((1, D_MODEL)),
        _const_spec((PLE_DIM, D_MODEL)),
    ]


def _prompt_layer(x, p, mixer_w, ffn_w):
    seq = x.shape[0]
    tm = PROMPT_TILE
    tiles = seq // tm
    cos_a, sin_a = _rope_lane_tables(np.arange(tiles, dtype=np.float64) * tm)
    cos_b, sin_b = _rope_lane_tables(np.arange(tm, dtype=np.float64))
    mixer_tile = lambda i: (jnp.minimum(i, tiles - 1), 0)
    mixer_tile3 = lambda i: (jnp.minimum(i, tiles - 1), 0, 0)
    ffn_tile = lambda i: (jnp.maximum(i - 1, 0), 0)
    fixed = lambda i: (0, 0)
    in_specs = [
        pl.BlockSpec((tm, D_MODEL), mixer_tile),
        pl.BlockSpec((1, 1, LANES), mixer_tile3),
        pl.BlockSpec((1, 1, LANES), mixer_tile3),
        _const_spec((tm, LANES)),
        _const_spec((tm, LANES)),
        _const_spec((1, LANES)),
    ] + _mixer_weight_specs() + [pl.BlockSpec((tm, PLE_DIM), ffn_tile)] + _ffn_weight_specs()
    assert len(in_specs) == N_MIXER_IN + 1 + N_FFN_W
    return pl.pallas_call(
        _prompt_layer_kernel,
        grid=(tiles + 1,),
        in_specs=in_specs,
        out_specs=[
            pl.BlockSpec((tm, D_MODEL), ffn_tile),
            pl.BlockSpec((WINDOW, KV_COLS), fixed),
            pl.BlockSpec((WINDOW, KV_COLS), fixed),
            pl.BlockSpec((POOL_BUF, POOL_WIDTH), fixed),
        ],
        out_shape=[
            jax.ShapeDtypeStruct((seq, D_MODEL), F32),
            jax.ShapeDtypeStruct((WINDOW, KV_COLS), F32),
            jax.ShapeDtypeStruct((WINDOW, KV_COLS), F32),
            jax.ShapeDtypeStruct((POOL_BUF, POOL_WIDTH), F32),
        ],
        scratch_shapes=[
            pltpu.VMEM((tm, Q_COLS), BF16),
            pltpu.VMEM((2 * N_KV_HEADS, WINDOW + tm, LANES), BF16),
            pltpu.VMEM((KV_COLS, WINDOW + tm), BF16),
            pltpu.VMEM((POOL_PAD + tm, POOL_WIDTH), F32),
            pltpu.VMEM((tm, Q_COLS), F32),
            pltpu.VMEM((tm, D_MODEL), F32),
        ],
        compiler_params=_params(("arbitrary",)),
        name="prompt_layer",
    )(x, jnp.asarray(cos_a)[:, None, :], jnp.asarray(sin_a)[:, None, :], jnp.asarray(cos_b), jnp.asarray(sin_b),
      jnp.asarray(_rope_sign()), *mixer_w, p, *ffn_w)


def _sample_mixer(x, state_k, state_v, state_pool, past_len, mixer_w):
    nseq_all, t = state_k.shape[0], x.shape[0] // state_k.shape[0]
    nseq = SAMPLE_SEQS
    rows = nseq * t
    steps = nseq_all // nseq
    cos, sin = _rope_lane_tables(past_len + np.arange(rows, dtype=np.float64) % t)
    sin = sin * _rope_sign()
    row = lambda i: (i, 0)
    seq3 = lambda i: (i, 0, 0)
    return pl.pallas_call(
        _sample_mixer_kernel,
        grid=(steps,),
        in_specs=[
            pl.BlockSpec((rows, D_MODEL), row),
            _const_spec((rows, LANES)),
            _const_spec((rows, LANES)),
            pl.BlockSpec((nseq, WINDOW, KV_COLS), seq3),
            pl.BlockSpec((nseq, WINDOW, KV_COLS), seq3),
            pl.BlockSpec((nseq, POOL_BUF, POOL_WIDTH), seq3),
        ] + _mixer_weight_specs(),
        out_specs=[
            pl.BlockSpec((rows, D_MODEL), row),
            pl.BlockSpec((nseq, WINDOW, KV_COLS), seq3),
            pl.BlockSpec((nseq, WINDOW, KV_COLS), seq3),
            pl.BlockSpec((nseq, POOL_BUF, POOL_WIDTH), seq3),
        ],
        out_shape=[
            jax.ShapeDtypeStruct((nseq_all * t, D_MODEL), F32),
            jax.ShapeDtypeStruct((nseq_all, WINDOW, KV_COLS), F32),
            jax.ShapeDtypeStruct((nseq_all, WINDOW, KV_COLS), F32),
            jax.ShapeDtypeStruct((nseq_all, POOL_BUF, POOL_WIDTH), F32),
        ],
        scratch_shapes=[pltpu.VMEM((nseq, POOL_PAD + t, POOL_WIDTH), F32)],
        compiler_params=_params(("arbitrary",)),
        name="sample_mixer",
    )(x, jnp.asarray(cos), jnp.asarray(sin), state_k, state_v, state_pool, *mixer_w)


def _ffn(x, p, ffn_w, tm):
    n = x.shape[0]
    row = lambda i: (i, 0)
    return pl.pallas_call(
        _ffn_kernel,
        grid=(n // tm,),
        in_specs=[
            pl.BlockSpec((tm, D_MODEL), row),
            pl.BlockSpec((tm, PLE_DIM), row),
        ] + _ffn_weight_specs(),
        out_specs=pl.BlockSpec((tm, D_MODEL), row),
        out_shape=jax.ShapeDtypeStruct((n, D_MODEL), F32),
        compiler_params=_params(("arbitrary",)),
        name="ffn",
    )(x, p, *ffn_w)


def kernel(x_prompt, x_sample, state_k, state_v, state_pool, p_prompt, p_sample, w_in, attn_sinks, pool_w,
           pool_scale, g_attn_out, g_pool_out, w_out, g_pre_mix, g_post_mix, g_pre_mlp, g_post_mlp, w_up, w_down,
           w_ple, w_ple_gate, b_ple_gate):
    depth = w_in.shape[0]
    batch, seq, _ = x_prompt.shape
    dec_batch, dec_seq, _ = x_sample.shape
    assert depth == 1 and batch == 1

    xp = x_prompt.reshape(seq, D_MODEL)
    xs = x_sample.reshape(dec_batch * dec_seq, D_MODEL)
    i = 0
    mixer_w = (
        g_pre_mix[i][None, :], w_in[i].astype(BF16), attn_sinks[i],
        pool_w[i].astype(BF16), pool_scale[i][None, :], g_attn_out[i][None, :], g_pool_out[i][None, :],
        w_out[i].astype(BF16), g_post_mix[i][None, :],
    )
    ffn_w = (
        g_pre_mlp[i][None, :], w_up[i].astype(BF16), w_down[i].astype(BF16), g_post_mlp[i][None, :],
        w_ple_gate[i].astype(BF16), b_ple_gate[i][None, :], w_ple[i].astype(BF16),
    )

    yp, kp, vp, up = _prompt_layer(xp, p_prompt[i, 0], mixer_w, ffn_w)

    xs_mid, ks, vs, us = _sample_mixer(
        xs, state_k[i].reshape(dec_batch, WINDOW, KV_COLS), state_v[i].reshape(dec_batch, WINDOW, KV_COLS),
        state_pool[i], float(PAST_LEN), mixer_w)
    ys = _ffn(xs_mid, p_sample[i].reshape(dec_batch * dec_seq, PLE_DIM), ffn_w, PROMPT_TILE)

    return (
        yp.reshape(batch, seq, D_MODEL),
        ys.reshape(dec_batch, dec_seq, D_MODEL),
        kp.reshape(depth, batch, WINDOW, N_KV_HEADS, HEAD_DIM),
        vp.reshape(depth, batch, WINDOW, N_KV_HEADS, HEAD_DIM),
        up.reshape(depth, batch, POOL_BUF, POOL_WIDTH),
        ks.reshape(depth, dec_batch, WINDOW, N_KV_HEADS, HEAD_DIM),
        vs.reshape(depth, dec_batch, WINDOW, N_KV_HEADS, HEAD_DIM),
        us.reshape(depth, dec_batch, POOL_BUF, POOL_WIDTH),
    )
```

```python
import jax
import jax.numpy as jnp
import numpy as np
from jax import lax
from jax.experimental import pallas as pl
from jax.experimental.pallas import tpu as pltpu

D_MODEL = 1024
HEAD_DIM = 64
N_HEADS = 8
N_KV_HEADS = 2
GROUP = N_HEADS // N_KV_HEADS
WINDOW = 128
Q_COLS = N_HEADS * HEAD_DIM
KV_COLS = N_KV_HEADS * HEAD_DIM
POOL_WIDTH = 512
POOL_WINDOWS = (2, 4, 8, 16)
POOL_CG = POOL_WIDTH // len(POOL_WINDOWS)
POOL_BUF = max(POOL_WINDOWS) - 1
POOL_PAD = POOL_BUF + 1
IN_COLS = Q_COLS + 2 * KV_COLS + POOL_WIDTH
D_FF = 4 * D_MODEL
FF_CHUNK = 1024
PLE_DIM = 256
ROPE_THETA = 10000.0
PAST_LEN = 16384
EPS = 1e-6
NEG = -1e30
LANES = 128
MXU_N = 256

ATTN_LOOKAHEAD = 2
PROMPT_PATTERN = "11" + "21" * (5 + ATTN_LOOKAHEAD + 16 + 4) + "111" + "22222" + "11111"

PROMPT_TILE = 512
SAMPLE_SEQS = 16
VMEM_LIMIT = 56 * 1024 * 1024

BF16 = jnp.bfloat16
F32 = jnp.float32


def _rms(x, g):
    y = x * lax.rsqrt(jnp.mean(x * x, axis=-1, keepdims=True) + EPS)
    return y * g


def _mm(a, w):
    return jnp.dot(a, w, preferred_element_type=F32)


def _rope_tables(cos_a, sin_a, cos_b, sin_b, sign):
    cos_t = cos_a * cos_b - sin_a * sin_b
    sin_t = (sin_a * cos_b + cos_a * sin_b) * sign
    return cos_t, sin_t


def _rope(x, cos_t, sin_t, first_half):
    n = x.shape[1] // LANES
    width = x.shape[1]
    partner = jnp.where(first_half, pltpu.roll(x, width - HEAD_DIM // 2, axis=1), pltpu.roll(x, HEAD_DIM // 2, axis=1))
    if n > 1:
        cos_t = jnp.concatenate([cos_t] * n, axis=1)
        sin_t = jnp.concatenate([sin_t] * n, axis=1)
    return x * cos_t + partner * sin_t


def _first_half_mask(rows, width):
    lane = lax.broadcasted_iota(jnp.int32, (rows, width), 1)
    return (lane % HEAD_DIM) < (HEAD_DIM // 2)


def _project(x, g_pre, w_in, tables):
    rows = x.shape[0]
    h = _rms(x, g_pre).astype(BF16)
    proj = _mm(h, w_in)
    cos_t, sin_t = tables
    q = _rope(proj[:, :Q_COLS], cos_t, sin_t, _first_half_mask(rows, Q_COLS)) * (HEAD_DIM ** -0.5)
    k = _rope(proj[:, Q_COLS:Q_COLS + KV_COLS], cos_t, sin_t, _first_half_mask(rows, KV_COLS))
    v = proj[:, Q_COLS + KV_COLS:Q_COLS + 2 * KV_COLS]
    u = proj[:, Q_COLS + 2 * KV_COLS:]
    return q, k, v, u


def _window_sums(ext):
    outs = []
    for g, w in enumerate(POOL_WINDOWS):
        a = ext[:, g * POOL_CG:(g + 1) * POOL_CG]
        step = 1
        while step < w:
            a = a + pltpu.roll(a, step, axis=0)
            step *= 2
        outs.append(a)
    return outs


def _pool_out(win, tok, inv_cnt, pool_w_ref, pool_scale):
    outs = []
    for g in range(len(POOL_WINDOWS)):
        r = win[g] * inv_cnt[g] - tok[:, g * POOL_CG:(g + 1) * POOL_CG]
        outs.append(_mm(r.astype(BF16), pool_w_ref[g]))
    return jnp.concatenate(outs, axis=1) * pool_scale


def _mix_out(x, attn, pooled, g_attn, g_pool, w_out, g_post):
    cat = jnp.concatenate([_rms(attn, g_attn), _rms(pooled, g_pool)], axis=1).astype(BF16)
    return x + _rms(_mm(cat, w_out), g_post)


def _softmax_parts(s_list, sink):
    m = sink
    for s in s_list:
        m = jnp.maximum(m, jnp.max(s, axis=-1, keepdims=True))
    e_list = [jnp.exp(s - m) for s in s_list]
    den = jnp.exp(sink - m)
    for e in e_list:
        den = den + jnp.sum(e, axis=-1, keepdims=True)
    return e_list, den


def _prompt_mixer_steps(tile, x_ref, cos_a_ref, sin_a_ref, cos_b_ref, sin_b_ref, sign_ref, g_pre_ref, w_in_ref,
                        sinks_ref, pool_w_ref, pool_scale_ref, g_attn_ref, g_pool_ref, w_out_ref, g_post_ref,
                        knew_ref, vnew_ref, unew_ref, qbuf, kbuf, vtbuf, ubuf, pbuf, catbuf,
                        g_pre_mlp_ref, xmid_ref, hmid_ref):
    tm = x_ref.shape[0]
    nblk = tm // WINDOW
    half_lanes = LANES // 2
    base = tile * tm

    h = _rms(x_ref[...], g_pre_ref[...]).astype(BF16)
    cos_t, sin_t = _rope_tables(cos_a_ref[0], sin_a_ref[0], cos_b_ref[...], sin_b_ref[...], sign_ref[...])

    kv = _mm(h, w_in_ref[:, Q_COLS:Q_COLS + 2 * KV_COLS])
    k = _rope(kv[:, :KV_COLS], cos_t, sin_t, _first_half_mask(tm, KV_COLS))
    v = kv[:, KV_COLS:]
    knew_ref[...] = k[tm - WINDOW:, :]
    vnew_ref[...] = v[tm - WINDOW:, :]
    low = lax.broadcasted_iota(jnp.int32, (tm, LANES), 1) < half_lanes
    k_swapped = pltpu.roll(k, half_lanes, axis=1)
    kbuf[0, WINDOW:, :] = jnp.where(low, k, 0.0).astype(BF16)
    kbuf[1, WINDOW:, :] = jnp.where(low, 0.0, k_swapped).astype(BF16)
    kbuf[2, WINDOW:, :] = jnp.where(low, k_swapped, 0.0).astype(BF16)
    kbuf[3, WINDOW:, :] = jnp.where(low, 0.0, k).astype(BF16)
    vtbuf[:, WINDOW:] = v.T.astype(BF16)
    yield

    u_col0 = Q_COLS + 2 * KV_COLS
    for c in range(Q_COLS // MXU_N):
        cols = slice(c * MXU_N, (c + 1) * MXU_N)
        q = _rope(_mm(h, w_in_ref[:, cols]), cos_t, sin_t, _first_half_mask(tm, MXU_N)) * (HEAD_DIM ** -0.5)
        qbuf[:, cols] = q.astype(BF16)
        yield
    for c in range(POOL_WIDTH // MXU_N):
        cols = slice(c * MXU_N, (c + 1) * MXU_N)
        ubuf[POOL_PAD:, cols] = _mm(h, w_in_ref[:, u_col0 + c * MXU_N:u_col0 + (c + 1) * MXU_N])
        yield

    pos = base + lax.broadcasted_iota(jnp.int32, (tm, 1), 0)
    pool_scale = pool_scale_ref[...]
    for g, w in enumerate(POOL_WINDOWS):
        cols = slice(g * POOL_CG, (g + 1) * POOL_CG)
        a = ubuf[:, cols]
        shift = 1
        while shift < w:
            a = a + pltpu.roll(a, shift, axis=0)
            shift *= 2
        inv_cnt = 1.0 / jnp.minimum(pos + 1, w).astype(F32)
        r = a[POOL_PAD:, :] * inv_cnt - ubuf[POOL_PAD:, cols]
        pbuf[:, cols] = _mm(r.astype(BF16), pool_w_ref[g]) * pool_scale[:, cols]
        yield
    catbuf[:, Q_COLS:] = _rms(pbuf[...], g_pool_ref[...]).astype(BF16)

    c_i = lax.broadcasted_iota(jnp.int32, (2 * WINDOW, 2 * WINDOW), 0)
    r_i = lax.broadcasted_iota(jnp.int32, (2 * WINDOW, 2 * WINDOW), 1) % WINDOW
    d_i = c_i - r_i
    band = (d_i >= 1) & (d_i <= WINDOW)
    band0 = band & (c_i >= WINDOW - base)

    def scores(j, h_kv, par):
        rows = slice(j * WINDOW, (j + 1) * WINDOW)
        keys = slice(j * WINDOW, (j + 2) * WINDOW)
        q2 = jnp.concatenate([qbuf[rows, (2 * h_kv) * LANES:(2 * h_kv + 1) * LANES],
                              qbuf[rows, (2 * h_kv + 1) * LANES:(2 * h_kv + 2) * LANES]], axis=0)
        return lax.dot_general(kbuf[2 * h_kv + par, keys, :], q2, (((1,), (1,)), ((), ())),
                               preferred_element_type=F32)

    units = [(j, h_kv, par) for j in range(nblk) for h_kv in range(N_KV_HEADS) for par in range(2)]
    pending = []
    for n in range(min(ATTN_LOOKAHEAD, len(units))):
        pending.append(scores(*units[n]))
        yield
    heads = [None] * N_HEADS
    for n, (j, h_kv, par) in enumerate(units):
        st = pending.pop(0)
        if n + ATTN_LOOKAHEAD < len(units):
            pending.append(scores(*units[n + ATTN_LOOKAHEAD]))
        mask = band0 if j == 0 else band
        st = jnp.where(mask, st, NEG)
        sink = jnp.concatenate([jnp.full((1, WINDOW), sinks_ref[GROUP * h_kv + par], F32),
                                jnp.full((1, WINDOW), sinks_ref[GROUP * h_kv + 2 + par], F32)], axis=1)
        m = jnp.maximum(jnp.max(st, axis=0, keepdims=True), sink)
        e = jnp.exp(st - m)
        den = jnp.sum(e, axis=0, keepdims=True) + jnp.exp(sink - m)
        vt = vtbuf[h_kv * HEAD_DIM:(h_kv + 1) * HEAD_DIM, j * WINDOW:(j + 2) * WINDOW]
        ot = _mm(vt, e.astype(BF16)) / den
        heads[GROUP * h_kv + par] = ot[:, :WINDOW]
        heads[GROUP * h_kv + 2 + par] = ot[:, WINDOW:]
        if h_kv == N_KV_HEADS - 1 and par == 1:
            attn = jnp.concatenate(heads, axis=0).T
            catbuf[j * WINDOW:(j + 1) * WINDOW, :Q_COLS] = _rms(attn, g_attn_ref[...]).astype(BF16)
        yield

    mix = []
    for c in range(D_MODEL // MXU_N):
        mix.append(_mm(catbuf[...], w_out_ref[:, c * MXU_N:(c + 1) * MXU_N]))
        yield
    x_mid = x_ref[...] + _rms(jnp.concatenate(mix, axis=1), g_post_ref[...])

    unew_ref[...] = ubuf[tm + 1:tm + POOL_PAD, :]
    kbuf[:, 0:WINDOW, :] = kbuf[:, tm:tm + WINDOW, :]
    vtbuf[:, 0:WINDOW] = vtbuf[:, tm:tm + WINDOW]
    ubuf[0:POOL_PAD, :] = ubuf[tm:tm + POOL_PAD, :]
    xmid_ref[...] = x_mid
    hmid_ref[...] = _rms(x_mid, g_pre_mlp_ref[...]).astype(BF16)


def _sample_mixer_kernel(x_ref, cos_ref, sin_ref, sk_ref, sv_ref, sp_ref, g_pre_ref, w_in_ref,
                         sinks_ref, pool_w_ref, pool_scale_ref, g_attn_ref, g_pool_ref, w_out_ref, g_post_ref,
                         xmid_ref, knew_ref, vnew_ref, unew_ref, ubuf):
    rows = x_ref.shape[0]
    nseq = sk_ref.shape[0]
    t = rows // nseq
    x = x_ref[...]
    q, k, v, u = _project(x, g_pre_ref[...], w_in_ref[...], (cos_ref[...], sin_ref[...]))

    k3 = k.reshape(nseq, t, KV_COLS)
    v3 = v.reshape(nseq, t, KV_COLS)
    knew_ref[:, 0:WINDOW - t, :] = sk_ref[:, t:, :]
    knew_ref[:, WINDOW - t:, :] = k3
    vnew_ref[:, 0:WINDOW - t, :] = sv_ref[:, t:, :]
    vnew_ref[:, WINDOW - t:, :] = v3

    r_i = lax.broadcasted_iota(jnp.int32, (GROUP * t, WINDOW), 0) % t
    c_i = lax.broadcasted_iota(jnp.int32, (GROUP * t, WINDOW), 1)
    mask_old = (c_i > r_i)[None]
    r_n = lax.broadcasted_iota(jnp.int32, (GROUP * t, t), 0) % t
    c_n = lax.broadcasted_iota(jnp.int32, (GROUP * t, t), 1)
    mask_new = (c_n <= r_n)[None]

    qb = q.astype(BF16)
    kb = k3.astype(BF16)
    vb = v3.astype(BF16)
    heads = [None] * N_HEADS
    for h in range(N_KV_HEADS):
        lanes = slice(h * HEAD_DIM, (h + 1) * HEAD_DIM)
        q4 = jnp.concatenate(
            [qb[:, (GROUP * h + g) * HEAD_DIM:(GROUP * h + g + 1) * HEAD_DIM].reshape(nseq, t, HEAD_DIM)
             for g in range(GROUP)], axis=1)
        k_old = sk_ref[:, :, lanes].astype(BF16)
        v_old = sv_ref[:, :, lanes].astype(BF16)
        s_old = jnp.einsum('bqd,bkd->bqk', q4, k_old, preferred_element_type=F32)
        s_new = jnp.einsum('bqd,bkd->bqk', q4, kb[:, :, lanes], preferred_element_type=F32)
        s_old = jnp.where(mask_old, s_old, NEG)
        s_new = jnp.where(mask_new, s_new, NEG)
        sink = jnp.concatenate(
            [jnp.full((1, t, 1), sinks_ref[GROUP * h + g], F32) for g in range(GROUP)], axis=1)
        (e_old, e_new), den = _softmax_parts([s_old, s_new], sink)
        o = jnp.einsum('bqk,bkd->bqd', e_old.astype(BF16), v_old, preferred_element_type=F32)
        o = o + jnp.einsum('bqk,bkd->bqd', e_new.astype(BF16), vb[:, :, lanes], preferred_element_type=F32)
        o = o / den
        for g in range(GROUP):
            heads[GROUP * h + g] = o[:, g * t:(g + 1) * t, :].reshape(rows, HEAD_DIM)
    attn = jnp.concatenate(heads, axis=1)

    ext = POOL_PAD + t
    ubuf[:, 0:1, :] = jnp.zeros((nseq, 1, POOL_WIDTH), F32)
    ubuf[:, 1:POOL_PAD, :] = sp_ref[...]
    ubuf[:, POOL_PAD:, :] = u.reshape(nseq, t, POOL_WIDTH)
    unew_ref[...] = ubuf[:, t + 1:ext, :]
    win = [a.reshape(nseq, ext, POOL_CG)[:, POOL_PAD:, :].reshape(rows, POOL_CG)
           for a in _window_sums(ubuf[...].reshape(nseq * ext, POOL_WIDTH))]
    inv_cnt = [1.0 / w for w in POOL_WINDOWS]
    pooled = _pool_out(win, u, inv_cnt, pool_w_ref, pool_scale_ref[...])

    xmid_ref[...] = _mix_out(x, attn, pooled, g_attn_ref[...], g_pool_ref[...], w_out_ref[...], g_post_ref[...])


def _ffn_body(x, p, g_pre_ref, w_up_ref, w_down_ref, g_post_ref, w_gate_ref, b_gate_ref, w_ple_ref):
    hf = _rms(x, g_pre_ref[...]).astype(BF16)
    ff = jnp.zeros(x.shape, F32)
    for c in range(D_FF // FF_CHUNK):
        up = _mm(hf, w_up_ref[:, c * FF_CHUNK:(c + 1) * FF_CHUNK])
        act = jnp.square(jnp.maximum(up, 0.0)).astype(BF16)
        ff = ff + _mm(act, w_down_ref[c * FF_CHUNK:(c + 1) * FF_CHUNK, :])
    x = x + _rms(ff, g_post_ref[...])
    gate = jax.nn.sigmoid(_mm(x.astype(BF16), w_gate_ref[...]) + b_gate_ref[...])
    return x + gate * _mm(p.astype(BF16), w_ple_ref[...])


def _ffn_kernel(x_ref, p_ref, *rest):
    *w_refs, y_ref = rest
    y_ref[...] = _ffn_body(x_ref[...], p_ref[...], *w_refs)


def _ffn_steps(x_ref, h_ref, p_ref, w_up_ref, w_down_ref, g_post_ref, w_gate_ref, b_gate_ref, w_ple_ref,
               y_ref, actbuf):
    for c in range(D_FF // MXU_N):
        cols = slice(c * MXU_N, (c + 1) * MXU_N)
        up = _mm(h_ref[...], w_up_ref[:, cols])
        actbuf[:, cols] = jnp.square(jnp.maximum(up, 0.0)).astype(BF16)
        yield
    ff = []
    for c in range(D_MODEL // MXU_N):
        cols = slice(c * MXU_N, (c + 1) * MXU_N)
        acc = None
        for r in range(D_FF // FF_CHUNK):
            rows = slice(r * FF_CHUNK, (r + 1) * FF_CHUNK)
            part = _mm(actbuf[:, rows], w_down_ref[rows, cols])
            acc = part if acc is None else acc + part
            yield
        ff.append(acc)
    x = x_ref[...] + _rms(jnp.concatenate(ff, axis=1), g_post_ref[...])
    xb = x.astype(BF16)
    ple = _mm(p_ref[...].astype(BF16), w_ple_ref[...])
    yield
    for c in range(D_MODEL // MXU_N):
        cols = slice(c * MXU_N, (c + 1) * MXU_N)
        gate = jax.nn.sigmoid(_mm(xb, w_gate_ref[:, cols]) + b_gate_ref[:, cols])
        y_ref[:, cols] = x[:, cols] + gate * ple[:, cols]
        yield


def _interleave(first, second, pattern):
    gens = {'1': first, '2': second}
    for tag in pattern:
        next(gens[tag], None)
    for _ in first:
        pass
    for _ in second:
        pass


N_MIXER_IN = 15
N_FFN_W = 7


def _prompt_layer_kernel(*refs):
    mixer_in = refs[:N_MIXER_IN]
    p_ref = refs[N_MIXER_IN]
    g_pre_mlp_ref, *ffn_w = refs[N_MIXER_IN + 1:N_MIXER_IN + 1 + N_FFN_W]
    (y_ref, knew_ref, vnew_ref, unew_ref,
     qbuf, kbuf, vtbuf, ubuf, pbuf, catbuf, actbuf, xmid, hmid) = refs[N_MIXER_IN + 1 + N_FFN_W:]
    step = pl.program_id(0)
    last_tile = pl.num_programs(0) - 2

    @pl.when(step == 0)
    def _():
        kbuf[:, 0:WINDOW, :] = jnp.zeros((2 * N_KV_HEADS, WINDOW, LANES), BF16)
        vtbuf[:, 0:WINDOW] = jnp.zeros((KV_COLS, WINDOW), BF16)
        ubuf[0:POOL_PAD, :] = jnp.zeros((POOL_PAD, POOL_WIDTH), F32)
        xmid[0] = jnp.zeros(xmid.shape[1:], F32)
        hmid[0] = jnp.zeros(hmid.shape[1:], BF16)

    slot = step % 2
    ffn = _ffn_steps(xmid.at[slot], hmid.at[slot], p_ref, *ffn_w, y_ref, actbuf)
    mixer = _prompt_mixer_steps(jnp.minimum(step, last_tile), *mixer_in, knew_ref, vnew_ref, unew_ref,
                                qbuf, kbuf, vtbuf, ubuf, pbuf, catbuf, g_pre_mlp_ref, xmid.at[1 - slot],
                                hmid.at[1 - slot])
    _interleave(ffn, mixer, PROMPT_PATTERN)


def _const_spec(shape):
    zeros = (0,) * len(shape)
    return pl.BlockSpec(shape, lambda i: zeros, pipeline_mode=pl.Buffered(1))


def _rope_lane_tables(pos):
    half = HEAD_DIM // 2
    inv = ROPE_THETA ** (-np.arange(half, dtype=np.float64) / half)
    ang = pos[:, None] * inv[None, :]
    reps = LANES // half
    return (np.tile(np.cos(ang), (1, reps)).astype(np.float32), np.tile(np.sin(ang), (1, reps)).astype(np.float32))


def _rope_sign():
    lane = np.arange(LANES)
    return np.where((lane % HEAD_DIM) < HEAD_DIM // 2, -1.0, 1.0).astype(np.float32)[None, :]


def _params(dims):
    return pltpu.CompilerParams(dimension_semantics=dims, vmem_limit_bytes=VMEM_LIMIT)


def _mixer_weight_specs():
    return [
        _const_spec((1, D_MODEL)),
        _const_spec((D_MODEL, IN_COLS)),
        pl.BlockSpec(memory_space=pltpu.SMEM),
        _const_spec((len(POOL_WINDOWS), POOL_CG, POOL_CG)),
        _const_spec((1, POOL_WIDTH)),
        _const_spec((1, Q_COLS)),
        _const_spec((1, POOL_WIDTH)),
        _const_spec((D_MODEL, D_MODEL)),
        _const_spec((1, D_MODEL)),
    ]


def _ffn_weight_specs():
    return [
        _const_spec((1, D_MODEL)),
        _const_spec((D_MODEL, D_FF)),
        _const_spec((D_FF, D_MODEL)),
        _const_spec((1, D_MODEL)),
        _const_spec((D_MODEL, D_MODEL)),
        _const_spec((1, D_MODEL)),
        _const_spec((PLE_DIM, D_MODEL)),
    ]


def _prompt_layer(x, p, mixer_w, ffn_w):
    seq = x.shape[0]
    tm = PROMPT_TILE
    tiles = seq // tm
    cos_a, sin_a = _rope_lane_tables(np.arange(tiles, dtype=np.float64) * tm)
    cos_b, sin_b = _rope_lane_tables(np.arange(tm, dtype=np.float64))
    mixer_tile = lambda i: (jnp.minimum(i, tiles - 1), 0)
    mixer_tile3 = lambda i: (jnp.minimum(i, tiles - 1), 0, 0)
    ffn_tile = lambda i: (jnp.maximum(i - 1, 0), 0)
    fixed = lambda i: (0, 0)
    in_specs = [
        pl.BlockSpec((tm, D_MODEL), mixer_tile),
        pl.BlockSpec((1, 1, LANES), mixer_tile3),
        pl.BlockSpec((1, 1, LANES), mixer_tile3),
        _const_spec((tm, LANES)),
        _const_spec((tm, LANES)),
        _const_spec((1, LANES)),
    ] + _mixer_weight_specs() + [pl.BlockSpec((tm, PLE_DIM), ffn_tile)] + _ffn_weight_specs()
    assert len(in_specs) == N_MIXER_IN + 1 + N_FFN_W
    return pl.pallas_call(
        _prompt_layer_kernel,
        grid=(tiles + 1,),
        in_specs=in_specs,
        out_specs=[
            pl.BlockSpec((tm, D_MODEL), ffn_tile),
            pl.BlockSpec((WINDOW, KV_COLS), fixed),
            pl.BlockSpec((WINDOW, KV_COLS), fixed),
            pl.BlockSpec((POOL_BUF, POOL_WIDTH), fixed),
        ],
        out_shape=[
            jax.ShapeDtypeStruct((seq, D_MODEL), F32),
            jax.ShapeDtypeStruct((WINDOW, KV_COLS), F32),
            jax.ShapeDtypeStruct((WINDOW, KV_COLS), F32),
            jax.ShapeDtypeStruct((POOL_BUF, POOL_WIDTH), F32),
        ],
        scratch_shapes=[
            pltpu.VMEM((tm, Q_COLS), BF16),
            pltpu.VMEM((2 * N_KV_HEADS, WINDOW + tm, LANES), BF16),
            pltpu.VMEM((KV_COLS, WINDOW + tm), BF16),
            pltpu.VMEM((POOL_PAD + tm, POOL_WIDTH), F32),
            pltpu.VMEM((tm, POOL_WIDTH), F32),
            pltpu.VMEM((tm, D_MODEL), BF16),
            pltpu.VMEM((tm, D_FF), BF16),
            pltpu.VMEM((2, tm, D_MODEL), F32),
            pltpu.VMEM((2, tm, D_MODEL), BF16),
        ],
        compiler_params=_params(("arbitrary",)),
        name="prompt_layer",
    )(x, jnp.asarray(cos_a)[:, None, :], jnp.asarray(sin_a)[:, None, :], jnp.asarray(cos_b), jnp.asarray(sin_b),
      jnp.asarray(_rope_sign()), *mixer_w, p, *ffn_w)


def _sample_mixer(x, state_k, state_v, state_pool, past_len, mixer_w):
    nseq_all, t = state_k.shape[0], x.shape[0] // state_k.shape[0]
    nseq = SAMPLE_SEQS
    rows = nseq * t
    steps = nseq_all // nseq
    cos, sin = _rope_lane_tables(past_len + np.arange(rows, dtype=np.float64) % t)
    sin = sin * _rope_sign()
    row = lambda i: (i, 0)
    seq3 = lambda i: (i, 0, 0)
    return pl.pallas_call(
        _sample_mixer_kernel,
        grid=(steps,),
        in_specs=[
            pl.BlockSpec((rows, D_MODEL), row),
            _const_spec((rows, LANES)),
            _const_spec((rows, LANES)),
            pl.BlockSpec((nseq, WINDOW, KV_COLS), seq3),
            pl.BlockSpec((nseq, WINDOW, KV_COLS), seq3),
            pl.BlockSpec((nseq, POOL_BUF, POOL_WIDTH), seq3),
        ] + _mixer_weight_specs(),
        out_specs=[
            pl.BlockSpec((rows, D_MODEL), row),
            pl.BlockSpec((nseq, WINDOW, KV_COLS), seq3),
            pl.BlockSpec((nseq, WINDOW, KV_COLS), seq3),
            pl.BlockSpec((nseq, POOL_BUF, POOL_WIDTH), seq3),
        ],
        out_shape=[
            jax.ShapeDtypeStruct((nseq_all * t, D_MODEL), F32),
            jax.ShapeDtypeStruct((nseq_all, WINDOW, KV_COLS), F32),
            jax.ShapeDtypeStruct((nseq_all, WINDOW, KV_COLS), F32),
            jax.ShapeDtypeStruct((nseq_all, POOL_BUF, POOL_WIDTH), F32),
        ],
        scratch_shapes=[pltpu.VMEM((nseq, POOL_PAD + t, POOL_WIDTH), F32)],
        compiler_params=_params(("arbitrary",)),
        name="sample_mixer",
    )(x, jnp.asarray(cos), jnp.asarray(sin), state_k, state_v, state_pool, *mixer_w)


def _ffn(x, p, ffn_w, tm):
    n = x.shape[0]
    row = lambda i: (i, 0)
    return pl.pallas_call(
        _ffn_kernel,
        grid=(n // tm,),
        in_specs=[
            pl.BlockSpec((tm, D_MODEL), row),
            pl.BlockSpec((tm, PLE_DIM), row),
        ] + _ffn_weight_specs(),
        out_specs=pl.BlockSpec((tm, D_MODEL), row),
        out_shape=jax.ShapeDtypeStruct((n, D_MODEL), F32),
        compiler_params=_params(("arbitrary",)),
        name="ffn",
    )(x, p, *ffn_w)


def kernel(x_prompt, x_sample, state_k, state_v, state_pool, p_prompt, p_sample, w_in, attn_sinks, pool_w,
           pool_scale, g_attn_out, g_pool_out, w_out, g_pre_mix, g_post_mix, g_pre_mlp, g_post_mlp, w_up, w_down,
           w_ple, w_ple_gate, b_ple_gate):
    depth = w_in.shape[0]
    batch, seq, _ = x_prompt.shape
    dec_batch, dec_seq, _ = x_sample.shape
    assert depth == 1 and batch == 1

    xp = x_prompt.reshape(seq, D_MODEL)
    xs = x_sample.reshape(dec_batch * dec_seq, D_MODEL)
    i = 0
    mixer_w = (
        g_pre_mix[i][None, :], w_in[i].astype(BF16), attn_sinks[i],
        pool_w[i].astype(BF16), pool_scale[i][None, :], g_attn_out[i][None, :], g_pool_out[i][None, :],
        w_out[i].astype(BF16), g_post_mix[i][None, :],
    )
    ffn_w = (
        g_pre_mlp[i][None, :], w_up[i].astype(BF16), w_down[i].astype(BF16), g_post_mlp[i][None, :],
        w_ple_gate[i].astype(BF16), b_ple_gate[i][None, :], w_ple[i].astype(BF16),
    )

    yp, kp, vp, up = _prompt_layer(xp, p_prompt[i, 0], mixer_w, ffn_w)

    xs_mid, ks, vs, us = _sample_mixer(
        xs, state_k[i].reshape(dec_batch, WINDOW, KV_COLS), state_v[i].reshape(dec_batch, WINDOW, KV_COLS),
        state_pool[i], float(PAST_LEN), mixer_w)
    ys = _ffn(xs_mid, p_sample[i].reshape(dec_batch * dec_seq, PLE_DIM), ffn_w, PROMPT_TILE)

    return (
        yp.reshape(batch, seq, D_MODEL),
        ys.reshape(dec_batch, dec_seq, D_MODEL),
        kp.reshape(depth, batch, WINDOW, N_KV_HEADS, HEAD_DIM),
        vp.reshape(depth, batch, WINDOW, N_KV_HEADS, HEAD_DIM),
        up.reshape(depth, batch, POOL_BUF, POOL_WIDTH),
        ks.reshape(depth, dec_batch, WINDOW, N_KV_HEADS, HEAD_DIM),
        vs.reshape(depth, dec_batch, WINDOW, N_KV_HEADS, HEAD_DIM),
        us.reshape(depth, dec_batch, POOL_BUF, POOL_WIDTH),
    )
```

```python
import jax
import jax.numpy as jnp
import numpy as np
from jax import lax
from jax.experimental import pallas as pl
from jax.experimental.pallas import tpu as pltpu

D_MODEL = 1024
HEAD_DIM = 64
N_HEADS = 8
N_KV_HEADS = 2
GROUP = N_HEADS // N_KV_HEADS
WINDOW = 128
Q_COLS = N_HEADS * HEAD_DIM
KV_COLS = N_KV_HEADS * HEAD_DIM
POOL_WIDTH = 512
POOL_WINDOWS = (2, 4, 8, 16)
POOL_CG = POOL_WIDTH // len(POOL_WINDOWS)
POOL_BUF = max(POOL_WINDOWS) - 1
POOL_PAD = POOL_BUF + 1
IN_COLS = Q_COLS + 2 * KV_COLS + POOL_WIDTH
D_FF = 4 * D_MODEL
FF_CHUNK = 1024
PLE_DIM = 256
ROPE_THETA = 10000.0
PAST_LEN = 16384
EPS = 1e-6
NEG = -1e30
LANES = 128
MXU_N = 256

ATTN_LOOKAHEAD = 2
PROMPT_PATTERN = "11" + "21" * (5 + ATTN_LOOKAHEAD + 16 + 4) + "111" + "22222" + "11111"

PROMPT_TILE = 512
SAMPLE_SEQS = 16
VMEM_LIMIT = 56 * 1024 * 1024

BF16 = jnp.bfloat16
F32 = jnp.float32


def _rms(x, g):
    y = x * lax.rsqrt(jnp.mean(x * x, axis=-1, keepdims=True) + EPS)
    return y * g


def _mm(a, w):
    return jnp.dot(a, w, preferred_element_type=F32)


def _rope_tables(cos_a, sin_a, cos_b, sin_b, sign):
    cos_t = cos_a * cos_b - sin_a * sin_b
    sin_t = (sin_a * cos_b + cos_a * sin_b) * sign
    return cos_t, sin_t


def _rope(x, cos_t, sin_t, first_half):
    n = x.shape[1] // LANES
    width = x.shape[1]
    partner = jnp.where(first_half, pltpu.roll(x, width - HEAD_DIM // 2, axis=1), pltpu.roll(x, HEAD_DIM // 2, axis=1))
    if n > 1:
        cos_t = jnp.concatenate([cos_t] * n, axis=1)
        sin_t = jnp.concatenate([sin_t] * n, axis=1)
    return x * cos_t + partner * sin_t


def _first_half_mask(rows, width):
    lane = lax.broadcasted_iota(jnp.int32, (rows, width), 1)
    return (lane % HEAD_DIM) < (HEAD_DIM // 2)


def _project(x, g_pre, w_in, tables):
    rows = x.shape[0]
    h = _rms(x, g_pre).astype(BF16)
    proj = _mm(h, w_in)
    cos_t, sin_t = tables
    q = _rope(proj[:, :Q_COLS], cos_t, sin_t, _first_half_mask(rows, Q_COLS)) * (HEAD_DIM ** -0.5)
    k = _rope(proj[:, Q_COLS:Q_COLS + KV_COLS], cos_t, sin_t, _first_half_mask(rows, KV_COLS))
    v = proj[:, Q_COLS + KV_COLS:Q_COLS + 2 * KV_COLS]
    u = proj[:, Q_COLS + 2 * KV_COLS:]
    return q, k, v, u


def _window_sums(ext):
    outs = []
    for g, w in enumerate(POOL_WINDOWS):
        a = ext[:, g * POOL_CG:(g + 1) * POOL_CG]
        step = 1
        while step < w:
            a = a + pltpu.roll(a, step, axis=0)
            step *= 2
        outs.append(a)
    return outs


def _pool_out(win, tok, inv_cnt, pool_w_ref, pool_scale):
    outs = []
    for g in range(len(POOL_WINDOWS)):
        r = win[g] * inv_cnt[g] - tok[:, g * POOL_CG:(g + 1) * POOL_CG]
        outs.append(_mm(r.astype(BF16), pool_w_ref[g]))
    return jnp.concatenate(outs, axis=1) * pool_scale


def _mix_out(x, attn, pooled, g_attn, g_pool, w_out, g_post):
    cat = jnp.concatenate([_rms(attn, g_attn), _rms(pooled, g_pool)], axis=1).astype(BF16)
    return x + _rms(_mm(cat, w_out), g_post)


def _softmax_parts(s_list, sink):
    m = sink
    for s in s_list:
        m = jnp.maximum(m, jnp.max(s, axis=-1, keepdims=True))
    e_list = [jnp.exp(s - m) for s in s_list]
    den = jnp.exp(sink - m)
    for e in e_list:
        den = den + jnp.sum(e, axis=-1, keepdims=True)
    return e_list, den


def _prompt_mixer_steps(tile, x_ref, cos_a_ref, sin_a_ref, cos_b_ref, sin_b_ref, sign_ref, g_pre_ref, w_in_ref,
                        sinks_ref, pool_w_ref, pool_scale_ref, g_attn_ref, g_pool_ref, w_out_ref, g_post_ref,
                        knew_ref, vnew_ref, unew_ref, qbuf, kbuf, vtbuf, ubuf, pbuf, catbuf,
                        g_pre_mlp_ref, xmid_ref, hmid_ref):
    tm = x_ref.shape[0]
    nblk = tm // WINDOW
    half_lanes = LANES // 2
    base = tile * tm

    h = _rms(x_ref[...], g_pre_ref[...]).astype(BF16)
    cos_t, sin_t = _rope_tables(cos_a_ref[0], sin_a_ref[0], cos_b_ref[...], sin_b_ref[...], sign_ref[...])

    kv = _mm(h, w_in_ref[:, Q_COLS:Q_COLS + 2 * KV_COLS])
    k = _rope(kv[:, :KV_COLS], cos_t, sin_t, _first_half_mask(tm, KV_COLS))
    v = kv[:, KV_COLS:]
    knew_ref[...] = k[tm - WINDOW:, :]
    vnew_ref[...] = v[tm - WINDOW:, :]
    low = lax.broadcasted_iota(jnp.int32, (tm, LANES), 1) < half_lanes
    k_swapped = pltpu.roll(k, half_lanes, axis=1)
    kbuf[0, WINDOW:, :] = jnp.where(low, k, 0.0).astype(BF16)
    kbuf[1, WINDOW:, :] = jnp.where(low, 0.0, k_swapped).astype(BF16)
    kbuf[2, WINDOW:, :] = jnp.where(low, k_swapped, 0.0).astype(BF16)
    kbuf[3, WINDOW:, :] = jnp.where(low, 0.0, k).astype(BF16)
    vtbuf[:, WINDOW:] = v.T.astype(BF16)
    yield

    u_col0 = Q_COLS + 2 * KV_COLS
    for c in range(Q_COLS // MXU_N):
        cols = slice(c * MXU_N, (c + 1) * MXU_N)
        q = _rope(_mm(h, w_in_ref[:, cols]), cos_t, sin_t, _first_half_mask(tm, MXU_N)) * (HEAD_DIM ** -0.5)
        qbuf[:, cols] = q.astype(BF16)
        yield
    for c in range(POOL_WIDTH // MXU_N):
        cols = slice(c * MXU_N, (c + 1) * MXU_N)
        ubuf[POOL_PAD:, cols] = _mm(h, w_in_ref[:, u_col0 + c * MXU_N:u_col0 + (c + 1) * MXU_N])
        yield

    pos = base + lax.broadcasted_iota(jnp.int32, (tm, 1), 0)
    pool_scale = pool_scale_ref[...]
    for g, w in enumerate(POOL_WINDOWS):
        cols = slice(g * POOL_CG, (g + 1) * POOL_CG)
        a = ubuf[:, cols]
        shift = 1
        while shift < w:
            a = a + pltpu.roll(a, shift, axis=0)
            shift *= 2
        inv_cnt = 1.0 / jnp.minimum(pos + 1, w).astype(F32)
        r = a[POOL_PAD:, :] * inv_cnt - ubuf[POOL_PAD:, cols]
        pbuf[:, cols] = _mm(r.astype(BF16), pool_w_ref[g]) * pool_scale[:, cols]
        yield
    catbuf[:, Q_COLS:] = _rms(pbuf[...], g_pool_ref[...]).astype(BF16)

    c_i = lax.broadcasted_iota(jnp.int32, (2 * WINDOW, 2 * WINDOW), 0)
    r_i = lax.broadcasted_iota(jnp.int32, (2 * WINDOW, 2 * WINDOW), 1) % WINDOW
    d_i = c_i - r_i
    band = (d_i >= 1) & (d_i <= WINDOW)
    band0 = band & (c_i >= WINDOW - base)

    def scores(j, h_kv, par):
        rows = slice(j * WINDOW, (j + 1) * WINDOW)
        keys = slice(j * WINDOW, (j + 2) * WINDOW)
        q2 = jnp.concatenate([qbuf[rows, (2 * h_kv) * LANES:(2 * h_kv + 1) * LANES],
                              qbuf[rows, (2 * h_kv + 1) * LANES:(2 * h_kv + 2) * LANES]], axis=0)
        return lax.dot_general(kbuf[2 * h_kv + par, keys, :], q2, (((1,), (1,)), ((), ())),
                               preferred_element_type=F32)

    units = [(j, h_kv, par) for j in range(nblk) for h_kv in range(N_KV_HEADS) for par in range(2)]
    pending = []
    for n in range(min(ATTN_LOOKAHEAD, len(units))):
        pending.append(scores(*units[n]))
        yield
    heads = [None] * N_HEADS
    for n, (j, h_kv, par) in enumerate(units):
        st = pending.pop(0)
        if n + ATTN_LOOKAHEAD < len(units):
            pending.append(scores(*units[n + ATTN_LOOKAHEAD]))
        mask = band0 if j == 0 else band
        st = jnp.where(mask, st, NEG)
        sink = jnp.concatenate([jnp.full((1, WINDOW), sinks_ref[GROUP * h_kv + par], F32),
                                jnp.full((1, WINDOW), sinks_ref[GROUP * h_kv + 2 + par], F32)], axis=1)
        m = jnp.maximum(jnp.max(st, axis=0, keepdims=True), sink)
        e = jnp.exp(st - m)
        den = jnp.sum(e, axis=0, keepdims=True) + jnp.exp(sink - m)
        vt = vtbuf[h_kv * HEAD_DIM:(h_kv + 1) * HEAD_DIM, j * WINDOW:(j + 2) * WINDOW]
        ot = _mm(vt, e.astype(BF16)) / den
        heads[GROUP * h_kv + par] = ot[:, :WINDOW]
        heads[GROUP * h_kv + 2 + par] = ot[:, WINDOW:]
        if h_kv == N_KV_HEADS - 1 and par == 1:
            attn = jnp.concatenate(heads, axis=0).T
            catbuf[j * WINDOW:(j + 1) * WINDOW, :Q_COLS] = _rms(attn, g_attn_ref[...]).astype(BF16)
        yield

    mix = []
    for c in range(D_MODEL // MXU_N):
        mix.append(_mm(catbuf[...], w_out_ref[:, c * MXU_N:(c + 1) * MXU_N]))
        yield
    x_mid = x_ref[...] + _rms(jnp.concatenate(mix, axis=1), g_post_ref[...])

    unew_ref[...] = ubuf[tm + 1:tm + POOL_PAD, :]
    kbuf[:, 0:WINDOW, :] = kbuf[:, tm:tm + WINDOW, :]
    vtbuf[:, 0:WINDOW] = vtbuf[:, tm:tm + WINDOW]
    ubuf[0:POOL_PAD, :] = ubuf[tm:tm + POOL_PAD, :]
    xmid_ref[...] = x_mid
    hmid_ref[...] = _rms(x_mid, g_pre_mlp_ref[...]).astype(BF16)


def _sample_mixer_kernel(x_ref, cos_ref, sin_ref, sk_ref, sv_ref, sp_ref, g_pre_ref, w_in_ref,
                         sinks_ref, pool_w_ref, pool_scale_ref, g_attn_ref, g_pool_ref, w_out_ref, g_post_ref,
                         xmid_ref, knew_ref, vnew_ref, unew_ref, ubuf):
    rows = x_ref.shape[0]
    nseq = sk_ref.shape[0]
    t = rows // nseq
    x = x_ref[...]
    q, k, v, u = _project(x, g_pre_ref[...], w_in_ref[...], (cos_ref[...], sin_ref[...]))

    k3 = k.reshape(nseq, t, KV_COLS)
    v3 = v.reshape(nseq, t, KV_COLS)
    knew_ref[:, 0:WINDOW - t, :] = sk_ref[:, t:, :]
    knew_ref[:, WINDOW - t:, :] = k3
    vnew_ref[:, 0:WINDOW - t, :] = sv_ref[:, t:, :]
    vnew_ref[:, WINDOW - t:, :] = v3

    r_i = lax.broadcasted_iota(jnp.int32, (GROUP * t, WINDOW), 0) % t
    c_i = lax.broadcasted_iota(jnp.int32, (GROUP * t, WINDOW), 1)
    mask_old = (c_i > r_i)[None]
    r_n = lax.broadcasted_iota(jnp.int32, (GROUP * t, t), 0) % t
    c_n = lax.broadcasted_iota(jnp.int32, (GROUP * t, t), 1)
    mask_new = (c_n <= r_n)[None]

    qb = q.astype(BF16)
    kb = k3.astype(BF16)
    vb = v3.astype(BF16)
    heads = [None] * N_HEADS
    for h in range(N_KV_HEADS):
        lanes = slice(h * HEAD_DIM, (h + 1) * HEAD_DIM)
        q4 = jnp.concatenate(
            [qb[:, (GROUP * h + g) * HEAD_DIM:(GROUP * h + g + 1) * HEAD_DIM].reshape(nseq, t, HEAD_DIM)
             for g in range(GROUP)], axis=1)
        k_old = sk_ref[:, :, lanes].astype(BF16)
        v_old = sv_ref[:, :, lanes].astype(BF16)
        s_old = jnp.einsum('bqd,bkd->bqk', q4, k_old, preferred_element_type=F32)
        s_new = jnp.einsum('bqd,bkd->bqk', q4, kb[:, :, lanes], preferred_element_type=F32)
        s_old = jnp.where(mask_old, s_old, NEG)
        s_new = jnp.where(mask_new, s_new, NEG)
        sink = jnp.concatenate(
            [jnp.full((1, t, 1), sinks_ref[GROUP * h + g], F32) for g in range(GROUP)], axis=1)
        (e_old, e_new), den = _softmax_parts([s_old, s_new], sink)
        o = jnp.einsum('bqk,bkd->bqd', e_old.astype(BF16), v_old, preferred_element_type=F32)
        o = o + jnp.einsum('bqk,bkd->bqd', e_new.astype(BF16), vb[:, :, lanes], preferred_element_type=F32)
        o = o / den
        for g in range(GROUP):
            heads[GROUP * h + g] = o[:, g * t:(g + 1) * t, :].reshape(rows, HEAD_DIM)
    attn = jnp.concatenate(heads, axis=1)

    ext = POOL_PAD + t
    ubuf[:, 0:1, :] = jnp.zeros((nseq, 1, POOL_WIDTH), F32)
    ubuf[:, 1:POOL_PAD, :] = sp_ref[...]
    ubuf[:, POOL_PAD:, :] = u.reshape(nseq, t, POOL_WIDTH)
    unew_ref[...] = ubuf[:, t + 1:ext, :]
    win = [a.reshape(nseq, ext, POOL_CG)[:, POOL_PAD:, :].reshape(rows, POOL_CG)
           for a in _window_sums(ubuf[...].reshape(nseq * ext, POOL_WIDTH))]
    inv_cnt = [1.0 / w for w in POOL_WINDOWS]
    pooled = _pool_out(win, u, inv_cnt, pool_w_ref, pool_scale_ref[...])

    xmid_ref[...] = _mix_out(x, attn, pooled, g_attn_ref[...], g_pool_ref[...], w_out_ref[...], g_post_ref[...])


def _ffn_body(x, p, g_pre_ref, w_up_ref, w_down_ref, g_post_ref, w_gate_ref, b_gate_ref, w_ple_ref):
    hf = _rms(x, g_pre_ref[...]).astype(BF16)
    ff = jnp.zeros(x.shape, F32)
    for c in range(D_FF // FF_CHUNK):
        up = _mm(hf, w_up_ref[:, c * FF_CHUNK:(c + 1) * FF_CHUNK])
        act = jnp.square(jnp.maximum(up, 0.0)).astype(BF16)
        ff = ff + _mm(act, w_down_ref[c * FF_CHUNK:(c + 1) * FF_CHUNK, :])
    x = x + _rms(ff, g_post_ref[...])
    gate = jax.nn.sigmoid(_mm(x.astype(BF16), w_gate_ref[...]) + b_gate_ref[...])
    return x + gate * _mm(p.astype(BF16), w_ple_ref[...])


def _ffn_kernel(x_ref, p_ref, *rest):
    *w_refs, y_ref = rest
    y_ref[...] = _ffn_body(x_ref[...], p_ref[...], *w_refs)


def _ffn_steps(x_ref, h_ref, p_ref, w_up_ref, w_down_ref, g_post_ref, w_gate_ref, b_gate_ref, w_ple_ref,
               y_ref, actbuf):
    for c in range(D_FF // MXU_N):
        cols = slice(c * MXU_N, (c + 1) * MXU_N)
        up = _mm(h_ref[...], w_up_ref[:, cols])
        actbuf[:, cols] = jnp.square(jnp.maximum(up, 0.0)).astype(BF16)
        yield
    ff = []
    for c in range(D_MODEL // MXU_N):
        cols = slice(c * MXU_N, (c + 1) * MXU_N)
        acc = None
        for r in range(D_FF // FF_CHUNK):
            rows = slice(r * FF_CHUNK, (r + 1) * FF_CHUNK)
            part = _mm(actbuf[:, rows], w_down_ref[rows, cols])
            acc = part if acc is None else acc + part
            yield
        ff.append(acc)
    x = x_ref[...] + _rms(jnp.concatenate(ff, axis=1), g_post_ref[...])
    xb = x.astype(BF16)
    ple = _mm(p_ref[...].astype(BF16), w_ple_ref[...])
    yield
    for c in range(D_MODEL // MXU_N):
        cols = slice(c * MXU_N, (c + 1) * MXU_N)
        gate = jax.nn.sigmoid(_mm(xb, w_gate_ref[:, cols]) + b_gate_ref[:, cols])
        y_ref[:, cols] = x[:, cols] + gate * ple[:, cols]
        yield


def _interleave(first, second, pattern):
    gens = {'1': first, '2': second}
    for tag in pattern:
        next(gens[tag], None)
    for _ in first:
        pass
    for _ in second:
        pass


def _cast_weights_to_vmem(srcs, dsts):
    tasks = []
    for src, dst in zip(srcs, dsts):
        if len(src.shape) != 2:
            tasks.append((src, dst, tuple(src.shape)))
            continue
        rows, cols = src.shape
        col_block = cols if cols <= STAGE_MAX_COLS else D_MODEL
        for c0 in range(0, cols, col_block):
            for r0 in range(0, rows, STAGE_ROWS):
                view = (slice(r0, r0 + STAGE_ROWS), slice(c0, c0 + col_block))
                tasks.append((src.at[view], dst.at[view], (STAGE_ROWS, col_block)))
    shapes = sorted(set(shape for _, _, shape in tasks))

    def body(*scoped):
        *stages, sems = scoped
        uses = [0] * len(shapes)
        copies = []
        for src, dst, shape in tasks:
            k = shapes.index(shape)
            slot = uses[k] % 2
            uses[k] += 1
            stage = stages[k].at[slot]
            copies.append((pltpu.make_async_copy(src, stage, sems.at[k, slot]), stage, dst))
        copies[0][0].start()
        for t, (copy, stage, dst) in enumerate(copies):
            if t + 1 < len(copies):
                copies[t + 1][0].start()
            copy.wait()
            dst[...] = stage[...].astype(BF16)

    pl.run_scoped(body, *[pltpu.VMEM((2,) + shape, F32) for shape in shapes],
                  pltpu.SemaphoreType.DMA((len(shapes), 2)))


def _weight_writebacks(bufs, outs, sems):
    return [pltpu.make_async_copy(buf, out, sems.at[n]) for n, (buf, out) in enumerate(zip(bufs, outs))]


N_TABLES = 5
MIXER_MATMUL_W = (1, 3, 7)
FFN_MATMUL_W = (1, 2, 4, 6)
N_MATMUL_W = len(MIXER_MATMUL_W) + len(FFN_MATMUL_W)
STAGE_ROWS = 256
STAGE_MAX_COLS = 1280


def _prompt_layer_kernel(*refs):
    refs = list(refs)
    x_ref, *tables = refs[:1 + N_TABLES]
    (g_pre_mix, w_in_hbm, sinks, pool_w_hbm, pool_scale, g_attn, g_pool, w_out_hbm, g_post_mix, p_ref,
     g_pre_mlp, w_up_hbm, w_down_hbm, g_post_mlp, w_gate_hbm, b_gate, w_ple_hbm) = refs[1 + N_TABLES:18 + N_TABLES]
    y_ref, knew_ref, vnew_ref, unew_ref = refs[18 + N_TABLES:22 + N_TABLES]
    w_outs = refs[22 + N_TABLES:22 + N_TABLES + N_MATMUL_W]
    (qbuf, kbuf, vtbuf, ubuf, pbuf, catbuf, actbuf, xmid, hmid,
     w_in, pool_w, w_out, w_up, w_down, w_gate, w_ple, out_sems) = refs[22 + N_TABLES + N_MATMUL_W:]
    w_hbm = [w_in_hbm, pool_w_hbm, w_out_hbm, w_up_hbm, w_down_hbm, w_gate_hbm, w_ple_hbm]
    w_bufs = [w_in, pool_w, w_out, w_up, w_down, w_gate, w_ple]
    step = pl.program_id(0)
    last_step = pl.num_programs(0) - 1

    @pl.when(step == 0)
    def _():
        _cast_weights_to_vmem(w_hbm, w_bufs)
        for copy in _weight_writebacks(w_bufs, w_outs, out_sems):
            copy.start()
        kbuf[:, 0:WINDOW, :] = jnp.zeros((2 * N_KV_HEADS, WINDOW, LANES), BF16)
        vtbuf[:, 0:WINDOW] = jnp.zeros((KV_COLS, WINDOW), BF16)
        ubuf[0:POOL_PAD, :] = jnp.zeros((POOL_PAD, POOL_WIDTH), F32)
        xmid[0] = jnp.zeros(xmid.shape[1:], F32)
        hmid[0] = jnp.zeros(hmid.shape[1:], BF16)

    slot = step % 2
    ffn = _ffn_steps(xmid.at[slot], hmid.at[slot], p_ref, w_up, w_down, g_post_mlp, w_gate, b_gate, w_ple,
                     y_ref, actbuf)
    mixer = _prompt_mixer_steps(jnp.minimum(step, last_step - 1), x_ref, *tables, g_pre_mix, w_in, sinks, pool_w,
                                pool_scale, g_attn, g_pool, w_out, g_post_mix, knew_ref, vnew_ref, unew_ref,
                                qbuf, kbuf, vtbuf, ubuf, pbuf, catbuf, g_pre_mlp, xmid.at[1 - slot],
                                hmid.at[1 - slot])
    _interleave(ffn, mixer, PROMPT_PATTERN)

    @pl.when(step == last_step)
    def _():
        for copy in _weight_writebacks(w_bufs, w_outs, out_sems):
            copy.wait()


def _const_spec(shape):
    zeros = (0,) * len(shape)
    return pl.BlockSpec(shape, lambda i: zeros, pipeline_mode=pl.Buffered(1))


def _rope_lane_tables(pos):
    half = HEAD_DIM // 2
    inv = ROPE_THETA ** (-np.arange(half, dtype=np.float64) / half)
    ang = pos[:, None] * inv[None, :]
    reps = LANES // half
    return (np.tile(np.cos(ang), (1, reps)).astype(np.float32), np.tile(np.sin(ang), (1, reps)).astype(np.float32))


def _rope_sign():
    lane = np.arange(LANES)
    return np.where((lane % HEAD_DIM) < HEAD_DIM // 2, -1.0, 1.0).astype(np.float32)[None, :]


def _params(dims):
    return pltpu.CompilerParams(dimension_semantics=dims, vmem_limit_bytes=VMEM_LIMIT)


def _mixer_weight_specs():
    return [
        _const_spec((1, D_MODEL)),
        _const_spec((D_MODEL, IN_COLS)),
        pl.BlockSpec(memory_space=pltpu.SMEM),
        _const_spec((len(POOL_WINDOWS), POOL_CG, POOL_CG)),
        _const_spec((1, POOL_WIDTH)),
        _const_spec((1, Q_COLS)),
        _const_spec((1, POOL_WIDTH)),
        _const_spec((D_MODEL, D_MODEL)),
        _const_spec((1, D_MODEL)),
    ]


def _ffn_weight_specs():
    return [
        _const_spec((1, D_MODEL)),
        _const_spec((D_MODEL, D_FF)),
        _const_spec((D_FF, D_MODEL)),
        _const_spec((1, D_MODEL)),
        _const_spec((D_MODEL, D_MODEL)),
        _const_spec((1, D_MODEL)),
        _const_spec((PLE_DIM, D_MODEL)),
    ]


def _prompt_layer(x, p, mixer_w, ffn_w):
    seq = x.shape[0]
    tm = PROMPT_TILE
    tiles = seq // tm
    cos_a, sin_a = _rope_lane_tables(np.arange(tiles, dtype=np.float64) * tm)
    cos_b, sin_b = _rope_lane_tables(np.arange(tm, dtype=np.float64))
    mixer_tile = lambda i: (jnp.minimum(i, tiles - 1), 0)
    mixer_tile3 = lambda i: (jnp.minimum(i, tiles - 1), 0, 0)
    ffn_tile = lambda i: (jnp.maximum(i - 1, 0), 0)
    fixed = lambda i: (0, 0)
    hbm = pl.BlockSpec(memory_space=pl.ANY)
    operands = [x, jnp.asarray(cos_a)[:, None, :], jnp.asarray(sin_a)[:, None, :], jnp.asarray(cos_b),
                jnp.asarray(sin_b), jnp.asarray(_rope_sign()), *mixer_w, p, *ffn_w]
    in_specs = [
        pl.BlockSpec((tm, D_MODEL), mixer_tile),
        pl.BlockSpec((1, 1, LANES), mixer_tile3),
        pl.BlockSpec((1, 1, LANES), mixer_tile3),
        _const_spec((tm, LANES)),
        _const_spec((tm, LANES)),
        _const_spec((1, LANES)),
    ] + _mixer_weight_specs() + [pl.BlockSpec((tm, PLE_DIM), ffn_tile)] + _ffn_weight_specs()
    n_lead = 1 + N_TABLES
    matmul_w = ([n_lead + n for n in MIXER_MATMUL_W] + [n_lead + len(mixer_w) + 1 + n for n in FFN_MATMUL_W])
    assert len(matmul_w) == N_MATMUL_W
    for n in matmul_w:
        in_specs[n] = hbm
    w_shapes = [operands[n].shape for n in matmul_w]
    outs = pl.pallas_call(
        _prompt_layer_kernel,
        grid=(tiles + 1,),
        in_specs=in_specs,
        out_specs=[
            pl.BlockSpec((tm, D_MODEL), ffn_tile),
            pl.BlockSpec((WINDOW, KV_COLS), fixed),
            pl.BlockSpec((WINDOW, KV_COLS), fixed),
            pl.BlockSpec((POOL_BUF, POOL_WIDTH), fixed),
        ] + [hbm] * N_MATMUL_W,
        out_shape=[
            jax.ShapeDtypeStruct((seq, D_MODEL), F32),
            jax.ShapeDtypeStruct((WINDOW, KV_COLS), F32),
            jax.ShapeDtypeStruct((WINDOW, KV_COLS), F32),
            jax.ShapeDtypeStruct((POOL_BUF, POOL_WIDTH), F32),
        ] + [jax.ShapeDtypeStruct(s, BF16) for s in w_shapes],
        scratch_shapes=[
            pltpu.VMEM((tm, Q_COLS), BF16),
            pltpu.VMEM((2 * N_KV_HEADS, WINDOW + tm, LANES), BF16),
            pltpu.VMEM((KV_COLS, WINDOW + tm), BF16),
            pltpu.VMEM((POOL_PAD + tm, POOL_WIDTH), F32),
            pltpu.VMEM((tm, POOL_WIDTH), F32),
            pltpu.VMEM((tm, D_MODEL), BF16),
            pltpu.VMEM((tm, D_FF), BF16),
            pltpu.VMEM((2, tm, D_MODEL), F32),
            pltpu.VMEM((2, tm, D_MODEL), BF16),
        ] + [pltpu.VMEM(s, BF16) for s in w_shapes] + [pltpu.SemaphoreType.DMA((N_MATMUL_W,))],
        compiler_params=_params(("arbitrary",)),
        name="prompt_layer",
    )(*operands)
    return outs[:4], outs[4:]


def _sample_mixer(x, state_k, state_v, state_pool, past_len, mixer_w):
    nseq_all, t = state_k.shape[0], x.shape[0] // state_k.shape[0]
    nseq = SAMPLE_SEQS
    rows = nseq * t
    steps = nseq_all // nseq
    cos, sin = _rope_lane_tables(past_len + np.arange(rows, dtype=np.float64) % t)
    sin = sin * _rope_sign()
    row = lambda i: (i, 0)
    seq3 = lambda i: (i, 0, 0)
    return pl.pallas_call(
        _sample_mixer_kernel,
        grid=(steps,),
        in_specs=[
            pl.BlockSpec((rows, D_MODEL), row),
            _const_spec((rows, LANES)),
            _const_spec((rows, LANES)),
            pl.BlockSpec((nseq, WINDOW, KV_COLS), seq3),
            pl.BlockSpec((nseq, WINDOW, KV_COLS), seq3),
            pl.BlockSpec((nseq, POOL_BUF, POOL_WIDTH), seq3),
        ] + _mixer_weight_specs(),
        out_specs=[
            pl.BlockSpec((rows, D_MODEL), row),
            pl.BlockSpec((nseq, WINDOW, KV_COLS), seq3),
            pl.BlockSpec((nseq, WINDOW, KV_COLS), seq3),
            pl.BlockSpec((nseq, POOL_BUF, POOL_WIDTH), seq3),
        ],
        out_shape=[
            jax.ShapeDtypeStruct((nseq_all * t, D_MODEL), F32),
            jax.ShapeDtypeStruct((nseq_all, WINDOW, KV_COLS), F32),
            jax.ShapeDtypeStruct((nseq_all, WINDOW, KV_COLS), F32),
            jax.ShapeDtypeStruct((nseq_all, POOL_BUF, POOL_WIDTH), F32),
        ],
        scratch_shapes=[pltpu.VMEM((nseq, POOL_PAD + t, POOL_WIDTH), F32)],
        compiler_params=_params(("arbitrary",)),
        name="sample_mixer",
    )(x, jnp.asarray(cos), jnp.asarray(sin), state_k, state_v, state_pool, *mixer_w)


def _ffn(x, p, ffn_w, tm):
    n = x.shape[0]
    row = lambda i: (i, 0)
    return pl.pallas_call(
        _ffn_kernel,
        grid=(n // tm,),
        in_specs=[
            pl.BlockSpec((tm, D_MODEL), row),
            pl.BlockSpec((tm, PLE_DIM), row),
        ] + _ffn_weight_specs(),
        out_specs=pl.BlockSpec((tm, D_MODEL), row),
        out_shape=jax.ShapeDtypeStruct((n, D_MODEL), F32),
        compiler_params=_params(("arbitrary",)),
        name="ffn",
    )(x, p, *ffn_w)


def kernel(x_prompt, x_sample, state_k, state_v, state_pool, p_prompt, p_sample, w_in, attn_sinks, pool_w,
           pool_scale, g_attn_out, g_pool_out, w_out, g_pre_mix, g_post_mix, g_pre_mlp, g_post_mlp, w_up, w_down,
           w_ple, w_ple_gate, b_ple_gate):
    depth = w_in.shape[0]
    batch, seq, _ = x_prompt.shape
    dec_batch, dec_seq, _ = x_sample.shape
    assert depth == 1 and batch == 1

    xp = x_prompt.reshape(seq, D_MODEL)
    xs = x_sample.reshape(dec_batch * dec_seq, D_MODEL)
    i = 0
    mixer_w = [
        g_pre_mix[i][None, :], w_in[i], attn_sinks[i], pool_w[i], pool_scale[i][None, :], g_attn_out[i][None, :],
        g_pool_out[i][None, :], w_out[i], g_post_mix[i][None, :],
    ]
    ffn_w = [
        g_pre_mlp[i][None, :], w_up[i], w_down[i], g_post_mlp[i][None, :], w_ple_gate[i], b_ple_gate[i][None, :],
        w_ple[i],
    ]

    (yp, kp, vp, up), w_bf16 = _prompt_layer(xp, p_prompt[i, 0], mixer_w, ffn_w)
    for n, w in zip(MIXER_MATMUL_W, w_bf16[:len(MIXER_MATMUL_W)]):
        mixer_w[n] = w
    for n, w in zip(FFN_MATMUL_W, w_bf16[len(MIXER_MATMUL_W):]):
        ffn_w[n] = w

    xs_mid, ks, vs, us = _sample_mixer(
        xs, state_k[i].reshape(dec_batch, WINDOW, KV_COLS), state_v[i].reshape(dec_batch, WINDOW, KV_COLS),
        state_pool[i], float(PAST_LEN), mixer_w)
    ys = _ffn(xs_mid, p_sample[i].reshape(dec_batch * dec_seq, PLE_DIM), ffn_w, PROMPT_TILE)

    return (
        yp.reshape(batch, seq, D_MODEL),
        ys.reshape(dec_batch, dec_seq, D_MODEL),
        kp.reshape(depth, batch, WINDOW, N_KV_HEADS, HEAD_DIM),
        vp.reshape(depth, batch, WINDOW, N_KV_HEADS, HEAD_DIM),
        up.reshape(depth, batch, POOL_BUF, POOL_WIDTH),
        ks.reshape(depth, dec_batch, WINDOW, N_KV_HEADS, HEAD_DIM),
        vs.reshape(depth, dec_batch, WINDOW, N_KV_HEADS, HEAD_DIM),
        us.reshape(depth, dec_batch, POOL_BUF, POOL_WIDTH),
    )
```

```python
import jax
import jax.numpy as jnp
import numpy as np
from jax import lax
from jax.experimental import pallas as pl
from jax.experimental.pallas import tpu as pltpu

D_MODEL = 1024
HEAD_DIM = 64
N_HEADS = 8
N_KV_HEADS = 2
GROUP = N_HEADS // N_KV_HEADS
WINDOW = 128
Q_COLS = N_HEADS * HEAD_DIM
KV_COLS = N_KV_HEADS * HEAD_DIM
POOL_WIDTH = 512
POOL_WINDOWS = (2, 4, 8, 16)
POOL_CG = POOL_WIDTH // len(POOL_WINDOWS)
POOL_BUF = max(POOL_WINDOWS) - 1
POOL_PAD = POOL_BUF + 1
IN_COLS = Q_COLS + 2 * KV_COLS + POOL_WIDTH
D_FF = 4 * D_MODEL
FF_CHUNK = 1024
PLE_DIM = 256
ROPE_THETA = 10000.0
PAST_LEN = 16384
EPS = 1e-6
NEG = -1e30
LANES = 128
MXU_N = 256

ATTN_LOOKAHEAD = 2
PROMPT_PATTERN = "11" + "21" * (5 + ATTN_LOOKAHEAD + 16 + 4) + "111" + "22222" + "11111"

PROMPT_TILE = 512
SAMPLE_SEQS = 16
VMEM_LIMIT = 56 * 1024 * 1024

BF16 = jnp.bfloat16
F32 = jnp.float32


def _rms(x, g):
    y = x * lax.rsqrt(jnp.mean(x * x, axis=-1, keepdims=True) + EPS)
    return y * g


def _mm(a, w):
    return jnp.dot(a, w, preferred_element_type=F32)


def _rope_tables(cos_a, sin_a, cos_b, sin_b, sign):
    cos_t = cos_a * cos_b - sin_a * sin_b
    sin_t = (sin_a * cos_b + cos_a * sin_b) * sign
    return cos_t, sin_t


def _rope(x, cos_t, sin_t, first_half):
    n = x.shape[1] // LANES
    width = x.shape[1]
    partner = jnp.where(first_half, pltpu.roll(x, width - HEAD_DIM // 2, axis=1), pltpu.roll(x, HEAD_DIM // 2, axis=1))
    if n > 1:
        cos_t = jnp.concatenate([cos_t] * n, axis=1)
        sin_t = jnp.concatenate([sin_t] * n, axis=1)
    return x * cos_t + partner * sin_t


def _first_half_mask(rows, width):
    lane = lax.broadcasted_iota(jnp.int32, (rows, width), 1)
    return (lane % HEAD_DIM) < (HEAD_DIM // 2)


def _project(x, g_pre, w_in, tables):
    rows = x.shape[0]
    h = _rms(x, g_pre).astype(BF16)
    proj = _mm(h, w_in)
    cos_t, sin_t = tables
    q = _rope(proj[:, :Q_COLS], cos_t, sin_t, _first_half_mask(rows, Q_COLS)) * (HEAD_DIM ** -0.5)
    k = _rope(proj[:, Q_COLS:Q_COLS + KV_COLS], cos_t, sin_t, _first_half_mask(rows, KV_COLS))
    v = proj[:, Q_COLS + KV_COLS:Q_COLS + 2 * KV_COLS]
    u = proj[:, Q_COLS + 2 * KV_COLS:]
    return q, k, v, u


def _window_sums(ext):
    outs = []
    for g, w in enumerate(POOL_WINDOWS):
        a = ext[:, g * POOL_CG:(g + 1) * POOL_CG]
        step = 1
        while step < w:
            a = a + pltpu.roll(a, step, axis=0)
            step *= 2
        outs.append(a)
    return outs


def _pool_out(win, tok, inv_cnt, pool_w_ref, pool_scale):
    outs = []
    for g in range(len(POOL_WINDOWS)):
        r = win[g] * inv_cnt[g] - tok[:, g * POOL_CG:(g + 1) * POOL_CG]
        outs.append(_mm(r.astype(BF16), pool_w_ref[g]))
    return jnp.concatenate(outs, axis=1) * pool_scale


def _mix_out(x, attn, pooled, g_attn, g_pool, w_out, g_post):
    cat = jnp.concatenate([_rms(attn, g_attn), _rms(pooled, g_pool)], axis=1).astype(BF16)
    return x + _rms(_mm(cat, w_out), g_post)


def _softmax_parts(s_list, sink):
    m = sink
    for s in s_list:
        m = jnp.maximum(m, jnp.max(s, axis=-1, keepdims=True))
    e_list = [jnp.exp(s - m) for s in s_list]
    den = jnp.exp(sink - m)
    for e in e_list:
        den = den + jnp.sum(e, axis=-1, keepdims=True)
    return e_list, den


def _prompt_mixer_steps(tile, x_ref, cos_a_ref, sin_a_ref, cos_b_ref, sin_b_ref, sign_ref, g_pre_ref, w_in_ref,
                        sinks_ref, pool_w_ref, pool_scale_ref, g_attn_ref, g_pool_ref, w_out_ref, g_post_ref,
                        knew_ref, vnew_ref, unew_ref, qbuf, kbuf, vtbuf, ubuf, pbuf, catbuf,
                        g_pre_mlp_ref, xmid_ref, hmid_ref):
    tm = x_ref.shape[0]
    nblk = tm // WINDOW
    half_lanes = LANES // 2
    base = tile * tm

    h = _rms(x_ref[...], g_pre_ref[...]).astype(BF16)
    cos_t, sin_t = _rope_tables(cos_a_ref[0], sin_a_ref[0], cos_b_ref[...], sin_b_ref[...], sign_ref[...])

    kv = _mm(h, w_in_ref[:, Q_COLS:Q_COLS + 2 * KV_COLS])
    k = _rope(kv[:, :KV_COLS], cos_t, sin_t, _first_half_mask(tm, KV_COLS))
    v = kv[:, KV_COLS:]
    knew_ref[...] = k[tm - WINDOW:, :]
    vnew_ref[...] = v[tm - WINDOW:, :]
    low = lax.broadcasted_iota(jnp.int32, (tm, LANES), 1) < half_lanes
    k_swapped = pltpu.roll(k, half_lanes, axis=1)
    kbuf[0, WINDOW:, :] = jnp.where(low, k, 0.0).astype(BF16)
    kbuf[1, WINDOW:, :] = jnp.where(low, 0.0, k_swapped).astype(BF16)
    kbuf[2, WINDOW:, :] = jnp.where(low, k_swapped, 0.0).astype(BF16)
    kbuf[3, WINDOW:, :] = jnp.where(low, 0.0, k).astype(BF16)
    vtbuf[:, WINDOW:] = v.T.astype(BF16)
    yield

    u_col0 = Q_COLS + 2 * KV_COLS
    for c in range(Q_COLS // MXU_N):
        cols = slice(c * MXU_N, (c + 1) * MXU_N)
        q = _rope(_mm(h, w_in_ref[:, cols]), cos_t, sin_t, _first_half_mask(tm, MXU_N)) * (HEAD_DIM ** -0.5)
        qbuf[:, cols] = q.astype(BF16)
        yield
    for c in range(POOL_WIDTH // MXU_N):
        cols = slice(c * MXU_N, (c + 1) * MXU_N)
        ubuf[POOL_PAD:, cols] = _mm(h, w_in_ref[:, u_col0 + c * MXU_N:u_col0 + (c + 1) * MXU_N])
        yield

    pos = base + lax.broadcasted_iota(jnp.int32, (tm, 1), 0)
    pool_scale = pool_scale_ref[...]
    for g, w in enumerate(POOL_WINDOWS):
        cols = slice(g * POOL_CG, (g + 1) * POOL_CG)
        a = ubuf[:, cols]
        shift = 1
        while shift < w:
            a = a + pltpu.roll(a, shift, axis=0)
            shift *= 2
        inv_cnt = 1.0 / jnp.minimum(pos + 1, w).astype(F32)
        r = a[POOL_PAD:, :] * inv_cnt - ubuf[POOL_PAD:, cols]
        pbuf[:, cols] = _mm(r.astype(BF16), pool_w_ref[g]) * pool_scale[:, cols]
        yield
    catbuf[:, Q_COLS:] = _rms(pbuf[...], g_pool_ref[...]).astype(BF16)

    c_i = lax.broadcasted_iota(jnp.int32, (2 * WINDOW, 2 * WINDOW), 0)
    r_i = lax.broadcasted_iota(jnp.int32, (2 * WINDOW, 2 * WINDOW), 1) % WINDOW
    d_i = c_i - r_i
    band = (d_i >= 1) & (d_i <= WINDOW)
    band0 = band & (c_i >= WINDOW - base)

    def scores(j, h_kv, par):
        rows = slice(j * WINDOW, (j + 1) * WINDOW)
        keys = slice(j * WINDOW, (j + 2) * WINDOW)
        q2 = jnp.concatenate([qbuf[rows, (2 * h_kv) * LANES:(2 * h_kv + 1) * LANES],
                              qbuf[rows, (2 * h_kv + 1) * LANES:(2 * h_kv + 2) * LANES]], axis=0)
        return lax.dot_general(kbuf[2 * h_kv + par, keys, :], q2, (((1,), (1,)), ((), ())),
                               preferred_element_type=F32)

    units = [(j, h_kv, par) for j in range(nblk) for h_kv in range(N_KV_HEADS) for par in range(2)]
    pending = []
    for n in range(min(ATTN_LOOKAHEAD, len(units))):
        pending.append(scores(*units[n]))
        yield
    heads = [None] * N_HEADS
    for n, (j, h_kv, par) in enumerate(units):
        st = pending.pop(0)
        if n + ATTN_LOOKAHEAD < len(units):
            pending.append(scores(*units[n + ATTN_LOOKAHEAD]))
        mask = band0 if j == 0 else band
        st = jnp.where(mask, st, NEG)
        sink = jnp.concatenate([jnp.full((1, WINDOW), sinks_ref[GROUP * h_kv + par], F32),
                                jnp.full((1, WINDOW), sinks_ref[GROUP * h_kv + 2 + par], F32)], axis=1)
        m = jnp.maximum(jnp.max(st, axis=0, keepdims=True), sink)
        e = jnp.exp(st - m)
        den = jnp.sum(e, axis=0, keepdims=True) + jnp.exp(sink - m)
        vt = vtbuf[h_kv * HEAD_DIM:(h_kv + 1) * HEAD_DIM, j * WINDOW:(j + 2) * WINDOW]
        ot = _mm(vt, e.astype(BF16)) / den
        heads[GROUP * h_kv + par] = ot[:, :WINDOW]
        heads[GROUP * h_kv + 2 + par] = ot[:, WINDOW:]
        if h_kv == N_KV_HEADS - 1 and par == 1:
            attn = jnp.concatenate(heads, axis=0).T
            catbuf[j * WINDOW:(j + 1) * WINDOW, :Q_COLS] = _rms(attn, g_attn_ref[...]).astype(BF16)
        yield

    mix = []
    for c in range(D_MODEL // MXU_N):
        mix.append(_mm(catbuf[...], w_out_ref[:, c * MXU_N:(c + 1) * MXU_N]))
        yield
    x_mid = x_ref[...] + _rms(jnp.concatenate(mix, axis=1), g_post_ref[...])

    unew_ref[...] = ubuf[tm + 1:tm + POOL_PAD, :]
    kbuf[:, 0:WINDOW, :] = kbuf[:, tm:tm + WINDOW, :]
    vtbuf[:, 0:WINDOW] = vtbuf[:, tm:tm + WINDOW]
    ubuf[0:POOL_PAD, :] = ubuf[tm:tm + POOL_PAD, :]
    xmid_ref[...] = x_mid
    hmid_ref[...] = _rms(x_mid, g_pre_mlp_ref[...]).astype(BF16)


def _sample_mixer_kernel(x_ref, cos_ref, sin_ref, sk_ref, sv_ref, sp_ref, g_pre_ref, w_in_hbm,
                         sinks_ref, pool_w_hbm, pool_scale_ref, g_attn_ref, g_pool_ref, w_out_hbm, g_post_ref,
                         g_pre_mlp_ref, w_up_hbm, w_down_hbm, w_gate_hbm, w_ple_hbm,
                         xmid_ref, hmid_ref, knew_ref, vnew_ref, unew_ref,
                         w_in_out, pool_w_out, w_out_out, w_up_out, w_down_out, w_gate_out, w_ple_out,
                         ubuf, w_in_ref, pool_w_ref, w_out_ref, w_up_buf, w_down_buf, w_gate_buf, w_ple_buf, out_sems):
    step = pl.program_id(0)
    w_bufs = [w_in_ref, pool_w_ref, w_out_ref, w_up_buf, w_down_buf, w_gate_buf, w_ple_buf]
    w_outs = [w_in_out, pool_w_out, w_out_out, w_up_out, w_down_out, w_gate_out, w_ple_out]

    @pl.when(step == 0)
    def _():
        _cast_weights_to_vmem([w_in_hbm, pool_w_hbm, w_out_hbm, w_up_hbm, w_down_hbm, w_gate_hbm, w_ple_hbm], w_bufs)
        for copy in _weight_writebacks(w_bufs, w_outs, out_sems):
            copy.start()

    rows = x_ref.shape[0]
    nseq = sk_ref.shape[0]
    t = rows // nseq
    x = x_ref[...]
    q, k, v, u = _project(x, g_pre_ref[...], w_in_ref[...], (cos_ref[...], sin_ref[...]))

    k3 = k.reshape(nseq, t, KV_COLS)
    v3 = v.reshape(nseq, t, KV_COLS)
    knew_ref[:, 0:WINDOW - t, :] = sk_ref[:, t:, :]
    knew_ref[:, WINDOW - t:, :] = k3
    vnew_ref[:, 0:WINDOW - t, :] = sv_ref[:, t:, :]
    vnew_ref[:, WINDOW - t:, :] = v3

    r_i = lax.broadcasted_iota(jnp.int32, (GROUP * t, WINDOW), 0) % t
    c_i = lax.broadcasted_iota(jnp.int32, (GROUP * t, WINDOW), 1)
    mask_old = (c_i > r_i)[None]
    r_n = lax.broadcasted_iota(jnp.int32, (GROUP * t, t), 0) % t
    c_n = lax.broadcasted_iota(jnp.int32, (GROUP * t, t), 1)
    mask_new = (c_n <= r_n)[None]

    qb = q.astype(BF16)
    kb = k3.astype(BF16)
    vb = v3.astype(BF16)
    heads = [None] * N_HEADS
    for h in range(N_KV_HEADS):
        lanes = slice(h * HEAD_DIM, (h + 1) * HEAD_DIM)
        q4 = jnp.concatenate(
            [qb[:, (GROUP * h + g) * HEAD_DIM:(GROUP * h + g + 1) * HEAD_DIM].reshape(nseq, t, HEAD_DIM)
             for g in range(GROUP)], axis=1)
        k_old = sk_ref[:, :, lanes].astype(BF16)
        v_old = sv_ref[:, :, lanes].astype(BF16)
        s_old = jnp.einsum('bqd,bkd->bqk', q4, k_old, preferred_element_type=F32)
        s_new = jnp.einsum('bqd,bkd->bqk', q4, kb[:, :, lanes], preferred_element_type=F32)
        s_old = jnp.where(mask_old, s_old, NEG)
        s_new = jnp.where(mask_new, s_new, NEG)
        sink = jnp.concatenate(
            [jnp.full((1, t, 1), sinks_ref[GROUP * h + g], F32) for g in range(GROUP)], axis=1)
        (e_old, e_new), den = _softmax_parts([s_old, s_new], sink)
        o = jnp.einsum('bqk,bkd->bqd', e_old.astype(BF16), v_old, preferred_element_type=F32)
        o = o + jnp.einsum('bqk,bkd->bqd', e_new.astype(BF16), vb[:, :, lanes], preferred_element_type=F32)
        o = o / den
        for g in range(GROUP):
            heads[GROUP * h + g] = o[:, g * t:(g + 1) * t, :].reshape(rows, HEAD_DIM)
    attn = jnp.concatenate(heads, axis=1)

    ext = POOL_PAD + t
    ubuf[:, 0:1, :] = jnp.zeros((nseq, 1, POOL_WIDTH), F32)
    ubuf[:, 1:POOL_PAD, :] = sp_ref[...]
    ubuf[:, POOL_PAD:, :] = u.reshape(nseq, t, POOL_WIDTH)
    unew_ref[...] = ubuf[:, t + 1:ext, :]
    win = [a.reshape(nseq, ext, POOL_CG)[:, POOL_PAD:, :].reshape(rows, POOL_CG)
           for a in _window_sums(ubuf[...].reshape(nseq * ext, POOL_WIDTH))]
    inv_cnt = [1.0 / w for w in POOL_WINDOWS]
    pooled = _pool_out(win, u, inv_cnt, pool_w_ref, pool_scale_ref[...])

    x_mid = _mix_out(x, attn, pooled, g_attn_ref[...], g_pool_ref[...], w_out_ref[...], g_post_ref[...])
    xmid_ref[...] = x_mid
    hmid_ref[...] = _rms(x_mid, g_pre_mlp_ref[...]).astype(BF16)

    @pl.when(step == pl.num_programs(0) - 1)
    def _():
        for copy in _weight_writebacks(w_bufs, w_outs, out_sems):
            copy.wait()


def _ffn_steps(x_ref, h_ref, load_p, w_up_ref, w_down_ref, g_post_ref, w_gate_ref, b_gate_ref, w_ple_ref,
               y_ref, actbuf):
    for c in range(D_FF // MXU_N):
        cols = slice(c * MXU_N, (c + 1) * MXU_N)
        up = _mm(h_ref[...], w_up_ref[:, cols])
        actbuf[:, cols] = jnp.square(jnp.maximum(up, 0.0)).astype(BF16)
        yield
    ff = []
    for c in range(D_MODEL // MXU_N):
        cols = slice(c * MXU_N, (c + 1) * MXU_N)
        acc = None
        for r in range(D_FF // FF_CHUNK):
            rows = slice(r * FF_CHUNK, (r + 1) * FF_CHUNK)
            part = _mm(actbuf[:, rows], w_down_ref[rows, cols])
            acc = part if acc is None else acc + part
            yield
        ff.append(acc)
    x = x_ref[...] + _rms(jnp.concatenate(ff, axis=1), g_post_ref[...])
    xb = x.astype(BF16)
    ple = _mm(load_p().astype(BF16), w_ple_ref[...])
    yield
    for c in range(D_MODEL // MXU_N):
        cols = slice(c * MXU_N, (c + 1) * MXU_N)
        gate = jax.nn.sigmoid(_mm(xb, w_gate_ref[:, cols]) + b_gate_ref[:, cols])
        y_ref[:, cols] = x[:, cols] + gate * ple[:, cols]
        yield


def _interleave(first, second, pattern):
    gens = {'1': first, '2': second}
    for tag in pattern:
        next(gens[tag], None)
    for _ in first:
        pass
    for _ in second:
        pass


def _cast_weights_to_vmem(srcs, dsts):
    tasks = []
    for src, dst in zip(srcs, dsts):
        if len(src.shape) != 2:
            tasks.append((src, dst, tuple(src.shape)))
            continue
        rows, cols = src.shape
        col_block = cols if cols <= STAGE_MAX_COLS else D_MODEL
        for c0 in range(0, cols, col_block):
            for r0 in range(0, rows, STAGE_ROWS):
                view = (slice(r0, r0 + STAGE_ROWS), slice(c0, c0 + col_block))
                tasks.append((src.at[view], dst.at[view], (STAGE_ROWS, col_block)))
    shapes = sorted(set(shape for _, _, shape in tasks))

    def body(*scoped):
        *stages, sems = scoped
        uses = [0] * len(shapes)
        copies = []
        for src, dst, shape in tasks:
            k = shapes.index(shape)
            slot = uses[k] % 2
            uses[k] += 1
            stage = stages[k].at[slot]
            copies.append((pltpu.make_async_copy(src, stage, sems.at[k, slot]), stage, dst))
        copies[0][0].start()
        for t, (copy, stage, dst) in enumerate(copies):
            if t + 1 < len(copies):
                copies[t + 1][0].start()
            copy.wait()
            dst[...] = stage[...].astype(BF16)

    pl.run_scoped(body, *[pltpu.VMEM((2,) + shape, F32) for shape in shapes],
                  pltpu.SemaphoreType.DMA((len(shapes), 2)))


def _weight_writebacks(bufs, outs, sems):
    return [pltpu.make_async_copy(buf, out, sems.at[n]) for n, (buf, out) in enumerate(zip(bufs, outs))]


N_TABLES = 5
MIXER_MATMUL_W = (1, 3, 7)
FFN_MATMUL_W = (1, 2, 4, 6)
N_MATMUL_W = len(MIXER_MATMUL_W) + len(FFN_MATMUL_W)
STAGE_ROWS = 256
STAGE_MAX_COLS = 1280


N_SAMPLE_TILES = 2


def _prompt_layer_kernel(*refs):
    x_ref, *tables = refs[:1 + N_TABLES]
    (g_pre_mix, w_in, sinks, pool_w, pool_scale, g_attn, g_pool, w_out, g_post_mix, p_ref,
     g_pre_mlp, w_up, w_down, g_post_mlp, w_gate, b_gate, w_ple,
     xs_hbm, hs_hbm, ps_ref) = refs[1 + N_TABLES:21 + N_TABLES]
    y_ref, ys_hbm, knew_ref, vnew_ref, unew_ref = refs[21 + N_TABLES:26 + N_TABLES]
    qbuf, kbuf, vtbuf, ubuf, pbuf, catbuf, actbuf, xmid, hmid, sems = refs[26 + N_TABLES:]
    tm = x_ref.shape[0]
    step = pl.program_id(0)
    last_tile = pl.num_programs(0) - 1 - N_SAMPLE_TILES
    on_sample = step < N_SAMPLE_TILES
    slot = step % 2
    sample_rows = pl.ds(pl.multiple_of(jnp.minimum(step, N_SAMPLE_TILES - 1) * tm, tm), tm)

    @pl.when(on_sample)
    def _():
        loads = [pltpu.make_async_copy(xs_hbm.at[sample_rows], xmid.at[slot], sems.at[0]),
                 pltpu.make_async_copy(hs_hbm.at[sample_rows], hmid.at[slot], sems.at[1])]
        for copy in loads:
            copy.start()
        kbuf[:, 0:WINDOW, :] = jnp.zeros((2 * N_KV_HEADS, WINDOW, LANES), BF16)
        vtbuf[:, 0:WINDOW] = jnp.zeros((KV_COLS, WINDOW), BF16)
        ubuf[0:POOL_PAD, :] = jnp.zeros((POOL_PAD, POOL_WIDTH), F32)
        for copy in loads:
            copy.wait()

    ffn = _ffn_steps(xmid.at[slot], hmid.at[slot], lambda: jnp.where(on_sample, ps_ref[...], p_ref[...]),
                     w_up, w_down, g_post_mlp, w_gate, b_gate, w_ple, y_ref, actbuf)
    mixer = _prompt_mixer_steps(jnp.clip(step - 1, 0, last_tile), x_ref, *tables, g_pre_mix, w_in, sinks, pool_w,
                                pool_scale, g_attn, g_pool, w_out, g_post_mix, knew_ref, vnew_ref, unew_ref,
                                qbuf, kbuf, vtbuf, ubuf, pbuf, catbuf, g_pre_mlp, xmid.at[1 - slot],
                                hmid.at[1 - slot])
    _interleave(ffn, mixer, PROMPT_PATTERN)

    @pl.when(on_sample)
    def _():
        store = pltpu.make_async_copy(y_ref, ys_hbm.at[sample_rows], sems.at[2])
        store.start()
        store.wait()


def _const_spec(shape):
    zeros = (0,) * len(shape)
    return pl.BlockSpec(shape, lambda i: zeros, pipeline_mode=pl.Buffered(1))


def _rope_lane_tables(pos):
    half = HEAD_DIM // 2
    inv = ROPE_THETA ** (-np.arange(half, dtype=np.float64) / half)
    ang = pos[:, None] * inv[None, :]
    reps = LANES // half
    return (np.tile(np.cos(ang), (1, reps)).astype(np.float32), np.tile(np.sin(ang), (1, reps)).astype(np.float32))


def _rope_sign():
    lane = np.arange(LANES)
    return np.where((lane % HEAD_DIM) < HEAD_DIM // 2, -1.0, 1.0).astype(np.float32)[None, :]


def _params(dims):
    return pltpu.CompilerParams(dimension_semantics=dims, vmem_limit_bytes=VMEM_LIMIT)


def _mixer_weight_specs():
    return [
        _const_spec((1, D_MODEL)),
        _const_spec((D_MODEL, IN_COLS)),
        pl.BlockSpec(memory_space=pltpu.SMEM),
        _const_spec((len(POOL_WINDOWS), POOL_CG, POOL_CG)),
        _const_spec((1, POOL_WIDTH)),
        _const_spec((1, Q_COLS)),
        _const_spec((1, POOL_WIDTH)),
        _const_spec((D_MODEL, D_MODEL)),
        _const_spec((1, D_MODEL)),
    ]


def _ffn_weight_specs():
    return [
        _const_spec((1, D_MODEL)),
        _const_spec((D_MODEL, D_FF)),
        _const_spec((D_FF, D_MODEL)),
        _const_spec((1, D_MODEL)),
        _const_spec((D_MODEL, D_MODEL)),
        _const_spec((1, D_MODEL)),
        _const_spec((PLE_DIM, D_MODEL)),
    ]


def _prompt_layer(x, p, xs_mid, hs_mid, ps, mixer_w, ffn_w):
    seq = x.shape[0]
    tm = PROMPT_TILE
    tiles = seq // tm
    assert xs_mid.shape[0] == N_SAMPLE_TILES * tm
    cos_a, sin_a = _rope_lane_tables(np.arange(tiles, dtype=np.float64) * tm)
    cos_b, sin_b = _rope_lane_tables(np.arange(tm, dtype=np.float64))
    mixer_tile = lambda i: (jnp.clip(i - 1, 0, tiles - 1), 0)
    mixer_tile3 = lambda i: (jnp.clip(i - 1, 0, tiles - 1), 0, 0)
    ffn_tile = lambda i: (jnp.clip(i - N_SAMPLE_TILES, 0, tiles - 1), 0)
    sample_tile = lambda i: (jnp.minimum(i, N_SAMPLE_TILES - 1), 0)
    fixed = lambda i: (0, 0)
    in_specs = [
        pl.BlockSpec((tm, D_MODEL), mixer_tile),
        pl.BlockSpec((1, 1, LANES), mixer_tile3),
        pl.BlockSpec((1, 1, LANES), mixer_tile3),
        _const_spec((tm, LANES)),
        _const_spec((tm, LANES)),
        _const_spec((1, LANES)),
    ] + _mixer_weight_specs() + [pl.BlockSpec((tm, PLE_DIM), ffn_tile)] + _ffn_weight_specs() + [
        pl.BlockSpec(memory_space=pl.ANY),
        pl.BlockSpec(memory_space=pl.ANY),
        pl.BlockSpec((tm, PLE_DIM), sample_tile, pipeline_mode=pl.Buffered(1)),
    ]
    return pl.pallas_call(
        _prompt_layer_kernel,
        grid=(tiles + N_SAMPLE_TILES,),
        in_specs=in_specs,
        out_specs=[
            pl.BlockSpec((tm, D_MODEL), ffn_tile),
            pl.BlockSpec(memory_space=pl.ANY),
            pl.BlockSpec((WINDOW, KV_COLS), fixed),
            pl.BlockSpec((WINDOW, KV_COLS), fixed),
            pl.BlockSpec((POOL_BUF, POOL_WIDTH), fixed),
        ],
        out_shape=[
            jax.ShapeDtypeStruct((seq, D_MODEL), F32),
            jax.ShapeDtypeStruct(xs_mid.shape, F32),
            jax.ShapeDtypeStruct((WINDOW, KV_COLS), F32),
            jax.ShapeDtypeStruct((WINDOW, KV_COLS), F32),
            jax.ShapeDtypeStruct((POOL_BUF, POOL_WIDTH), F32),
        ],
        scratch_shapes=[
            pltpu.VMEM((tm, Q_COLS), BF16),
            pltpu.VMEM((2 * N_KV_HEADS, WINDOW + tm, LANES), BF16),
            pltpu.VMEM((KV_COLS, WINDOW + tm), BF16),
            pltpu.VMEM((POOL_PAD + tm, POOL_WIDTH), F32),
            pltpu.VMEM((tm, POOL_WIDTH), F32),
            pltpu.VMEM((tm, D_MODEL), BF16),
            pltpu.VMEM((tm, D_FF), BF16),
            pltpu.VMEM((2, tm, D_MODEL), F32),
            pltpu.VMEM((2, tm, D_MODEL), BF16),
            pltpu.SemaphoreType.DMA((3,)),
        ],
        compiler_params=_params(("arbitrary",)),
        name="prompt_layer",
    )(x, jnp.asarray(cos_a)[:, None, :], jnp.asarray(sin_a)[:, None, :], jnp.asarray(cos_b), jnp.asarray(sin_b),
      jnp.asarray(_rope_sign()), *mixer_w, p, *ffn_w, xs_mid, hs_mid, ps)


def _sample_mixer(x, state_k, state_v, state_pool, past_len, mixer_w, ffn_w):
    nseq_all, t = state_k.shape[0], x.shape[0] // state_k.shape[0]
    nseq = SAMPLE_SEQS
    rows = nseq * t
    steps = nseq_all // nseq
    cos, sin = _rope_lane_tables(past_len + np.arange(rows, dtype=np.float64) % t)
    sin = sin * _rope_sign()
    row = lambda i: (i, 0)
    seq3 = lambda i: (i, 0, 0)
    hbm = pl.BlockSpec(memory_space=pl.ANY)
    mixer_specs = _mixer_weight_specs()
    for n in MIXER_MATMUL_W:
        mixer_specs[n] = hbm
    g_pre_mlp = ffn_w[0]
    ffn_matmul_w = [ffn_w[n] for n in FFN_MATMUL_W]
    matmul_w = [mixer_w[n] for n in MIXER_MATMUL_W] + ffn_matmul_w
    outs = pl.pallas_call(
        _sample_mixer_kernel,
        grid=(steps,),
        in_specs=[
            pl.BlockSpec((rows, D_MODEL), row),
            _const_spec((rows, LANES)),
            _const_spec((rows, LANES)),
            pl.BlockSpec((nseq, WINDOW, KV_COLS), seq3),
            pl.BlockSpec((nseq, WINDOW, KV_COLS), seq3),
            pl.BlockSpec((nseq, POOL_BUF, POOL_WIDTH), seq3),
        ] + mixer_specs + [_const_spec((1, D_MODEL))] + [hbm] * len(ffn_matmul_w),
        out_specs=[
            pl.BlockSpec((rows, D_MODEL), row),
            pl.BlockSpec((rows, D_MODEL), row),
            pl.BlockSpec((nseq, WINDOW, KV_COLS), seq3),
            pl.BlockSpec((nseq, WINDOW, KV_COLS), seq3),
            pl.BlockSpec((nseq, POOL_BUF, POOL_WIDTH), seq3),
        ] + [hbm] * N_MATMUL_W,
        out_shape=[
            jax.ShapeDtypeStruct((nseq_all * t, D_MODEL), F32),
            jax.ShapeDtypeStruct((nseq_all * t, D_MODEL), BF16),
            jax.ShapeDtypeStruct((nseq_all, WINDOW, KV_COLS), F32),
            jax.ShapeDtypeStruct((nseq_all, WINDOW, KV_COLS), F32),
            jax.ShapeDtypeStruct((nseq_all, POOL_BUF, POOL_WIDTH), F32),
        ] + [jax.ShapeDtypeStruct(w.shape, BF16) for w in matmul_w],
        scratch_shapes=[pltpu.VMEM((nseq, POOL_PAD + t, POOL_WIDTH), F32)]
        + [pltpu.VMEM(w.shape, BF16) for w in matmul_w] + [pltpu.SemaphoreType.DMA((N_MATMUL_W,))],
        compiler_params=_params(("arbitrary",)),
        name="sample_mixer",
    )(x, jnp.asarray(cos), jnp.asarray(sin), state_k, state_v, state_pool, *mixer_w, g_pre_mlp, *ffn_matmul_w)
    return outs[:5], outs[5:]


def kernel(x_prompt, x_sample, state_k, state_v, state_pool, p_prompt, p_sample, w_in, attn_sinks, pool_w,
           pool_scale, g_attn_out, g_pool_out, w_out, g_pre_mix, g_post_mix, g_pre_mlp, g_post_mlp, w_up, w_down,
           w_ple, w_ple_gate, b_ple_gate):
    depth = w_in.shape[0]
    batch, seq, _ = x_prompt.shape
    dec_batch, dec_seq, _ = x_sample.shape
    assert depth == 1 and batch == 1

    xp = x_prompt.reshape(seq, D_MODEL)
    xs = x_sample.reshape(dec_batch * dec_seq, D_MODEL)
    i = 0
    mixer_w = [
        g_pre_mix[i][None, :], w_in[i], attn_sinks[i], pool_w[i], pool_scale[i][None, :], g_attn_out[i][None, :],
        g_pool_out[i][None, :], w_out[i], g_post_mix[i][None, :],
    ]
    ffn_w = [
        g_pre_mlp[i][None, :], w_up[i], w_down[i], g_post_mlp[i][None, :], w_ple_gate[i], b_ple_gate[i][None, :],
        w_ple[i],
    ]

    (xs_mid, hs_mid, ks, vs, us), w_bf16 = _sample_mixer(
        xs, state_k[i].reshape(dec_batch, WINDOW, KV_COLS), state_v[i].reshape(dec_batch, WINDOW, KV_COLS),
        state_pool[i], float(PAST_LEN), mixer_w, ffn_w)
    for n, w in zip(MIXER_MATMUL_W, w_bf16[:len(MIXER_MATMUL_W)]):
        mixer_w[n] = w
    for n, w in zip(FFN_MATMUL_W, w_bf16[len(MIXER_MATMUL_W):]):
        ffn_w[n] = w
    yp, ys, kp, vp, up = _prompt_layer(xp, p_prompt[i, 0], xs_mid, hs_mid,
                                       p_sample[i].reshape(dec_batch * dec_seq, PLE_DIM), mixer_w, ffn_w)

    return (
        yp.reshape(batch, seq, D_MODEL),
        ys.reshape(dec_batch, dec_seq, D_MODEL),
        kp.reshape(depth, batch, WINDOW, N_KV_HEADS, HEAD_DIM),
        vp.reshape(depth, batch, WINDOW, N_KV_HEADS, HEAD_DIM),
        up.reshape(depth, batch, POOL_BUF, POOL_WIDTH),
        ks.reshape(depth, dec_batch, WINDOW, N_KV_HEADS, HEAD_DIM),
        vs.reshape(depth, dec_batch, WINDOW, N_KV_HEADS, HEAD_DIM),
        us.reshape(depth, dec_batch, POOL_BUF, POOL_WIDTH),
    )
```

```python
import jax
import jax.numpy as jnp
import numpy as np
from jax import lax
from jax.experimental import pallas as pl
from jax.experimental.pallas import tpu as pltpu

D_MODEL = 1024
HEAD_DIM = 64
N_HEADS = 8
N_KV_HEADS = 2
GROUP = N_HEADS // N_KV_HEADS
WINDOW = 128
Q_COLS = N_HEADS * HEAD_DIM
KV_COLS = N_KV_HEADS * HEAD_DIM
POOL_WIDTH = 512
POOL_WINDOWS = (2, 4, 8, 16)
POOL_CG = POOL_WIDTH // len(POOL_WINDOWS)
POOL_BUF = max(POOL_WINDOWS) - 1
POOL_PAD = POOL_BUF + 1
IN_COLS = Q_COLS + 2 * KV_COLS + POOL_WIDTH
D_FF = 4 * D_MODEL
FF_CHUNK = 1024
PLE_DIM = 256
ROPE_THETA = 10000.0
PAST_LEN = 16384
EPS = 1e-6
NEG = -1e30
LANES = 128
MXU_N = 256

ATTN_LOOKAHEAD = 2
PROMPT_PATTERN = "11" + "21" * (5 + ATTN_LOOKAHEAD + 16 + 4) + "111" + "22222" + "11111"

PROMPT_TILE = 512
SAMPLE_SEQS = 16
VMEM_LIMIT = 56 * 1024 * 1024

BF16 = jnp.bfloat16
F32 = jnp.float32


def _rms(x, g):
    y = x * lax.rsqrt(jnp.mean(x * x, axis=-1, keepdims=True) + EPS)
    return y * g


def _mm(a, w):
    return jnp.dot(a, w, preferred_element_type=F32)


def _rope_tables(cos_a, sin_a, cos_b, sin_b, sign):
    cos_t = cos_a * cos_b - sin_a * sin_b
    sin_t = (sin_a * cos_b + cos_a * sin_b) * sign
    return cos_t, sin_t


def _rope(x, cos_t, sin_t, first_half):
    n = x.shape[1] // LANES
    width = x.shape[1]
    partner = jnp.where(first_half, pltpu.roll(x, width - HEAD_DIM // 2, axis=1), pltpu.roll(x, HEAD_DIM // 2, axis=1))
    if n > 1:
        cos_t = jnp.concatenate([cos_t] * n, axis=1)
        sin_t = jnp.concatenate([sin_t] * n, axis=1)
    return x * cos_t + partner * sin_t


def _first_half_mask(rows, width):
    lane = lax.broadcasted_iota(jnp.int32, (rows, width), 1)
    return (lane % HEAD_DIM) < (HEAD_DIM // 2)


def _project(x, g_pre, w_in, tables):
    rows = x.shape[0]
    h = _rms(x, g_pre).astype(BF16)
    proj = _mm(h, w_in)
    cos_t, sin_t = tables
    q = _rope(proj[:, :Q_COLS], cos_t, sin_t, _first_half_mask(rows, Q_COLS)) * (HEAD_DIM ** -0.5)
    k = _rope(proj[:, Q_COLS:Q_COLS + KV_COLS], cos_t, sin_t, _first_half_mask(rows, KV_COLS))
    v = proj[:, Q_COLS + KV_COLS:Q_COLS + 2 * KV_COLS]
    u = proj[:, Q_COLS + 2 * KV_COLS:]
    return q, k, v, u


def _pool_out(win, tok, inv_cnt, pool_w_ref, pool_scale):
    outs = []
    for g in range(len(POOL_WINDOWS)):
        r = win[g] * inv_cnt[g] - tok[:, g * POOL_CG:(g + 1) * POOL_CG]
        outs.append(_mm(r.astype(BF16), pool_w_ref[g]))
    return jnp.concatenate(outs, axis=1) * pool_scale


def _mix_out(x, attn, pooled, g_attn, g_pool, w_out, g_post):
    cat = jnp.concatenate([_rms(attn, g_attn), _rms(pooled, g_pool)], axis=1).astype(BF16)
    return x + _rms(_mm(cat, w_out), g_post)


def _softmax_parts(s_list, sink):
    m = sink
    for s in s_list:
        m = jnp.maximum(m, jnp.max(s, axis=-1, keepdims=True))
    e_list = [jnp.exp(s - m) for s in s_list]
    den = jnp.exp(sink - m)
    for e in e_list:
        den = den + jnp.sum(e, axis=-1, keepdims=True)
    return e_list, den


def _prompt_mixer_steps(tile, x_ref, cos_a_ref, sin_a_ref, cos_b_ref, sin_b_ref, sign_ref, g_pre_ref, w_in_ref,
                        sinks_ref, pool_w_ref, pool_scale_ref, g_attn_ref, g_pool_ref, w_out_ref, g_post_ref,
                        knew_ref, vnew_ref, unew_ref, qbuf, kbuf, vtbuf, ubuf, pbuf, catbuf,
                        g_pre_mlp_ref, xmid_ref, hmid_ref):
    tm = x_ref.shape[0]
    nblk = tm // WINDOW
    half_lanes = LANES // 2
    base = tile * tm

    h = _rms(x_ref[...], g_pre_ref[...]).astype(BF16)
    cos_t, sin_t = _rope_tables(cos_a_ref[0], sin_a_ref[0], cos_b_ref[...], sin_b_ref[...], sign_ref[...])

    kv = _mm(h, w_in_ref[:, Q_COLS:Q_COLS + 2 * KV_COLS])
    k = _rope(kv[:, :KV_COLS], cos_t, sin_t, _first_half_mask(tm, KV_COLS))
    v = kv[:, KV_COLS:]
    knew_ref[...] = k[tm - WINDOW:, :].T
    vnew_ref[...] = v[tm - WINDOW:, :].T
    low = lax.broadcasted_iota(jnp.int32, (tm, LANES), 1) < half_lanes
    k_swapped = pltpu.roll(k, half_lanes, axis=1)
    kbuf[0, WINDOW:, :] = jnp.where(low, k, 0.0).astype(BF16)
    kbuf[1, WINDOW:, :] = jnp.where(low, 0.0, k_swapped).astype(BF16)
    kbuf[2, WINDOW:, :] = jnp.where(low, k_swapped, 0.0).astype(BF16)
    kbuf[3, WINDOW:, :] = jnp.where(low, 0.0, k).astype(BF16)
    vtbuf[:, WINDOW:] = v.T.astype(BF16)
    yield

    u_col0 = Q_COLS + 2 * KV_COLS
    for c in range(Q_COLS // MXU_N):
        cols = slice(c * MXU_N, (c + 1) * MXU_N)
        q = _rope(_mm(h, w_in_ref[:, cols]), cos_t, sin_t, _first_half_mask(tm, MXU_N)) * (HEAD_DIM ** -0.5)
        qbuf[:, cols] = q.astype(BF16)
        yield
    for c in range(POOL_WIDTH // MXU_N):
        cols = slice(c * MXU_N, (c + 1) * MXU_N)
        ubuf[POOL_PAD:, cols] = _mm(h, w_in_ref[:, u_col0 + c * MXU_N:u_col0 + (c + 1) * MXU_N])
        yield

    pos = base + lax.broadcasted_iota(jnp.int32, (tm, 1), 0)
    pool_scale = pool_scale_ref[...]
    for g, w in enumerate(POOL_WINDOWS):
        cols = slice(g * POOL_CG, (g + 1) * POOL_CG)
        a = ubuf[:, cols]
        shift = 1
        while shift < w:
            a = a + pltpu.roll(a, shift, axis=0)
            shift *= 2
        inv_cnt = 1.0 / jnp.minimum(pos + 1, w).astype(F32)
        r = a[POOL_PAD:, :] * inv_cnt - ubuf[POOL_PAD:, cols]
        pbuf[:, cols] = _mm(r.astype(BF16), pool_w_ref[g]) * pool_scale[:, cols]
        yield
    catbuf[:, Q_COLS:] = _rms(pbuf[...], g_pool_ref[...]).astype(BF16)

    c_i = lax.broadcasted_iota(jnp.int32, (2 * WINDOW, 2 * WINDOW), 0)
    r_i = lax.broadcasted_iota(jnp.int32, (2 * WINDOW, 2 * WINDOW), 1) % WINDOW
    d_i = c_i - r_i
    band = (d_i >= 1) & (d_i <= WINDOW)
    band0 = band & (c_i >= WINDOW - base)

    def scores(j, h_kv, par):
        rows = slice(j * WINDOW, (j + 1) * WINDOW)
        keys = slice(j * WINDOW, (j + 2) * WINDOW)
        q2 = jnp.concatenate([qbuf[rows, (2 * h_kv) * LANES:(2 * h_kv + 1) * LANES],
                              qbuf[rows, (2 * h_kv + 1) * LANES:(2 * h_kv + 2) * LANES]], axis=0)
        return lax.dot_general(kbuf[2 * h_kv + par, keys, :], q2, (((1,), (1,)), ((), ())),
                               preferred_element_type=F32)

    units = [(j, h_kv, par) for j in range(nblk) for h_kv in range(N_KV_HEADS) for par in range(2)]
    pending = []
    for n in range(min(ATTN_LOOKAHEAD, len(units))):
        pending.append(scores(*units[n]))
        yield
    heads = [None] * N_HEADS
    for n, (j, h_kv, par) in enumerate(units):
        st = pending.pop(0)
        if n + ATTN_LOOKAHEAD < len(units):
            pending.append(scores(*units[n + ATTN_LOOKAHEAD]))
        mask = band0 if j == 0 else band
        st = jnp.where(mask, st, NEG)
        sink = jnp.concatenate([jnp.full((1, WINDOW), sinks_ref[GROUP * h_kv + par], F32),
                                jnp.full((1, WINDOW), sinks_ref[GROUP * h_kv + 2 + par], F32)], axis=1)
        m = jnp.maximum(jnp.max(st, axis=0, keepdims=True), sink)
        e = jnp.exp(st - m)
        den = jnp.sum(e, axis=0, keepdims=True) + jnp.exp(sink - m)
        vt = vtbuf[h_kv * HEAD_DIM:(h_kv + 1) * HEAD_DIM, j * WINDOW:(j + 2) * WINDOW]
        ot = _mm(vt, e.astype(BF16)) / den
        heads[GROUP * h_kv + par] = ot[:, :WINDOW]
        heads[GROUP * h_kv + 2 + par] = ot[:, WINDOW:]
        if h_kv == N_KV_HEADS - 1 and par == 1:
            attn = jnp.concatenate(heads, axis=0).T
            catbuf[j * WINDOW:(j + 1) * WINDOW, :Q_COLS] = _rms(attn, g_attn_ref[...]).astype(BF16)
        yield

    mix = []
    for c in range(D_MODEL // MXU_N):
        mix.append(_mm(catbuf[...], w_out_ref[:, c * MXU_N:(c + 1) * MXU_N]))
        yield
    x_mid = x_ref[...] + _rms(jnp.concatenate(mix, axis=1), g_post_ref[...])

    unew_ref[...] = ubuf[tm + 1:tm + POOL_PAD, :]
    kbuf[:, 0:WINDOW, :] = kbuf[:, tm:tm + WINDOW, :]
    vtbuf[:, 0:WINDOW] = vtbuf[:, tm:tm + WINDOW]
    ubuf[0:POOL_PAD, :] = ubuf[tm:tm + POOL_PAD, :]
    xmid_ref[...] = x_mid
    hmid_ref[...] = _rms(x_mid, g_pre_mlp_ref[...]).astype(BF16)


def _sample_mixer_kernel(x_ref, cos_ref, sin_ref, skt_ref, svt_ref, sp_ref, g_pre_ref, w_in_hbm,
                         sinks_ref, pool_w_hbm, pool_scale_ref, g_attn_ref, g_pool_ref, w_out_hbm, g_post_ref,
                         g_pre_mlp_ref, w_up_hbm, w_down_hbm, w_gate_hbm, w_ple_hbm,
                         xmid_ref, hmid_ref, knew_ref, vnew_ref, unew_ref,
                         w_in_out, pool_w_out, w_out_out, w_up_out, w_down_out, w_gate_out, w_ple_out,
                         ubuf, w_in_ref, pool_w_ref, w_out_ref, w_up_buf, w_down_buf, w_gate_buf, w_ple_buf, out_sems):
    step = pl.program_id(0)
    w_bufs = [w_in_ref, pool_w_ref, w_out_ref, w_up_buf, w_down_buf, w_gate_buf, w_ple_buf]
    w_outs = [w_in_out, pool_w_out, w_out_out, w_up_out, w_down_out, w_gate_out, w_ple_out]

    @pl.when(step == 0)
    def _():
        _cast_weights_to_vmem([w_in_hbm, pool_w_hbm, w_out_hbm, w_up_hbm, w_down_hbm, w_gate_hbm, w_ple_hbm], w_bufs)
        for copy in _weight_writebacks(w_bufs, w_outs, out_sems):
            copy.start()

    rows = x_ref.shape[0]
    nseq = skt_ref.shape[0]
    t = rows // nseq
    x = x_ref[...]
    q, k, v, u = _project(x, g_pre_ref[...], w_in_ref[...], (cos_ref[...], sin_ref[...]))

    k3 = k.reshape(nseq, t, KV_COLS)
    v3 = v.reshape(nseq, t, KV_COLS)
    sk = jnp.swapaxes(skt_ref[...], 1, 2)
    sv = jnp.swapaxes(svt_ref[...], 1, 2)
    knew_ref[...] = jnp.swapaxes(jnp.concatenate([sk[:, t:, :], k3], axis=1), 1, 2)
    vnew_ref[...] = jnp.swapaxes(jnp.concatenate([sv[:, t:, :], v3], axis=1), 1, 2)

    r_i = lax.broadcasted_iota(jnp.int32, (GROUP * t, WINDOW), 0) % t
    c_i = lax.broadcasted_iota(jnp.int32, (GROUP * t, WINDOW), 1)
    mask_old = (c_i > r_i)[None]
    r_n = lax.broadcasted_iota(jnp.int32, (GROUP * t, t), 0) % t
    c_n = lax.broadcasted_iota(jnp.int32, (GROUP * t, t), 1)
    mask_new = (c_n <= r_n)[None]

    qb = q.astype(BF16)
    kb = k3.astype(BF16)
    vb = v3.astype(BF16)
    heads = [None] * N_HEADS
    for h in range(N_KV_HEADS):
        lanes = slice(h * HEAD_DIM, (h + 1) * HEAD_DIM)
        q4 = jnp.concatenate(
            [qb[:, (GROUP * h + g) * HEAD_DIM:(GROUP * h + g + 1) * HEAD_DIM].reshape(nseq, t, HEAD_DIM)
             for g in range(GROUP)], axis=1)
        k_old = sk[:, :, lanes].astype(BF16)
        v_old = sv[:, :, lanes].astype(BF16)
        s_old = jnp.einsum('bqd,bkd->bqk', q4, k_old, preferred_element_type=F32)
        s_new = jnp.einsum('bqd,bkd->bqk', q4, kb[:, :, lanes], preferred_element_type=F32)
        s_old = jnp.where(mask_old, s_old, NEG)
        s_new = jnp.where(mask_new, s_new, NEG)
        sink = jnp.concatenate(
            [jnp.full((1, t, 1), sinks_ref[GROUP * h + g], F32) for g in range(GROUP)], axis=1)
        (e_old, e_new), den = _softmax_parts([s_old, s_new], sink)
        o = jnp.einsum('bqk,bkd->bqd', e_old.astype(BF16), v_old, preferred_element_type=F32)
        o = o + jnp.einsum('bqk,bkd->bqd', e_new.astype(BF16), vb[:, :, lanes], preferred_element_type=F32)
        o = o / den
        for g in range(GROUP):
            heads[GROUP * h + g] = o[:, g * t:(g + 1) * t, :].reshape(rows, HEAD_DIM)
    attn = jnp.concatenate(heads, axis=1)

    ext = POOL_PAD + t
    per_seq = lambda j: pl.ds(j, nseq, stride=ext)
    win = []
    for g, w in enumerate(POOL_WINDOWS):
        cols = slice(g * POOL_CG, (g + 1) * POOL_CG)
        ubuf[g, per_seq(0), :] = jnp.zeros((nseq, POOL_CG), F32)
        for j in range(POOL_BUF):
            ubuf[g, per_seq(1 + j), :] = sp_ref[j, :, cols]
        for s_i in range(nseq):
            ubuf[g, s_i * ext + POOL_PAD:(s_i + 1) * ext, :] = u[s_i * t:(s_i + 1) * t, cols]
        for j in range(POOL_BUF):
            unew_ref[j, :, cols] = ubuf[g, per_seq(t + 1 + j), :]
        a = ubuf[g]
        shift = 1
        while shift < w:
            a = a + pltpu.roll(a, shift, axis=0)
            shift *= 2
        win.append(a.reshape(nseq, ext, POOL_CG)[:, POOL_PAD:, :].reshape(rows, POOL_CG))
    inv_cnt = [1.0 / w for w in POOL_WINDOWS]
    pooled = _pool_out(win, u, inv_cnt, pool_w_ref, pool_scale_ref[...])

    x_mid = _mix_out(x, attn, pooled, g_attn_ref[...], g_pool_ref[...], w_out_ref[...], g_post_ref[...])
    xmid_ref[...] = x_mid
    hmid_ref[...] = _rms(x_mid, g_pre_mlp_ref[...]).astype(BF16)

    @pl.when(step == pl.num_programs(0) - 1)
    def _():
        for copy in _weight_writebacks(w_bufs, w_outs, out_sems):
            copy.wait()


def _ffn_steps(x_ref, h_ref, load_p, w_up_ref, w_down_ref, g_post_ref, w_gate_ref, b_gate_ref, w_ple_ref,
               y_ref, actbuf):
    for c in range(D_FF // MXU_N):
        cols = slice(c * MXU_N, (c + 1) * MXU_N)
        up = _mm(h_ref[...], w_up_ref[:, cols])
        actbuf[:, cols] = jnp.square(jnp.maximum(up, 0.0)).astype(BF16)
        yield
    ff = []
    for c in range(D_MODEL // MXU_N):
        cols = slice(c * MXU_N, (c + 1) * MXU_N)
        acc = None
        for r in range(D_FF // FF_CHUNK):
            rows = slice(r * FF_CHUNK, (r + 1) * FF_CHUNK)
            part = _mm(actbuf[:, rows], w_down_ref[rows, cols])
            acc = part if acc is None else acc + part
            yield
        ff.append(acc)
    x = x_ref[...] + _rms(jnp.concatenate(ff, axis=1), g_post_ref[...])
    xb = x.astype(BF16)
    ple = _mm(load_p().astype(BF16), w_ple_ref[...])
    yield
    for c in range(D_MODEL // MXU_N):
        cols = slice(c * MXU_N, (c + 1) * MXU_N)
        gate = jax.nn.sigmoid(_mm(xb, w_gate_ref[:, cols]) + b_gate_ref[:, cols])
        y_ref[:, cols] = x[:, cols] + gate * ple[:, cols]
        yield


def _interleave(first, second, pattern):
    gens = {'1': first, '2': second}
    for tag in pattern:
        next(gens[tag], None)
    for _ in first:
        pass
    for _ in second:
        pass


def _cast_weights_to_vmem(srcs, dsts):
    tasks = []
    for src, dst in zip(srcs, dsts):
        if len(src.shape) != 2:
            tasks.append((src, dst, tuple(src.shape)))
            continue
        rows, cols = src.shape
        col_block = cols if cols <= STAGE_MAX_COLS else D_MODEL
        for c0 in range(0, cols, col_block):
            for r0 in range(0, rows, STAGE_ROWS):
                view = (slice(r0, r0 + STAGE_ROWS), slice(c0, c0 + col_block))
                tasks.append((src.at[view], dst.at[view], (STAGE_ROWS, col_block)))
    shapes = sorted(set(shape for _, _, shape in tasks))

    def body(*scoped):
        *stages, sems = scoped
        uses = [0] * len(shapes)
        copies = []
        for src, dst, shape in tasks:
            k = shapes.index(shape)
            slot = uses[k] % 2
            uses[k] += 1
            stage = stages[k].at[slot]
            copies.append((pltpu.make_async_copy(src, stage, sems.at[k, slot]), stage, dst))
        copies[0][0].start()
        for t, (copy, stage, dst) in enumerate(copies):
            if t + 1 < len(copies):
                copies[t + 1][0].start()
            copy.wait()
            dst[...] = stage[...].astype(BF16)

    pl.run_scoped(body, *[pltpu.VMEM((2,) + shape, F32) for shape in shapes],
                  pltpu.SemaphoreType.DMA((len(shapes), 2)))


def _weight_writebacks(bufs, outs, sems):
    return [pltpu.make_async_copy(buf, out, sems.at[n]) for n, (buf, out) in enumerate(zip(bufs, outs))]


N_TABLES = 5
MIXER_MATMUL_W = (1, 3, 7)
FFN_MATMUL_W = (1, 2, 4, 6)
N_MATMUL_W = len(MIXER_MATMUL_W) + len(FFN_MATMUL_W)
STAGE_ROWS = 256
STAGE_MAX_COLS = 1280


N_SAMPLE_TILES = 2


def _prompt_layer_kernel(*refs):
    x_ref, *tables = refs[:1 + N_TABLES]
    (g_pre_mix, w_in, sinks, pool_w, pool_scale, g_attn, g_pool, w_out, g_post_mix, p_ref,
     g_pre_mlp, w_up, w_down, g_post_mlp, w_gate, b_gate, w_ple,
     xs_hbm, hs_hbm, ps_ref) = refs[1 + N_TABLES:21 + N_TABLES]
    y_ref, ys_hbm, knew_ref, vnew_ref, unew_ref = refs[21 + N_TABLES:26 + N_TABLES]
    qbuf, kbuf, vtbuf, ubuf, pbuf, catbuf, actbuf, xmid, hmid, sems = refs[26 + N_TABLES:]
    tm = x_ref.shape[0]
    step = pl.program_id(0)
    last_tile = pl.num_programs(0) - 1 - N_SAMPLE_TILES
    on_sample = step < N_SAMPLE_TILES
    slot = step % 2
    sample_rows = pl.ds(pl.multiple_of(jnp.minimum(step, N_SAMPLE_TILES - 1) * tm, tm), tm)

    @pl.when(on_sample)
    def _():
        loads = [pltpu.make_async_copy(xs_hbm.at[sample_rows], xmid.at[slot], sems.at[0]),
                 pltpu.make_async_copy(hs_hbm.at[sample_rows], hmid.at[slot], sems.at[1])]
        for copy in loads:
            copy.start()
        kbuf[:, 0:WINDOW, :] = jnp.zeros((2 * N_KV_HEADS, WINDOW, LANES), BF16)
        vtbuf[:, 0:WINDOW] = jnp.zeros((KV_COLS, WINDOW), BF16)
        ubuf[0:POOL_PAD, :] = jnp.zeros((POOL_PAD, POOL_WIDTH), F32)
        for copy in loads:
            copy.wait()

    ffn = _ffn_steps(xmid.at[slot], hmid.at[slot], lambda: jnp.where(on_sample, ps_ref[...], p_ref[...]),
                     w_up, w_down, g_post_mlp, w_gate, b_gate, w_ple, y_ref, actbuf)
    mixer = _prompt_mixer_steps(jnp.clip(step - 1, 0, last_tile), x_ref, *tables, g_pre_mix, w_in, sinks, pool_w,
                                pool_scale, g_attn, g_pool, w_out, g_post_mix, knew_ref, vnew_ref, unew_ref,
                                qbuf, kbuf, vtbuf, ubuf, pbuf, catbuf, g_pre_mlp, xmid.at[1 - slot],
                                hmid.at[1 - slot])
    _interleave(ffn, mixer, PROMPT_PATTERN)

    @pl.when(on_sample)
    def _():
        store = pltpu.make_async_copy(y_ref, ys_hbm.at[sample_rows], sems.at[2])
        store.start()
        store.wait()


def _const_spec(shape):
    zeros = (0,) * len(shape)
    return pl.BlockSpec(shape, lambda i: zeros, pipeline_mode=pl.Buffered(1))


def _rope_lane_tables(pos):
    half = HEAD_DIM // 2
    inv = ROPE_THETA ** (-np.arange(half, dtype=np.float64) / half)
    ang = pos[:, None] * inv[None, :]
    reps = LANES // half
    return (np.tile(np.cos(ang), (1, reps)).astype(np.float32), np.tile(np.sin(ang), (1, reps)).astype(np.float32))


def _rope_sign():
    lane = np.arange(LANES)
    return np.where((lane % HEAD_DIM) < HEAD_DIM // 2, -1.0, 1.0).astype(np.float32)[None, :]


def _params(dims):
    return pltpu.CompilerParams(dimension_semantics=dims, vmem_limit_bytes=VMEM_LIMIT)


def _mixer_weight_specs():
    return [
        _const_spec((1, D_MODEL)),
        _const_spec((D_MODEL, IN_COLS)),
        pl.BlockSpec(memory_space=pltpu.SMEM),
        _const_spec((len(POOL_WINDOWS), POOL_CG, POOL_CG)),
        _const_spec((1, POOL_WIDTH)),
        _const_spec((1, Q_COLS)),
        _const_spec((1, POOL_WIDTH)),
        _const_spec((D_MODEL, D_MODEL)),
        _const_spec((1, D_MODEL)),
    ]


def _ffn_weight_specs():
    return [
        _const_spec((1, D_MODEL)),
        _const_spec((D_MODEL, D_FF)),
        _const_spec((D_FF, D_MODEL)),
        _const_spec((1, D_MODEL)),
        _const_spec((D_MODEL, D_MODEL)),
        _const_spec((1, D_MODEL)),
        _const_spec((PLE_DIM, D_MODEL)),
    ]


def _prompt_layer(x, p, xs_mid, hs_mid, ps, mixer_w, ffn_w):
    seq = x.shape[0]
    tm = PROMPT_TILE
    tiles = seq // tm
    assert xs_mid.shape[0] == N_SAMPLE_TILES * tm
    cos_a, sin_a = _rope_lane_tables(np.arange(tiles, dtype=np.float64) * tm)
    cos_b, sin_b = _rope_lane_tables(np.arange(tm, dtype=np.float64))
    mixer_tile = lambda i: (jnp.clip(i - 1, 0, tiles - 1), 0)
    mixer_tile3 = lambda i: (jnp.clip(i - 1, 0, tiles - 1), 0, 0)
    ffn_tile = lambda i: (jnp.clip(i - N_SAMPLE_TILES, 0, tiles - 1), 0)
    sample_tile = lambda i: (jnp.minimum(i, N_SAMPLE_TILES - 1), 0)
    fixed = lambda i: (0, 0)
    in_specs = [
        pl.BlockSpec((tm, D_MODEL), mixer_tile),
        pl.BlockSpec((1, 1, LANES), mixer_tile3),
        pl.BlockSpec((1, 1, LANES), mixer_tile3),
        _const_spec((tm, LANES)),
        _const_spec((tm, LANES)),
        _const_spec((1, LANES)),
    ] + _mixer_weight_specs() + [pl.BlockSpec((tm, PLE_DIM), ffn_tile)] + _ffn_weight_specs() + [
        pl.BlockSpec(memory_space=pl.ANY),
        pl.BlockSpec(memory_space=pl.ANY),
        pl.BlockSpec((tm, PLE_DIM), sample_tile, pipeline_mode=pl.Buffered(1)),
    ]
    return pl.pallas_call(
        _prompt_layer_kernel,
        grid=(tiles + N_SAMPLE_TILES,),
        in_specs=in_specs,
        out_specs=[
            pl.BlockSpec((tm, D_MODEL), ffn_tile),
            pl.BlockSpec(memory_space=pl.ANY),
            pl.BlockSpec((WINDOW, KV_COLS), fixed),
            pl.BlockSpec((WINDOW, KV_COLS), fixed),
            pl.BlockSpec((POOL_BUF, POOL_WIDTH), fixed),
        ],
        out_shape=[
            jax.ShapeDtypeStruct((seq, D_MODEL), F32),
            jax.ShapeDtypeStruct(xs_mid.shape, F32),
            jax.ShapeDtypeStruct((WINDOW, KV_COLS), F32),
            jax.ShapeDtypeStruct((WINDOW, KV_COLS), F32),
            jax.ShapeDtypeStruct((POOL_BUF, POOL_WIDTH), F32),
        ],
        scratch_shapes=[
            pltpu.VMEM((tm, Q_COLS), BF16),
            pltpu.VMEM((2 * N_KV_HEADS, WINDOW + tm, LANES), BF16),
            pltpu.VMEM((KV_COLS, WINDOW + tm), BF16),
            pltpu.VMEM((POOL_PAD + tm, POOL_WIDTH), F32),
            pltpu.VMEM((tm, POOL_WIDTH), F32),
            pltpu.VMEM((tm, D_MODEL), BF16),
            pltpu.VMEM((tm, D_FF), BF16),
            pltpu.VMEM((2, tm, D_MODEL), F32),
            pltpu.VMEM((2, tm, D_MODEL), BF16),
            pltpu.SemaphoreType.DMA((3,)),
        ],
        compiler_params=_params(("arbitrary",)),
        name="prompt_layer",
    )(x, jnp.asarray(cos_a)[:, None, :], jnp.asarray(sin_a)[:, None, :], jnp.asarray(cos_b), jnp.asarray(sin_b),
      jnp.asarray(_rope_sign()), *mixer_w, p, *ffn_w, xs_mid, hs_mid, ps)


def _sample_mixer(x, state_kt, state_vt, state_pool_t, past_len, mixer_w, ffn_w):
    nseq_all, t = state_kt.shape[0], x.shape[0] // state_kt.shape[0]
    nseq = SAMPLE_SEQS
    rows = nseq * t
    steps = nseq_all // nseq
    cos, sin = _rope_lane_tables(past_len + np.arange(rows, dtype=np.float64) % t)
    sin = sin * _rope_sign()
    row = lambda i: (i, 0)
    seq3 = lambda i: (i, 0, 0)
    pos3 = lambda i: (0, i, 0)
    hbm = pl.BlockSpec(memory_space=pl.ANY)
    mixer_specs = _mixer_weight_specs()
    for n in MIXER_MATMUL_W:
        mixer_specs[n] = hbm
    g_pre_mlp = ffn_w[0]
    ffn_matmul_w = [ffn_w[n] for n in FFN_MATMUL_W]
    matmul_w = [mixer_w[n] for n in MIXER_MATMUL_W] + ffn_matmul_w
    outs = pl.pallas_call(
        _sample_mixer_kernel,
        grid=(steps,),
        in_specs=[
            pl.BlockSpec((rows, D_MODEL), row),
            _const_spec((rows, LANES)),
            _const_spec((rows, LANES)),
            pl.BlockSpec((nseq, KV_COLS, WINDOW), seq3),
            pl.BlockSpec((nseq, KV_COLS, WINDOW), seq3),
            pl.BlockSpec((POOL_BUF, nseq, POOL_WIDTH), pos3),
        ] + mixer_specs + [_const_spec((1, D_MODEL))] + [hbm] * len(ffn_matmul_w),
        out_specs=[
            pl.BlockSpec((rows, D_MODEL), row),
            pl.BlockSpec((rows, D_MODEL), row),
            pl.BlockSpec((nseq, KV_COLS, WINDOW), seq3),
            pl.BlockSpec((nseq, KV_COLS, WINDOW), seq3),
            pl.BlockSpec((POOL_BUF, nseq, POOL_WIDTH), pos3),
        ] + [hbm] * N_MATMUL_W,
        out_shape=[
            jax.ShapeDtypeStruct((nseq_all * t, D_MODEL), F32),
            jax.ShapeDtypeStruct((nseq_all * t, D_MODEL), BF16),
            jax.ShapeDtypeStruct((nseq_all, KV_COLS, WINDOW), F32),
            jax.ShapeDtypeStruct((nseq_all, KV_COLS, WINDOW), F32),
            jax.ShapeDtypeStruct((POOL_BUF, nseq_all, POOL_WIDTH), F32),
        ] + [jax.ShapeDtypeStruct(w.shape, BF16) for w in matmul_w],
        scratch_shapes=[pltpu.VMEM((len(POOL_WINDOWS), nseq * (POOL_PAD + t), POOL_CG), F32)]
        + [pltpu.VMEM(w.shape, BF16) for w in matmul_w] + [pltpu.SemaphoreType.DMA((N_MATMUL_W,))],
        compiler_params=_params(("arbitrary",)),
        name="sample_mixer",
    )(x, jnp.asarray(cos), jnp.asarray(sin), state_kt, state_vt, state_pool_t, *mixer_w, g_pre_mlp, *ffn_matmul_w)
    return outs[:5], outs[5:]


def kernel(x_prompt, x_sample, state_k, state_v, state_pool, p_prompt, p_sample, w_in, attn_sinks, pool_w,
           pool_scale, g_attn_out, g_pool_out, w_out, g_pre_mix, g_post_mix, g_pre_mlp, g_post_mlp, w_up, w_down,
           w_ple, w_ple_gate, b_ple_gate):
    depth = w_in.shape[0]
    batch, seq, _ = x_prompt.shape
    dec_batch, dec_seq, _ = x_sample.shape
    assert depth == 1 and batch == 1

    xp = x_prompt.reshape(seq, D_MODEL)
    xs = x_sample.reshape(dec_batch * dec_seq, D_MODEL)
    i = 0
    mixer_w = [
        g_pre_mix[i][None, :], w_in[i], attn_sinks[i], pool_w[i], pool_scale[i][None, :], g_attn_out[i][None, :],
        g_pool_out[i][None, :], w_out[i], g_post_mix[i][None, :],
    ]
    ffn_w = [
        g_pre_mlp[i][None, :], w_up[i], w_down[i], g_post_mlp[i][None, :], w_ple_gate[i], b_ple_gate[i][None, :],
        w_ple[i],
    ]

    (xs_mid, hs_mid, ks_t, vs_t, us_t), w_bf16 = _sample_mixer(
        xs, state_k[i].reshape(dec_batch, WINDOW, KV_COLS).transpose(0, 2, 1),
        state_v[i].reshape(dec_batch, WINDOW, KV_COLS).transpose(0, 2, 1),
        state_pool[i].transpose(1, 0, 2), float(PAST_LEN), mixer_w, ffn_w)
    ks, vs, us = ks_t.transpose(0, 2, 1), vs_t.transpose(0, 2, 1), us_t.transpose(1, 0, 2)
    for n, w in zip(MIXER_MATMUL_W, w_bf16[:len(MIXER_MATMUL_W)]):
        mixer_w[n] = w
    for n, w in zip(FFN_MATMUL_W, w_bf16[len(MIXER_MATMUL_W):]):
        ffn_w[n] = w
    yp, ys, kp, vp, up = _prompt_layer(xp, p_prompt[i, 0], xs_mid, hs_mid,
                                       p_sample[i].reshape(dec_batch * dec_seq, PLE_DIM), mixer_w, ffn_w)

    return (
        yp.reshape(batch, seq, D_MODEL),
        ys.reshape(dec_batch, dec_seq, D_MODEL),
        kp.T.reshape(depth, batch, WINDOW, N_KV_HEADS, HEAD_DIM),
        vp.T.reshape(depth, batch, WINDOW, N_KV_HEADS, HEAD_DIM),
        up.reshape(depth, batch, POOL_BUF, POOL_WIDTH),
        ks.reshape(depth, dec_batch, WINDOW, N_KV_HEADS, HEAD_DIM),
        vs.reshape(depth, dec_batch, WINDOW, N_KV_HEADS, HEAD_DIM),
        us.reshape(depth, dec_batch, POOL_BUF, POOL_WIDTH),
    )
```

```python
import functools

import jax
import jax.numpy as jnp
import numpy as np
from jax import lax
from jax.experimental import pallas as pl
from jax.experimental.pallas import tpu as pltpu

D_MODEL = 1024
HEAD_DIM = 64
N_HEADS = 8
N_KV_HEADS = 2
GROUP = N_HEADS // N_KV_HEADS
WINDOW = 128
Q_COLS = N_HEADS * HEAD_DIM
KV_COLS = N_KV_HEADS * HEAD_DIM
POOL_WIDTH = 512
POOL_WINDOWS = (2, 4, 8, 16)
POOL_CG = POOL_WIDTH // len(POOL_WINDOWS)
POOL_BUF = max(POOL_WINDOWS) - 1
POOL_PAD = POOL_BUF + 1
IN_COLS = Q_COLS + 2 * KV_COLS + POOL_WIDTH
D_FF = 4 * D_MODEL
FF_CHUNK = 1024
PLE_DIM = 256
ROPE_THETA = 10000.0
PAST_LEN = 16384
EPS = 1e-6
NEG = -1e30
LANES = 128
MXU_N = 256

ATTN_LOOKAHEAD = 2
PROMPT_PATTERN = "11" + "21" * (5 + ATTN_LOOKAHEAD + 16 + 4) + "111" + "22222" + "11111"

PROMPT_TILE = 512
SAMPLE_SEQS = 32
VMEM_LIMIT = 58 * 1024 * 1024

BF16 = jnp.bfloat16
F32 = jnp.float32


def _rms(x, g):
    y = x * lax.rsqrt(jnp.mean(x * x, axis=-1, keepdims=True) + EPS)
    return y * g


def _mm(a, w):
    return jnp.dot(a, w, preferred_element_type=F32)


def _rope_tables(cos_a, sin_a, cos_b, sin_b, sign):
    cos_t = cos_a * cos_b - sin_a * sin_b
    sin_t = (sin_a * cos_b + cos_a * sin_b) * sign
    return cos_t, sin_t


def _rope(x, cos_t, sin_t, first_half):
    n = x.shape[1] // LANES
    width = x.shape[1]
    partner = jnp.where(first_half, pltpu.roll(x, width - HEAD_DIM // 2, axis=1), pltpu.roll(x, HEAD_DIM // 2, axis=1))
    if n > 1:
        cos_t = jnp.concatenate([cos_t] * n, axis=1)
        sin_t = jnp.concatenate([sin_t] * n, axis=1)
    return x * cos_t + partner * sin_t


def _first_half_mask(rows, width):
    lane = lax.broadcasted_iota(jnp.int32, (rows, width), 1)
    return (lane % HEAD_DIM) < (HEAD_DIM // 2)


def _project(x, g_pre, w_in, tables):
    rows = x.shape[0]
    h = _rms(x, g_pre).astype(BF16)
    proj = _mm(h, w_in)
    cos_t, sin_t = tables
    q = _rope(proj[:, :Q_COLS], cos_t, sin_t, _first_half_mask(rows, Q_COLS)) * (HEAD_DIM ** -0.5)
    k = _rope(proj[:, Q_COLS:Q_COLS + KV_COLS], cos_t, sin_t, _first_half_mask(rows, KV_COLS))
    v = proj[:, Q_COLS + KV_COLS:Q_COLS + 2 * KV_COLS]
    u = proj[:, Q_COLS + 2 * KV_COLS:]
    return q, k, v, u


def _pool_out(win, tok, inv_cnt, pool_w_ref, pool_scale):
    outs = []
    for g in range(len(POOL_WINDOWS)):
        r = win[g] * inv_cnt[g] - tok[:, g * POOL_CG:(g + 1) * POOL_CG]
        outs.append(_mm(r.astype(BF16), pool_w_ref[g]))
    return jnp.concatenate(outs, axis=1) * pool_scale


def _mix_out(x, attn, pooled, g_attn, g_pool, w_out, g_post):
    cat = jnp.concatenate([_rms(attn, g_attn), _rms(pooled, g_pool)], axis=1).astype(BF16)
    return x + _rms(_mm(cat, w_out), g_post)


def _softmax_parts(s_list, sink):
    m = sink
    for s in s_list:
        m = jnp.maximum(m, jnp.max(s, axis=-1, keepdims=True))
    e_list = [jnp.exp(s - m) for s in s_list]
    den = jnp.exp(sink - m)
    for e in e_list:
        den = den + jnp.sum(e, axis=-1, keepdims=True)
    return e_list, den


def _prompt_mixer_steps(tile, x_ref, cos_a_ref, sin_a_ref, cos_b_ref, sin_b_ref, sign_ref, g_pre_ref, w_in_ref,
                        sinks_ref, pool_w_ref, pool_scale_ref, g_attn_ref, g_pool_ref, w_out_ref, g_post_ref,
                        knew_ref, vnew_ref, unew_ref, qbuf, kbuf, vtbuf, ubuf, pbuf, catbuf,
                        g_pre_mlp_ref, xmid_ref, hmid_ref):
    tm = x_ref.shape[0]
    nblk = tm // WINDOW
    half_lanes = LANES // 2
    base = tile * tm

    h = _rms(x_ref[...], g_pre_ref[...]).astype(BF16)
    cos_t, sin_t = _rope_tables(cos_a_ref[0], sin_a_ref[0], cos_b_ref[...], sin_b_ref[...], sign_ref[...])

    kv = _mm(h, w_in_ref[:, Q_COLS:Q_COLS + 2 * KV_COLS])
    k = _rope(kv[:, :KV_COLS], cos_t, sin_t, _first_half_mask(tm, KV_COLS))
    v = kv[:, KV_COLS:]
    knew_ref[...] = k[tm - WINDOW:, :].T
    vnew_ref[...] = v[tm - WINDOW:, :].T
    low = lax.broadcasted_iota(jnp.int32, (tm, LANES), 1) < half_lanes
    k_swapped = pltpu.roll(k, half_lanes, axis=1)
    kbuf[0, WINDOW:, :] = jnp.where(low, k, 0.0).astype(BF16)
    kbuf[1, WINDOW:, :] = jnp.where(low, 0.0, k_swapped).astype(BF16)
    kbuf[2, WINDOW:, :] = jnp.where(low, k_swapped, 0.0).astype(BF16)
    kbuf[3, WINDOW:, :] = jnp.where(low, 0.0, k).astype(BF16)
    vtbuf[:, WINDOW:] = v.T.astype(BF16)
    yield

    u_col0 = Q_COLS + 2 * KV_COLS
    for c in range(Q_COLS // MXU_N):
        cols = slice(c * MXU_N, (c + 1) * MXU_N)
        q = _rope(_mm(h, w_in_ref[:, cols]), cos_t, sin_t, _first_half_mask(tm, MXU_N)) * (HEAD_DIM ** -0.5)
        qbuf[:, cols] = q.astype(BF16)
        yield
    for c in range(POOL_WIDTH // MXU_N):
        cols = slice(c * MXU_N, (c + 1) * MXU_N)
        ubuf[POOL_PAD:, cols] = _mm(h, w_in_ref[:, u_col0 + c * MXU_N:u_col0 + (c + 1) * MXU_N])
        yield

    pos = base + lax.broadcasted_iota(jnp.int32, (tm, 1), 0)
    pool_scale = pool_scale_ref[...]
    for g, w in enumerate(POOL_WINDOWS):
        cols = slice(g * POOL_CG, (g + 1) * POOL_CG)
        a = ubuf[:, cols]
        shift = 1
        while shift < w:
            a = a + pltpu.roll(a, shift, axis=0)
            shift *= 2
        inv_cnt = 1.0 / jnp.minimum(pos + 1, w).astype(F32)
        r = a[POOL_PAD:, :] * inv_cnt - ubuf[POOL_PAD:, cols]
        pbuf[:, cols] = _mm(r.astype(BF16), pool_w_ref[g]) * pool_scale[:, cols]
        yield
    catbuf[:, Q_COLS:] = _rms(pbuf[...], g_pool_ref[...]).astype(BF16)

    c_i = lax.broadcasted_iota(jnp.int32, (2 * WINDOW, 2 * WINDOW), 0)
    r_i = lax.broadcasted_iota(jnp.int32, (2 * WINDOW, 2 * WINDOW), 1) % WINDOW
    d_i = c_i - r_i
    band = (d_i >= 1) & (d_i <= WINDOW)
    band0 = band & (c_i >= WINDOW - base)

    def scores(j, h_kv, par):
        rows = slice(j * WINDOW, (j + 1) * WINDOW)
        keys = slice(j * WINDOW, (j + 2) * WINDOW)
        q2 = jnp.concatenate([qbuf[rows, (2 * h_kv) * LANES:(2 * h_kv + 1) * LANES],
                              qbuf[rows, (2 * h_kv + 1) * LANES:(2 * h_kv + 2) * LANES]], axis=0)
        return lax.dot_general(kbuf[2 * h_kv + par, keys, :], q2, (((1,), (1,)), ((), ())),
                               preferred_element_type=F32)

    units = [(j, h_kv, par) for j in range(nblk) for h_kv in range(N_KV_HEADS) for par in range(2)]
    pending = []
    for n in range(min(ATTN_LOOKAHEAD, len(units))):
        pending.append(scores(*units[n]))
        yield
    heads = [None] * N_HEADS
    for n, (j, h_kv, par) in enumerate(units):
        st = pending.pop(0)
        if n + ATTN_LOOKAHEAD < len(units):
            pending.append(scores(*units[n + ATTN_LOOKAHEAD]))
        mask = band0 if j == 0 else band
        st = jnp.where(mask, st, NEG)
        sink = jnp.concatenate([jnp.full((1, WINDOW), sinks_ref[GROUP * h_kv + par], F32),
                                jnp.full((1, WINDOW), sinks_ref[GROUP * h_kv + 2 + par], F32)], axis=1)
        m = jnp.maximum(jnp.max(st, axis=0, keepdims=True), sink)
        e = jnp.exp(st - m)
        den = jnp.sum(e, axis=0, keepdims=True) + jnp.exp(sink - m)
        vt = vtbuf[h_kv * HEAD_DIM:(h_kv + 1) * HEAD_DIM, j * WINDOW:(j + 2) * WINDOW]
        ot = _mm(vt, e.astype(BF16)) / den
        heads[GROUP * h_kv + par] = ot[:, :WINDOW]
        heads[GROUP * h_kv + 2 + par] = ot[:, WINDOW:]
        if h_kv == N_KV_HEADS - 1 and par == 1:
            attn = jnp.concatenate(heads, axis=0).T
            catbuf[j * WINDOW:(j + 1) * WINDOW, :Q_COLS] = _rms(attn, g_attn_ref[...]).astype(BF16)
        yield

    mix = []
    for c in range(D_MODEL // MXU_N):
        mix.append(_mm(catbuf[...], w_out_ref[:, c * MXU_N:(c + 1) * MXU_N]))
        yield
    x_mid = x_ref[...] + _rms(jnp.concatenate(mix, axis=1), g_post_ref[...])

    unew_ref[...] = ubuf[tm + 1:tm + POOL_PAD, :]
    kbuf[:, 0:WINDOW, :] = kbuf[:, tm:tm + WINDOW, :]
    vtbuf[:, 0:WINDOW] = vtbuf[:, tm:tm + WINDOW]
    ubuf[0:POOL_PAD, :] = ubuf[tm:tm + POOL_PAD, :]
    xmid_ref[...] = x_mid
    hmid_ref[...] = _rms(x_mid, g_pre_mlp_ref[...]).astype(BF16)


def _ffn_weight_chunks(srcs, dsts):
    chunks = []
    for src, dst in zip(srcs, dsts):
        rows, cols = src.shape
        for c0 in range(0, cols, D_MODEL):
            for r0 in range(0, rows, STAGE_ROWS):
                view = (slice(r0, r0 + STAGE_ROWS), slice(c0, c0 + D_MODEL))
                chunks.append((src.at[view], dst.at[view]))
    return chunks


def _sample_mixer_kernel(n_steps, x_ref, cos_ref, sin_ref, skt_ref, svt_ref, sp_ref, g_pre_ref, w_in_hbm,
                         sinks_ref, pool_w_hbm, pool_scale_ref, g_attn_ref, g_pool_ref, w_out_hbm, g_post_ref,
                         g_pre_mlp_ref, w_up_hbm, w_down_hbm, w_gate_hbm, w_ple_hbm,
                         xmid_ref, hmid_ref, knew_ref, vnew_ref, unew_ref,
                         w_in_out, pool_w_out, w_out_out, w_up_out, w_down_out, w_gate_out, w_ple_out,
                         ubuf, w_in_ref, pool_w_ref, w_out_ref, out_sems, fstage, bstage, in_sems, chunk_sems):
    step = pl.program_id(0)
    w_bufs = [w_in_ref, pool_w_ref, w_out_ref]
    w_outs = [w_in_out, pool_w_out, w_out_out]
    chunks = _ffn_weight_chunks([w_up_hbm, w_down_hbm, w_gate_hbm, w_ple_hbm],
                                [w_up_out, w_down_out, w_gate_out, w_ple_out])
    per_step = fstage.shape[0]
    assert per_step * n_steps >= len(chunks)

    def chunk_loads(i):
        return [pltpu.make_async_copy(src, fstage.at[n], in_sems.at[n])
                for n, (src, _) in enumerate(chunks[i * per_step:(i + 1) * per_step])]

    def chunk_stores(i):
        return [pltpu.make_async_copy(bstage.at[n], dst, chunk_sems.at[n])
                for n, (_, dst) in enumerate(chunks[i * per_step:(i + 1) * per_step])]

    @pl.when(step == 0)
    def _():
        _cast_weights_to_vmem([w_in_hbm, pool_w_hbm, w_out_hbm], w_bufs)
        for copy in _weight_writebacks(w_bufs, w_outs, out_sems):
            copy.start()

    for i in range(n_steps):
        @pl.when(step == i)
        def _(i=i):
            for copy in chunk_loads(i):
                copy.start()

    rows = x_ref.shape[0]
    nseq = skt_ref.shape[0]
    t = rows // nseq
    x = x_ref[...]
    q, k, v, u = _project(x, g_pre_ref[...], w_in_ref[...], (cos_ref[...], sin_ref[...]))

    k3 = k.reshape(nseq, t, KV_COLS)
    v3 = v.reshape(nseq, t, KV_COLS)
    sk = jnp.swapaxes(skt_ref[...], 1, 2)
    sv = jnp.swapaxes(svt_ref[...], 1, 2)
    knew_ref[...] = jnp.swapaxes(jnp.concatenate([sk[:, t:, :], k3], axis=1), 1, 2)
    vnew_ref[...] = jnp.swapaxes(jnp.concatenate([sv[:, t:, :], v3], axis=1), 1, 2)

    r_i = lax.broadcasted_iota(jnp.int32, (GROUP * t, WINDOW), 0) % t
    c_i = lax.broadcasted_iota(jnp.int32, (GROUP * t, WINDOW), 1)
    mask_old = (c_i > r_i)[None]
    r_n = lax.broadcasted_iota(jnp.int32, (GROUP * t, t), 0) % t
    c_n = lax.broadcasted_iota(jnp.int32, (GROUP * t, t), 1)
    mask_new = (c_n <= r_n)[None]

    qb = q.astype(BF16)
    kb = k3.astype(BF16)
    vb = v3.astype(BF16)
    heads = [None] * N_HEADS
    for h in range(N_KV_HEADS):
        lanes = slice(h * HEAD_DIM, (h + 1) * HEAD_DIM)
        q4 = jnp.concatenate(
            [qb[:, (GROUP * h + g) * HEAD_DIM:(GROUP * h + g + 1) * HEAD_DIM].reshape(nseq, t, HEAD_DIM)
             for g in range(GROUP)], axis=1)
        k_old = sk[:, :, lanes].astype(BF16)
        v_old = sv[:, :, lanes].astype(BF16)
        s_old = jnp.einsum('bqd,bkd->bqk', q4, k_old, preferred_element_type=F32)
        s_new = jnp.einsum('bqd,bkd->bqk', q4, kb[:, :, lanes], preferred_element_type=F32)
        s_old = jnp.where(mask_old, s_old, NEG)
        s_new = jnp.where(mask_new, s_new, NEG)
        sink = jnp.concatenate(
            [jnp.full((1, t, 1), sinks_ref[GROUP * h + g], F32) for g in range(GROUP)], axis=1)
        (e_old, e_new), den = _softmax_parts([s_old, s_new], sink)
        o = jnp.einsum('bqk,bkd->bqd', e_old.astype(BF16), v_old, preferred_element_type=F32)
        o = o + jnp.einsum('bqk,bkd->bqd', e_new.astype(BF16), vb[:, :, lanes], preferred_element_type=F32)
        o = o / den
        for g in range(GROUP):
            heads[GROUP * h + g] = o[:, g * t:(g + 1) * t, :].reshape(rows, HEAD_DIM)
    attn = jnp.concatenate(heads, axis=1)

    ext = POOL_PAD + t
    per_seq = lambda j: pl.ds(j, nseq, stride=ext)
    win = []
    for g, w in enumerate(POOL_WINDOWS):
        cols = slice(g * POOL_CG, (g + 1) * POOL_CG)
        ubuf[g, per_seq(0), :] = jnp.zeros((nseq, POOL_CG), F32)
        for j in range(POOL_BUF):
            ubuf[g, per_seq(1 + j), :] = sp_ref[j, :, cols]
        for s_i in range(nseq):
            ubuf[g, s_i * ext + POOL_PAD:(s_i + 1) * ext, :] = u[s_i * t:(s_i + 1) * t, cols]
        for j in range(POOL_BUF):
            unew_ref[j, :, cols] = ubuf[g, per_seq(t + 1 + j), :]
        a = ubuf[g]
        shift = 1
        while shift < w:
            a = a + pltpu.roll(a, shift, axis=0)
            shift *= 2
        win.append(a.reshape(nseq, ext, POOL_CG)[:, POOL_PAD:, :].reshape(rows, POOL_CG))
    inv_cnt = [1.0 / w for w in POOL_WINDOWS]
    pooled = _pool_out(win, u, inv_cnt, pool_w_ref, pool_scale_ref[...])

    x_mid = _mix_out(x, attn, pooled, g_attn_ref[...], g_pool_ref[...], w_out_ref[...], g_post_ref[...])
    xmid_ref[...] = x_mid
    hmid_ref[...] = _rms(x_mid, g_pre_mlp_ref[...]).astype(BF16)

    for i in range(n_steps):
        @pl.when(step == i)
        def _(i=i):
            previous = chunk_stores(i - 1) if i > 0 else []
            loads, stores = chunk_loads(i), chunk_stores(i)
            for n in range(max(len(loads), len(previous))):
                if n < len(previous):
                    previous[n].wait()
                if n < len(loads):
                    loads[n].wait()
                    bstage[n] = fstage[n].astype(BF16)
                    stores[n].start()
            if i == n_steps - 1:
                for copy in stores + _weight_writebacks(w_bufs, w_outs, out_sems):
                    copy.wait()


def _ffn_steps(x_ref, h_ref, load_p, w_up_ref, w_down_ref, g_post_ref, w_gate_ref, b_gate_ref, w_ple_ref,
               y_ref, actbuf):
    for c in range(D_FF // MXU_N):
        cols = slice(c * MXU_N, (c + 1) * MXU_N)
        up = _mm(h_ref[...], w_up_ref[:, cols])
        actbuf[:, cols] = jnp.square(jnp.maximum(up, 0.0)).astype(BF16)
        yield
    ff = []
    for c in range(D_MODEL // MXU_N):
        cols = slice(c * MXU_N, (c + 1) * MXU_N)
        acc = None
        for r in range(D_FF // FF_CHUNK):
            rows = slice(r * FF_CHUNK, (r + 1) * FF_CHUNK)
            part = _mm(actbuf[:, rows], w_down_ref[rows, cols])
            acc = part if acc is None else acc + part
            yield
        ff.append(acc)
    x = x_ref[...] + _rms(jnp.concatenate(ff, axis=1), g_post_ref[...])
    xb = x.astype(BF16)
    ple = _mm(load_p().astype(BF16), w_ple_ref[...])
    yield
    for c in range(D_MODEL // MXU_N):
        cols = slice(c * MXU_N, (c + 1) * MXU_N)
        gate = jax.nn.sigmoid(_mm(xb, w_gate_ref[:, cols]) + b_gate_ref[:, cols])
        y_ref[:, cols] = x[:, cols] + gate * ple[:, cols]
        yield


def _interleave(first, second, pattern):
    gens = {'1': first, '2': second}
    for tag in pattern:
        next(gens[tag], None)
    for _ in first:
        pass
    for _ in second:
        pass


def _cast_weights_to_vmem(srcs, dsts):
    tasks = []
    for src, dst in zip(srcs, dsts):
        if len(src.shape) != 2:
            tasks.append((src, dst, tuple(src.shape)))
            continue
        rows, cols = src.shape
        col_block = cols if cols <= STAGE_MAX_COLS else D_MODEL
        for c0 in range(0, cols, col_block):
            for r0 in range(0, rows, STAGE_ROWS):
                view = (slice(r0, r0 + STAGE_ROWS), slice(c0, c0 + col_block))
                tasks.append((src.at[view], dst.at[view], (STAGE_ROWS, col_block)))
    shapes = sorted(set(shape for _, _, shape in tasks))

    def body(*scoped):
        *stages, sems = scoped
        uses = [0] * len(shapes)
        copies = []
        for src, dst, shape in tasks:
            k = shapes.index(shape)
            slot = uses[k] % 2
            uses[k] += 1
            stage = stages[k].at[slot]
            copies.append((pltpu.make_async_copy(src, stage, sems.at[k, slot]), stage, dst))
        copies[0][0].start()
        for t, (copy, stage, dst) in enumerate(copies):
            if t + 1 < len(copies):
                copies[t + 1][0].start()
            copy.wait()
            dst[...] = stage[...].astype(BF16)

    pl.run_scoped(body, *[pltpu.VMEM((2,) + shape, F32) for shape in shapes],
                  pltpu.SemaphoreType.DMA((len(shapes), 2)))


def _weight_writebacks(bufs, outs, sems):
    return [pltpu.make_async_copy(buf, out, sems.at[n]) for n, (buf, out) in enumerate(zip(bufs, outs))]


N_TABLES = 5
MIXER_MATMUL_W = (1, 3, 7)
FFN_MATMUL_W = (1, 2, 4, 6)
N_MATMUL_W = len(MIXER_MATMUL_W) + len(FFN_MATMUL_W)
STAGE_ROWS = 256
STAGE_MAX_COLS = 1280


N_SAMPLE_TILES = 2


def _prompt_layer_kernel(*refs):
    x_ref, *tables = refs[:1 + N_TABLES]
    (g_pre_mix, w_in, sinks, pool_w, pool_scale, g_attn, g_pool, w_out, g_post_mix, p_ref,
     g_pre_mlp, w_up, w_down, g_post_mlp, w_gate, b_gate, w_ple,
     xs_hbm, hs_hbm, ps_ref) = refs[1 + N_TABLES:21 + N_TABLES]
    y_ref, ys_hbm, knew_ref, vnew_ref, unew_ref = refs[21 + N_TABLES:26 + N_TABLES]
    qbuf, kbuf, vtbuf, ubuf, pbuf, catbuf, actbuf, xmid, hmid, sems = refs[26 + N_TABLES:]
    tm = x_ref.shape[0]
    step = pl.program_id(0)
    last_tile = pl.num_programs(0) - 1 - N_SAMPLE_TILES
    on_sample = step < N_SAMPLE_TILES
    slot = step % 2
    sample_rows = pl.ds(pl.multiple_of(jnp.minimum(step, N_SAMPLE_TILES - 1) * tm, tm), tm)

    @pl.when(on_sample)
    def _():
        loads = [pltpu.make_async_copy(xs_hbm.at[sample_rows], xmid.at[slot], sems.at[0]),
                 pltpu.make_async_copy(hs_hbm.at[sample_rows], hmid.at[slot], sems.at[1])]
        for copy in loads:
            copy.start()
        kbuf[:, 0:WINDOW, :] = jnp.zeros((2 * N_KV_HEADS, WINDOW, LANES), BF16)
        vtbuf[:, 0:WINDOW] = jnp.zeros((KV_COLS, WINDOW), BF16)
        ubuf[0:POOL_PAD, :] = jnp.zeros((POOL_PAD, POOL_WIDTH), F32)
        for copy in loads:
            copy.wait()

    ffn = _ffn_steps(xmid.at[slot], hmid.at[slot], lambda: jnp.where(on_sample, ps_ref[...], p_ref[...]),
                     w_up, w_down, g_post_mlp, w_gate, b_gate, w_ple, y_ref, actbuf)
    mixer = _prompt_mixer_steps(jnp.clip(step - 1, 0, last_tile), x_ref, *tables, g_pre_mix, w_in, sinks, pool_w,
                                pool_scale, g_attn, g_pool, w_out, g_post_mix, knew_ref, vnew_ref, unew_ref,
                                qbuf, kbuf, vtbuf, ubuf, pbuf, catbuf, g_pre_mlp, xmid.at[1 - slot],
                                hmid.at[1 - slot])
    _interleave(ffn, mixer, PROMPT_PATTERN)

    @pl.when(on_sample)
    def _():
        store = pltpu.make_async_copy(y_ref, ys_hbm.at[sample_rows], sems.at[2])
        store.start()
        store.wait()


def _const_spec(shape):
    zeros = (0,) * len(shape)
    return pl.BlockSpec(shape, lambda i: zeros, pipeline_mode=pl.Buffered(1))


def _rope_lane_tables(pos):
    half = HEAD_DIM // 2
    inv = ROPE_THETA ** (-np.arange(half, dtype=np.float64) / half)
    ang = pos[:, None] * inv[None, :]
    reps = LANES // half
    return (np.tile(np.cos(ang), (1, reps)).astype(np.float32), np.tile(np.sin(ang), (1, reps)).astype(np.float32))


def _rope_sign():
    lane = np.arange(LANES)
    return np.where((lane % HEAD_DIM) < HEAD_DIM // 2, -1.0, 1.0).astype(np.float32)[None, :]


def _params(dims):
    return pltpu.CompilerParams(dimension_semantics=dims, vmem_limit_bytes=VMEM_LIMIT)


def _mixer_weight_specs():
    return [
        _const_spec((1, D_MODEL)),
        _const_spec((D_MODEL, IN_COLS)),
        pl.BlockSpec(memory_space=pltpu.SMEM),
        _const_spec((len(POOL_WINDOWS), POOL_CG, POOL_CG)),
        _const_spec((1, POOL_WIDTH)),
        _const_spec((1, Q_COLS)),
        _const_spec((1, POOL_WIDTH)),
        _const_spec((D_MODEL, D_MODEL)),
        _const_spec((1, D_MODEL)),
    ]


def _ffn_weight_specs():
    return [
        _const_spec((1, D_MODEL)),
        _const_spec((D_MODEL, D_FF)),
        _const_spec((D_FF, D_MODEL)),
        _const_spec((1, D_MODEL)),
        _const_spec((D_MODEL, D_MODEL)),
        _const_spec((1, D_MODEL)),
        _const_spec((PLE_DIM, D_MODEL)),
    ]


def _prompt_layer(x, p, xs_mid, hs_mid, ps, mixer_w, ffn_w):
    seq = x.shape[0]
    tm = PROMPT_TILE
    tiles = seq // tm
    assert xs_mid.shape[0] == N_SAMPLE_TILES * tm
    cos_a, sin_a = _rope_lane_tables(np.arange(tiles, dtype=np.float64) * tm)
    cos_b, sin_b = _rope_lane_tables(np.arange(tm, dtype=np.float64))
    mixer_tile = lambda i: (jnp.clip(i - 1, 0, tiles - 1), 0)
    mixer_tile3 = lambda i: (jnp.clip(i - 1, 0, tiles - 1), 0, 0)
    ffn_tile = lambda i: (jnp.clip(i - N_SAMPLE_TILES, 0, tiles - 1), 0)
    sample_tile = lambda i: (jnp.minimum(i, N_SAMPLE_TILES - 1), 0)
    fixed = lambda i: (0, 0)
    in_specs = [
        pl.BlockSpec((tm, D_MODEL), mixer_tile),
        pl.BlockSpec((1, 1, LANES), mixer_tile3),
        pl.BlockSpec((1, 1, LANES), mixer_tile3),
        _const_spec((tm, LANES)),
        _const_spec((tm, LANES)),
        _const_spec((1, LANES)),
    ] + _mixer_weight_specs() + [pl.BlockSpec((tm, PLE_DIM), ffn_tile)] + _ffn_weight_specs() + [
        pl.BlockSpec(memory_space=pl.ANY),
        pl.BlockSpec(memory_space=pl.ANY),
        pl.BlockSpec((tm, PLE_DIM), sample_tile, pipeline_mode=pl.Buffered(1)),
    ]
    return pl.pallas_call(
        _prompt_layer_kernel,
        grid=(tiles + N_SAMPLE_TILES,),
        in_specs=in_specs,
        out_specs=[
            pl.BlockSpec((tm, D_MODEL), ffn_tile),
            pl.BlockSpec(memory_space=pl.ANY),
            pl.BlockSpec((WINDOW, KV_COLS), fixed),
            pl.BlockSpec((WINDOW, KV_COLS), fixed),
            pl.BlockSpec((POOL_BUF, POOL_WIDTH), fixed),
        ],
        out_shape=[
            jax.ShapeDtypeStruct((seq, D_MODEL), F32),
            jax.ShapeDtypeStruct(xs_mid.shape, F32),
            jax.ShapeDtypeStruct((WINDOW, KV_COLS), F32),
            jax.ShapeDtypeStruct((WINDOW, KV_COLS), F32),
            jax.ShapeDtypeStruct((POOL_BUF, POOL_WIDTH), F32),
        ],
        scratch_shapes=[
            pltpu.VMEM((tm, Q_COLS), BF16),
            pltpu.VMEM((2 * N_KV_HEADS, WINDOW + tm, LANES), BF16),
            pltpu.VMEM((KV_COLS, WINDOW + tm), BF16),
            pltpu.VMEM((POOL_PAD + tm, POOL_WIDTH), F32),
            pltpu.VMEM((tm, POOL_WIDTH), F32),
            pltpu.VMEM((tm, D_MODEL), BF16),
            pltpu.VMEM((tm, D_FF), BF16),
            pltpu.VMEM((2, tm, D_MODEL), F32),
            pltpu.VMEM((2, tm, D_MODEL), BF16),
            pltpu.SemaphoreType.DMA((3,)),
        ],
        compiler_params=_params(("arbitrary",)),
        name="prompt_layer",
    )(x, jnp.asarray(cos_a)[:, None, :], jnp.asarray(sin_a)[:, None, :], jnp.asarray(cos_b), jnp.asarray(sin_b),
      jnp.asarray(_rope_sign()), *mixer_w, p, *ffn_w, xs_mid, hs_mid, ps)


def _sample_mixer(x, state_kt, state_vt, state_pool_t, past_len, mixer_w, ffn_w):
    nseq_all, t = state_kt.shape[0], x.shape[0] // state_kt.shape[0]
    nseq = SAMPLE_SEQS
    rows = nseq * t
    steps = nseq_all // nseq
    cos, sin = _rope_lane_tables(past_len + np.arange(rows, dtype=np.float64) % t)
    sin = sin * _rope_sign()
    row = lambda i: (i, 0)
    seq3 = lambda i: (i, 0, 0)
    pos3 = lambda i: (0, i, 0)
    hbm = pl.BlockSpec(memory_space=pl.ANY)
    mixer_specs = _mixer_weight_specs()
    for n in MIXER_MATMUL_W:
        mixer_specs[n] = hbm
    g_pre_mlp = ffn_w[0]
    ffn_matmul_w = [ffn_w[n] for n in FFN_MATMUL_W]
    matmul_w = [mixer_w[n] for n in MIXER_MATMUL_W] + ffn_matmul_w
    n_chunks = sum(w.size for w in ffn_matmul_w) // (STAGE_ROWS * D_MODEL)
    chunks_per_step = -(-n_chunks // steps)
    outs = pl.pallas_call(
        functools.partial(_sample_mixer_kernel, steps),
        grid=(steps,),
        in_specs=[
            pl.BlockSpec((rows, D_MODEL), row),
            _const_spec((rows, LANES)),
            _const_spec((rows, LANES)),
            pl.BlockSpec((nseq, KV_COLS, WINDOW), seq3),
            pl.BlockSpec((nseq, KV_COLS, WINDOW), seq3),
            pl.BlockSpec((POOL_BUF, nseq, POOL_WIDTH), pos3),
        ] + mixer_specs + [_const_spec((1, D_MODEL))] + [hbm] * len(ffn_matmul_w),
        out_specs=[
            pl.BlockSpec((rows, D_MODEL), row),
            pl.BlockSpec((rows, D_MODEL), row),
            pl.BlockSpec((nseq, KV_COLS, WINDOW), seq3),
            pl.BlockSpec((nseq, KV_COLS, WINDOW), seq3),
            pl.BlockSpec((POOL_BUF, nseq, POOL_WIDTH), pos3),
        ] + [hbm] * N_MATMUL_W,
        out_shape=[
            jax.ShapeDtypeStruct((nseq_all * t, D_MODEL), F32),
            jax.ShapeDtypeStruct((nseq_all * t, D_MODEL), BF16),
            jax.ShapeDtypeStruct((nseq_all, KV_COLS, WINDOW), F32),
            jax.ShapeDtypeStruct((nseq_all, KV_COLS, WINDOW), F32),
            jax.ShapeDtypeStruct((POOL_BUF, nseq_all, POOL_WIDTH), F32),
        ] + [jax.ShapeDtypeStruct(w.shape, BF16) for w in matmul_w],
        scratch_shapes=[pltpu.VMEM((len(POOL_WINDOWS), nseq * (POOL_PAD + t), POOL_CG), F32)]
        + [pltpu.VMEM(mixer_w[n].shape, BF16) for n in MIXER_MATMUL_W] + [
            pltpu.SemaphoreType.DMA((len(MIXER_MATMUL_W),)),
            pltpu.VMEM((chunks_per_step, STAGE_ROWS, D_MODEL), F32),
            pltpu.VMEM((chunks_per_step, STAGE_ROWS, D_MODEL), BF16),
            pltpu.SemaphoreType.DMA((chunks_per_step,)),
            pltpu.SemaphoreType.DMA((chunks_per_step,)),
        ],
        compiler_params=_params(("arbitrary",)),
        name="sample_mixer",
    )(x, jnp.asarray(cos), jnp.asarray(sin), state_kt, state_vt, state_pool_t, *mixer_w, g_pre_mlp, *ffn_matmul_w)
    return outs[:5], outs[5:]


def kernel(x_prompt, x_sample, state_k, state_v, state_pool, p_prompt, p_sample, w_in, attn_sinks, pool_w,
           pool_scale, g_attn_out, g_pool_out, w_out, g_pre_mix, g_post_mix, g_pre_mlp, g_post_mlp, w_up, w_down,
           w_ple, w_ple_gate, b_ple_gate):
    depth = w_in.shape[0]
    batch, seq, _ = x_prompt.shape
    dec_batch, dec_seq, _ = x_sample.shape
    assert depth == 1 and batch == 1

    xp = x_prompt.reshape(seq, D_MODEL)
    xs = x_sample.reshape(dec_batch * dec_seq, D_MODEL)
    i = 0
    mixer_w = [
        g_pre_mix[i][None, :], w_in[i], attn_sinks[i], pool_w[i], pool_scale[i][None, :], g_attn_out[i][None, :],
        g_pool_out[i][None, :], w_out[i], g_post_mix[i][None, :],
    ]
    ffn_w = [
        g_pre_mlp[i][None, :], w_up[i], w_down[i], g_post_mlp[i][None, :], w_ple_gate[i], b_ple_gate[i][None, :],
        w_ple[i],
    ]

    (xs_mid, hs_mid, ks_t, vs_t, us_t), w_bf16 = _sample_mixer(
        xs, state_k[i].reshape(dec_batch, WINDOW, KV_COLS).transpose(0, 2, 1),
        state_v[i].reshape(dec_batch, WINDOW, KV_COLS).transpose(0, 2, 1),
        state_pool[i].transpose(1, 0, 2), float(PAST_LEN), mixer_w, ffn_w)
    ks, vs, us = ks_t.transpose(0, 2, 1), vs_t.transpose(0, 2, 1), us_t.transpose(1, 0, 2)
    for n, w in zip(MIXER_MATMUL_W, w_bf16[:len(MIXER_MATMUL_W)]):
        mixer_w[n] = w
    for n, w in zip(FFN_MATMUL_W, w_bf16[len(MIXER_MATMUL_W):]):
        ffn_w[n] = w
    yp, ys, kp, vp, up = _prompt_layer(xp, p_prompt[i, 0], xs_mid, hs_mid,
                                       p_sample[i].reshape(dec_batch * dec_seq, PLE_DIM), mixer_w, ffn_w)

    return (
        yp.reshape(batch, seq, D_MODEL),
        ys.reshape(dec_batch, dec_seq, D_MODEL),
        kp.T.reshape(depth, batch, WINDOW, N_KV_HEADS, HEAD_DIM),
        vp.T.reshape(depth, batch, WINDOW, N_KV_HEADS, HEAD_DIM),
        up.reshape(depth, batch, POOL_BUF, POOL_WIDTH),
        ks.reshape(depth, dec_batch, WINDOW, N_KV_HEADS, HEAD_DIM),
        vs.reshape(depth, dec_batch, WINDOW, N_KV_HEADS, HEAD_DIM),
        us.reshape(depth, dec_batch, POOL_BUF, POOL_WIDTH),
    )
```

```python
import functools

import jax
import jax.numpy as jnp
import numpy as np
from jax import lax
from jax.experimental import pallas as pl
from jax.experimental.pallas import tpu as pltpu

D_MODEL = 1024
HEAD_DIM = 64
N_HEADS = 8
N_KV_HEADS = 2
GROUP = N_HEADS // N_KV_HEADS
WINDOW = 128
Q_COLS = N_HEADS * HEAD_DIM
KV_COLS = N_KV_HEADS * HEAD_DIM
POOL_WIDTH = 512
POOL_WINDOWS = (2, 4, 8, 16)
POOL_CG = POOL_WIDTH // len(POOL_WINDOWS)
POOL_BUF = max(POOL_WINDOWS) - 1
POOL_PAD = POOL_BUF + 1
IN_COLS = Q_COLS + 2 * KV_COLS + POOL_WIDTH
D_FF = 4 * D_MODEL
FF_CHUNK = 1024
PLE_DIM = 256
ROPE_THETA = 10000.0
PAST_LEN = 16384
EPS = 1e-6
NEG = -1e30
LANES = 128
MXU_N = 256

ATTN_LOOKAHEAD = 4
PROMPT_PATTERN = "11" + "21" * (5 + ATTN_LOOKAHEAD + 16 + 4) + "11" + "22" + "11" + "222" + "11"

PROMPT_TILE = 512
SAMPLE_SEQS = 32
VMEM_LIMIT = 58 * 1024 * 1024

BF16 = jnp.bfloat16
F32 = jnp.float32


def _rms(x, g):
    y = x * lax.rsqrt(jnp.mean(x * x, axis=-1, keepdims=True) + EPS)
    return y * g


def _mm(a, w):
    return jnp.dot(a, w, preferred_element_type=F32)


def _rope_tables(cos_a, sin_a, cos_b, sin_b, sign):
    cos_t = cos_a * cos_b - sin_a * sin_b
    sin_t = (sin_a * cos_b + cos_a * sin_b) * sign
    return cos_t, sin_t


def _rope(x, cos_t, sin_t, first_half):
    n = x.shape[1] // LANES
    width = x.shape[1]
    partner = jnp.where(first_half, pltpu.roll(x, width - HEAD_DIM // 2, axis=1), pltpu.roll(x, HEAD_DIM // 2, axis=1))
    if n > 1:
        cos_t = jnp.concatenate([cos_t] * n, axis=1)
        sin_t = jnp.concatenate([sin_t] * n, axis=1)
    return x * cos_t + partner * sin_t


def _first_half_mask(rows, width):
    lane = lax.broadcasted_iota(jnp.int32, (rows, width), 1)
    return (lane % HEAD_DIM) < (HEAD_DIM // 2)


def _project(x, g_pre, w_in, tables):
    rows = x.shape[0]
    h = _rms(x, g_pre).astype(BF16)
    proj = _mm(h, w_in)
    cos_t, sin_t = tables
    q = _rope(proj[:, :Q_COLS], cos_t, sin_t, _first_half_mask(rows, Q_COLS)) * (HEAD_DIM ** -0.5)
    k = _rope(proj[:, Q_COLS:Q_COLS + KV_COLS], cos_t, sin_t, _first_half_mask(rows, KV_COLS))
    v = proj[:, Q_COLS + KV_COLS:Q_COLS + 2 * KV_COLS]
    u = proj[:, Q_COLS + 2 * KV_COLS:]
    return q, k, v, u


def _pool_out(win, tok, inv_cnt, pool_w_ref, pool_scale):
    outs = []
    for g in range(len(POOL_WINDOWS)):
        r = win[g] * inv_cnt[g] - tok[:, g * POOL_CG:(g + 1) * POOL_CG]
        outs.append(_mm(r.astype(BF16), pool_w_ref[g]))
    return jnp.concatenate(outs, axis=1) * pool_scale


def _mix_out(x, attn, pooled, g_attn, g_pool, w_out, g_post):
    cat = jnp.concatenate([_rms(attn, g_attn), _rms(pooled, g_pool)], axis=1).astype(BF16)
    return x + _rms(_mm(cat, w_out), g_post)


def _softmax_parts(s_list, sink):
    m = sink
    for s in s_list:
        m = jnp.maximum(m, jnp.max(s, axis=-1, keepdims=True))
    e_list = [jnp.exp(s - m) for s in s_list]
    den = jnp.exp(sink - m)
    for e in e_list:
        den = den + jnp.sum(e, axis=-1, keepdims=True)
    return e_list, den


def _prompt_mixer_steps(tile, x_ref, cos_a_ref, sin_a_ref, cos_b_ref, sin_b_ref, sign_ref, g_pre_ref, w_in_ref,
                        sinks_ref, pool_w_ref, pool_scale_ref, g_attn_ref, g_pool_ref, w_out_ref, g_post_ref,
                        knew_ref, vnew_ref, unew_ref, qbuf, kbuf, vtbuf, ubuf, pbuf, catbuf,
                        g_pre_mlp_ref, xmid_ref, hmid_ref):
    tm = x_ref.shape[0]
    nblk = tm // WINDOW
    half_lanes = LANES // 2
    base = tile * tm

    h = _rms(x_ref[...], g_pre_ref[...]).astype(BF16)
    cos_t, sin_t = _rope_tables(cos_a_ref[0], sin_a_ref[0], cos_b_ref[...], sin_b_ref[...], sign_ref[...])

    kv = _mm(h, w_in_ref[:, Q_COLS:Q_COLS + 2 * KV_COLS])
    k = _rope(kv[:, :KV_COLS], cos_t, sin_t, _first_half_mask(tm, KV_COLS))
    v = kv[:, KV_COLS:]
    knew_ref[...] = k[tm - WINDOW:, :].T
    vnew_ref[...] = v[tm - WINDOW:, :].T
    low = lax.broadcasted_iota(jnp.int32, (tm, LANES), 1) < half_lanes
    k_swapped = pltpu.roll(k, half_lanes, axis=1)
    kbuf[0, WINDOW:, :] = jnp.where(low, k, 0.0).astype(BF16)
    kbuf[1, WINDOW:, :] = jnp.where(low, 0.0, k_swapped).astype(BF16)
    kbuf[2, WINDOW:, :] = jnp.where(low, k_swapped, 0.0).astype(BF16)
    kbuf[3, WINDOW:, :] = jnp.where(low, 0.0, k).astype(BF16)
    vtbuf[:, WINDOW:] = v.T.astype(BF16)
    yield

    u_col0 = Q_COLS + 2 * KV_COLS
    for c in range(Q_COLS // MXU_N):
        cols = slice(c * MXU_N, (c + 1) * MXU_N)
        q = _rope(_mm(h, w_in_ref[:, cols]), cos_t, sin_t, _first_half_mask(tm, MXU_N)) * (HEAD_DIM ** -0.5)
        qbuf[:, cols] = q.astype(BF16)
        yield
    for c in range(POOL_WIDTH // MXU_N):
        cols = slice(c * MXU_N, (c + 1) * MXU_N)
        ubuf[POOL_PAD:, cols] = _mm(h, w_in_ref[:, u_col0 + c * MXU_N:u_col0 + (c + 1) * MXU_N])
        yield

    pos = base + lax.broadcasted_iota(jnp.int32, (tm, 1), 0)
    pool_scale = pool_scale_ref[...]
    for g, w in enumerate(POOL_WINDOWS):
        cols = slice(g * POOL_CG, (g + 1) * POOL_CG)
        a = ubuf[:, cols]
        shift = 1
        while shift < w:
            a = a + pltpu.roll(a, shift, axis=0)
            shift *= 2
        inv_cnt = 1.0 / jnp.minimum(pos + 1, w).astype(F32)
        r = a[POOL_PAD:, :] * inv_cnt - ubuf[POOL_PAD:, cols]
        pbuf[:, cols] = _mm(r.astype(BF16), pool_w_ref[g]) * pool_scale[:, cols]
        yield
    catbuf[:, Q_COLS:] = _rms(pbuf[...], g_pool_ref[...]).astype(BF16)

    c_i = lax.broadcasted_iota(jnp.int32, (2 * WINDOW, 2 * WINDOW), 0)
    r_i = lax.broadcasted_iota(jnp.int32, (2 * WINDOW, 2 * WINDOW), 1) % WINDOW
    d_i = c_i - r_i
    band = (d_i >= 1) & (d_i <= WINDOW)
    band0 = band & (c_i >= WINDOW - base)

    def scores(j, h_kv, par):
        rows = slice(j * WINDOW, (j + 1) * WINDOW)
        keys = slice(j * WINDOW, (j + 2) * WINDOW)
        q2 = jnp.concatenate([qbuf[rows, (2 * h_kv) * LANES:(2 * h_kv + 1) * LANES],
                              qbuf[rows, (2 * h_kv + 1) * LANES:(2 * h_kv + 2) * LANES]], axis=0)
        return lax.dot_general(kbuf[2 * h_kv + par, keys, :], q2, (((1,), (1,)), ((), ())),
                               preferred_element_type=F32)

    units = [(j, h_kv, par) for j in range(nblk) for h_kv in range(N_KV_HEADS) for par in range(2)]
    pending = []
    for n in range(min(ATTN_LOOKAHEAD, len(units))):
        pending.append(scores(*units[n]))
        yield
    heads = [None] * N_HEADS
    for n, (j, h_kv, par) in enumerate(units):
        st = pending.pop(0)
        if n + ATTN_LOOKAHEAD < len(units):
            pending.append(scores(*units[n + ATTN_LOOKAHEAD]))
        mask = band0 if j == 0 else band
        st = jnp.where(mask, st, NEG)
        sink = jnp.concatenate([jnp.full((1, WINDOW), sinks_ref[GROUP * h_kv + par], F32),
                                jnp.full((1, WINDOW), sinks_ref[GROUP * h_kv + 2 + par], F32)], axis=1)
        m = jnp.maximum(jnp.max(st, axis=0, keepdims=True), sink)
        e = jnp.exp(st - m)
        den = jnp.sum(e, axis=0, keepdims=True) + jnp.exp(sink - m)
        vt = vtbuf[h_kv * HEAD_DIM:(h_kv + 1) * HEAD_DIM, j * WINDOW:(j + 2) * WINDOW]
        ot = _mm(vt, e.astype(BF16)) / den
        heads[GROUP * h_kv + par] = ot[:, :WINDOW]
        heads[GROUP * h_kv + 2 + par] = ot[:, WINDOW:]
        if h_kv == N_KV_HEADS - 1 and par == 1:
            attn = jnp.concatenate(heads, axis=0).T
            catbuf[j * WINDOW:(j + 1) * WINDOW, :Q_COLS] = _rms(attn, g_attn_ref[...]).astype(BF16)
        yield

    mix = []
    for c in range(D_MODEL // MXU_N):
        mix.append(_mm(catbuf[...], w_out_ref[:, c * MXU_N:(c + 1) * MXU_N]))
        yield
    x_mid = x_ref[...] + _rms(jnp.concatenate(mix, axis=1), g_post_ref[...])

    unew_ref[...] = ubuf[tm + 1:tm + POOL_PAD, :]
    kbuf[:, 0:WINDOW, :] = kbuf[:, tm:tm + WINDOW, :]
    vtbuf[:, 0:WINDOW] = vtbuf[:, tm:tm + WINDOW]
    ubuf[0:POOL_PAD, :] = ubuf[tm:tm + POOL_PAD, :]
    xmid_ref[...] = x_mid
    hmid_ref[...] = _rms(x_mid, g_pre_mlp_ref[...]).astype(BF16)


def _ffn_weight_chunks(srcs, dsts):
    chunks = []
    for src, dst in zip(srcs, dsts):
        rows, cols = src.shape
        for c0 in range(0, cols, D_MODEL):
            for r0 in range(0, rows, STAGE_ROWS):
                view = (slice(r0, r0 + STAGE_ROWS), slice(c0, c0 + D_MODEL))
                chunks.append((src.at[view], dst.at[view]))
    return chunks


def _sample_mixer_kernel(n_steps, x_ref, cos_ref, sin_ref, skt_ref, svt_ref, sp_ref, g_pre_ref, w_in_hbm,
                         sinks_ref, pool_w_hbm, pool_scale_ref, g_attn_ref, g_pool_ref, w_out_hbm, g_post_ref,
                         g_pre_mlp_ref, w_up_hbm, w_down_hbm, w_gate_hbm, w_ple_hbm,
                         xmid_ref, hmid_ref, knew_ref, vnew_ref, unew_ref,
                         w_in_out, pool_w_out, w_out_out, w_up_out, w_down_out, w_gate_out, w_ple_out,
                         ubuf, w_in_ref, pool_w_ref, w_out_ref, out_sems, fstage, bstage, in_sems, chunk_sems):
    step = pl.program_id(0)
    w_bufs = [w_in_ref, pool_w_ref, w_out_ref]
    w_outs = [w_in_out, pool_w_out, w_out_out]
    chunks = _ffn_weight_chunks([w_up_hbm, w_down_hbm, w_gate_hbm, w_ple_hbm],
                                [w_up_out, w_down_out, w_gate_out, w_ple_out])
    per_step = fstage.shape[0]
    assert per_step * n_steps >= len(chunks)

    def chunk_loads(i):
        return [pltpu.make_async_copy(src, fstage.at[n], in_sems.at[n])
                for n, (src, _) in enumerate(chunks[i * per_step:(i + 1) * per_step])]

    def chunk_stores(i):
        return [pltpu.make_async_copy(bstage.at[n], dst, chunk_sems.at[n])
                for n, (_, dst) in enumerate(chunks[i * per_step:(i + 1) * per_step])]

    @pl.when(step == 0)
    def _():
        _cast_weights_to_vmem([w_in_hbm, pool_w_hbm, w_out_hbm], w_bufs)
        for copy in _weight_writebacks(w_bufs, w_outs, out_sems):
            copy.start()

    for i in range(n_steps):
        @pl.when(step == i)
        def _(i=i):
            for copy in chunk_loads(i):
                copy.start()

    rows = x_ref.shape[0]
    nseq = skt_ref.shape[0]
    t = rows // nseq
    x = x_ref[...]
    q, k, v, u = _project(x, g_pre_ref[...], w_in_ref[...], (cos_ref[...], sin_ref[...]))

    k3 = k.reshape(nseq, t, KV_COLS)
    v3 = v.reshape(nseq, t, KV_COLS)
    sk = jnp.swapaxes(skt_ref[...], 1, 2)
    sv = jnp.swapaxes(svt_ref[...], 1, 2)
    knew_ref[...] = jnp.swapaxes(jnp.concatenate([sk[:, t:, :], k3], axis=1), 1, 2)
    vnew_ref[...] = jnp.swapaxes(jnp.concatenate([sv[:, t:, :], v3], axis=1), 1, 2)

    r_i = lax.broadcasted_iota(jnp.int32, (GROUP * t, WINDOW), 0) % t
    c_i = lax.broadcasted_iota(jnp.int32, (GROUP * t, WINDOW), 1)
    mask_old = (c_i > r_i)[None]
    r_n = lax.broadcasted_iota(jnp.int32, (GROUP * t, t), 0) % t
    c_n = lax.broadcasted_iota(jnp.int32, (GROUP * t, t), 1)
    mask_new = (c_n <= r_n)[None]

    qb = q.astype(BF16)
    kb = k3.astype(BF16)
    vb = v3.astype(BF16)
    heads = [None] * N_HEADS
    for h in range(N_KV_HEADS):
        lanes = slice(h * HEAD_DIM, (h + 1) * HEAD_DIM)
        q4 = jnp.concatenate(
            [qb[:, (GROUP * h + g) * HEAD_DIM:(GROUP * h + g + 1) * HEAD_DIM].reshape(nseq, t, HEAD_DIM)
             for g in range(GROUP)], axis=1)
        k_old = sk[:, :, lanes].astype(BF16)
        v_old = sv[:, :, lanes].astype(BF16)
        s_old = jnp.einsum('bqd,bkd->bqk', q4, k_old, preferred_element_type=F32)
        s_new = jnp.einsum('bqd,bkd->bqk', q4, kb[:, :, lanes], preferred_element_type=F32)
        s_old = jnp.where(mask_old, s_old, NEG)
        s_new = jnp.where(mask_new, s_new, NEG)
        sink = jnp.concatenate(
            [jnp.full((1, t, 1), sinks_ref[GROUP * h + g], F32) for g in range(GROUP)], axis=1)
        (e_old, e_new), den = _softmax_parts([s_old, s_new], sink)
        o = jnp.einsum('bqk,bkd->bqd', e_old.astype(BF16), v_old, preferred_element_type=F32)
        o = o + jnp.einsum('bqk,bkd->bqd', e_new.astype(BF16), vb[:, :, lanes], preferred_element_type=F32)
        o = o / den
        for g in range(GROUP):
            heads[GROUP * h + g] = o[:, g * t:(g + 1) * t, :].reshape(rows, HEAD_DIM)
    attn = jnp.concatenate(heads, axis=1)

    ext = POOL_PAD + t
    per_seq = lambda j: pl.ds(j, nseq, stride=ext)
    win = []
    for g, w in enumerate(POOL_WINDOWS):
        cols = slice(g * POOL_CG, (g + 1) * POOL_CG)
        ubuf[g, per_seq(0), :] = jnp.zeros((nseq, POOL_CG), F32)
        for j in range(POOL_BUF):
            ubuf[g, per_seq(1 + j), :] = sp_ref[j, :, cols]
        for s_i in range(nseq):
            ubuf[g, s_i * ext + POOL_PAD:(s_i + 1) * ext, :] = u[s_i * t:(s_i + 1) * t, cols]
        for j in range(POOL_BUF):
            unew_ref[j, :, cols] = ubuf[g, per_seq(t + 1 + j), :]
        a = ubuf[g]
        shift = 1
        while shift < w:
            a = a + pltpu.roll(a, shift, axis=0)
            shift *= 2
        win.append(a.reshape(nseq, ext, POOL_CG)[:, POOL_PAD:, :].reshape(rows, POOL_CG))
    inv_cnt = [1.0 / w for w in POOL_WINDOWS]
    pooled = _pool_out(win, u, inv_cnt, pool_w_ref, pool_scale_ref[...])

    x_mid = _mix_out(x, attn, pooled, g_attn_ref[...], g_pool_ref[...], w_out_ref[...], g_post_ref[...])
    xmid_ref[...] = x_mid
    hmid_ref[...] = _rms(x_mid, g_pre_mlp_ref[...]).astype(BF16)

    for i in range(n_steps):
        @pl.when(step == i)
        def _(i=i):
            previous = chunk_stores(i - 1) if i > 0 else []
            loads, stores = chunk_loads(i), chunk_stores(i)
            for n in range(max(len(loads), len(previous))):
                if n < len(previous):
                    previous[n].wait()
                if n < len(loads):
                    loads[n].wait()
                    bstage[n] = fstage[n].astype(BF16)
                    stores[n].start()
            if i == n_steps - 1:
                for copy in stores + _weight_writebacks(w_bufs, w_outs, out_sems):
                    copy.wait()


def _ffn_steps(x_ref, h_ref, load_p, w_up_ref, w_down_ref, g_post_ref, w_gate_ref, b_gate_ref, w_ple_ref,
               y_ref, actbuf):
    for c in range(D_FF // MXU_N):
        cols = slice(c * MXU_N, (c + 1) * MXU_N)
        up = _mm(h_ref[...], w_up_ref[:, cols])
        actbuf[:, cols] = jnp.square(jnp.maximum(up, 0.0)).astype(BF16)
        yield
    ff = []
    n_col, n_row = D_MODEL // MXU_N, D_FF // FF_CHUNK
    for c in range(n_col):
        cols = slice(c * MXU_N, (c + 1) * MXU_N)
        acc = None
        for r in range(n_row):
            rows = slice(r * FF_CHUNK, (r + 1) * FF_CHUNK)
            part = _mm(actbuf[:, rows], w_down_ref[rows, cols])
            acc = part if acc is None else acc + part
            if (c, r) != (n_col - 1, n_row - 1):
                yield
        ff.append(acc)
    x = x_ref[...] + _rms(jnp.concatenate(ff, axis=1), g_post_ref[...])
    xb = x.astype(BF16)
    yield
    ple = _mm(load_p().astype(BF16), w_ple_ref[...])
    yield
    for c in range(D_MODEL // MXU_N):
        cols = slice(c * MXU_N, (c + 1) * MXU_N)
        gate = jax.nn.sigmoid(_mm(xb, w_gate_ref[:, cols]) + b_gate_ref[:, cols])
        y_ref[:, cols] = x[:, cols] + gate * ple[:, cols]
        yield


def _interleave(first, second, pattern):
    gens = {'1': first, '2': second}
    for tag in pattern:
        next(gens[tag], None)
    for _ in first:
        pass
    for _ in second:
        pass


def _cast_weights_to_vmem(srcs, dsts):
    tasks = []
    for src, dst in zip(srcs, dsts):
        if len(src.shape) != 2:
            tasks.append((src, dst, tuple(src.shape)))
            continue
        rows, cols = src.shape
        col_block = cols if cols <= STAGE_MAX_COLS else D_MODEL
        for c0 in range(0, cols, col_block):
            for r0 in range(0, rows, STAGE_ROWS):
                view = (slice(r0, r0 + STAGE_ROWS), slice(c0, c0 + col_block))
                tasks.append((src.at[view], dst.at[view], (STAGE_ROWS, col_block)))
    shapes = sorted(set(shape for _, _, shape in tasks))

    def body(*scoped):
        *stages, sems = scoped
        uses = [0] * len(shapes)
        copies = []
        for src, dst, shape in tasks:
            k = shapes.index(shape)
            slot = uses[k] % 2
            uses[k] += 1
            stage = stages[k].at[slot]
            copies.append((pltpu.make_async_copy(src, stage, sems.at[k, slot]), stage, dst))
        copies[0][0].start()
        for t, (copy, stage, dst) in enumerate(copies):
            if t + 1 < len(copies):
                copies[t + 1][0].start()
            copy.wait()
            dst[...] = stage[...].astype(BF16)

    pl.run_scoped(body, *[pltpu.VMEM((2,) + shape, F32) for shape in shapes],
                  pltpu.SemaphoreType.DMA((len(shapes), 2)))


def _weight_writebacks(bufs, outs, sems):
    return [pltpu.make_async_copy(buf, out, sems.at[n]) for n, (buf, out) in enumerate(zip(bufs, outs))]


N_TABLES = 5
MIXER_MATMUL_W = (1, 3, 7)
FFN_MATMUL_W = (1, 2, 4, 6)
N_MATMUL_W = len(MIXER_MATMUL_W) + len(FFN_MATMUL_W)
STAGE_ROWS = 256
STAGE_MAX_COLS = 1280


N_SAMPLE_TILES = 2


def _prompt_layer_kernel(*refs):
    x_ref, *tables = refs[:1 + N_TABLES]
    (g_pre_mix, w_in, sinks, pool_w, pool_scale, g_attn, g_pool, w_out, g_post_mix, p_ref,
     g_pre_mlp, w_up, w_down, g_post_mlp, w_gate, b_gate, w_ple,
     xs_hbm, hs_hbm, ps_ref) = refs[1 + N_TABLES:21 + N_TABLES]
    y_ref, ys_hbm, knew_ref, vnew_ref, unew_ref = refs[21 + N_TABLES:26 + N_TABLES]
    qbuf, kbuf, vtbuf, ubuf, pbuf, catbuf, actbuf, xmid, hmid, sems = refs[26 + N_TABLES:]
    tm = x_ref.shape[0]
    step = pl.program_id(0)
    last_tile = pl.num_programs(0) - 1 - N_SAMPLE_TILES
    on_sample = step < N_SAMPLE_TILES
    slot = step % 2
    sample_rows = pl.ds(pl.multiple_of(jnp.minimum(step, N_SAMPLE_TILES - 1) * tm, tm), tm)

    @pl.when(on_sample)
    def _():
        loads = [pltpu.make_async_copy(xs_hbm.at[sample_rows], xmid.at[slot], sems.at[0]),
                 pltpu.make_async_copy(hs_hbm.at[sample_rows], hmid.at[slot], sems.at[1])]
        for copy in loads:
            copy.start()
        kbuf[:, 0:WINDOW, :] = jnp.zeros((2 * N_KV_HEADS, WINDOW, LANES), BF16)
        vtbuf[:, 0:WINDOW] = jnp.zeros((KV_COLS, WINDOW), BF16)
        ubuf[0:POOL_PAD, :] = jnp.zeros((POOL_PAD, POOL_WIDTH), F32)
        for copy in loads:
            copy.wait()

    ffn = _ffn_steps(xmid.at[slot], hmid.at[slot], lambda: jnp.where(on_sample, ps_ref[...], p_ref[...]),
                     w_up, w_down, g_post_mlp, w_gate, b_gate, w_ple, y_ref, actbuf)
    mixer = _prompt_mixer_steps(jnp.clip(step - 1, 0, last_tile), x_ref, *tables, g_pre_mix, w_in, sinks, pool_w,
                                pool_scale, g_attn, g_pool, w_out, g_post_mix, knew_ref, vnew_ref, unew_ref,
                                qbuf, kbuf, vtbuf, ubuf, pbuf, catbuf, g_pre_mlp, xmid.at[1 - slot],
                                hmid.at[1 - slot])
    _interleave(ffn, mixer, PROMPT_PATTERN)

    @pl.when(on_sample)
    def _():
        store = pltpu.make_async_copy(y_ref, ys_hbm.at[sample_rows], sems.at[2])
        store.start()
        store.wait()


def _const_spec(shape):
    zeros = (0,) * len(shape)
    return pl.BlockSpec(shape, lambda i: zeros, pipeline_mode=pl.Buffered(1))


def _rope_lane_tables(pos):
    half = HEAD_DIM // 2
    inv = ROPE_THETA ** (-np.arange(half, dtype=np.float64) / half)
    ang = pos[:, None] * inv[None, :]
    reps = LANES // half
    return (np.tile(np.cos(ang), (1, reps)).astype(np.float32), np.tile(np.sin(ang), (1, reps)).astype(np.float32))


def _rope_sign():
    lane = np.arange(LANES)
    return np.where((lane % HEAD_DIM) < HEAD_DIM // 2, -1.0, 1.0).astype(np.float32)[None, :]


def _params(dims):
    return pltpu.CompilerParams(dimension_semantics=dims, vmem_limit_bytes=VMEM_LIMIT)


def _mixer_weight_specs():
    return [
        _const_spec((1, D_MODEL)),
        _const_spec((D_MODEL, IN_COLS)),
        pl.BlockSpec(memory_space=pltpu.SMEM),
        _const_spec((len(POOL_WINDOWS), POOL_CG, POOL_CG)),
        _const_spec((1, POOL_WIDTH)),
        _const_spec((1, Q_COLS)),
        _const_spec((1, POOL_WIDTH)),
        _const_spec((D_MODEL, D_MODEL)),
        _const_spec((1, D_MODEL)),
    ]


def _ffn_weight_specs():
    return [
        _const_spec((1, D_MODEL)),
        _const_spec((D_MODEL, D_FF)),
        _const_spec((D_FF, D_MODEL)),
        _const_spec((1, D_MODEL)),
        _const_spec((D_MODEL, D_MODEL)),
        _const_spec((1, D_MODEL)),
        _const_spec((PLE_DIM, D_MODEL)),
    ]


def _prompt_layer(x, p, xs_mid, hs_mid, ps, mixer_w, ffn_w):
    seq = x.shape[0]
    tm = PROMPT_TILE
    tiles = seq // tm
    assert xs_mid.shape[0] == N_SAMPLE_TILES * tm
    cos_a, sin_a = _rope_lane_tables(np.arange(tiles, dtype=np.float64) * tm)
    cos_b, sin_b = _rope_lane_tables(np.arange(tm, dtype=np.float64))
    mixer_tile = lambda i: (jnp.clip(i - 1, 0, tiles - 1), 0)
    mixer_tile3 = lambda i: (jnp.clip(i - 1, 0, tiles - 1), 0, 0)
    ffn_tile = lambda i: (jnp.clip(i - N_SAMPLE_TILES, 0, tiles - 1), 0)
    sample_tile = lambda i: (jnp.minimum(i, N_SAMPLE_TILES - 1), 0)
    fixed = lambda i: (0, 0)
    in_specs = [
        pl.BlockSpec((tm, D_MODEL), mixer_tile),
        pl.BlockSpec((1, 1, LANES), mixer_tile3),
        pl.BlockSpec((1, 1, LANES), mixer_tile3),
        _const_spec((tm, LANES)),
        _const_spec((tm, LANES)),
        _const_spec((1, LANES)),
    ] + _mixer_weight_specs() + [pl.BlockSpec((tm, PLE_DIM), ffn_tile)] + _ffn_weight_specs() + [
        pl.BlockSpec(memory_space=pl.ANY),
        pl.BlockSpec(memory_space=pl.ANY),
        pl.BlockSpec((tm, PLE_DIM), sample_tile, pipeline_mode=pl.Buffered(1)),
    ]
    return pl.pallas_call(
        _prompt_layer_kernel,
        grid=(tiles + N_SAMPLE_TILES,),
        in_specs=in_specs,
        out_specs=[
            pl.BlockSpec((tm, D_MODEL), ffn_tile),
            pl.BlockSpec(memory_space=pl.ANY),
            pl.BlockSpec((WINDOW, KV_COLS), fixed),
            pl.BlockSpec((WINDOW, KV_COLS), fixed),
            pl.BlockSpec((POOL_BUF, POOL_WIDTH), fixed),
        ],
        out_shape=[
            jax.ShapeDtypeStruct((seq, D_MODEL), F32),
            jax.ShapeDtypeStruct(xs_mid.shape, F32),
            jax.ShapeDtypeStruct((WINDOW, KV_COLS), F32),
            jax.ShapeDtypeStruct((WINDOW, KV_COLS), F32),
            jax.ShapeDtypeStruct((POOL_BUF, POOL_WIDTH), F32),
        ],
        scratch_shapes=[
            pltpu.VMEM((tm, Q_COLS), BF16),
            pltpu.VMEM((2 * N_KV_HEADS, WINDOW + tm, LANES), BF16),
            pltpu.VMEM((KV_COLS, WINDOW + tm), BF16),
            pltpu.VMEM((POOL_PAD + tm, POOL_WIDTH), F32),
            pltpu.VMEM((tm, POOL_WIDTH), F32),
            pltpu.VMEM((tm, D_MODEL), BF16),
            pltpu.VMEM((tm, D_FF), BF16),
            pltpu.VMEM((2, tm, D_MODEL), F32),
            pltpu.VMEM((2, tm, D_MODEL), BF16),
            pltpu.SemaphoreType.DMA((3,)),
        ],
        compiler_params=_params(("arbitrary",)),
        name="prompt_layer",
    )(x, jnp.asarray(cos_a)[:, None, :], jnp.asarray(sin_a)[:, None, :], jnp.asarray(cos_b), jnp.asarray(sin_b),
      jnp.asarray(_rope_sign()), *mixer_w, p, *ffn_w, xs_mid, hs_mid, ps)


def _sample_mixer(x, state_kt, state_vt, state_pool_t, past_len, mixer_w, ffn_w):
    nseq_all, t = state_kt.shape[0], x.shape[0] // state_kt.shape[0]
    nseq = SAMPLE_SEQS
    rows = nseq * t
    steps = nseq_all // nseq
    cos, sin = _rope_lane_tables(past_len + np.arange(rows, dtype=np.float64) % t)
    sin = sin * _rope_sign()
    row = lambda i: (i, 0)
    seq3 = lambda i: (i, 0, 0)
    pos3 = lambda i: (0, i, 0)
    hbm = pl.BlockSpec(memory_space=pl.ANY)
    mixer_specs = _mixer_weight_specs()
    for n in MIXER_MATMUL_W:
        mixer_specs[n] = hbm
    g_pre_mlp = ffn_w[0]
    ffn_matmul_w = [ffn_w[n] for n in FFN_MATMUL_W]
    matmul_w = [mixer_w[n] for n in MIXER_MATMUL_W] + ffn_matmul_w
    n_chunks = sum(w.size for w in ffn_matmul_w) // (STAGE_ROWS * D_MODEL)
    chunks_per_step = -(-n_chunks // steps)
    outs = pl.pallas_call(
        functools.partial(_sample_mixer_kernel, steps),
        grid=(steps,),
        in_specs=[
            pl.BlockSpec((rows, D_MODEL), row),
            _const_spec((rows, LANES)),
            _const_spec((rows, LANES)),
            pl.BlockSpec((nseq, KV_COLS, WINDOW), seq3),
            pl.BlockSpec((nseq, KV_COLS, WINDOW), seq3),
            pl.BlockSpec((POOL_BUF, nseq, POOL_WIDTH), pos3),
        ] + mixer_specs + [_const_spec((1, D_MODEL))] + [hbm] * len(ffn_matmul_w),
        out_specs=[
            pl.BlockSpec((rows, D_MODEL), row),
            pl.BlockSpec((rows, D_MODEL), row),
            pl.BlockSpec((nseq, KV_COLS, WINDOW), seq3),
            pl.BlockSpec((nseq, KV_COLS, WINDOW), seq3),
            pl.BlockSpec((POOL_BUF, nseq, POOL_WIDTH), pos3),
        ] + [hbm] * N_MATMUL_W,
        out_shape=[
            jax.ShapeDtypeStruct((nseq_all * t, D_MODEL), F32),
            jax.ShapeDtypeStruct((nseq_all * t, D_MODEL), BF16),
            jax.ShapeDtypeStruct((nseq_all, KV_COLS, WINDOW), F32),
            jax.ShapeDtypeStruct((nseq_all, KV_COLS, WINDOW), F32),
            jax.ShapeDtypeStruct((POOL_BUF, nseq_all, POOL_WIDTH), F32),
        ] + [jax.ShapeDtypeStruct(w.shape, BF16) for w in matmul_w],
        scratch_shapes=[pltpu.VMEM((len(POOL_WINDOWS), nseq * (POOL_PAD + t), POOL_CG), F32)]
        + [pltpu.VMEM(mixer_w[n].shape, BF16) for n in MIXER_MATMUL_W] + [
            pltpu.SemaphoreType.DMA((len(MIXER_MATMUL_W),)),
            pltpu.VMEM((chunks_per_step, STAGE_ROWS, D_MODEL), F32),
            pltpu.VMEM((chunks_per_step, STAGE_ROWS, D_MODEL), BF16),
            pltpu.SemaphoreType.DMA((chunks_per_step,)),
            pltpu.SemaphoreType.DMA((chunks_per_step,)),
        ],
        compiler_params=_params(("arbitrary",)),
        name="sample_mixer",
    )(x, jnp.asarray(cos), jnp.asarray(sin), state_kt, state_vt, state_pool_t, *mixer_w, g_pre_mlp, *ffn_matmul_w)
    return outs[:5], outs[5:]


def kernel(x_prompt, x_sample, state_k, state_v, state_pool, p_prompt, p_sample, w_in, attn_sinks, pool_w,
           pool_scale, g_attn_out, g_pool_out, w_out, g_pre_mix, g_post_mix, g_pre_mlp, g_post_mlp, w_up, w_down,
           w_ple, w_ple_gate, b_ple_gate):
    depth = w_in.shape[0]
    batch, seq, _ = x_prompt.shape
    dec_batch, dec_seq, _ = x_sample.shape
    assert depth == 1 and batch == 1

    xp = x_prompt.reshape(seq, D_MODEL)
    xs = x_sample.reshape(dec_batch * dec_seq, D_MODEL)
    i = 0
    mixer_w = [
        g_pre_mix[i][None, :], w_in[i], attn_sinks[i], pool_w[i], pool_scale[i][None, :], g_attn_out[i][None, :],
        g_pool_out[i][None, :], w_out[i], g_post_mix[i][None, :],
    ]
    ffn_w = [
        g_pre_mlp[i][None, :], w_up[i], w_down[i], g_post_mlp[i][None, :], w_ple_gate[i], b_ple_gate[i][None, :],
        w_ple[i],
    ]

    (xs_mid, hs_mid, ks_t, vs_t, us_t), w_bf16 = _sample_mixer(
        xs, state_k[i].reshape(dec_batch, WINDOW, KV_COLS).transpose(0, 2, 1),
        state_v[i].reshape(dec_batch, WINDOW, KV_COLS).transpose(0, 2, 1),
        state_pool[i].transpose(1, 0, 2), float(PAST_LEN), mixer_w, ffn_w)
    ks, vs, us = ks_t.transpose(0, 2, 1), vs_t.transpose(0, 2, 1), us_t.transpose(1, 0, 2)
    for n, w in zip(MIXER_MATMUL_W, w_bf16[:len(MIXER_MATMUL_W)]):
        mixer_w[n] = w
    for n, w in zip(FFN_MATMUL_W, w_bf16[len(MIXER_MATMUL_W):]):
        ffn_w[n] = w
    yp, ys, kp, vp, up = _prompt_layer(xp, p_prompt[i, 0], xs_mid, hs_mid,
                                       p_sample[i].reshape(dec_batch * dec_seq, PLE_DIM), mixer_w, ffn_w)

    return (
        yp.reshape(batch, seq, D_MODEL),
        ys.reshape(dec_batch, dec_seq, D_MODEL),
        kp.T.reshape(depth, batch, WINDOW, N_KV_HEADS, HEAD_DIM),
        vp.T.reshape(depth, batch, WINDOW, N_KV_HEADS, HEAD_DIM),
        up.reshape(depth, batch, POOL_BUF, POOL_WIDTH),
        ks.reshape(depth, dec_batch, WINDOW, N_KV_HEADS, HEAD_DIM),
        vs.reshape(depth, dec_batch, WINDOW, N_KV_HEADS, HEAD_DIM),
        us.reshape(depth, dec_batch, POOL_BUF, POOL_WIDTH),
    )
```

```python
import jax
import jax.numpy as jnp
import numpy as np
from jax import lax
from jax.experimental import pallas as pl
from jax.experimental.pallas import tpu as pltpu

D_MODEL = 1024
HEAD_DIM = 64
N_HEADS = 8
N_KV_HEADS = 2
GROUP = N_HEADS // N_KV_HEADS
WINDOW = 128
Q_COLS = N_HEADS * HEAD_DIM
KV_COLS = N_KV_HEADS * HEAD_DIM
POOL_WIDTH = 512
POOL_WINDOWS = (2, 4, 8, 16)
POOL_CG = POOL_WIDTH // len(POOL_WINDOWS)
POOL_BUF = max(POOL_WINDOWS) - 1
POOL_PAD = POOL_BUF + 1
IN_COLS = Q_COLS + 2 * KV_COLS + POOL_WIDTH
D_FF = 4 * D_MODEL
FF_CHUNK = 1024
PLE_DIM = 256
ROPE_THETA = 10000.0
PAST_LEN = 16384
EPS = 1e-6
NEG = -1e30
LANES = 128
MXU_N = 256

ATTN_LOOKAHEAD = 4
PROMPT_PATTERN = "11" + "21" * (5 + ATTN_LOOKAHEAD + 16 + 4) + "11" + "22" + "11" + "222" + "11"

PROMPT_TILE = 512
SAMPLE_SEQS = 32
VMEM_LIMIT = 58 * 1024 * 1024

BF16 = jnp.bfloat16
F32 = jnp.float32


def _rms(x, g):
    y = x * lax.rsqrt(jnp.mean(x * x, axis=-1, keepdims=True) + EPS)
    return y * g


def _mm(a, w):
    return jnp.dot(a, w, preferred_element_type=F32)


def _rope_tables(cos_a, sin_a, cos_b, sin_b, sign):
    cos_t = cos_a * cos_b - sin_a * sin_b
    sin_t = (sin_a * cos_b + cos_a * sin_b) * sign
    return cos_t, sin_t


def _rope(x, cos_t, sin_t, first_half):
    n = x.shape[1] // LANES
    width = x.shape[1]
    partner = jnp.where(first_half, pltpu.roll(x, width - HEAD_DIM // 2, axis=1), pltpu.roll(x, HEAD_DIM // 2, axis=1))
    if n > 1:
        cos_t = jnp.concatenate([cos_t] * n, axis=1)
        sin_t = jnp.concatenate([sin_t] * n, axis=1)
    return x * cos_t + partner * sin_t


def _first_half_mask(rows, width):
    lane = lax.broadcasted_iota(jnp.int32, (rows, width), 1)
    return (lane % HEAD_DIM) < (HEAD_DIM // 2)


def _project(x, g_pre, w_in, tables):
    rows = x.shape[0]
    h = _rms(x, g_pre).astype(BF16)
    proj = _mm(h, w_in)
    cos_t, sin_t = tables
    q = _rope(proj[:, :Q_COLS], cos_t, sin_t, _first_half_mask(rows, Q_COLS)) * (HEAD_DIM ** -0.5)
    k = _rope(proj[:, Q_COLS:Q_COLS + KV_COLS], cos_t, sin_t, _first_half_mask(rows, KV_COLS))
    v = proj[:, Q_COLS + KV_COLS:Q_COLS + 2 * KV_COLS]
    u = proj[:, Q_COLS + 2 * KV_COLS:]
    return q, k, v, u


def _pool_out(win, tok, inv_cnt, pool_w_ref, pool_scale):
    outs = []
    for g in range(len(POOL_WINDOWS)):
        r = win[g] * inv_cnt[g] - tok[:, g * POOL_CG:(g + 1) * POOL_CG]
        outs.append(_mm(r.astype(BF16), pool_w_ref[g]))
    return jnp.concatenate(outs, axis=1) * pool_scale


def _mix_out(x, attn, pooled, g_attn, g_pool, w_out, g_post):
    cat = jnp.concatenate([_rms(attn, g_attn), _rms(pooled, g_pool)], axis=1).astype(BF16)
    return x + _rms(_mm(cat, w_out), g_post)


def _softmax_parts(s_list, sink):
    m = sink
    for s in s_list:
        m = jnp.maximum(m, jnp.max(s, axis=-1, keepdims=True))
    e_list = [jnp.exp(s - m) for s in s_list]
    den = jnp.exp(sink - m)
    for e in e_list:
        den = den + jnp.sum(e, axis=-1, keepdims=True)
    return e_list, den


def _prompt_mixer_steps(tile, x_ref, cos_a_ref, sin_a_ref, cos_b_ref, sin_b_ref, sign_ref, g_pre_ref, w_in_ref,
                        sinks_ref, pool_w_ref, pool_scale_ref, g_attn_ref, g_pool_ref, w_out_ref, g_post_ref,
                        knew_ref, vnew_ref, unew_ref, qbuf, kbuf, vtbuf, ubuf, pbuf, catbuf,
                        g_pre_mlp_ref, xmid_ref, hmid_ref):
    tm = x_ref.shape[0]
    nblk = tm // WINDOW
    half_lanes = LANES // 2
    base = tile * tm

    h = _rms(x_ref[...], g_pre_ref[...]).astype(BF16)
    cos_t, sin_t = _rope_tables(cos_a_ref[0], sin_a_ref[0], cos_b_ref[...], sin_b_ref[...], sign_ref[...])

    kv = _mm(h, w_in_ref[:, Q_COLS:Q_COLS + 2 * KV_COLS])
    k = _rope(kv[:, :KV_COLS], cos_t, sin_t, _first_half_mask(tm, KV_COLS))
    v = kv[:, KV_COLS:]
    knew_ref[...] = k[tm - WINDOW:, :].T
    vnew_ref[...] = v[tm - WINDOW:, :].T
    low = lax.broadcasted_iota(jnp.int32, (tm, LANES), 1) < half_lanes
    k_swapped = pltpu.roll(k, half_lanes, axis=1)
    kbuf[0, WINDOW:, :] = jnp.where(low, k, 0.0).astype(BF16)
    kbuf[1, WINDOW:, :] = jnp.where(low, 0.0, k_swapped).astype(BF16)
    kbuf[2, WINDOW:, :] = jnp.where(low, k_swapped, 0.0).astype(BF16)
    kbuf[3, WINDOW:, :] = jnp.where(low, 0.0, k).astype(BF16)
    vtbuf[:, WINDOW:] = v.T.astype(BF16)
    yield

    u_col0 = Q_COLS + 2 * KV_COLS
    for c in range(Q_COLS // MXU_N):
        cols = slice(c * MXU_N, (c + 1) * MXU_N)
        q = _rope(_mm(h, w_in_ref[:, cols]), cos_t, sin_t, _first_half_mask(tm, MXU_N)) * (HEAD_DIM ** -0.5)
        qbuf[:, cols] = q.astype(BF16)
        yield
    for c in range(POOL_WIDTH // MXU_N):
        cols = slice(c * MXU_N, (c + 1) * MXU_N)
        ubuf[POOL_PAD:, cols] = _mm(h, w_in_ref[:, u_col0 + c * MXU_N:u_col0 + (c + 1) * MXU_N])
        yield

    pos = base + lax.broadcasted_iota(jnp.int32, (tm, 1), 0)
    pool_scale = pool_scale_ref[...]
    for g, w in enumerate(POOL_WINDOWS):
        cols = slice(g * POOL_CG, (g + 1) * POOL_CG)
        a = ubuf[:, cols]
        shift = 1
        while shift < w:
            a = a + pltpu.roll(a, shift, axis=0)
            shift *= 2
        inv_cnt = 1.0 / jnp.minimum(pos + 1, w).astype(F32)
        r = a[POOL_PAD:, :] * inv_cnt - ubuf[POOL_PAD:, cols]
        pbuf[:, cols] = _mm(r.astype(BF16), pool_w_ref[g]) * pool_scale[:, cols]
        yield
    catbuf[:, Q_COLS:] = _rms(pbuf[...], g_pool_ref[...]).astype(BF16)

    c_i = lax.broadcasted_iota(jnp.int32, (2 * WINDOW, 2 * WINDOW), 0)
    r_i = lax.broadcasted_iota(jnp.int32, (2 * WINDOW, 2 * WINDOW), 1) % WINDOW
    d_i = c_i - r_i
    band = (d_i >= 1) & (d_i <= WINDOW)
    band0 = band & (c_i >= WINDOW - base)

    def scores(j, h_kv, par):
        rows = slice(j * WINDOW, (j + 1) * WINDOW)
        keys = slice(j * WINDOW, (j + 2) * WINDOW)
        q2 = jnp.concatenate([qbuf[rows, (2 * h_kv) * LANES:(2 * h_kv + 1) * LANES],
                              qbuf[rows, (2 * h_kv + 1) * LANES:(2 * h_kv + 2) * LANES]], axis=0)
        return lax.dot_general(kbuf[2 * h_kv + par, keys, :], q2, (((1,), (1,)), ((), ())),
                               preferred_element_type=F32)

    units = [(j, h_kv, par) for j in range(nblk) for h_kv in range(N_KV_HEADS) for par in range(2)]
    pending = []
    for n in range(min(ATTN_LOOKAHEAD, len(units))):
        pending.append(scores(*units[n]))
        yield
    heads = [None] * N_HEADS
    for n, (j, h_kv, par) in enumerate(units):
        st = pending.pop(0)
        if n + ATTN_LOOKAHEAD < len(units):
            pending.append(scores(*units[n + ATTN_LOOKAHEAD]))
        mask = band0 if j == 0 else band
        st = jnp.where(mask, st, NEG)
        sink = jnp.concatenate([jnp.full((1, WINDOW), sinks_ref[GROUP * h_kv + par], F32),
                                jnp.full((1, WINDOW), sinks_ref[GROUP * h_kv + 2 + par], F32)], axis=1)
        m = jnp.maximum(jnp.max(st, axis=0, keepdims=True), sink)
        e = jnp.exp(st - m)
        den = jnp.sum(e, axis=0, keepdims=True) + jnp.exp(sink - m)
        vt = vtbuf[h_kv * HEAD_DIM:(h_kv + 1) * HEAD_DIM, j * WINDOW:(j + 2) * WINDOW]
        ot = _mm(vt, e.astype(BF16)) / den
        heads[GROUP * h_kv + par] = ot[:, :WINDOW]
        heads[GROUP * h_kv + 2 + par] = ot[:, WINDOW:]
        if h_kv == N_KV_HEADS - 1 and par == 1:
            attn = jnp.concatenate(heads, axis=0).T
            catbuf[j * WINDOW:(j + 1) * WINDOW, :Q_COLS] = _rms(attn, g_attn_ref[...]).astype(BF16)
        yield

    mix = []
    for c in range(D_MODEL // MXU_N):
        mix.append(_mm(catbuf[...], w_out_ref[:, c * MXU_N:(c + 1) * MXU_N]))
        yield
    x_mid = x_ref[...] + _rms(jnp.concatenate(mix, axis=1), g_post_ref[...])

    unew_ref[...] = ubuf[tm + 1:tm + POOL_PAD, :]
    kbuf[:, 0:WINDOW, :] = kbuf[:, tm:tm + WINDOW, :]
    vtbuf[:, 0:WINDOW] = vtbuf[:, tm:tm + WINDOW]
    ubuf[0:POOL_PAD, :] = ubuf[tm:tm + POOL_PAD, :]
    xmid_ref[...] = x_mid
    hmid_ref[...] = _rms(x_mid, g_pre_mlp_ref[...]).astype(BF16)


def _sample_mixer_kernel(x_ref, cos_ref, sin_ref, skt_ref, svt_ref, sp_ref, g_pre_ref, w_in_hbm,
                         sinks_ref, pool_w_hbm, pool_scale_ref, g_attn_ref, g_pool_ref, w_out_hbm, g_post_ref,
                         g_pre_mlp_ref, xmid_ref, hmid_ref, knew_ref, vnew_ref, unew_ref,
                         ubuf, w_in_ref, pool_w_ref, w_out_ref):
    @pl.when(pl.program_id(0) == 0)
    def _():
        _cast_weights_to_vmem([w_in_hbm, pool_w_hbm, w_out_hbm], [w_in_ref, pool_w_ref, w_out_ref])

    rows = x_ref.shape[0]
    nseq = skt_ref.shape[0]
    t = rows // nseq
    x = x_ref[...]
    q, k, v, u = _project(x, g_pre_ref[...], w_in_ref[...], (cos_ref[...], sin_ref[...]))

    k3 = k.reshape(nseq, t, KV_COLS)
    v3 = v.reshape(nseq, t, KV_COLS)
    sk = jnp.swapaxes(skt_ref[...], 1, 2)
    sv = jnp.swapaxes(svt_ref[...], 1, 2)
    knew_ref[...] = jnp.swapaxes(jnp.concatenate([sk[:, t:, :], k3], axis=1), 1, 2)
    vnew_ref[...] = jnp.swapaxes(jnp.concatenate([sv[:, t:, :], v3], axis=1), 1, 2)

    r_i = lax.broadcasted_iota(jnp.int32, (GROUP * t, WINDOW), 0) % t
    c_i = lax.broadcasted_iota(jnp.int32, (GROUP * t, WINDOW), 1)
    mask_old = (c_i > r_i)[None]
    r_n = lax.broadcasted_iota(jnp.int32, (GROUP * t, t), 0) % t
    c_n = lax.broadcasted_iota(jnp.int32, (GROUP * t, t), 1)
    mask_new = (c_n <= r_n)[None]

    qb = q.astype(BF16)
    kb = k3.astype(BF16)
    vb = v3.astype(BF16)
    heads = [None] * N_HEADS
    for h in range(N_KV_HEADS):
        lanes = slice(h * HEAD_DIM, (h + 1) * HEAD_DIM)
        q4 = jnp.concatenate(
            [qb[:, (GROUP * h + g) * HEAD_DIM:(GROUP * h + g + 1) * HEAD_DIM].reshape(nseq, t, HEAD_DIM)
             for g in range(GROUP)], axis=1)
        k_old = sk[:, :, lanes].astype(BF16)
        v_old = sv[:, :, lanes].astype(BF16)
        s_old = jnp.einsum('bqd,bkd->bqk', q4, k_old, preferred_element_type=F32)
        s_new = jnp.einsum('bqd,bkd->bqk', q4, kb[:, :, lanes], preferred_element_type=F32)
        s_old = jnp.where(mask_old, s_old, NEG)
        s_new = jnp.where(mask_new, s_new, NEG)
        sink = jnp.concatenate(
            [jnp.full((1, t, 1), sinks_ref[GROUP * h + g], F32) for g in range(GROUP)], axis=1)
        (e_old, e_new), den = _softmax_parts([s_old, s_new], sink)
        o = jnp.einsum('bqk,bkd->bqd', e_old.astype(BF16), v_old, preferred_element_type=F32)
        o = o + jnp.einsum('bqk,bkd->bqd', e_new.astype(BF16), vb[:, :, lanes], preferred_element_type=F32)
        o = o / den
        for g in range(GROUP):
            heads[GROUP * h + g] = o[:, g * t:(g + 1) * t, :].reshape(rows, HEAD_DIM)
    attn = jnp.concatenate(heads, axis=1)

    ext = POOL_PAD + t
    per_seq = lambda j: pl.ds(j, nseq, stride=ext)
    win = []
    for g, w in enumerate(POOL_WINDOWS):
        cols = slice(g * POOL_CG, (g + 1) * POOL_CG)
        ubuf[g, per_seq(0), :] = jnp.zeros((nseq, POOL_CG), F32)
        for j in range(POOL_BUF):
            ubuf[g, per_seq(1 + j), :] = sp_ref[j, :, cols]
        for s_i in range(nseq):
            ubuf[g, s_i * ext + POOL_PAD:(s_i + 1) * ext, :] = u[s_i * t:(s_i + 1) * t, cols]
        for j in range(POOL_BUF):
            unew_ref[j, :, cols] = ubuf[g, per_seq(t + 1 + j), :]
        a = ubuf[g]
        shift = 1
        while shift < w:
            a = a + pltpu.roll(a, shift, axis=0)
            shift *= 2
        win.append(a.reshape(nseq, ext, POOL_CG)[:, POOL_PAD:, :].reshape(rows, POOL_CG))
    inv_cnt = [1.0 / w for w in POOL_WINDOWS]
    pooled = _pool_out(win, u, inv_cnt, pool_w_ref, pool_scale_ref[...])

    x_mid = _mix_out(x, attn, pooled, g_attn_ref[...], g_pool_ref[...], w_out_ref[...], g_post_ref[...])
    xmid_ref[...] = x_mid
    hmid_ref[...] = _rms(x_mid, g_pre_mlp_ref[...]).astype(BF16)


def _ffn_steps(x_ref, h_ref, load_p, w_up_ref, w_down_ref, g_post_ref, w_gate_ref, b_gate_ref, w_ple_ref,
               y_ref, actbuf):
    for c in range(D_FF // MXU_N):
        cols = slice(c * MXU_N, (c + 1) * MXU_N)
        up = _mm(h_ref[...], w_up_ref[:, cols])
        actbuf[:, cols] = jnp.square(jnp.maximum(up, 0.0)).astype(BF16)
        yield
    ff = []
    n_col, n_row = D_MODEL // MXU_N, D_FF // FF_CHUNK
    for c in range(n_col):
        cols = slice(c * MXU_N, (c + 1) * MXU_N)
        acc = None
        for r in range(n_row):
            rows = slice(r * FF_CHUNK, (r + 1) * FF_CHUNK)
            part = _mm(actbuf[:, rows], w_down_ref[rows, cols])
            acc = part if acc is None else acc + part
            if (c, r) != (n_col - 1, n_row - 1):
                yield
        ff.append(acc)
    x = x_ref[...] + _rms(jnp.concatenate(ff, axis=1), g_post_ref[...])
    xb = x.astype(BF16)
    yield
    ple = _mm(load_p().astype(BF16), w_ple_ref[...])
    yield
    for c in range(D_MODEL // MXU_N):
        cols = slice(c * MXU_N, (c + 1) * MXU_N)
        gate = jax.nn.sigmoid(_mm(xb, w_gate_ref[:, cols]) + b_gate_ref[:, cols])
        y_ref[:, cols] = x[:, cols] + gate * ple[:, cols]
        yield


def _interleave(first, second, pattern):
    gens = {'1': first, '2': second}
    for tag in pattern:
        next(gens[tag], None)
    for _ in first:
        pass
    for _ in second:
        pass


def _cast_weights_to_vmem(srcs, dsts):
    tasks = []
    for src, dst in zip(srcs, dsts):
        if len(src.shape) != 2:
            tasks.append((src, dst, tuple(src.shape)))
            continue
        rows, cols = src.shape
        col_block = cols if cols <= STAGE_MAX_COLS else D_MODEL
        for c0 in range(0, cols, col_block):
            for r0 in range(0, rows, STAGE_ROWS):
                view = (slice(r0, r0 + STAGE_ROWS), slice(c0, c0 + col_block))
                tasks.append((src.at[view], dst.at[view], (STAGE_ROWS, col_block)))
    shapes = sorted(set(shape for _, _, shape in tasks))
    counts = [sum(1 for t in tasks if t[2] == shape) for shape in shapes]
    slots = [STAGE_SLOTS if n >= 2 * STAGE_SLOTS else min(n, 2) for n in counts]

    def body(*scoped):
        *stages, sems = scoped
        uses = [0] * len(shapes)
        copies, frees = [], []
        last_in_slot = {}
        for t, (src, dst, shape) in enumerate(tasks):
            k = shapes.index(shape)
            slot = uses[k] % slots[k]
            uses[k] += 1
            stage = stages[k].at[slot]
            copies.append((pltpu.make_async_copy(src, stage, sems.at[k, slot]), stage, dst))
            frees.append(last_in_slot.get((k, slot)))
            last_in_slot[(k, slot)] = t
        started = 0
        for t, (copy, stage, dst) in enumerate(copies):
            while started < len(copies) and (frees[started] is None or frees[started] < t):
                copies[started][0].start()
                started += 1
            copy.wait()
            dst[...] = stage[...].astype(BF16)

    pl.run_scoped(body, *[pltpu.VMEM((n,) + shape, F32) for n, shape in zip(slots, shapes)],
                  pltpu.SemaphoreType.DMA((len(shapes), max(slots))))


N_TABLES = 5
MIXER_MATMUL_W = (1, 3, 7)
FFN_MATMUL_W = (1, 2, 4, 6)
STAGE_ROWS = 256
STAGE_MAX_COLS = 1280
STAGE_SLOTS = 4


N_SAMPLE_TILES = 2


def _prompt_layer_kernel(*refs):
    x_ref, *tables = refs[:1 + N_TABLES]
    (g_pre_mix, w_in_hbm, sinks, pool_w_hbm, pool_scale, g_attn, g_pool, w_out_hbm, g_post_mix, p_ref,
     g_pre_mlp, w_up_hbm, w_down_hbm, g_post_mlp, w_gate_hbm, b_gate, w_ple_hbm,
     xs_hbm, hs_hbm, ps_ref) = refs[1 + N_TABLES:21 + N_TABLES]
    y_ref, ys_hbm, knew_ref, vnew_ref, unew_ref = refs[21 + N_TABLES:26 + N_TABLES]
    (qbuf, kbuf, vtbuf, ubuf, pbuf, catbuf, actbuf, xmid, hmid, sems,
     w_in, pool_w, w_out, w_up, w_down, w_gate, w_ple) = refs[26 + N_TABLES:]
    tm = x_ref.shape[0]
    step = pl.program_id(0)
    last_tile = pl.num_programs(0) - 1 - N_SAMPLE_TILES
    on_sample = step < N_SAMPLE_TILES
    slot = step % 2
    sample_rows = pl.ds(pl.multiple_of(jnp.minimum(step, N_SAMPLE_TILES - 1) * tm, tm), tm)

    @pl.when(step == 0)
    def _():
        _cast_weights_to_vmem([w_in_hbm, pool_w_hbm, w_out_hbm, w_up_hbm, w_down_hbm, w_gate_hbm, w_ple_hbm],
                              [w_in, pool_w, w_out, w_up, w_down, w_gate, w_ple])

    @pl.when(on_sample)
    def _():
        loads = [pltpu.make_async_copy(xs_hbm.at[sample_rows], xmid.at[slot], sems.at[0]),
                 pltpu.make_async_copy(hs_hbm.at[sample_rows], hmid.at[slot], sems.at[1])]
        for copy in loads:
            copy.start()
        kbuf[:, 0:WINDOW, :] = jnp.zeros((2 * N_KV_HEADS, WINDOW, LANES), BF16)
        vtbuf[:, 0:WINDOW] = jnp.zeros((KV_COLS, WINDOW), BF16)
        ubuf[0:POOL_PAD, :] = jnp.zeros((POOL_PAD, POOL_WIDTH), F32)
        for copy in loads:
            copy.wait()

    ffn = _ffn_steps(xmid.at[slot], hmid.at[slot], lambda: jnp.where(on_sample, ps_ref[...], p_ref[...]),
                     w_up, w_down, g_post_mlp, w_gate, b_gate, w_ple, y_ref, actbuf)
    mixer = _prompt_mixer_steps(jnp.clip(step - 1, 0, last_tile), x_ref, *tables, g_pre_mix, w_in, sinks, pool_w,
                                pool_scale, g_attn, g_pool, w_out, g_post_mix, knew_ref, vnew_ref, unew_ref,
                                qbuf, kbuf, vtbuf, ubuf, pbuf, catbuf, g_pre_mlp, xmid.at[1 - slot],
                                hmid.at[1 - slot])
    _interleave(ffn, mixer, PROMPT_PATTERN)

    @pl.when(on_sample)
    def _():
        store = pltpu.make_async_copy(y_ref, ys_hbm.at[sample_rows], sems.at[2])
        store.start()
        store.wait()


def _const_spec(shape):
    zeros = (0,) * len(shape)
    return pl.BlockSpec(shape, lambda i: zeros, pipeline_mode=pl.Buffered(1))


def _rope_lane_tables(pos):
    half = HEAD_DIM // 2
    inv = ROPE_THETA ** (-np.arange(half, dtype=np.float64) / half)
    ang = pos[:, None] * inv[None, :]
    reps = LANES // half
    return (np.tile(np.cos(ang), (1, reps)).astype(np.float32), np.tile(np.sin(ang), (1, reps)).astype(np.float32))


def _rope_sign():
    lane = np.arange(LANES)
    return np.where((lane % HEAD_DIM) < HEAD_DIM // 2, -1.0, 1.0).astype(np.float32)[None, :]


def _params(dims):
    return pltpu.CompilerParams(dimension_semantics=dims, vmem_limit_bytes=VMEM_LIMIT)


def _mixer_weight_specs():
    return [
        _const_spec((1, D_MODEL)),
        _const_spec((D_MODEL, IN_COLS)),
        pl.BlockSpec(memory_space=pltpu.SMEM),
        _const_spec((len(POOL_WINDOWS), POOL_CG, POOL_CG)),
        _const_spec((1, POOL_WIDTH)),
        _const_spec((1, Q_COLS)),
        _const_spec((1, POOL_WIDTH)),
        _const_spec((D_MODEL, D_MODEL)),
        _const_spec((1, D_MODEL)),
    ]


def _ffn_weight_specs():
    return [
        _const_spec((1, D_MODEL)),
        _const_spec((D_MODEL, D_FF)),
        _const_spec((D_FF, D_MODEL)),
        _const_spec((1, D_MODEL)),
        _const_spec((D_MODEL, D_MODEL)),
        _const_spec((1, D_MODEL)),
        _const_spec((PLE_DIM, D_MODEL)),
    ]


def _prompt_layer(x, p, xs_mid, hs_mid, ps, mixer_w, ffn_w):
    seq = x.shape[0]
    tm = PROMPT_TILE
    tiles = seq // tm
    assert xs_mid.shape[0] == N_SAMPLE_TILES * tm
    cos_a, sin_a = _rope_lane_tables(np.arange(tiles, dtype=np.float64) * tm)
    cos_b, sin_b = _rope_lane_tables(np.arange(tm, dtype=np.float64))
    mixer_tile = lambda i: (jnp.clip(i - 1, 0, tiles - 1), 0)
    mixer_tile3 = lambda i: (jnp.clip(i - 1, 0, tiles - 1), 0, 0)
    ffn_tile = lambda i: (jnp.clip(i - N_SAMPLE_TILES, 0, tiles - 1), 0)
    sample_tile = lambda i: (jnp.minimum(i, N_SAMPLE_TILES - 1), 0)
    fixed = lambda i: (0, 0)
    in_specs = [
        pl.BlockSpec((tm, D_MODEL), mixer_tile),
        pl.BlockSpec((1, 1, LANES), mixer_tile3),
        pl.BlockSpec((1, 1, LANES), mixer_tile3),
        _const_spec((tm, LANES)),
        _const_spec((tm, LANES)),
        _const_spec((1, LANES)),
    ] + _mixer_weight_specs() + [pl.BlockSpec((tm, PLE_DIM), ffn_tile)] + _ffn_weight_specs() + [
        pl.BlockSpec(memory_space=pl.ANY),
        pl.BlockSpec(memory_space=pl.ANY),
        pl.BlockSpec((tm, PLE_DIM), sample_tile, pipeline_mode=pl.Buffered(1)),
    ]
    n_lead = 1 + N_TABLES
    matmul_w = [mixer_w[n] for n in MIXER_MATMUL_W] + [ffn_w[n] for n in FFN_MATMUL_W]
    for n in [n_lead + n for n in MIXER_MATMUL_W] + [n_lead + len(mixer_w) + 1 + n for n in FFN_MATMUL_W]:
        in_specs[n] = pl.BlockSpec(memory_space=pl.ANY)
    return pl.pallas_call(
        _prompt_layer_kernel,
        grid=(tiles + N_SAMPLE_TILES,),
        in_specs=in_specs,
        out_specs=[
            pl.BlockSpec((tm, D_MODEL), ffn_tile),
            pl.BlockSpec(memory_space=pl.ANY),
            pl.BlockSpec((WINDOW, KV_COLS), fixed),
            pl.BlockSpec((WINDOW, KV_COLS), fixed),
            pl.BlockSpec((POOL_BUF, POOL_WIDTH), fixed),
        ],
        out_shape=[
            jax.ShapeDtypeStruct((seq, D_MODEL), F32),
            jax.ShapeDtypeStruct(xs_mid.shape, F32),
            jax.ShapeDtypeStruct((WINDOW, KV_COLS), F32),
            jax.ShapeDtypeStruct((WINDOW, KV_COLS), F32),
            jax.ShapeDtypeStruct((POOL_BUF, POOL_WIDTH), F32),
        ],
        scratch_shapes=[
            pltpu.VMEM((tm, Q_COLS), BF16),
            pltpu.VMEM((2 * N_KV_HEADS, WINDOW + tm, LANES), BF16),
            pltpu.VMEM((KV_COLS, WINDOW + tm), BF16),
            pltpu.VMEM((POOL_PAD + tm, POOL_WIDTH), F32),
            pltpu.VMEM((tm, POOL_WIDTH), F32),
            pltpu.VMEM((tm, D_MODEL), BF16),
            pltpu.VMEM((tm, D_FF), BF16),
            pltpu.VMEM((2, tm, D_MODEL), F32),
            pltpu.VMEM((2, tm, D_MODEL), BF16),
            pltpu.SemaphoreType.DMA((3,)),
        ] + [pltpu.VMEM(w.shape, BF16) for w in matmul_w],
        compiler_params=_params(("arbitrary",)),
        name="prompt_layer",
    )(x, jnp.asarray(cos_a)[:, None, :], jnp.asarray(sin_a)[:, None, :], jnp.asarray(cos_b), jnp.asarray(sin_b),
      jnp.asarray(_rope_sign()), *mixer_w, p, *ffn_w, xs_mid, hs_mid, ps)


def _sample_mixer(x, state_kt, state_vt, state_pool_t, past_len, mixer_w, g_pre_mlp):
    nseq_all, t = state_kt.shape[0], x.shape[0] // state_kt.shape[0]
    nseq = SAMPLE_SEQS
    rows = nseq * t
    steps = nseq_all // nseq
    cos, sin = _rope_lane_tables(past_len + np.arange(rows, dtype=np.float64) % t)
    sin = sin * _rope_sign()
    row = lambda i: (i, 0)
    seq3 = lambda i: (i, 0, 0)
    pos3 = lambda i: (0, i, 0)
    mixer_specs = _mixer_weight_specs()
    for n in MIXER_MATMUL_W:
        mixer_specs[n] = pl.BlockSpec(memory_space=pl.ANY)
    return pl.pallas_call(
        _sample_mixer_kernel,
        grid=(steps,),
        in_specs=[
            pl.BlockSpec((rows, D_MODEL), row),
            _const_spec((rows, LANES)),
            _const_spec((rows, LANES)),
            pl.BlockSpec((nseq, KV_COLS, WINDOW), seq3),
            pl.BlockSpec((nseq, KV_COLS, WINDOW), seq3),
            pl.BlockSpec((POOL_BUF, nseq, POOL_WIDTH), pos3),
        ] + mixer_specs + [_const_spec((1, D_MODEL))],
        out_specs=[
            pl.BlockSpec((rows, D_MODEL), row),
            pl.BlockSpec((rows, D_MODEL), row),
            pl.BlockSpec((nseq, KV_COLS, WINDOW), seq3),
            pl.BlockSpec((nseq, KV_COLS, WINDOW), seq3),
            pl.BlockSpec((POOL_BUF, nseq, POOL_WIDTH), pos3),
        ],
        out_shape=[
            jax.ShapeDtypeStruct((nseq_all * t, D_MODEL), F32),
            jax.ShapeDtypeStruct((nseq_all * t, D_MODEL), BF16),
            jax.ShapeDtypeStruct((nseq_all, KV_COLS, WINDOW), F32),
            jax.ShapeDtypeStruct((nseq_all, KV_COLS, WINDOW), F32),
            jax.ShapeDtypeStruct((POOL_BUF, nseq_all, POOL_WIDTH), F32),
        ],
        scratch_shapes=[pltpu.VMEM((len(POOL_WINDOWS), nseq * (POOL_PAD + t), POOL_CG), F32)]
        + [pltpu.VMEM(mixer_w[n].shape, BF16) for n in MIXER_MATMUL_W],
        compiler_params=_params(("arbitrary",)),
        name="sample_mixer",
    )(x, jnp.asarray(cos), jnp.asarray(sin), state_kt, state_vt, state_pool_t, *mixer_w, g_pre_mlp)


def kernel(x_prompt, x_sample, state_k, state_v, state_pool, p_prompt, p_sample, w_in, attn_sinks, pool_w,
           pool_scale, g_attn_out, g_pool_out, w_out, g_pre_mix, g_post_mix, g_pre_mlp, g_post_mlp, w_up, w_down,
           w_ple, w_ple_gate, b_ple_gate):
    depth = w_in.shape[0]
    batch, seq, _ = x_prompt.shape
    dec_batch, dec_seq, _ = x_sample.shape
    assert depth == 1 and batch == 1

    xp = x_prompt.reshape(seq, D_MODEL)
    xs = x_sample.reshape(dec_batch * dec_seq, D_MODEL)
    i = 0
    mixer_w = [
        g_pre_mix[i][None, :], w_in[i], attn_sinks[i], pool_w[i], pool_scale[i][None, :], g_attn_out[i][None, :],
        g_pool_out[i][None, :], w_out[i], g_post_mix[i][None, :],
    ]
    ffn_w = [
        g_pre_mlp[i][None, :], w_up[i], w_down[i], g_post_mlp[i][None, :], w_ple_gate[i], b_ple_gate[i][None, :],
        w_ple[i],
    ]

    xs_mid, hs_mid, ks_t, vs_t, us_t = _sample_mixer(
        xs, state_k[i].reshape(dec_batch, WINDOW, KV_COLS).transpose(0, 2, 1),
        state_v[i].reshape(dec_batch, WINDOW, KV_COLS).transpose(0, 2, 1),
        state_pool[i].transpose(1, 0, 2), float(PAST_LEN), mixer_w, ffn_w[0])
    ks, vs, us = ks_t.transpose(0, 2, 1), vs_t.transpose(0, 2, 1), us_t.transpose(1, 0, 2)
    yp, ys, kp, vp, up = _prompt_layer(xp, p_prompt[i, 0], xs_mid, hs_mid,
                                       p_sample[i].reshape(dec_batch * dec_seq, PLE_DIM), mixer_w, ffn_w)

    return (
        yp.reshape(batch, seq, D_MODEL),
        ys.reshape(dec_batch, dec_seq, D_MODEL),
        kp.T.reshape(depth, batch, WINDOW, N_KV_HEADS, HEAD_DIM),
        vp.T.reshape(depth, batch, WINDOW, N_KV_HEADS, HEAD_DIM),
        up.reshape(depth, batch, POOL_BUF, POOL_WIDTH),
        ks.reshape(depth, dec_batch, WINDOW, N_KV_HEADS, HEAD_DIM),
        vs.reshape(depth, dec_batch, WINDOW, N_KV_HEADS, HEAD_DIM),
        us.reshape(depth, dec_batch, POOL_BUF, POOL_WIDTH),
    )
```

```python
import jax
import jax.numpy as jnp
import numpy as np
from jax import lax
from jax.experimental import pallas as pl
from jax.experimental.pallas import tpu as pltpu

D_MODEL = 1024
HEAD_DIM = 64
N_HEADS = 8
N_KV_HEADS = 2
GROUP = N_HEADS // N_KV_HEADS
WINDOW = 128
Q_COLS = N_HEADS * HEAD_DIM
KV_COLS = N_KV_HEADS * HEAD_DIM
POOL_WIDTH = 512
POOL_WINDOWS = (2, 4, 8, 16)
POOL_CG = POOL_WIDTH // len(POOL_WINDOWS)
POOL_BUF = max(POOL_WINDOWS) - 1
POOL_PAD = POOL_BUF + 1
IN_COLS = Q_COLS + 2 * KV_COLS + POOL_WIDTH
D_FF = 4 * D_MODEL
FF_CHUNK = 1024
PLE_DIM = 256
ROPE_THETA = 10000.0
PAST_LEN = 16384
EPS = 1e-6
NEG = -1e30
LANES = 128
MXU_N = 256

ATTN_LOOKAHEAD = 4
PROMPT_PATTERN = "11" + "21" * (5 + ATTN_LOOKAHEAD + 16 + 4) + "11" + "22" + "11" + "222" + "11"

PROMPT_TILE = 512
SAMPLE_SEQS = 32
VMEM_LIMIT = 58 * 1024 * 1024

BF16 = jnp.bfloat16
F32 = jnp.float32


def _rms(x, g):
    y = x * lax.rsqrt(jnp.mean(x * x, axis=-1, keepdims=True) + EPS)
    return y * g


def _mm(a, w):
    return jnp.dot(a, w, preferred_element_type=F32)


def _rope_tables(cos_a, sin_a, cos_b, sin_b, sign):
    cos_t = cos_a * cos_b - sin_a * sin_b
    sin_t = (sin_a * cos_b + cos_a * sin_b) * sign
    return cos_t, sin_t


def _rope(x, cos_t, sin_t, first_half):
    n = x.shape[1] // LANES
    width = x.shape[1]
    partner = jnp.where(first_half, pltpu.roll(x, width - HEAD_DIM // 2, axis=1), pltpu.roll(x, HEAD_DIM // 2, axis=1))
    if n > 1:
        cos_t = jnp.concatenate([cos_t] * n, axis=1)
        sin_t = jnp.concatenate([sin_t] * n, axis=1)
    return x * cos_t + partner * sin_t


def _first_half_mask(rows, width):
    lane = lax.broadcasted_iota(jnp.int32, (rows, width), 1)
    return (lane % HEAD_DIM) < (HEAD_DIM // 2)


def _project(x, g_pre, w_in, tables):
    rows = x.shape[0]
    h = _rms(x, g_pre).astype(BF16)
    proj = _mm(h, w_in)
    cos_t, sin_t = tables
    q = _rope(proj[:, :Q_COLS], cos_t, sin_t, _first_half_mask(rows, Q_COLS)) * (HEAD_DIM ** -0.5)
    k = _rope(proj[:, Q_COLS:Q_COLS + KV_COLS], cos_t, sin_t, _first_half_mask(rows, KV_COLS))
    v = proj[:, Q_COLS + KV_COLS:Q_COLS + 2 * KV_COLS]
    u = proj[:, Q_COLS + 2 * KV_COLS:]
    return q, k, v, u


def _pool_out(win, tok, inv_cnt, pool_w_ref, pool_scale):
    outs = []
    for g in range(len(POOL_WINDOWS)):
        r = win[g] * inv_cnt[g] - tok[:, g * POOL_CG:(g + 1) * POOL_CG]
        outs.append(_mm(r.astype(BF16), pool_w_ref[g]))
    return jnp.concatenate(outs, axis=1) * pool_scale


def _mix_out(x, attn, pooled, g_attn, g_pool, w_out, g_post):
    cat = jnp.concatenate([_rms(attn, g_attn), _rms(pooled, g_pool)], axis=1).astype(BF16)
    return x + _rms(_mm(cat, w_out), g_post)


def _softmax_parts(s_list, sink):
    m = sink
    for s in s_list:
        m = jnp.maximum(m, jnp.max(s, axis=-1, keepdims=True))
    e_list = [jnp.exp(s - m) for s in s_list]
    den = jnp.exp(sink - m)
    for e in e_list:
        den = den + jnp.sum(e, axis=-1, keepdims=True)
    return e_list, den


def _prompt_mixer_steps(tile, x_ref, cos_a_ref, sin_a_ref, cos_b_ref, sin_b_ref, sign_ref, g_pre_ref, w_in_ref,
                        sinks_ref, pool_w_ref, pool_scale_ref, g_attn_ref, g_pool_ref, w_out_ref, g_post_ref,
                        knew_ref, vnew_ref, unew_ref, qbuf, kbuf, vtbuf, ubuf, pbuf, catbuf,
                        g_pre_mlp_ref, xmid_ref, hmid_ref):
    tm = x_ref.shape[0]
    nblk = tm // WINDOW
    half_lanes = LANES // 2
    base = tile * tm

    h = _rms(x_ref[...], g_pre_ref[...]).astype(BF16)
    cos_t, sin_t = _rope_tables(cos_a_ref[0], sin_a_ref[0], cos_b_ref[...], sin_b_ref[...], sign_ref[...])

    kv = _mm(h, w_in_ref[:, Q_COLS:Q_COLS + 2 * KV_COLS])
    k = _rope(kv[:, :KV_COLS], cos_t, sin_t, _first_half_mask(tm, KV_COLS))
    v = kv[:, KV_COLS:]
    knew_ref[...] = k[tm - WINDOW:, :].T
    vnew_ref[...] = v[tm - WINDOW:, :].T
    low = lax.broadcasted_iota(jnp.int32, (tm, LANES), 1) < half_lanes
    k_swapped = pltpu.roll(k, half_lanes, axis=1)
    kbuf[0, WINDOW:, :] = jnp.where(low, k, 0.0).astype(BF16)
    kbuf[1, WINDOW:, :] = jnp.where(low, 0.0, k_swapped).astype(BF16)
    kbuf[2, WINDOW:, :] = jnp.where(low, k_swapped, 0.0).astype(BF16)
    kbuf[3, WINDOW:, :] = jnp.where(low, 0.0, k).astype(BF16)
    vtbuf[:, WINDOW:] = v.T.astype(BF16)
    yield

    u_col0 = Q_COLS + 2 * KV_COLS
    for c in range(Q_COLS // MXU_N):
        cols = slice(c * MXU_N, (c + 1) * MXU_N)
        q = _rope(_mm(h, w_in_ref[:, cols]), cos_t, sin_t, _first_half_mask(tm, MXU_N)) * (HEAD_DIM ** -0.5)
        qbuf[:, cols] = q.astype(BF16)
        yield
    for c in range(POOL_WIDTH // MXU_N):
        cols = slice(c * MXU_N, (c + 1) * MXU_N)
        ubuf[POOL_PAD:, cols] = _mm(h, w_in_ref[:, u_col0 + c * MXU_N:u_col0 + (c + 1) * MXU_N])
        yield

    pos = base + lax.broadcasted_iota(jnp.int32, (tm, 1), 0)
    pool_scale = pool_scale_ref[...]
    for g, w in enumerate(POOL_WINDOWS):
        cols = slice(g * POOL_CG, (g + 1) * POOL_CG)
        a = ubuf[:, cols]
        shift = 1
        while shift < w:
            a = a + pltpu.roll(a, shift, axis=0)
            shift *= 2
        inv_cnt = 1.0 / jnp.minimum(pos + 1, w).astype(F32)
        r = a[POOL_PAD:, :] * inv_cnt - ubuf[POOL_PAD:, cols]
        pbuf[:, cols] = _mm(r.astype(BF16), pool_w_ref[g]) * pool_scale[:, cols]
        yield
    catbuf[:, Q_COLS:] = _rms(pbuf[...], g_pool_ref[...]).astype(BF16)

    c_i = lax.broadcasted_iota(jnp.int32, (2 * WINDOW, 2 * WINDOW), 0)
    r_i = lax.broadcasted_iota(jnp.int32, (2 * WINDOW, 2 * WINDOW), 1) % WINDOW
    d_i = c_i - r_i
    band = (d_i >= 1) & (d_i <= WINDOW)
    band0 = band & (c_i >= WINDOW - base)

    def scores(j, h_kv, par):
        rows = slice(j * WINDOW, (j + 1) * WINDOW)
        keys = slice(j * WINDOW, (j + 2) * WINDOW)
        q2 = jnp.concatenate([qbuf[rows, (2 * h_kv) * LANES:(2 * h_kv + 1) * LANES],
                              qbuf[rows, (2 * h_kv + 1) * LANES:(2 * h_kv + 2) * LANES]], axis=0)
        return lax.dot_general(kbuf[2 * h_kv + par, keys, :], q2, (((1,), (1,)), ((), ())),
                               preferred_element_type=F32)

    units = [(j, h_kv, par) for j in range(nblk) for h_kv in range(N_KV_HEADS) for par in range(2)]
    pending = []
    for n in range(min(ATTN_LOOKAHEAD, len(units))):
        pending.append(scores(*units[n]))
        yield
    heads = [None] * N_HEADS
    for n, (j, h_kv, par) in enumerate(units):
        st = pending.pop(0)
        if n + ATTN_LOOKAHEAD < len(units):
            pending.append(scores(*units[n + ATTN_LOOKAHEAD]))
        mask = band0 if j == 0 else band
        st = jnp.where(mask, st, NEG)
        sink = jnp.concatenate([jnp.full((1, WINDOW), sinks_ref[GROUP * h_kv + par], F32),
                                jnp.full((1, WINDOW), sinks_ref[GROUP * h_kv + 2 + par], F32)], axis=1)
        m = jnp.maximum(jnp.max(st, axis=0, keepdims=True), sink)
        e = jnp.exp(st - m)
        den = jnp.sum(e, axis=0, keepdims=True) + jnp.exp(sink - m)
        vt = vtbuf[h_kv * HEAD_DIM:(h_kv + 1) * HEAD_DIM, j * WINDOW:(j + 2) * WINDOW]
        ot = _mm(vt, e.astype(BF16)) / den
        heads[GROUP * h_kv + par] = ot[:, :WINDOW]
        heads[GROUP * h_kv + 2 + par] = ot[:, WINDOW:]
        if h_kv == N_KV_HEADS - 1 and par == 1:
            attn = jnp.concatenate(heads, axis=0).T
            catbuf[j * WINDOW:(j + 1) * WINDOW, :Q_COLS] = _rms(attn, g_attn_ref[...]).astype(BF16)
        yield

    mix = []
    for c in range(D_MODEL // MXU_N):
        mix.append(_mm(catbuf[...], w_out_ref[:, c * MXU_N:(c + 1) * MXU_N]))
        yield
    x_mid = x_ref[...] + _rms(jnp.concatenate(mix, axis=1), g_post_ref[...])

    unew_ref[...] = ubuf[tm + 1:tm + POOL_PAD, :]
    kbuf[:, 0:WINDOW, :] = kbuf[:, tm:tm + WINDOW, :]
    vtbuf[:, 0:WINDOW] = vtbuf[:, tm:tm + WINDOW]
    ubuf[0:POOL_PAD, :] = ubuf[tm:tm + POOL_PAD, :]
    xmid_ref[...] = x_mid
    hmid_ref[...] = _rms(x_mid, g_pre_mlp_ref[...]).astype(BF16)


def _sample_mixer_kernel(x_ref, cos_ref, sin_ref, skt_ref, svt_ref, sp_ref, g_pre_ref, w_in_hbm,
                         sinks_ref, pool_w_hbm, pool_scale_ref, g_attn_ref, g_pool_ref, w_out_hbm, g_post_ref,
                         g_pre_mlp_ref, xmid_ref, hmid_ref, knew_ref, vnew_ref, unew_ref,
                         ubuf, w_in_ref, pool_w_ref, w_out_ref):
    @pl.when(pl.program_id(0) == 0)
    def _():
        _cast_weights_to_vmem([w_in_hbm, pool_w_hbm, w_out_hbm], [w_in_ref, pool_w_ref, w_out_ref])

    rows = x_ref.shape[0]
    nseq = skt_ref.shape[0]
    t = rows // nseq
    x = x_ref[...]
    q, k, v, u = _project(x, g_pre_ref[...], w_in_ref[...], (cos_ref[...], sin_ref[...]))

    k3 = k.reshape(nseq, t, KV_COLS)
    v3 = v.reshape(nseq, t, KV_COLS)
    sk = jnp.swapaxes(skt_ref[...], 1, 2)
    sv = jnp.swapaxes(svt_ref[...], 1, 2)
    knew_ref[...] = jnp.swapaxes(jnp.concatenate([sk[:, t:, :], k3], axis=1), 1, 2)
    vnew_ref[...] = jnp.swapaxes(jnp.concatenate([sv[:, t:, :], v3], axis=1), 1, 2)

    r_i = lax.broadcasted_iota(jnp.int32, (GROUP * t, WINDOW), 0) % t
    c_i = lax.broadcasted_iota(jnp.int32, (GROUP * t, WINDOW), 1)
    mask_old = (c_i > r_i)[None]
    r_n = lax.broadcasted_iota(jnp.int32, (GROUP * t, t), 0) % t
    c_n = lax.broadcasted_iota(jnp.int32, (GROUP * t, t), 1)
    mask_new = (c_n <= r_n)[None]

    qb = q.astype(BF16)
    kb = k3.astype(BF16)
    vb = v3.astype(BF16)
    heads = [None] * N_HEADS
    scores = []
    for h in range(N_KV_HEADS):
        lanes = slice(h * HEAD_DIM, (h + 1) * HEAD_DIM)
        q4 = jnp.concatenate(
            [qb[:, (GROUP * h + g) * HEAD_DIM:(GROUP * h + g + 1) * HEAD_DIM].reshape(nseq, t, HEAD_DIM)
             for g in range(GROUP)], axis=1)
        s_old = jnp.einsum('bqd,bkd->bqk', q4, sk[:, :, lanes].astype(BF16), preferred_element_type=F32)
        s_new = jnp.einsum('bqd,bkd->bqk', q4, kb[:, :, lanes], preferred_element_type=F32)
        scores.append((s_old, s_new))
    for h, (s_old, s_new) in enumerate(scores):
        lanes = slice(h * HEAD_DIM, (h + 1) * HEAD_DIM)
        s_old = jnp.where(mask_old, s_old, NEG)
        s_new = jnp.where(mask_new, s_new, NEG)
        sink = jnp.concatenate(
            [jnp.full((1, t, 1), sinks_ref[GROUP * h + g], F32) for g in range(GROUP)], axis=1)
        (e_old, e_new), den = _softmax_parts([s_old, s_new], sink)
        o = jnp.einsum('bqk,bkd->bqd', e_old.astype(BF16), sv[:, :, lanes].astype(BF16), preferred_element_type=F32)
        o = o + jnp.einsum('bqk,bkd->bqd', e_new.astype(BF16), vb[:, :, lanes], preferred_element_type=F32)
        o = o / den
        for g in range(GROUP):
            heads[GROUP * h + g] = o[:, g * t:(g + 1) * t, :].reshape(rows, HEAD_DIM)
    attn = jnp.concatenate(heads, axis=1)

    ext = POOL_PAD + t
    per_seq = lambda j: pl.ds(j, nseq, stride=ext)
    win = []
    for g, w in enumerate(POOL_WINDOWS):
        cols = slice(g * POOL_CG, (g + 1) * POOL_CG)
        ubuf[g, per_seq(0), :] = jnp.zeros((nseq, POOL_CG), F32)
        for j in range(POOL_BUF):
            ubuf[g, per_seq(1 + j), :] = sp_ref[j, :, cols]
        for s_i in range(nseq):
            ubuf[g, s_i * ext + POOL_PAD:(s_i + 1) * ext, :] = u[s_i * t:(s_i + 1) * t, cols]
        for j in range(POOL_BUF):
            unew_ref[j, :, cols] = ubuf[g, per_seq(t + 1 + j), :]
        a = ubuf[g]
        shift = 1
        while shift < w:
            a = a + pltpu.roll(a, shift, axis=0)
            shift *= 2
        win.append(a.reshape(nseq, ext, POOL_CG)[:, POOL_PAD:, :].reshape(rows, POOL_CG))
    inv_cnt = [1.0 / w for w in POOL_WINDOWS]
    pooled = _pool_out(win, u, inv_cnt, pool_w_ref, pool_scale_ref[...])

    x_mid = _mix_out(x, attn, pooled, g_attn_ref[...], g_pool_ref[...], w_out_ref[...], g_post_ref[...])
    xmid_ref[...] = x_mid
    hmid_ref[...] = _rms(x_mid, g_pre_mlp_ref[...]).astype(BF16)


def _ffn_steps(x_ref, h_ref, load_p, w_up_ref, w_down_ref, g_post_ref, w_gate_ref, b_gate_ref, w_ple_ref,
               y_ref, actbuf):
    for c in range(D_FF // MXU_N):
        cols = slice(c * MXU_N, (c + 1) * MXU_N)
        up = _mm(h_ref[...], w_up_ref[:, cols])
        actbuf[:, cols] = jnp.square(jnp.maximum(up, 0.0)).astype(BF16)
        yield
    ff = []
    n_col, n_row = D_MODEL // MXU_N, D_FF // FF_CHUNK
    for c in range(n_col):
        cols = slice(c * MXU_N, (c + 1) * MXU_N)
        acc = None
        for r in range(n_row):
            rows = slice(r * FF_CHUNK, (r + 1) * FF_CHUNK)
            part = _mm(actbuf[:, rows], w_down_ref[rows, cols])
            acc = part if acc is None else acc + part
            if (c, r) != (n_col - 1, n_row - 1):
                yield
        ff.append(acc)
    x = x_ref[...] + _rms(jnp.concatenate(ff, axis=1), g_post_ref[...])
    xb = x.astype(BF16)
    yield
    ple = _mm(load_p().astype(BF16), w_ple_ref[...])
    yield
    for c in range(D_MODEL // MXU_N):
        cols = slice(c * MXU_N, (c + 1) * MXU_N)
        gate = jax.nn.sigmoid(_mm(xb, w_gate_ref[:, cols]) + b_gate_ref[:, cols])
        y_ref[:, cols] = x[:, cols] + gate * ple[:, cols]
        yield


def _interleave(first, second, pattern):
    gens = {'1': first, '2': second}
    for tag in pattern:
        next(gens[tag], None)
    for _ in first:
        pass
    for _ in second:
        pass


def _cast_weights_to_vmem(srcs, dsts):
    tasks = []
    for src, dst in zip(srcs, dsts):
        if len(src.shape) != 2:
            tasks.append((src, dst, tuple(src.shape)))
            continue
        rows, cols = src.shape
        col_block = cols if cols <= STAGE_MAX_COLS else D_MODEL
        for c0 in range(0, cols, col_block):
            for r0 in range(0, rows, STAGE_ROWS):
                view = (slice(r0, r0 + STAGE_ROWS), slice(c0, c0 + col_block))
                tasks.append((src.at[view], dst.at[view], (STAGE_ROWS, col_block)))
    shapes = sorted(set(shape for _, _, shape in tasks))
    counts = [sum(1 for t in tasks if t[2] == shape) for shape in shapes]
    slots = [STAGE_SLOTS if n >= 2 * STAGE_SLOTS else min(n, 2) for n in counts]

    def body(*scoped):
        *stages, sems = scoped
        uses = [0] * len(shapes)
        copies, frees = [], []
        last_in_slot = {}
        for t, (src, dst, shape) in enumerate(tasks):
            k = shapes.index(shape)
            slot = uses[k] % slots[k]
            uses[k] += 1
            stage = stages[k].at[slot]
            copies.append((pltpu.make_async_copy(src, stage, sems.at[k, slot]), stage, dst))
            frees.append(last_in_slot.get((k, slot)))
            last_in_slot[(k, slot)] = t
        started = 0
        for t, (copy, stage, dst) in enumerate(copies):
            while started < len(copies) and (frees[started] is None or frees[started] < t):
                copies[started][0].start()
                started += 1
            copy.wait()
            dst[...] = stage[...].astype(BF16)

    pl.run_scoped(body, *[pltpu.VMEM((n,) + shape, F32) for n, shape in zip(slots, shapes)],
                  pltpu.SemaphoreType.DMA((len(shapes), max(slots))))


N_TABLES = 5
MIXER_MATMUL_W = (1, 3, 7)
FFN_MATMUL_W = (1, 2, 4, 6)
STAGE_ROWS = 256
STAGE_MAX_COLS = 1280
STAGE_SLOTS = 4


N_SAMPLE_TILES = 2


def _prompt_layer_kernel(*refs):
    x_ref, *tables = refs[:1 + N_TABLES]
    (g_pre_mix, w_in_hbm, sinks, pool_w_hbm, pool_scale, g_attn, g_pool, w_out_hbm, g_post_mix, p_ref,
     g_pre_mlp, w_up_hbm, w_down_hbm, g_post_mlp, w_gate_hbm, b_gate, w_ple_hbm,
     xs_hbm, hs_hbm, ps_ref) = refs[1 + N_TABLES:21 + N_TABLES]
    y_ref, ys_hbm, knew_ref, vnew_ref, unew_ref = refs[21 + N_TABLES:26 + N_TABLES]
    (qbuf, kbuf, vtbuf, ubuf, pbuf, catbuf, actbuf, xmid, hmid, sems,
     w_in, pool_w, w_out, w_up, w_down, w_gate, w_ple) = refs[26 + N_TABLES:]
    tm = x_ref.shape[0]
    step = pl.program_id(0)
    last_tile = pl.num_programs(0) - 1 - N_SAMPLE_TILES
    on_sample = step < N_SAMPLE_TILES
    slot = step % 2
    sample_rows = pl.ds(pl.multiple_of(jnp.minimum(step, N_SAMPLE_TILES - 1) * tm, tm), tm)

    @pl.when(step == 0)
    def _():
        _cast_weights_to_vmem([w_in_hbm, pool_w_hbm, w_out_hbm, w_up_hbm, w_down_hbm, w_gate_hbm, w_ple_hbm],
                              [w_in, pool_w, w_out, w_up, w_down, w_gate, w_ple])

    @pl.when(on_sample)
    def _():
        loads = [pltpu.make_async_copy(xs_hbm.at[sample_rows], xmid.at[slot], sems.at[0]),
                 pltpu.make_async_copy(hs_hbm.at[sample_rows], hmid.at[slot], sems.at[1])]
        for copy in loads:
            copy.start()
        kbuf[:, 0:WINDOW, :] = jnp.zeros((2 * N_KV_HEADS, WINDOW, LANES), BF16)
        vtbuf[:, 0:WINDOW] = jnp.zeros((KV_COLS, WINDOW), BF16)
        ubuf[0:POOL_PAD, :] = jnp.zeros((POOL_PAD, POOL_WIDTH), F32)
        for copy in loads:
            copy.wait()

    ffn = _ffn_steps(xmid.at[slot], hmid.at[slot], lambda: jnp.where(on_sample, ps_ref[...], p_ref[...]),
                     w_up, w_down, g_post_mlp, w_gate, b_gate, w_ple, y_ref, actbuf)
    mixer = _prompt_mixer_steps(jnp.clip(step - 1, 0, last_tile), x_ref, *tables, g_pre_mix, w_in, sinks, pool_w,
                                pool_scale, g_attn, g_pool, w_out, g_post_mix, knew_ref, vnew_ref, unew_ref,
                                qbuf, kbuf, vtbuf, ubuf, pbuf, catbuf, g_pre_mlp, xmid.at[1 - slot],
                                hmid.at[1 - slot])
    _interleave(ffn, mixer, PROMPT_PATTERN)

    @pl.when(on_sample)
    def _():
        store = pltpu.make_async_copy(y_ref, ys_hbm.at[sample_rows], sems.at[2])
        store.start()
        store.wait()


def _const_spec(shape):
    zeros = (0,) * len(shape)
    return pl.BlockSpec(shape, lambda i: zeros, pipeline_mode=pl.Buffered(1))


def _rope_lane_tables(pos):
    half = HEAD_DIM // 2
    inv = ROPE_THETA ** (-np.arange(half, dtype=np.float64) / half)
    ang = pos[:, None] * inv[None, :]
    reps = LANES // half
    return (np.tile(np.cos(ang), (1, reps)).astype(np.float32), np.tile(np.sin(ang), (1, reps)).astype(np.float32))


def _rope_sign():
    lane = np.arange(LANES)
    return np.where((lane % HEAD_DIM) < HEAD_DIM // 2, -1.0, 1.0).astype(np.float32)[None, :]


def _params(dims):
    return pltpu.CompilerParams(dimension_semantics=dims, vmem_limit_bytes=VMEM_LIMIT)


def _mixer_weight_specs():
    return [
        _const_spec((1, D_MODEL)),
        _const_spec((D_MODEL, IN_COLS)),
        pl.BlockSpec(memory_space=pltpu.SMEM),
        _const_spec((len(POOL_WINDOWS), POOL_CG, POOL_CG)),
        _const_spec((1, POOL_WIDTH)),
        _const_spec((1, Q_COLS)),
        _const_spec((1, POOL_WIDTH)),
        _const_spec((D_MODEL, D_MODEL)),
        _const_spec((1, D_MODEL)),
    ]


def _ffn_weight_specs():
    return [
        _const_spec((1, D_MODEL)),
        _const_spec((D_MODEL, D_FF)),
        _const_spec((D_FF, D_MODEL)),
        _const_spec((1, D_MODEL)),
        _const_spec((D_MODEL, D_MODEL)),
        _const_spec((1, D_MODEL)),
        _const_spec((PLE_DIM, D_MODEL)),
    ]


def _prompt_layer(x, p, xs_mid, hs_mid, ps, mixer_w, ffn_w):
    seq = x.shape[0]
    tm = PROMPT_TILE
    tiles = seq // tm
    assert xs_mid.shape[0] == N_SAMPLE_TILES * tm
    cos_a, sin_a = _rope_lane_tables(np.arange(tiles, dtype=np.float64) * tm)
    cos_b, sin_b = _rope_lane_tables(np.arange(tm, dtype=np.float64))
    mixer_tile = lambda i: (jnp.clip(i - 1, 0, tiles - 1), 0)
    mixer_tile3 = lambda i: (jnp.clip(i - 1, 0, tiles - 1), 0, 0)
    ffn_tile = lambda i: (jnp.clip(i - N_SAMPLE_TILES, 0, tiles - 1), 0)
    sample_tile = lambda i: (jnp.minimum(i, N_SAMPLE_TILES - 1), 0)
    fixed = lambda i: (0, 0)
    in_specs = [
        pl.BlockSpec((tm, D_MODEL), mixer_tile),
        pl.BlockSpec((1, 1, LANES), mixer_tile3),
        pl.BlockSpec((1, 1, LANES), mixer_tile3),
        _const_spec((tm, LANES)),
        _const_spec((tm, LANES)),
        _const_spec((1, LANES)),
    ] + _mixer_weight_specs() + [pl.BlockSpec((tm, PLE_DIM), ffn_tile)] + _ffn_weight_specs() + [
        pl.BlockSpec(memory_space=pl.ANY),
        pl.BlockSpec(memory_space=pl.ANY),
        pl.BlockSpec((tm, PLE_DIM), sample_tile, pipeline_mode=pl.Buffered(1)),
    ]
    n_lead = 1 + N_TABLES
    matmul_w = [mixer_w[n] for n in MIXER_MATMUL_W] + [ffn_w[n] for n in FFN_MATMUL_W]
    for n in [n_lead + n for n in MIXER_MATMUL_W] + [n_lead + len(mixer_w) + 1 + n for n in FFN_MATMUL_W]:
        in_specs[n] = pl.BlockSpec(memory_space=pl.ANY)
    return pl.pallas_call(
        _prompt_layer_kernel,
        grid=(tiles + N_SAMPLE_TILES,),
        in_specs=in_specs,
        out_specs=[
            pl.BlockSpec((tm, D_MODEL), ffn_tile),
            pl.BlockSpec(memory_space=pl.ANY),
            pl.BlockSpec((WINDOW, KV_COLS), fixed),
            pl.BlockSpec((WINDOW, KV_COLS), fixed),
            pl.BlockSpec((POOL_BUF, POOL_WIDTH), fixed),
        ],
        out_shape=[
            jax.ShapeDtypeStruct((seq, D_MODEL), F32),
            jax.ShapeDtypeStruct(xs_mid.shape, F32),
            jax.ShapeDtypeStruct((WINDOW, KV_COLS), F32),
            jax.ShapeDtypeStruct((WINDOW, KV_COLS), F32),
            jax.ShapeDtypeStruct((POOL_BUF, POOL_WIDTH), F32),
        ],
        scratch_shapes=[
            pltpu.VMEM((tm, Q_COLS), BF16),
            pltpu.VMEM((2 * N_KV_HEADS, WINDOW + tm, LANES), BF16),
            pltpu.VMEM((KV_COLS, WINDOW + tm), BF16),
            pltpu.VMEM((POOL_PAD + tm, POOL_WIDTH), F32),
            pltpu.VMEM((tm, POOL_WIDTH), F32),
            pltpu.VMEM((tm, D_MODEL), BF16),
            pltpu.VMEM((tm, D_FF), BF16),
            pltpu.VMEM((2, tm, D_MODEL), F32),
            pltpu.VMEM((2, tm, D_MODEL), BF16),
            pltpu.SemaphoreType.DMA((3,)),
        ] + [pltpu.VMEM(w.shape, BF16) for w in matmul_w],
        compiler_params=_params(("arbitrary",)),
        name="prompt_layer",
    )(x, jnp.asarray(cos_a)[:, None, :], jnp.asarray(sin_a)[:, None, :], jnp.asarray(cos_b), jnp.asarray(sin_b),
      jnp.asarray(_rope_sign()), *mixer_w, p, *ffn_w, xs_mid, hs_mid, ps)


def _sample_mixer(x, state_kt, state_vt, state_pool_t, past_len, mixer_w, g_pre_mlp):
    nseq_all, t = state_kt.shape[0], x.shape[0] // state_kt.shape[0]
    nseq = SAMPLE_SEQS
    rows = nseq * t
    steps = nseq_all // nseq
    cos, sin = _rope_lane_tables(past_len + np.arange(rows, dtype=np.float64) % t)
    sin = sin * _rope_sign()
    row = lambda i: (i, 0)
    seq3 = lambda i: (i, 0, 0)
    pos3 = lambda i: (0, i, 0)
    mixer_specs = _mixer_weight_specs()
    for n in MIXER_MATMUL_W:
        mixer_specs[n] = pl.BlockSpec(memory_space=pl.ANY)
    return pl.pallas_call(
        _sample_mixer_kernel,
        grid=(steps,),
        in_specs=[
            pl.BlockSpec((rows, D_MODEL), row),
            _const_spec((rows, LANES)),
            _const_spec((rows, LANES)),
            pl.BlockSpec((nseq, KV_COLS, WINDOW), seq3),
            pl.BlockSpec((nseq, KV_COLS, WINDOW), seq3),
            pl.BlockSpec((POOL_BUF, nseq, POOL_WIDTH), pos3),
        ] + mixer_specs + [_const_spec((1, D_MODEL))],
        out_specs=[
            pl.BlockSpec((rows, D_MODEL), row),
            pl.BlockSpec((rows, D_MODEL), row),
            pl.BlockSpec((nseq, KV_COLS, WINDOW), seq3),
            pl.BlockSpec((nseq, KV_COLS, WINDOW), seq3),
            pl.BlockSpec((POOL_BUF, nseq, POOL_WIDTH), pos3),
        ],
        out_shape=[
            jax.ShapeDtypeStruct((nseq_all * t, D_MODEL), F32),
            jax.ShapeDtypeStruct((nseq_all * t, D_MODEL), BF16),
            jax.ShapeDtypeStruct((nseq_all, KV_COLS, WINDOW), F32),
            jax.ShapeDtypeStruct((nseq_all, KV_COLS, WINDOW), F32),
            jax.ShapeDtypeStruct((POOL_BUF, nseq_all, POOL_WIDTH), F32),
        ],
        scratch_shapes=[pltpu.VMEM((len(POOL_WINDOWS), nseq * (POOL_PAD + t), POOL_CG), F32)]
        + [pltpu.VMEM(mixer_w[n].shape, BF16) for n in MIXER_MATMUL_W],
        compiler_params=_params(("arbitrary",)),
        name="sample_mixer",
    )(x, jnp.asarray(cos), jnp.asarray(sin), state_kt, state_vt, state_pool_t, *mixer_w, g_pre_mlp)


def kernel(x_prompt, x_sample, state_k, state_v, state_pool, p_prompt, p_sample, w_in, attn_sinks, pool_w,
           pool_scale, g_attn_out, g_pool_out, w_out, g_pre_mix, g_post_mix, g_pre_mlp, g_post_mlp, w_up, w_down,
           w_ple, w_ple_gate, b_ple_gate):
    depth = w_in.shape[0]
    batch, seq, _ = x_prompt.shape
    dec_batch, dec_seq, _ = x_sample.shape
    assert depth == 1 and batch == 1

    xp = x_prompt.reshape(seq, D_MODEL)
    xs = x_sample.reshape(dec_batch * dec_seq, D_MODEL)
    i = 0
    mixer_w = [
        g_pre_mix[i][None, :], w_in[i], attn_sinks[i], pool_w[i], pool_scale[i][None, :], g_attn_out[i][None, :],
        g_pool_out[i][None, :], w_out[i], g_post_mix[i][None, :],
    ]
    ffn_w = [
        g_pre_mlp[i][None, :], w_up[i], w_down[i], g_post_mlp[i][None, :], w_ple_gate[i], b_ple_gate[i][None, :],
        w_ple[i],
    ]

    xs_mid, hs_mid, ks_t, vs_t, us_t = _sample_mixer(
        xs, state_k[i].reshape(dec_batch, WINDOW, KV_COLS).transpose(0, 2, 1),
        state_v[i].reshape(dec_batch, WINDOW, KV_COLS).transpose(0, 2, 1),
        state_pool[i].transpose(1, 0, 2), float(PAST_LEN), mixer_w, ffn_w[0])
    ks, vs, us = ks_t.transpose(0, 2, 1), vs_t.transpose(0, 2, 1), us_t.transpose(1, 0, 2)
    yp, ys, kp, vp, up = _prompt_layer(xp, p_prompt[i, 0], xs_mid, hs_mid,
                                       p_sample[i].reshape(dec_batch * dec_seq, PLE_DIM), mixer_w, ffn_w)

    return (
        yp.reshape(batch, seq, D_MODEL),
        ys.reshape(dec_batch, dec_seq, D_MODEL),
        kp.T.reshape(depth, batch, WINDOW, N_KV_HEADS, HEAD_DIM),
        vp.T.reshape(depth, batch, WINDOW, N_KV_HEADS, HEAD_DIM),
        up.reshape(depth, batch, POOL_BUF, POOL_WIDTH),
        ks.reshape(depth, dec_batch, WINDOW, N_KV_HEADS, HEAD_DIM),
        vs.reshape(depth, dec_batch, WINDOW, N_KV_HEADS, HEAD_DIM),
        us.reshape(depth, dec_batch, POOL_BUF, POOL_WIDTH),
    )
```

```python
import jax
import jax.numpy as jnp
import numpy as np
from jax import lax
from jax.experimental import pallas as pl
from jax.experimental.pallas import tpu as pltpu

D_MODEL = 1024
HEAD_DIM = 64
N_HEADS = 8
N_KV_HEADS = 2
GROUP = N_HEADS // N_KV_HEADS
WINDOW = 128
Q_COLS = N_HEADS * HEAD_DIM
KV_COLS = N_KV_HEADS * HEAD_DIM
POOL_WIDTH = 512
POOL_WINDOWS = (2, 4, 8, 16)
POOL_CG = POOL_WIDTH // len(POOL_WINDOWS)
POOL_BUF = max(POOL_WINDOWS) - 1
POOL_PAD = POOL_BUF + 1
IN_COLS = Q_COLS + 2 * KV_COLS + POOL_WIDTH
D_FF = 4 * D_MODEL
FF_CHUNK = 1024
PLE_DIM = 256
ROPE_THETA = 10000.0
PAST_LEN = 16384
EPS = 1e-6
NEG = -1e30
LANES = 128
MXU_N = 256

ATTN_LOOKAHEAD = 4
PROMPT_PATTERN = "11" + "21" * (5 + ATTN_LOOKAHEAD + 16 + 4) + "11" + "22" + "11" + "222" + "11"

PROMPT_TILE = 512
SAMPLE_SEQS = 32
VMEM_LIMIT = 58 * 1024 * 1024

BF16 = jnp.bfloat16
F32 = jnp.float32


def _rms(x, g):
    y = x * lax.rsqrt(jnp.mean(x * x, axis=-1, keepdims=True) + EPS)
    return y * g


def _mm(a, w):
    return jnp.dot(a, w, preferred_element_type=F32)


def _rope_tables(cos_a, sin_a, cos_b, sin_b, sign):
    cos_t = cos_a * cos_b - sin_a * sin_b
    sin_t = (sin_a * cos_b + cos_a * sin_b) * sign
    return cos_t, sin_t


def _rope(x, cos_t, sin_t, first_half):
    n = x.shape[1] // LANES
    width = x.shape[1]
    partner = jnp.where(first_half, pltpu.roll(x, width - HEAD_DIM // 2, axis=1), pltpu.roll(x, HEAD_DIM // 2, axis=1))
    if n > 1:
        cos_t = jnp.concatenate([cos_t] * n, axis=1)
        sin_t = jnp.concatenate([sin_t] * n, axis=1)
    return x * cos_t + partner * sin_t


def _first_half_mask(rows, width):
    lane = lax.broadcasted_iota(jnp.int32, (rows, width), 1)
    return (lane % HEAD_DIM) < (HEAD_DIM // 2)


def _project(x, g_pre, w_in, tables):
    rows = x.shape[0]
    h = _rms(x, g_pre).astype(BF16)
    proj = _mm(h, w_in)
    cos_t, sin_t = tables
    q = _rope(proj[:, :Q_COLS], cos_t, sin_t, _first_half_mask(rows, Q_COLS)) * (HEAD_DIM ** -0.5)
    k = _rope(proj[:, Q_COLS:Q_COLS + KV_COLS], cos_t, sin_t, _first_half_mask(rows, KV_COLS))
    v = proj[:, Q_COLS + KV_COLS:Q_COLS + 2 * KV_COLS]
    u = proj[:, Q_COLS + 2 * KV_COLS:]
    return q, k, v, u


def _pool_out(win, tok, inv_cnt, pool_w_ref, pool_scale):
    outs = []
    for g in range(len(POOL_WINDOWS)):
        r = win[g] * inv_cnt[g] - tok[:, g * POOL_CG:(g + 1) * POOL_CG]
        outs.append(_mm(r.astype(BF16), pool_w_ref[g]))
    return jnp.concatenate(outs, axis=1) * pool_scale


def _mix_out(x, attn, pooled, g_attn, g_pool, w_out, g_post):
    cat = jnp.concatenate([_rms(attn, g_attn), _rms(pooled, g_pool)], axis=1).astype(BF16)
    return x + _rms(_mm(cat, w_out), g_post)


def _softmax_parts(s_list, sink):
    m = sink
    for s in s_list:
        m = jnp.maximum(m, jnp.max(s, axis=-1, keepdims=True))
    e_list = [jnp.exp(s - m) for s in s_list]
    den = jnp.exp(sink - m)
    for e in e_list:
        den = den + jnp.sum(e, axis=-1, keepdims=True)
    return e_list, den


def _prompt_mixer_steps(tile, x_ref, cos_a_ref, sin_a_ref, cos_b_ref, sin_b_ref, sign_ref, g_pre_ref, w_in_ref,
                        sinks_ref, pool_w_ref, pool_scale_ref, g_attn_ref, g_pool_ref, w_out_ref, g_post_ref,
                        knew_ref, vnew_ref, unew_ref, qbuf, kbuf, vtbuf, ubuf, pbuf, catbuf,
                        g_pre_mlp_ref, xmid_ref, hmid_ref):
    tm = x_ref.shape[0]
    nblk = tm // WINDOW
    half_lanes = LANES // 2
    base = tile * tm

    h = _rms(x_ref[...], g_pre_ref[...]).astype(BF16)
    cos_t, sin_t = _rope_tables(cos_a_ref[0], sin_a_ref[0], cos_b_ref[...], sin_b_ref[...], sign_ref[...])

    kv = _mm(h, w_in_ref[:, Q_COLS:Q_COLS + 2 * KV_COLS])
    k = _rope(kv[:, :KV_COLS], cos_t, sin_t, _first_half_mask(tm, KV_COLS))
    v = kv[:, KV_COLS:]
    knew_ref[...] = k[tm - WINDOW:, :].T
    vnew_ref[...] = v[tm - WINDOW:, :].T
    low = lax.broadcasted_iota(jnp.int32, (tm, LANES), 1) < half_lanes
    k_swapped = pltpu.roll(k, half_lanes, axis=1)
    kbuf[0, WINDOW:, :] = jnp.where(low, k, 0.0).astype(BF16)
    kbuf[1, WINDOW:, :] = jnp.where(low, 0.0, k_swapped).astype(BF16)
    kbuf[2, WINDOW:, :] = jnp.where(low, k_swapped, 0.0).astype(BF16)
    kbuf[3, WINDOW:, :] = jnp.where(low, 0.0, k).astype(BF16)
    vtbuf[:, WINDOW:] = v.T.astype(BF16)
    yield

    u_col0 = Q_COLS + 2 * KV_COLS
    for c in range(Q_COLS // MXU_N):
        cols = slice(c * MXU_N, (c + 1) * MXU_N)
        q = _rope(_mm(h, w_in_ref[:, cols]), cos_t, sin_t, _first_half_mask(tm, MXU_N)) * (HEAD_DIM ** -0.5)
        qbuf[:, cols] = q.astype(BF16)
        yield
    for c in range(POOL_WIDTH // MXU_N):
        cols = slice(c * MXU_N, (c + 1) * MXU_N)
        ubuf[POOL_PAD:, cols] = _mm(h, w_in_ref[:, u_col0 + c * MXU_N:u_col0 + (c + 1) * MXU_N])
        yield

    pos = base + lax.broadcasted_iota(jnp.int32, (tm, 1), 0)
    pool_scale = pool_scale_ref[...]
    for g, w in enumerate(POOL_WINDOWS):
        cols = slice(g * POOL_CG, (g + 1) * POOL_CG)
        a = ubuf[:, cols]
        shift = 1
        while shift < w:
            a = a + pltpu.roll(a, shift, axis=0)
            shift *= 2
        inv_cnt = 1.0 / jnp.minimum(pos + 1, w).astype(F32)
        r = a[POOL_PAD:, :] * inv_cnt - ubuf[POOL_PAD:, cols]
        pbuf[:, cols] = _mm(r.astype(BF16), pool_w_ref[g]) * pool_scale[:, cols]
        yield
    catbuf[:, Q_COLS:] = _rms(pbuf[...], g_pool_ref[...]).astype(BF16)

    c_i = lax.broadcasted_iota(jnp.int32, (2 * WINDOW, 2 * WINDOW), 0)
    r_i = lax.broadcasted_iota(jnp.int32, (2 * WINDOW, 2 * WINDOW), 1) % WINDOW
    d_i = c_i - r_i
    band = (d_i >= 1) & (d_i <= WINDOW)
    band0 = band & (c_i >= WINDOW - base)

    def scores(j, h_kv, par):
        rows = slice(j * WINDOW, (j + 1) * WINDOW)
        keys = slice(j * WINDOW, (j + 2) * WINDOW)
        q2 = jnp.concatenate([qbuf[rows, (2 * h_kv) * LANES:(2 * h_kv + 1) * LANES],
                              qbuf[rows, (2 * h_kv + 1) * LANES:(2 * h_kv + 2) * LANES]], axis=0)
        return lax.dot_general(kbuf[2 * h_kv + par, keys, :], q2, (((1,), (1,)), ((), ())),
                               preferred_element_type=F32)

    units = [(j, h_kv, par) for j in range(nblk) for h_kv in range(N_KV_HEADS) for par in range(2)]
    pending = []
    for n in range(min(ATTN_LOOKAHEAD, len(units))):
        pending.append(scores(*units[n]))
        yield
    heads = [None] * N_HEADS
    for n, (j, h_kv, par) in enumerate(units):
        st = pending.pop(0)
        if n + ATTN_LOOKAHEAD < len(units):
            pending.append(scores(*units[n + ATTN_LOOKAHEAD]))
        mask = band0 if j == 0 else band
        st = jnp.where(mask, st, NEG)
        sink = jnp.concatenate([jnp.full((1, WINDOW), sinks_ref[GROUP * h_kv + par], F32),
                                jnp.full((1, WINDOW), sinks_ref[GROUP * h_kv + 2 + par], F32)], axis=1)
        m = jnp.maximum(jnp.max(st, axis=0, keepdims=True), sink)
        e = jnp.exp(st - m)
        den = jnp.sum(e, axis=0, keepdims=True) + jnp.exp(sink - m)
        vt = vtbuf[h_kv * HEAD_DIM:(h_kv + 1) * HEAD_DIM, j * WINDOW:(j + 2) * WINDOW]
        ot = _mm(vt, e.astype(BF16)) / den
        heads[GROUP * h_kv + par] = ot[:, :WINDOW]
        heads[GROUP * h_kv + 2 + par] = ot[:, WINDOW:]
        if h_kv == N_KV_HEADS - 1 and par == 1:
            attn = jnp.concatenate(heads, axis=0).T
            catbuf[j * WINDOW:(j + 1) * WINDOW, :Q_COLS] = _rms(attn, g_attn_ref[...]).astype(BF16)
        yield

    mix = []
    for c in range(D_MODEL // MXU_N):
        mix.append(_mm(catbuf[...], w_out_ref[:, c * MXU_N:(c + 1) * MXU_N]))
        yield
    x_mid = x_ref[...] + _rms(jnp.concatenate(mix, axis=1), g_post_ref[...])

    unew_ref[...] = ubuf[tm + 1:tm + POOL_PAD, :]
    kbuf[:, 0:WINDOW, :] = kbuf[:, tm:tm + WINDOW, :]
    vtbuf[:, 0:WINDOW] = vtbuf[:, tm:tm + WINDOW]
    ubuf[0:POOL_PAD, :] = ubuf[tm:tm + POOL_PAD, :]
    xmid_ref[...] = x_mid
    hmid_ref[...] = _rms(x_mid, g_pre_mlp_ref[...]).astype(BF16)


def _sample_mixer_kernel(x_ref, cos_ref, sin_ref, skt_ref, svt_ref, sp_ref, g_pre_ref, w_in_hbm,
                         sinks_ref, pool_w_hbm, pool_scale_ref, g_attn_ref, g_pool_ref, w_out_hbm, g_post_ref,
                         g_pre_mlp_ref, xmid_ref, hmid_ref, knew_ref, vnew_ref, unew_ref,
                         ubuf, w_in_ref, pool_w_ref, w_out_ref):
    @pl.when(pl.program_id(0) == 0)
    def _():
        _cast_weights_to_vmem([w_in_hbm, pool_w_hbm, w_out_hbm], [w_in_ref, pool_w_ref, w_out_ref])

    rows = x_ref.shape[0]
    nseq = skt_ref.shape[0]
    t = rows // nseq
    x = x_ref[...]
    q, k, v, u = _project(x, g_pre_ref[...], w_in_ref[...], (cos_ref[...], sin_ref[...]))

    k3 = k.reshape(nseq, t, KV_COLS)
    v3 = v.reshape(nseq, t, KV_COLS)
    sk = jnp.swapaxes(skt_ref[...], 1, 2)
    sv = jnp.swapaxes(svt_ref[...], 1, 2)
    knew_ref[...] = jnp.swapaxes(jnp.concatenate([sk[:, t:, :], k3], axis=1), 1, 2)
    vnew_ref[...] = jnp.swapaxes(jnp.concatenate([sv[:, t:, :], v3], axis=1), 1, 2)

    r_i = lax.broadcasted_iota(jnp.int32, (GROUP * t, WINDOW), 0) % t
    c_i = lax.broadcasted_iota(jnp.int32, (GROUP * t, WINDOW), 1)
    mask_old = (c_i > r_i)[None]
    r_n = lax.broadcasted_iota(jnp.int32, (GROUP * t, t), 0) % t
    c_n = lax.broadcasted_iota(jnp.int32, (GROUP * t, t), 1)
    mask_new = (c_n <= r_n)[None]

    qb = q.astype(BF16)
    kb = k3.astype(BF16)
    vb = v3.astype(BF16)
    heads = [None] * N_HEADS
    scores = []
    for h in range(N_KV_HEADS):
        lanes = slice(h * HEAD_DIM, (h + 1) * HEAD_DIM)
        q4 = jnp.concatenate(
            [qb[:, (GROUP * h + g) * HEAD_DIM:(GROUP * h + g + 1) * HEAD_DIM].reshape(nseq, t, HEAD_DIM)
             for g in range(GROUP)], axis=1)
        s_old = jnp.einsum('bqd,bkd->bqk', q4, sk[:, :, lanes].astype(BF16), preferred_element_type=F32)
        s_new = jnp.einsum('bqd,bkd->bqk', q4, kb[:, :, lanes], preferred_element_type=F32)
        scores.append((s_old, s_new))
    for h, (s_old, s_new) in enumerate(scores):
        lanes = slice(h * HEAD_DIM, (h + 1) * HEAD_DIM)
        s_old = jnp.where(mask_old, s_old, NEG)
        s_new = jnp.where(mask_new, s_new, NEG)
        sink = jnp.concatenate(
            [jnp.full((1, t, 1), sinks_ref[GROUP * h + g], F32) for g in range(GROUP)], axis=1)
        (e_old, e_new), den = _softmax_parts([s_old, s_new], sink)
        o = jnp.einsum('bqk,bkd->bqd', e_old.astype(BF16), sv[:, :, lanes].astype(BF16), preferred_element_type=F32)
        o = o + jnp.einsum('bqk,bkd->bqd', e_new.astype(BF16), vb[:, :, lanes], preferred_element_type=F32)
        o = o / den
        for g in range(GROUP):
            heads[GROUP * h + g] = o[:, g * t:(g + 1) * t, :].reshape(rows, HEAD_DIM)
    attn = jnp.concatenate(heads, axis=1)

    ext = POOL_PAD + t
    per_seq = lambda j: pl.ds(j, nseq, stride=ext)
    win = []
    for g, w in enumerate(POOL_WINDOWS):
        cols = slice(g * POOL_CG, (g + 1) * POOL_CG)
        ubuf[g, per_seq(0), :] = jnp.zeros((nseq, POOL_CG), F32)
        for j in range(POOL_BUF):
            ubuf[g, per_seq(1 + j), :] = sp_ref[j, :, cols]
        for s_i in range(nseq):
            ubuf[g, s_i * ext + POOL_PAD:(s_i + 1) * ext, :] = u[s_i * t:(s_i + 1) * t, cols]
        for j in range(POOL_BUF):
            unew_ref[j, :, cols] = ubuf[g, per_seq(t + 1 + j), :]
        a = ubuf[g]
        shift = 1
        while shift < w:
            a = a + pltpu.roll(a, shift, axis=0)
            shift *= 2
        win.append(a.reshape(nseq, ext, POOL_CG)[:, POOL_PAD:, :].reshape(rows, POOL_CG))
    inv_cnt = [1.0 / w for w in POOL_WINDOWS]
    pooled = _pool_out(win, u, inv_cnt, pool_w_ref, pool_scale_ref[...])

    x_mid = _mix_out(x, attn, pooled, g_attn_ref[...], g_pool_ref[...], w_out_ref[...], g_post_ref[...])
    xmid_ref[...] = x_mid
    hmid_ref[...] = _rms(x_mid, g_pre_mlp_ref[...]).astype(BF16)


def _ffn_steps(x_ref, h_ref, load_p, w_up_ref, w_down_ref, g_post_ref, w_gate_ref, b_gate_ref, w_ple_ref,
               y_ref, actbuf):
    for c in range(D_FF // MXU_N):
        cols = slice(c * MXU_N, (c + 1) * MXU_N)
        up = _mm(h_ref[...], w_up_ref[:, cols])
        actbuf[:, cols] = jnp.square(jnp.maximum(up, 0.0)).astype(BF16)
        yield
    ff = []
    n_col, n_row = D_MODEL // MXU_N, D_FF // FF_CHUNK
    for c in range(n_col):
        cols = slice(c * MXU_N, (c + 1) * MXU_N)
        acc = None
        for r in range(n_row):
            rows = slice(r * FF_CHUNK, (r + 1) * FF_CHUNK)
            part = _mm(actbuf[:, rows], w_down_ref[rows, cols])
            acc = part if acc is None else acc + part
            if (c, r) != (n_col - 1, n_row - 1):
                yield
        ff.append(acc)
    x = x_ref[...] + _rms(jnp.concatenate(ff, axis=1), g_post_ref[...])
    xb = x.astype(BF16)
    yield
    ple = _mm(load_p().astype(BF16), w_ple_ref[...])
    yield
    for c in range(D_MODEL // MXU_N):
        cols = slice(c * MXU_N, (c + 1) * MXU_N)
        gate = jax.nn.sigmoid(_mm(xb, w_gate_ref[:, cols]) + b_gate_ref[:, cols])
        y_ref[:, cols] = x[:, cols] + gate * ple[:, cols]
        yield


def _interleave(first, second, pattern):
    gens = {'1': first, '2': second}
    for tag in pattern:
        next(gens[tag], None)
    for _ in first:
        pass
    for _ in second:
        pass


def _cast_weights_to_vmem(srcs, dsts):
    tasks = []
    for src, dst in zip(srcs, dsts):
        if len(src.shape) != 2:
            tasks.append((src, dst, tuple(src.shape)))
            continue
        rows, cols = src.shape
        col_block = cols if cols <= STAGE_MAX_COLS else D_MODEL
        for c0 in range(0, cols, col_block):
            for r0 in range(0, rows, STAGE_ROWS):
                view = (slice(r0, r0 + STAGE_ROWS), slice(c0, c0 + col_block))
                tasks.append((src.at[view], dst.at[view], (STAGE_ROWS, col_block)))
    shapes = sorted(set(shape for _, _, shape in tasks))
    counts = [sum(1 for t in tasks if t[2] == shape) for shape in shapes]
    slots = [STAGE_SLOTS if n >= 2 * STAGE_SLOTS else min(n, 2) for n in counts]

    def body(*scoped):
        *stages, sems = scoped
        uses = [0] * len(shapes)
        copies, frees = [], []
        last_in_slot = {}
        for t, (src, dst, shape) in enumerate(tasks):
            k = shapes.index(shape)
            slot = uses[k] % slots[k]
            uses[k] += 1
            stage = stages[k].at[slot]
            copies.append((pltpu.make_async_copy(src, stage, sems.at[k, slot]), stage, dst))
            frees.append(last_in_slot.get((k, slot)))
            last_in_slot[(k, slot)] = t
        started = 0
        for t, (copy, stage, dst) in enumerate(copies):
            while started < len(copies) and (frees[started] is None or frees[started] < t):
                copies[started][0].start()
                started += 1
            copy.wait()
            dst[...] = stage[...].astype(BF16)

    pl.run_scoped(body, *[pltpu.VMEM((n,) + shape, F32) for n, shape in zip(slots, shapes)],
                  pltpu.SemaphoreType.DMA((len(shapes), max(slots))))


N_TABLES = 5
MIXER_MATMUL_W = (1, 3, 7)
FFN_MATMUL_W = (1, 2, 4, 6)
STAGE_ROWS = 256
STAGE_MAX_COLS = 1280
STAGE_SLOTS = 4


N_SAMPLE_TILES = 2


def _prompt_layer_kernel(*refs):
    x_ref, *tables = refs[:1 + N_TABLES]
    (g_pre_mix, w_in_hbm, sinks, pool_w_hbm, pool_scale, g_attn, g_pool, w_out_hbm, g_post_mix, p_ref,
     g_pre_mlp, w_up_hbm, w_down_hbm, g_post_mlp, w_gate_hbm, b_gate, w_ple_hbm,
     xs_hbm, hs_hbm, ps_ref) = refs[1 + N_TABLES:21 + N_TABLES]
    y_ref, ys_hbm, knew_ref, vnew_ref, unew_ref = refs[21 + N_TABLES:26 + N_TABLES]
    (qbuf, kbuf, vtbuf, ubuf, pbuf, catbuf, actbuf, xmid, hmid, sems,
     w_in, pool_w, w_out, w_up, w_down, w_gate, w_ple) = refs[26 + N_TABLES:]
    tm = x_ref.shape[0]
    step = pl.program_id(0)
    last_tile = pl.num_programs(0) - 1 - N_SAMPLE_TILES
    on_sample = step < N_SAMPLE_TILES
    slot = step % 2
    sample_rows = pl.ds(pl.multiple_of(jnp.minimum(step, N_SAMPLE_TILES - 1) * tm, tm), tm)

    @pl.when(step == 0)
    def _():
        _cast_weights_to_vmem([w_in_hbm, pool_w_hbm, w_out_hbm, w_up_hbm, w_down_hbm, w_gate_hbm, w_ple_hbm],
                              [w_in, pool_w, w_out, w_up, w_down, w_gate, w_ple])

    @pl.when(on_sample)
    def _():
        loads = [pltpu.make_async_copy(xs_hbm.at[sample_rows], xmid.at[slot], sems.at[0]),
                 pltpu.make_async_copy(hs_hbm.at[sample_rows], hmid.at[slot], sems.at[1])]
        for copy in loads:
            copy.start()
        kbuf[:, 0:WINDOW, :] = jnp.zeros((2 * N_KV_HEADS, WINDOW, LANES), BF16)
        vtbuf[:, 0:WINDOW] = jnp.zeros((KV_COLS, WINDOW), BF16)
        ubuf[0:POOL_PAD, :] = jnp.zeros((POOL_PAD, POOL_WIDTH), F32)
        for copy in loads:
            copy.wait()

    def ffn_steps():
        return _ffn_steps(xmid.at[slot], hmid.at[slot], lambda: jnp.where(on_sample, ps_ref[...], p_ref[...]),
                          w_up, w_down, g_post_mlp, w_gate, b_gate, w_ple, y_ref, actbuf)

    ffn_only = (step == 0) | (step == pl.num_programs(0) - 1)

    @pl.when(ffn_only)
    def _():
        for _ in ffn_steps():
            pass

    @pl.when(jnp.logical_not(ffn_only))
    def _():
        mixer = _prompt_mixer_steps(jnp.clip(step - 1, 0, last_tile), x_ref, *tables, g_pre_mix, w_in, sinks, pool_w,
                                    pool_scale, g_attn, g_pool, w_out, g_post_mix, knew_ref, vnew_ref, unew_ref,
                                    qbuf, kbuf, vtbuf, ubuf, pbuf, catbuf, g_pre_mlp, xmid.at[1 - slot],
                                    hmid.at[1 - slot])
        _interleave(ffn_steps(), mixer, PROMPT_PATTERN)

    @pl.when(on_sample)
    def _():
        store = pltpu.make_async_copy(y_ref, ys_hbm.at[sample_rows], sems.at[2])
        store.start()
        store.wait()


def _const_spec(shape):
    zeros = (0,) * len(shape)
    return pl.BlockSpec(shape, lambda i: zeros, pipeline_mode=pl.Buffered(1))


def _rope_lane_tables(pos):
    half = HEAD_DIM // 2
    inv = ROPE_THETA ** (-np.arange(half, dtype=np.float64) / half)
    ang = pos[:, None] * inv[None, :]
    reps = LANES // half
    return (np.tile(np.cos(ang), (1, reps)).astype(np.float32), np.tile(np.sin(ang), (1, reps)).astype(np.float32))


def _rope_sign():
    lane = np.arange(LANES)
    return np.where((lane % HEAD_DIM) < HEAD_DIM // 2, -1.0, 1.0).astype(np.float32)[None, :]


def _params(dims):
    return pltpu.CompilerParams(dimension_semantics=dims, vmem_limit_bytes=VMEM_LIMIT)


def _mixer_weight_specs():
    return [
        _const_spec((1, D_MODEL)),
        _const_spec((D_MODEL, IN_COLS)),
        pl.BlockSpec(memory_space=pltpu.SMEM),
        _const_spec((len(POOL_WINDOWS), POOL_CG, POOL_CG)),
        _const_spec((1, POOL_WIDTH)),
        _const_spec((1, Q_COLS)),
        _const_spec((1, POOL_WIDTH)),
        _const_spec((D_MODEL, D_MODEL)),
        _const_spec((1, D_MODEL)),
    ]


def _ffn_weight_specs():
    return [
        _const_spec((1, D_MODEL)),
        _const_spec((D_MODEL, D_FF)),
        _const_spec((D_FF, D_MODEL)),
        _const_spec((1, D_MODEL)),
        _const_spec((D_MODEL, D_MODEL)),
        _const_spec((1, D_MODEL)),
        _const_spec((PLE_DIM, D_MODEL)),
    ]


def _prompt_layer(x, p, xs_mid, hs_mid, ps, mixer_w, ffn_w):
    seq = x.shape[0]
    tm = PROMPT_TILE
    tiles = seq // tm
    assert xs_mid.shape[0] == N_SAMPLE_TILES * tm
    cos_a, sin_a = _rope_lane_tables(np.arange(tiles, dtype=np.float64) * tm)
    cos_b, sin_b = _rope_lane_tables(np.arange(tm, dtype=np.float64))
    mixer_tile = lambda i: (jnp.clip(i - 1, 0, tiles - 1), 0)
    mixer_tile3 = lambda i: (jnp.clip(i - 1, 0, tiles - 1), 0, 0)
    ffn_tile = lambda i: (jnp.clip(i - N_SAMPLE_TILES, 0, tiles - 1), 0)
    sample_tile = lambda i: (jnp.minimum(i, N_SAMPLE_TILES - 1), 0)
    fixed = lambda i: (0, 0)
    in_specs = [
        pl.BlockSpec((tm, D_MODEL), mixer_tile),
        pl.BlockSpec((1, 1, LANES), mixer_tile3),
        pl.BlockSpec((1, 1, LANES), mixer_tile3),
        _const_spec((tm, LANES)),
        _const_spec((tm, LANES)),
        _const_spec((1, LANES)),
    ] + _mixer_weight_specs() + [pl.BlockSpec((tm, PLE_DIM), ffn_tile)] + _ffn_weight_specs() + [
        pl.BlockSpec(memory_space=pl.ANY),
        pl.BlockSpec(memory_space=pl.ANY),
        pl.BlockSpec((tm, PLE_DIM), sample_tile, pipeline_mode=pl.Buffered(1)),
    ]
    n_lead = 1 + N_TABLES
    matmul_w = [mixer_w[n] for n in MIXER_MATMUL_W] + [ffn_w[n] for n in FFN_MATMUL_W]
    for n in [n_lead + n for n in MIXER_MATMUL_W] + [n_lead + len(mixer_w) + 1 + n for n in FFN_MATMUL_W]:
        in_specs[n] = pl.BlockSpec(memory_space=pl.ANY)
    return pl.pallas_call(
        _prompt_layer_kernel,
        grid=(tiles + N_SAMPLE_TILES,),
        in_specs=in_specs,
        out_specs=[
            pl.BlockSpec((tm, D_MODEL), ffn_tile),
            pl.BlockSpec(memory_space=pl.ANY),
            pl.BlockSpec((WINDOW, KV_COLS), fixed),
            pl.BlockSpec((WINDOW, KV_COLS), fixed),
            pl.BlockSpec((POOL_BUF, POOL_WIDTH), fixed),
        ],
        out_shape=[
            jax.ShapeDtypeStruct((seq, D_MODEL), F32),
            jax.ShapeDtypeStruct(xs_mid.shape, F32),
            jax.ShapeDtypeStruct((WINDOW, KV_COLS), F32),
            jax.ShapeDtypeStruct((WINDOW, KV_COLS), F32),
            jax.ShapeDtypeStruct((POOL_BUF, POOL_WIDTH), F32),
        ],
        scratch_shapes=[
            pltpu.VMEM((tm, Q_COLS), BF16),
            pltpu.VMEM((2 * N_KV_HEADS, WINDOW + tm, LANES), BF16),
            pltpu.VMEM((KV_COLS, WINDOW + tm), BF16),
            pltpu.VMEM((POOL_PAD + tm, POOL_WIDTH), F32),
            pltpu.VMEM((tm, POOL_WIDTH), F32),
            pltpu.VMEM((tm, D_MODEL), BF16),
            pltpu.VMEM((tm, D_FF), BF16),
            pltpu.VMEM((2, tm, D_MODEL), F32),
            pltpu.VMEM((2, tm, D_MODEL), BF16),
            pltpu.SemaphoreType.DMA((3,)),
        ] + [pltpu.VMEM(w.shape, BF16) for w in matmul_w],
        compiler_params=_params(("arbitrary",)),
        name="prompt_layer",
    )(x, jnp.asarray(cos_a)[:, None, :], jnp.asarray(sin_a)[:, None, :], jnp.asarray(cos_b), jnp.asarray(sin_b),
      jnp.asarray(_rope_sign()), *mixer_w, p, *ffn_w, xs_mid, hs_mid, ps)


def _sample_mixer(x, state_kt, state_vt, state_pool_t, past_len, mixer_w, g_pre_mlp):
    nseq_all, t = state_kt.shape[0], x.shape[0] // state_kt.shape[0]
    nseq = SAMPLE_SEQS
    rows = nseq * t
    steps = nseq_all // nseq
    cos, sin = _rope_lane_tables(past_len + np.arange(rows, dtype=np.float64) % t)
    sin = sin * _rope_sign()
    row = lambda i: (i, 0)
    seq3 = lambda i: (i, 0, 0)
    pos3 = lambda i: (0, i, 0)
    mixer_specs = _mixer_weight_specs()
    for n in MIXER_MATMUL_W:
        mixer_specs[n] = pl.BlockSpec(memory_space=pl.ANY)
    return pl.pallas_call(
        _sample_mixer_kernel,
        grid=(steps,),
        in_specs=[
            pl.BlockSpec((rows, D_MODEL), row),
            _const_spec((rows, LANES)),
            _const_spec((rows, LANES)),
            pl.BlockSpec((nseq, KV_COLS, WINDOW), seq3),
            pl.BlockSpec((nseq, KV_COLS, WINDOW), seq3),
            pl.BlockSpec((POOL_BUF, nseq, POOL_WIDTH), pos3),
        ] + mixer_specs + [_const_spec((1, D_MODEL))],
        out_specs=[
            pl.BlockSpec((rows, D_MODEL), row),
            pl.BlockSpec((rows, D_MODEL), row),
            pl.BlockSpec((nseq, KV_COLS, WINDOW), seq3),
            pl.BlockSpec((nseq, KV_COLS, WINDOW), seq3),
            pl.BlockSpec((POOL_BUF, nseq, POOL_WIDTH), pos3),
        ],
        out_shape=[
            jax.ShapeDtypeStruct((nseq_all * t, D_MODEL), F32),
            jax.ShapeDtypeStruct((nseq_all * t, D_MODEL), BF16),
            jax.ShapeDtypeStruct((nseq_all, KV_COLS, WINDOW), F32),
            jax.ShapeDtypeStruct((nseq_all, KV_COLS, WINDOW), F32),
            jax.ShapeDtypeStruct((POOL_BUF, nseq_all, POOL_WIDTH), F32),
        ],
        scratch_shapes=[pltpu.VMEM((len(POOL_WINDOWS), nseq * (POOL_PAD + t), POOL_CG), F32)]
        + [pltpu.VMEM(mixer_w[n].shape, BF16) for n in MIXER_MATMUL_W],
        compiler_params=_params(("arbitrary",)),
        name="sample_mixer",
    )(x, jnp.asarray(cos), jnp.asarray(sin), state_kt, state_vt, state_pool_t, *mixer_w, g_pre_mlp)


def kernel(x_prompt, x_sample, state_k, state_v, state_pool, p_prompt, p_sample, w_in, attn_sinks, pool_w,
           pool_scale, g_attn_out, g_pool_out, w_out, g_pre_mix, g_post_mix, g_pre_mlp, g_post_mlp, w_up, w_down,
           w_ple, w_ple_gate, b_ple_gate):
    depth = w_in.shape[0]
    batch, seq, _ = x_prompt.shape
    dec_batch, dec_seq, _ = x_sample.shape
    assert depth == 1 and batch == 1

    xp = x_prompt.reshape(seq, D_MODEL)
    xs = x_sample.reshape(dec_batch * dec_seq, D_MODEL)
    i = 0
    mixer_w = [
        g_pre_mix[i][None, :], w_in[i], attn_sinks[i], pool_w[i], pool_scale[i][None, :], g_attn_out[i][None, :],
        g_pool_out[i][None, :], w_out[i], g_post_mix[i][None, :],
    ]
    ffn_w = [
        g_pre_mlp[i][None, :], w_up[i], w_down[i], g_post_mlp[i][None, :], w_ple_gate[i], b_ple_gate[i][None, :],
        w_ple[i],
    ]

    xs_mid, hs_mid, ks_t, vs_t, us_t = _sample_mixer(
        xs, state_k[i].reshape(dec_batch, WINDOW, KV_COLS).transpose(0, 2, 1),
        state_v[i].reshape(dec_batch, WINDOW, KV_COLS).transpose(0, 2, 1),
        state_pool[i].transpose(1, 0, 2), float(PAST_LEN), mixer_w, ffn_w[0])
    ks, vs, us = ks_t.transpose(0, 2, 1), vs_t.transpose(0, 2, 1), us_t.transpose(1, 0, 2)
    yp, ys, kp, vp, up = _prompt_layer(xp, p_prompt[i, 0], xs_mid, hs_mid,
                                       p_sample[i].reshape(dec_batch * dec_seq, PLE_DIM), mixer_w, ffn_w)

    return (
        yp.reshape(batch, seq, D_MODEL),
        ys.reshape(dec_batch, dec_seq, D_MODEL),
        kp.T.reshape(depth, batch, WINDOW, N_KV_HEADS, HEAD_DIM),
        vp.T.reshape(depth, batch, WINDOW, N_KV_HEADS, HEAD_DIM),
        up.reshape(depth, batch, POOL_BUF, POOL_WIDTH),
        ks.reshape(depth, dec_batch, WINDOW, N_KV_HEADS, HEAD_DIM),
        vs.reshape(depth, dec_batch, WINDOW, N_KV_HEADS, HEAD_DIM),
        us.reshape(depth, dec_batch, POOL_BUF, POOL_WIDTH),
    )
```

```python
import jax
import jax.numpy as jnp
import numpy as np
from jax import lax
from jax.experimental import pallas as pl
from jax.experimental.pallas import tpu as pltpu

D_MODEL = 1024
HEAD_DIM = 64
N_HEADS = 8
N_KV_HEADS = 2
GROUP = N_HEADS // N_KV_HEADS
WINDOW = 128
Q_COLS = N_HEADS * HEAD_DIM
KV_COLS = N_KV_HEADS * HEAD_DIM
POOL_WIDTH = 512
POOL_WINDOWS = (2, 4, 8, 16)
POOL_CG = POOL_WIDTH // len(POOL_WINDOWS)
POOL_BUF = max(POOL_WINDOWS) - 1
POOL_PAD = POOL_BUF + 1
IN_COLS = Q_COLS + 2 * KV_COLS + POOL_WIDTH
D_FF = 4 * D_MODEL
FF_CHUNK = 1024
PLE_DIM = 256
ROPE_THETA = 10000.0
PAST_LEN = 16384
EPS = 1e-6
NEG = -1e30
LANES = 128
MXU_N = 256

ATTN_LOOKAHEAD = 4
PROMPT_PATTERN = "11" + "21" * (5 + ATTN_LOOKAHEAD + 16 + 4) + "11" + "22" + "11" + "222" + "11"

PROMPT_TILE = 512
SAMPLE_SEQS = 32
VMEM_LIMIT = 58 * 1024 * 1024

BF16 = jnp.bfloat16
F32 = jnp.float32


def _rms(x, g):
    y = x * lax.rsqrt(jnp.mean(x * x, axis=-1, keepdims=True) + EPS)
    return y * g


def _mm(a, w):
    return jnp.dot(a, w, preferred_element_type=F32)


def _rope_tables(cos_a, sin_a, cos_b, sin_b, sign):
    cos_t = cos_a * cos_b - sin_a * sin_b
    sin_t = (sin_a * cos_b + cos_a * sin_b) * sign
    return cos_t, sin_t


def _rope(x, cos_t, sin_t, first_half):
    n = x.shape[1] // LANES
    width = x.shape[1]
    partner = jnp.where(first_half, pltpu.roll(x, width - HEAD_DIM // 2, axis=1), pltpu.roll(x, HEAD_DIM // 2, axis=1))
    if n > 1:
        cos_t = jnp.concatenate([cos_t] * n, axis=1)
        sin_t = jnp.concatenate([sin_t] * n, axis=1)
    return x * cos_t + partner * sin_t


def _first_half_mask(rows, width):
    lane = lax.broadcasted_iota(jnp.int32, (rows, width), 1)
    return (lane % HEAD_DIM) < (HEAD_DIM // 2)


def _project(x, g_pre, w_in, tables):
    rows = x.shape[0]
    h = _rms(x, g_pre).astype(BF16)
    proj = _mm(h, w_in)
    cos_t, sin_t = tables
    q = _rope(proj[:, :Q_COLS], cos_t, sin_t, _first_half_mask(rows, Q_COLS)) * (HEAD_DIM ** -0.5)
    k = _rope(proj[:, Q_COLS:Q_COLS + KV_COLS], cos_t, sin_t, _first_half_mask(rows, KV_COLS))
    v = proj[:, Q_COLS + KV_COLS:Q_COLS + 2 * KV_COLS]
    u = proj[:, Q_COLS + 2 * KV_COLS:]
    return q, k, v, u


def _pool_out(win, tok, inv_cnt, pool_w_ref, pool_scale):
    outs = []
    for g in range(len(POOL_WINDOWS)):
        r = win[g] * inv_cnt[g] - tok[:, g * POOL_CG:(g + 1) * POOL_CG]
        outs.append(_mm(r.astype(BF16), pool_w_ref[g]))
    return jnp.concatenate(outs, axis=1) * pool_scale


def _mix_out(x, attn, pooled, g_attn, g_pool, w_out, g_post):
    cat = jnp.concatenate([_rms(attn, g_attn), _rms(pooled, g_pool)], axis=1).astype(BF16)
    return x + _rms(_mm(cat, w_out), g_post)


def _softmax_parts(s_list, sink):
    m = sink
    for s in s_list:
        m = jnp.maximum(m, jnp.max(s, axis=-1, keepdims=True))
    e_list = [jnp.exp(s - m) for s in s_list]
    den = jnp.exp(sink - m)
    for e in e_list:
        den = den + jnp.sum(e, axis=-1, keepdims=True)
    return e_list, den


def _prompt_mixer_steps(tile, x_ref, cos_a_ref, sin_a_ref, cos_b_ref, sin_b_ref, sign_ref, g_pre_ref, w_in_ref,
                        sinks_ref, pool_w_ref, pool_scale_ref, g_attn_ref, g_pool_ref, w_out_ref, g_post_ref,
                        knew_ref, vnew_ref, unew_ref, qbuf, kbuf, vtbuf, ubuf, pbuf, catbuf,
                        g_pre_mlp_ref, xmid_ref, hmid_ref):
    tm = x_ref.shape[0]
    nblk = tm // WINDOW
    half_lanes = LANES // 2
    base = tile * tm

    h = _rms(x_ref[...], g_pre_ref[...]).astype(BF16)
    cos_t, sin_t = _rope_tables(cos_a_ref[0], sin_a_ref[0], cos_b_ref[...], sin_b_ref[...], sign_ref[...])

    kv = _mm(h, w_in_ref[:, Q_COLS:Q_COLS + 2 * KV_COLS])
    k = _rope(kv[:, :KV_COLS], cos_t, sin_t, _first_half_mask(tm, KV_COLS))
    v = kv[:, KV_COLS:]
    knew_ref[...] = k[tm - WINDOW:, :].T
    vnew_ref[...] = v[tm - WINDOW:, :].T
    low = lax.broadcasted_iota(jnp.int32, (tm, LANES), 1) < half_lanes
    k_swapped = pltpu.roll(k, half_lanes, axis=1)
    kbuf[0, WINDOW:, :] = jnp.where(low, k, 0.0).astype(BF16)
    kbuf[1, WINDOW:, :] = jnp.where(low, 0.0, k_swapped).astype(BF16)
    kbuf[2, WINDOW:, :] = jnp.where(low, k_swapped, 0.0).astype(BF16)
    kbuf[3, WINDOW:, :] = jnp.where(low, 0.0, k).astype(BF16)
    vtbuf[:, WINDOW:] = v.T.astype(BF16)
    yield

    u_col0 = Q_COLS + 2 * KV_COLS
    for c in range(Q_COLS // MXU_N):
        cols = slice(c * MXU_N, (c + 1) * MXU_N)
        q = _rope(_mm(h, w_in_ref[:, cols]), cos_t, sin_t, _first_half_mask(tm, MXU_N)) * (HEAD_DIM ** -0.5)
        qbuf[:, cols] = q.astype(BF16)
        yield
    for c in range(POOL_WIDTH // MXU_N):
        cols = slice(c * MXU_N, (c + 1) * MXU_N)
        ubuf[POOL_PAD:, cols] = _mm(h, w_in_ref[:, u_col0 + c * MXU_N:u_col0 + (c + 1) * MXU_N])
        yield

    pos = base + lax.broadcasted_iota(jnp.int32, (tm, 1), 0)
    pool_scale = pool_scale_ref[...]
    for g, w in enumerate(POOL_WINDOWS):
        cols = slice(g * POOL_CG, (g + 1) * POOL_CG)
        a = ubuf[:, cols]
        shift = 1
        while shift < w:
            a = a + pltpu.roll(a, shift, axis=0)
            shift *= 2
        inv_cnt = 1.0 / jnp.minimum(pos + 1, w).astype(F32)
        r = a[POOL_PAD:, :] * inv_cnt - ubuf[POOL_PAD:, cols]
        pbuf[:, cols] = _mm(r.astype(BF16), pool_w_ref[g]) * pool_scale[:, cols]
        yield
    catbuf[:, Q_COLS:] = _rms(pbuf[...], g_pool_ref[...]).astype(BF16)

    c_i = lax.broadcasted_iota(jnp.int32, (2 * WINDOW, 2 * WINDOW), 0)
    r_i = lax.broadcasted_iota(jnp.int32, (2 * WINDOW, 2 * WINDOW), 1) % WINDOW
    d_i = c_i - r_i
    band = (d_i >= 1) & (d_i <= WINDOW)
    band0 = band & (c_i >= WINDOW - base)

    def scores(j, h_kv, par):
        rows = slice(j * WINDOW, (j + 1) * WINDOW)
        keys = slice(j * WINDOW, (j + 2) * WINDOW)
        q2 = jnp.concatenate([qbuf[rows, (2 * h_kv) * LANES:(2 * h_kv + 1) * LANES],
                              qbuf[rows, (2 * h_kv + 1) * LANES:(2 * h_kv + 2) * LANES]], axis=0)
        return lax.dot_general(kbuf[2 * h_kv + par, keys, :], q2, (((1,), (1,)), ((), ())),
                               preferred_element_type=F32)

    units = [(j, h_kv, par) for j in range(nblk) for h_kv in range(N_KV_HEADS) for par in range(2)]
    pending = []
    for n in range(min(ATTN_LOOKAHEAD, len(units))):
        pending.append(scores(*units[n]))
        yield
    heads = [None] * N_HEADS
    for n, (j, h_kv, par) in enumerate(units):
        st = pending.pop(0)
        if n + ATTN_LOOKAHEAD < len(units):
            pending.append(scores(*units[n + ATTN_LOOKAHEAD]))
        mask = band0 if j == 0 else band
        st = jnp.where(mask, st, NEG)
        sink = jnp.concatenate([jnp.full((1, WINDOW), sinks_ref[GROUP * h_kv + par], F32),
                                jnp.full((1, WINDOW), sinks_ref[GROUP * h_kv + 2 + par], F32)], axis=1)
        m = jnp.maximum(jnp.max(st, axis=0, keepdims=True), sink)
        e = jnp.exp(st - m)
        den = jnp.sum(e, axis=0, keepdims=True) + jnp.exp(sink - m)
        vt = vtbuf[h_kv * HEAD_DIM:(h_kv + 1) * HEAD_DIM, j * WINDOW:(j + 2) * WINDOW]
        ot = _mm(vt, e.astype(BF16)) / den
        heads[GROUP * h_kv + par] = ot[:, :WINDOW]
        heads[GROUP * h_kv + 2 + par] = ot[:, WINDOW:]
        if h_kv == N_KV_HEADS - 1 and par == 1:
            attn = jnp.concatenate(heads, axis=0).T
            catbuf[j * WINDOW:(j + 1) * WINDOW, :Q_COLS] = _rms(attn, g_attn_ref[...]).astype(BF16)
        yield

    mix = []
    for c in range(D_MODEL // MXU_N):
        mix.append(_mm(catbuf[...], w_out_ref[:, c * MXU_N:(c + 1) * MXU_N]))
        yield
    x_mid = x_ref[...] + _rms(jnp.concatenate(mix, axis=1), g_post_ref[...])

    unew_ref[...] = ubuf[tm + 1:tm + POOL_PAD, :]
    kbuf[:, 0:WINDOW, :] = kbuf[:, tm:tm + WINDOW, :]
    vtbuf[:, 0:WINDOW] = vtbuf[:, tm:tm + WINDOW]
    ubuf[0:POOL_PAD, :] = ubuf[tm:tm + POOL_PAD, :]
    xmid_ref[...] = x_mid
    hmid_ref[...] = _rms(x_mid, g_pre_mlp_ref[...]).astype(BF16)


def _sample_mixer_kernel(x_ref, cos_ref, sin_ref, skt_ref, svt_ref, sp_ref, g_pre_ref, w_in_hbm,
                         sinks_ref, pool_w_hbm, pool_scale_ref, g_attn_ref, g_pool_ref, w_out_hbm, g_post_ref,
                         g_pre_mlp_ref, xmid_ref, hmid_ref, knew_ref, vnew_ref, unew_ref,
                         ubuf, w_in_ref, pool_w_ref, w_out_ref):
    @pl.when(pl.program_id(0) == 0)
    def _():
        _cast_weights_to_vmem([w_in_hbm, pool_w_hbm, w_out_hbm], [w_in_ref, pool_w_ref, w_out_ref])

    rows = x_ref.shape[0]
    nseq = skt_ref.shape[0]
    t = rows // nseq
    x = x_ref[...]
    q, k, v, u = _project(x, g_pre_ref[...], w_in_ref[...], (cos_ref[...], sin_ref[...]))

    k3 = k.reshape(nseq, t, KV_COLS)
    v3 = v.reshape(nseq, t, KV_COLS)
    sk = jnp.swapaxes(skt_ref[...], 1, 2)
    sv = jnp.swapaxes(svt_ref[...], 1, 2)
    knew_ref[...] = jnp.swapaxes(jnp.concatenate([sk[:, t:, :], k3], axis=1), 1, 2)
    vnew_ref[...] = jnp.swapaxes(jnp.concatenate([sv[:, t:, :], v3], axis=1), 1, 2)

    r_i = lax.broadcasted_iota(jnp.int32, (GROUP * t, WINDOW), 0) % t
    c_i = lax.broadcasted_iota(jnp.int32, (GROUP * t, WINDOW), 1)
    mask_old = (c_i > r_i)[None]
    r_n = lax.broadcasted_iota(jnp.int32, (GROUP * t, t), 0) % t
    c_n = lax.broadcasted_iota(jnp.int32, (GROUP * t, t), 1)
    mask_new = (c_n <= r_n)[None]

    qb = q.astype(BF16)
    kb = k3.astype(BF16)
    vb = v3.astype(BF16)
    heads = [None] * N_HEADS
    scores = []
    for h in range(N_KV_HEADS):
        lanes = slice(h * HEAD_DIM, (h + 1) * HEAD_DIM)
        q4 = jnp.concatenate(
            [qb[:, (GROUP * h + g) * HEAD_DIM:(GROUP * h + g + 1) * HEAD_DIM].reshape(nseq, t, HEAD_DIM)
             for g in range(GROUP)], axis=1)
        s_old = jnp.einsum('bqd,bkd->bqk', q4, sk[:, :, lanes].astype(BF16), preferred_element_type=F32)
        s_new = jnp.einsum('bqd,bkd->bqk', q4, kb[:, :, lanes], preferred_element_type=F32)
        scores.append((s_old, s_new))
    for h, (s_old, s_new) in enumerate(scores):
        lanes = slice(h * HEAD_DIM, (h + 1) * HEAD_DIM)
        s_old = jnp.where(mask_old, s_old, NEG)
        s_new = jnp.where(mask_new, s_new, NEG)
        sink = jnp.concatenate(
            [jnp.full((1, t, 1), sinks_ref[GROUP * h + g], F32) for g in range(GROUP)], axis=1)
        (e_old, e_new), den = _softmax_parts([s_old, s_new], sink)
        o = jnp.einsum('bqk,bkd->bqd', e_old.astype(BF16), sv[:, :, lanes].astype(BF16), preferred_element_type=F32)
        o = o + jnp.einsum('bqk,bkd->bqd', e_new.astype(BF16), vb[:, :, lanes], preferred_element_type=F32)
        o = o / den
        for g in range(GROUP):
            heads[GROUP * h + g] = o[:, g * t:(g + 1) * t, :].reshape(rows, HEAD_DIM)
    attn = jnp.concatenate(heads, axis=1)

    ext = POOL_PAD + t
    per_seq = lambda j: pl.ds(j, nseq, stride=ext)
    win = []
    for g, w in enumerate(POOL_WINDOWS):
        cols = slice(g * POOL_CG, (g + 1) * POOL_CG)
        ubuf[g, per_seq(0), :] = jnp.zeros((nseq, POOL_CG), F32)
        for j in range(POOL_BUF):
            ubuf[g, per_seq(1 + j), :] = sp_ref[j, :, cols]
        for s_i in range(nseq):
            ubuf[g, s_i * ext + POOL_PAD:(s_i + 1) * ext, :] = u[s_i * t:(s_i + 1) * t, cols]
        for j in range(POOL_BUF):
            unew_ref[j, :, cols] = ubuf[g, per_seq(t + 1 + j), :]
        a = ubuf[g]
        shift = 1
        while shift < w:
            a = a + pltpu.roll(a, shift, axis=0)
            shift *= 2
        win.append(a.reshape(nseq, ext, POOL_CG)[:, POOL_PAD:, :].reshape(rows, POOL_CG))
    inv_cnt = [1.0 / w for w in POOL_WINDOWS]
    pooled = _pool_out(win, u, inv_cnt, pool_w_ref, pool_scale_ref[...])

    x_mid = _mix_out(x, attn, pooled, g_attn_ref[...], g_pool_ref[...], w_out_ref[...], g_post_ref[...])
    xmid_ref[...] = x_mid
    hmid_ref[...] = _rms(x_mid, g_pre_mlp_ref[...]).astype(BF16)


def _ffn_steps(x_ref, h_ref, load_p, w_up_ref, w_down_ref, g_post_ref, w_gate_ref, b_gate_ref, w_ple_ref,
               y_ref, actbuf):
    for c in range(D_FF // MXU_N):
        cols = slice(c * MXU_N, (c + 1) * MXU_N)
        up = _mm(h_ref[...], w_up_ref[:, cols])
        actbuf[:, cols] = jnp.square(jnp.maximum(up, 0.0)).astype(BF16)
        yield
    ff = []
    n_col, n_row = D_MODEL // MXU_N, D_FF // FF_CHUNK
    for c in range(n_col):
        cols = slice(c * MXU_N, (c + 1) * MXU_N)
        acc = None
        for r in range(n_row):
            rows = slice(r * FF_CHUNK, (r + 1) * FF_CHUNK)
            part = _mm(actbuf[:, rows], w_down_ref[rows, cols])
            acc = part if acc is None else acc + part
            if (c, r) != (n_col - 1, n_row - 1):
                yield
        ff.append(acc)
    x = x_ref[...] + _rms(jnp.concatenate(ff, axis=1), g_post_ref[...])
    xb = x.astype(BF16)
    yield
    ple = _mm(load_p().astype(BF16), w_ple_ref[...])
    yield
    for c in range(D_MODEL // MXU_N):
        cols = slice(c * MXU_N, (c + 1) * MXU_N)
        gate = jax.nn.sigmoid(_mm(xb, w_gate_ref[:, cols]) + b_gate_ref[:, cols])
        y_ref[:, cols] = x[:, cols] + gate * ple[:, cols]
        yield


def _interleave(first, second, pattern):
    gens = {'1': first, '2': second}
    for tag in pattern:
        next(gens[tag], None)
    for _ in first:
        pass
    for _ in second:
        pass


def _cast_weights_to_vmem(srcs, dsts):
    tasks = []
    for src, dst in zip(srcs, dsts):
        if len(src.shape) != 2:
            tasks.append((src, dst, tuple(src.shape)))
            continue
        rows, cols = src.shape
        col_block = cols if cols <= STAGE_MAX_COLS else D_MODEL
        for c0 in range(0, cols, col_block):
            for r0 in range(0, rows, STAGE_ROWS):
                view = (slice(r0, r0 + STAGE_ROWS), slice(c0, c0 + col_block))
                tasks.append((src.at[view], dst.at[view], (STAGE_ROWS, col_block)))
    shapes = sorted(set(shape for _, _, shape in tasks))
    counts = [sum(1 for t in tasks if t[2] == shape) for shape in shapes]
    slots = [STAGE_SLOTS if n >= 2 * STAGE_SLOTS else min(n, 2) for n in counts]

    def body(*scoped):
        *stages, sems = scoped
        uses = [0] * len(shapes)
        copies, frees = [], []
        last_in_slot = {}
        for t, (src, dst, shape) in enumerate(tasks):
            k = shapes.index(shape)
            slot = uses[k] % slots[k]
            uses[k] += 1
            stage = stages[k].at[slot]
            copies.append((pltpu.make_async_copy(src, stage, sems.at[k, slot]), stage, dst))
            frees.append(last_in_slot.get((k, slot)))
            last_in_slot[(k, slot)] = t
        started = 0
        for t, (copy, stage, dst) in enumerate(copies):
            while started < len(copies) and (frees[started] is None or frees[started] < t):
                copies[started][0].start()
                started += 1
            copy.wait()
            dst[...] = stage[...].astype(BF16)

    pl.run_scoped(body, *[pltpu.VMEM((n,) + shape, F32) for n, shape in zip(slots, shapes)],
                  pltpu.SemaphoreType.DMA((len(shapes), max(slots))))


N_TABLES = 5
MIXER_MATMUL_W = (1, 3, 7)
FFN_MATMUL_W = (1, 2, 4, 6)
STAGE_ROWS = 256
STAGE_MAX_COLS = 1280
STAGE_SLOTS = 6


N_SAMPLE_TILES = 2


def _prompt_layer_kernel(*refs):
    x_ref, *tables = refs[:1 + N_TABLES]
    (g_pre_mix, w_in_hbm, sinks, pool_w_hbm, pool_scale, g_attn, g_pool, w_out_hbm, g_post_mix, p_ref,
     g_pre_mlp, w_up_hbm, w_down_hbm, g_post_mlp, w_gate_hbm, b_gate, w_ple_hbm,
     xs_hbm, hs_hbm, ps_ref) = refs[1 + N_TABLES:21 + N_TABLES]
    y_ref, ys_hbm, knew_ref, vnew_ref, unew_ref = refs[21 + N_TABLES:26 + N_TABLES]
    (qbuf, kbuf, vtbuf, ubuf, pbuf, catbuf, actbuf, xmid, hmid, sems,
     w_in, pool_w, w_out, w_up, w_down, w_gate, w_ple) = refs[26 + N_TABLES:]
    tm = x_ref.shape[0]
    step = pl.program_id(0)
    last_tile = pl.num_programs(0) - 1 - N_SAMPLE_TILES
    on_sample = step < N_SAMPLE_TILES
    slot = step % 2
    sample_rows = pl.ds(pl.multiple_of(jnp.minimum(step, N_SAMPLE_TILES - 1) * tm, tm), tm)

    def sample_loads(n):
        rows = pl.ds(n * tm, tm)
        return [pltpu.make_async_copy(xs_hbm.at[rows], xmid.at[n], sems.at[2 * n]),
                pltpu.make_async_copy(hs_hbm.at[rows], hmid.at[n], sems.at[2 * n + 1])]

    @pl.when(step == 0)
    def _():
        for n in range(N_SAMPLE_TILES):
            for copy in sample_loads(n):
                copy.start()
        _cast_weights_to_vmem([w_in_hbm, pool_w_hbm, w_out_hbm, w_up_hbm, w_down_hbm, w_gate_hbm, w_ple_hbm],
                              [w_in, pool_w, w_out, w_up, w_down, w_gate, w_ple])
        for copy in sample_loads(0):
            copy.wait()

    @pl.when(step == 1)
    def _():
        kbuf[:, 0:WINDOW, :] = jnp.zeros((2 * N_KV_HEADS, WINDOW, LANES), BF16)
        vtbuf[:, 0:WINDOW] = jnp.zeros((KV_COLS, WINDOW), BF16)
        ubuf[0:POOL_PAD, :] = jnp.zeros((POOL_PAD, POOL_WIDTH), F32)
        for copy in sample_loads(1):
            copy.wait()

    def ffn_steps():
        return _ffn_steps(xmid.at[slot], hmid.at[slot], lambda: jnp.where(on_sample, ps_ref[...], p_ref[...]),
                          w_up, w_down, g_post_mlp, w_gate, b_gate, w_ple, y_ref, actbuf)

    ffn_only = (step == 0) | (step == pl.num_programs(0) - 1)

    @pl.when(ffn_only)
    def _():
        for _ in ffn_steps():
            pass

    @pl.when(jnp.logical_not(ffn_only))
    def _():
        mixer = _prompt_mixer_steps(jnp.clip(step - 1, 0, last_tile), x_ref, *tables, g_pre_mix, w_in, sinks, pool_w,
                                    pool_scale, g_attn, g_pool, w_out, g_post_mix, knew_ref, vnew_ref, unew_ref,
                                    qbuf, kbuf, vtbuf, ubuf, pbuf, catbuf, g_pre_mlp, xmid.at[1 - slot],
                                    hmid.at[1 - slot])
        _interleave(ffn_steps(), mixer, PROMPT_PATTERN)

    @pl.when(on_sample)
    def _():
        store = pltpu.make_async_copy(y_ref, ys_hbm.at[sample_rows], sems.at[2 * N_SAMPLE_TILES])
        store.start()
        store.wait()


def _const_spec(shape):
    zeros = (0,) * len(shape)
    return pl.BlockSpec(shape, lambda i: zeros, pipeline_mode=pl.Buffered(1))


def _rope_lane_tables(pos):
    half = HEAD_DIM // 2
    inv = ROPE_THETA ** (-np.arange(half, dtype=np.float64) / half)
    ang = pos[:, None] * inv[None, :]
    reps = LANES // half
    return (np.tile(np.cos(ang), (1, reps)).astype(np.float32), np.tile(np.sin(ang), (1, reps)).astype(np.float32))


def _rope_sign():
    lane = np.arange(LANES)
    return np.where((lane % HEAD_DIM) < HEAD_DIM // 2, -1.0, 1.0).astype(np.float32)[None, :]


def _params(dims):
    return pltpu.CompilerParams(dimension_semantics=dims, vmem_limit_bytes=VMEM_LIMIT)


def _mixer_weight_specs():
    return [
        _const_spec((1, D_MODEL)),
        _const_spec((D_MODEL, IN_COLS)),
        pl.BlockSpec(memory_space=pltpu.SMEM),
        _const_spec((len(POOL_WINDOWS), POOL_CG, POOL_CG)),
        _const_spec((1, POOL_WIDTH)),
        _const_spec((1, Q_COLS)),
        _const_spec((1, POOL_WIDTH)),
        _const_spec((D_MODEL, D_MODEL)),
        _const_spec((1, D_MODEL)),
    ]


def _ffn_weight_specs():
    return [
        _const_spec((1, D_MODEL)),
        _const_spec((D_MODEL, D_FF)),
        _const_spec((D_FF, D_MODEL)),
        _const_spec((1, D_MODEL)),
        _const_spec((D_MODEL, D_MODEL)),
        _const_spec((1, D_MODEL)),
        _const_spec((PLE_DIM, D_MODEL)),
    ]


def _prompt_layer(x, p, xs_mid, hs_mid, ps, mixer_w, ffn_w):
    seq = x.shape[0]
    tm = PROMPT_TILE
    tiles = seq // tm
    assert xs_mid.shape[0] == N_SAMPLE_TILES * tm
    cos_a, sin_a = _rope_lane_tables(np.arange(tiles, dtype=np.float64) * tm)
    cos_b, sin_b = _rope_lane_tables(np.arange(tm, dtype=np.float64))
    mixer_tile = lambda i: (jnp.clip(i - 1, 0, tiles - 1), 0)
    mixer_tile3 = lambda i: (jnp.clip(i - 1, 0, tiles - 1), 0, 0)
    ffn_tile = lambda i: (jnp.clip(i - N_SAMPLE_TILES, 0, tiles - 1), 0)
    sample_tile = lambda i: (jnp.minimum(i, N_SAMPLE_TILES - 1), 0)
    fixed = lambda i: (0, 0)
    in_specs = [
        pl.BlockSpec((tm, D_MODEL), mixer_tile),
        pl.BlockSpec((1, 1, LANES), mixer_tile3),
        pl.BlockSpec((1, 1, LANES), mixer_tile3),
        _const_spec((tm, LANES)),
        _const_spec((tm, LANES)),
        _const_spec((1, LANES)),
    ] + _mixer_weight_specs() + [pl.BlockSpec((tm, PLE_DIM), ffn_tile)] + _ffn_weight_specs() + [
        pl.BlockSpec(memory_space=pl.ANY),
        pl.BlockSpec(memory_space=pl.ANY),
        pl.BlockSpec((tm, PLE_DIM), sample_tile, pipeline_mode=pl.Buffered(1)),
    ]
    n_lead = 1 + N_TABLES
    matmul_w = [mixer_w[n] for n in MIXER_MATMUL_W] + [ffn_w[n] for n in FFN_MATMUL_W]
    for n in [n_lead + n for n in MIXER_MATMUL_W] + [n_lead + len(mixer_w) + 1 + n for n in FFN_MATMUL_W]:
        in_specs[n] = pl.BlockSpec(memory_space=pl.ANY)
    return pl.pallas_call(
        _prompt_layer_kernel,
        grid=(tiles + N_SAMPLE_TILES,),
        in_specs=in_specs,
        out_specs=[
            pl.BlockSpec((tm, D_MODEL), ffn_tile),
            pl.BlockSpec(memory_space=pl.ANY),
            pl.BlockSpec((WINDOW, KV_COLS), fixed),
            pl.BlockSpec((WINDOW, KV_COLS), fixed),
            pl.BlockSpec((POOL_BUF, POOL_WIDTH), fixed),
        ],
        out_shape=[
            jax.ShapeDtypeStruct((seq, D_MODEL), F32),
            jax.ShapeDtypeStruct(xs_mid.shape, F32),
            jax.ShapeDtypeStruct((WINDOW, KV_COLS), F32),
            jax.ShapeDtypeStruct((WINDOW, KV_COLS), F32),
            jax.ShapeDtypeStruct((POOL_BUF, POOL_WIDTH), F32),
        ],
        scratch_shapes=[
            pltpu.VMEM((tm, Q_COLS), BF16),
            pltpu.VMEM((2 * N_KV_HEADS, WINDOW + tm, LANES), BF16),
            pltpu.VMEM((KV_COLS, WINDOW + tm), BF16),
            pltpu.VMEM((POOL_PAD + tm, POOL_WIDTH), F32),
            pltpu.VMEM((tm, POOL_WIDTH), F32),
            pltpu.VMEM((tm, D_MODEL), BF16),
            pltpu.VMEM((tm, D_FF), BF16),
            pltpu.VMEM((2, tm, D_MODEL), F32),
            pltpu.VMEM((2, tm, D_MODEL), BF16),
            pltpu.SemaphoreType.DMA((2 * N_SAMPLE_TILES + 1,)),
        ] + [pltpu.VMEM(w.shape, BF16) for w in matmul_w],
        compiler_params=_params(("arbitrary",)),
        name="prompt_layer",
    )(x, jnp.asarray(cos_a)[:, None, :], jnp.asarray(sin_a)[:, None, :], jnp.asarray(cos_b), jnp.asarray(sin_b),
      jnp.asarray(_rope_sign()), *mixer_w, p, *ffn_w, xs_mid, hs_mid, ps)


def _sample_mixer(x, state_kt, state_vt, state_pool_t, past_len, mixer_w, g_pre_mlp):
    nseq_all, t = state_kt.shape[0], x.shape[0] // state_kt.shape[0]
    nseq = SAMPLE_SEQS
    rows = nseq * t
    steps = nseq_all // nseq
    cos, sin = _rope_lane_tables(past_len + np.arange(rows, dtype=np.float64) % t)
    sin = sin * _rope_sign()
    row = lambda i: (i, 0)
    seq3 = lambda i: (i, 0, 0)
    pos3 = lambda i: (0, i, 0)
    mixer_specs = _mixer_weight_specs()
    for n in MIXER_MATMUL_W:
        mixer_specs[n] = pl.BlockSpec(memory_space=pl.ANY)
    return pl.pallas_call(
        _sample_mixer_kernel,
        grid=(steps,),
        in_specs=[
            pl.BlockSpec((rows, D_MODEL), row),
            _const_spec((rows, LANES)),
            _const_spec((rows, LANES)),
            pl.BlockSpec((nseq, KV_COLS, WINDOW), seq3),
            pl.BlockSpec((nseq, KV_COLS, WINDOW), seq3),
            pl.BlockSpec((POOL_BUF, nseq, POOL_WIDTH), pos3),
        ] + mixer_specs + [_const_spec((1, D_MODEL))],
        out_specs=[
            pl.BlockSpec((rows, D_MODEL), row),
            pl.BlockSpec((rows, D_MODEL), row),
            pl.BlockSpec((nseq, KV_COLS, WINDOW), seq3),
            pl.BlockSpec((nseq, KV_COLS, WINDOW), seq3),
            pl.BlockSpec((POOL_BUF, nseq, POOL_WIDTH), pos3),
        ],
        out_shape=[
            jax.ShapeDtypeStruct((nseq_all * t, D_MODEL), F32),
            jax.ShapeDtypeStruct((nseq_all * t, D_MODEL), BF16),
            jax.ShapeDtypeStruct((nseq_all, KV_COLS, WINDOW), F32),
            jax.ShapeDtypeStruct((nseq_all, KV_COLS, WINDOW), F32),
            jax.ShapeDtypeStruct((POOL_BUF, nseq_all, POOL_WIDTH), F32),
        ],
        scratch_shapes=[pltpu.VMEM((len(POOL_WINDOWS), nseq * (POOL_PAD + t), POOL_CG), F32)]
        + [pltpu.VMEM(mixer_w[n].shape, BF16) for n in MIXER_MATMUL_W],
        compiler_params=_params(("arbitrary",)),
        name="sample_mixer",
    )(x, jnp.asarray(cos), jnp.asarray(sin), state_kt, state_vt, state_pool_t, *mixer_w, g_pre_mlp)


def kernel(x_prompt, x_sample, state_k, state_v, state_pool, p_prompt, p_sample, w_in, attn_sinks, pool_w,
           pool_scale, g_attn_out, g_pool_out, w_out, g_pre_mix, g_post_mix, g_pre_mlp, g_post_mlp, w_up, w_down,
           w_ple, w_ple_gate, b_ple_gate):
    depth = w_in.shape[0]
    batch, seq, _ = x_prompt.shape
    dec_batch, dec_seq, _ = x_sample.shape
    assert depth == 1 and batch == 1

    xp = x_prompt.reshape(seq, D_MODEL)
    xs = x_sample.reshape(dec_batch * dec_seq, D_MODEL)
    i = 0
    mixer_w = [
        g_pre_mix[i][None, :], w_in[i], attn_sinks[i], pool_w[i], pool_scale[i][None, :], g_attn_out[i][None, :],
        g_pool_out[i][None, :], w_out[i], g_post_mix[i][None, :],
    ]
    ffn_w = [
        g_pre_mlp[i][None, :], w_up[i], w_down[i], g_post_mlp[i][None, :], w_ple_gate[i], b_ple_gate[i][None, :],
        w_ple[i],
    ]

    xs_mid, hs_mid, ks_t, vs_t, us_t = _sample_mixer(
        xs, state_k[i].reshape(dec_batch, WINDOW, KV_COLS).transpose(0, 2, 1),
        state_v[i].reshape(dec_batch, WINDOW, KV_COLS).transpose(0, 2, 1),
        state_pool[i].transpose(1, 0, 2), float(PAST_LEN), mixer_w, ffn_w[0])
    ks, vs, us = ks_t.transpose(0, 2, 1), vs_t.transpose(0, 2, 1), us_t.transpose(1, 0, 2)
    yp, ys, kp, vp, up = _prompt_layer(xp, p_prompt[i, 0], xs_mid, hs_mid,
                                       p_sample[i].reshape(dec_batch * dec_seq, PLE_DIM), mixer_w, ffn_w)

    return (
        yp.reshape(batch, seq, D_MODEL),
        ys.reshape(dec_batch, dec_seq, D_MODEL),
        kp.T.reshape(depth, batch, WINDOW, N_KV_HEADS, HEAD_DIM),
        vp.T.reshape(depth, batch, WINDOW, N_KV_HEADS, HEAD_DIM),
        up.reshape(depth, batch, POOL_BUF, POOL_WIDTH),
        ks.reshape(depth, dec_batch, WINDOW, N_KV_HEADS, HEAD_DIM),
        vs.reshape(depth, dec_batch, WINDOW, N_KV_HEADS, HEAD_DIM),
        us.reshape(depth, dec_batch, POOL_BUF, POOL_WIDTH),
    )
```

```python
import jax
import jax.numpy as jnp
import numpy as np
from jax import lax
from jax.experimental import pallas as pl
from jax.experimental.pallas import tpu as pltpu

D_MODEL = 1024
HEAD_DIM = 64
N_HEADS = 8
N_KV_HEADS = 2
GROUP = N_HEADS // N_KV_HEADS
WINDOW = 128
Q_COLS = N_HEADS * HEAD_DIM
KV_COLS = N_KV_HEADS * HEAD_DIM
POOL_WIDTH = 512
POOL_WINDOWS = (2, 4, 8, 16)
POOL_CG = POOL_WIDTH // len(POOL_WINDOWS)
POOL_BUF = max(POOL_WINDOWS) - 1
POOL_PAD = POOL_BUF + 1
IN_COLS = Q_COLS + 2 * KV_COLS + POOL_WIDTH
D_FF = 4 * D_MODEL
FF_CHUNK = 1024
PLE_DIM = 256
ROPE_THETA = 10000.0
PAST_LEN = 16384
EPS = 1e-6
NEG = -1e30
LANES = 128
MXU_N = 256

ATTN_LOOKAHEAD = 4
PROMPT_PATTERN = "11" + "21" * (5 + ATTN_LOOKAHEAD + 16 + 4) + "11" + "22" + "11" + "222" + "11"

PROMPT_TILE = 512
SAMPLE_SEQS = 32
VMEM_LIMIT = 58 * 1024 * 1024

BF16 = jnp.bfloat16
F32 = jnp.float32


def _rms(x, g):
    y = x * lax.rsqrt(jnp.mean(x * x, axis=-1, keepdims=True) + EPS)
    return y * g


def _mm(a, w):
    return jnp.dot(a, w, preferred_element_type=F32)


def _rope_tables(cos_a, sin_a, cos_b, sin_b, sign):
    cos_t = cos_a * cos_b - sin_a * sin_b
    sin_t = (sin_a * cos_b + cos_a * sin_b) * sign
    return cos_t, sin_t


def _rope(x, cos_t, sin_t, first_half):
    n = x.shape[1] // LANES
    width = x.shape[1]
    partner = jnp.where(first_half, pltpu.roll(x, width - HEAD_DIM // 2, axis=1), pltpu.roll(x, HEAD_DIM // 2, axis=1))
    if n > 1:
        cos_t = jnp.concatenate([cos_t] * n, axis=1)
        sin_t = jnp.concatenate([sin_t] * n, axis=1)
    return x * cos_t + partner * sin_t


def _first_half_mask(rows, width):
    lane = lax.broadcasted_iota(jnp.int32, (rows, width), 1)
    return (lane % HEAD_DIM) < (HEAD_DIM // 2)


def _project(x, g_pre, w_in, tables):
    rows = x.shape[0]
    h = _rms(x, g_pre).astype(BF16)
    proj = _mm(h, w_in)
    cos_t, sin_t = tables
    q = _rope(proj[:, :Q_COLS], cos_t, sin_t, _first_half_mask(rows, Q_COLS)) * (HEAD_DIM ** -0.5)
    k = _rope(proj[:, Q_COLS:Q_COLS + KV_COLS], cos_t, sin_t, _first_half_mask(rows, KV_COLS))
    v = proj[:, Q_COLS + KV_COLS:Q_COLS + 2 * KV_COLS]
    u = proj[:, Q_COLS + 2 * KV_COLS:]
    return q, k, v, u


def _pool_out(win, tok, inv_cnt, pool_w_ref, pool_scale):
    outs = []
    for g in range(len(POOL_WINDOWS)):
        r = win[g] * inv_cnt[g] - tok[:, g * POOL_CG:(g + 1) * POOL_CG]
        outs.append(_mm(r.astype(BF16), pool_w_ref[g]))
    return jnp.concatenate(outs, axis=1) * pool_scale


def _mix_out(x, attn, pooled, g_attn, g_pool, w_out, g_post):
    cat = jnp.concatenate([_rms(attn, g_attn), _rms(pooled, g_pool)], axis=1).astype(BF16)
    return x + _rms(_mm(cat, w_out), g_post)


def _softmax_parts(s_list, sink):
    m = sink
    for s in s_list:
        m = jnp.maximum(m, jnp.max(s, axis=-1, keepdims=True))
    e_list = [jnp.exp(s - m) for s in s_list]
    den = jnp.exp(sink - m)
    for e in e_list:
        den = den + jnp.sum(e, axis=-1, keepdims=True)
    return e_list, den


def _prompt_mixer_steps(tile, x_ref, cos_a_ref, sin_a_ref, cos_b_ref, sin_b_ref, sign_ref, g_pre_ref, w_in_ref,
                        sinks_ref, pool_w_ref, pool_scale_ref, g_attn_ref, g_pool_ref, w_out_ref, g_post_ref,
                        knew_ref, vnew_ref, unew_ref, qbuf, kbuf, vtbuf, ubuf, pbuf, catbuf,
                        g_pre_mlp_ref, xmid_ref, hmid_ref):
    tm = x_ref.shape[0]
    nblk = tm // WINDOW
    half_lanes = LANES // 2
    base = tile * tm

    h = _rms(x_ref[...], g_pre_ref[...]).astype(BF16)
    cos_t, sin_t = _rope_tables(cos_a_ref[0], sin_a_ref[0], cos_b_ref[...], sin_b_ref[...], sign_ref[...])

    kv = _mm(h, w_in_ref[:, Q_COLS:Q_COLS + 2 * KV_COLS])
    k = _rope(kv[:, :KV_COLS], cos_t, sin_t, _first_half_mask(tm, KV_COLS))
    v = kv[:, KV_COLS:]
    knew_ref[...] = k[tm - WINDOW:, :].T
    vnew_ref[...] = v[tm - WINDOW:, :].T
    low = lax.broadcasted_iota(jnp.int32, (tm, LANES), 1) < half_lanes
    k_swapped = pltpu.roll(k, half_lanes, axis=1)
    kbuf[0, WINDOW:, :] = jnp.where(low, k, 0.0).astype(BF16)
    kbuf[1, WINDOW:, :] = jnp.where(low, 0.0, k_swapped).astype(BF16)
    kbuf[2, WINDOW:, :] = jnp.where(low, k_swapped, 0.0).astype(BF16)
    kbuf[3, WINDOW:, :] = jnp.where(low, 0.0, k).astype(BF16)
    vtbuf[:, WINDOW:] = v.T.astype(BF16)
    yield

    u_col0 = Q_COLS + 2 * KV_COLS
    for c in range(Q_COLS // MXU_N):
        cols = slice(c * MXU_N, (c + 1) * MXU_N)
        q = _rope(_mm(h, w_in_ref[:, cols]), cos_t, sin_t, _first_half_mask(tm, MXU_N)) * (HEAD_DIM ** -0.5)
        qbuf[:, cols] = q.astype(BF16)
        yield
    for c in range(POOL_WIDTH // MXU_N):
        cols = slice(c * MXU_N, (c + 1) * MXU_N)
        ubuf[POOL_PAD:, cols] = _mm(h, w_in_ref[:, u_col0 + c * MXU_N:u_col0 + (c + 1) * MXU_N])
        yield

    pos = base + lax.broadcasted_iota(jnp.int32, (tm, 1), 0)
    pool_scale = pool_scale_ref[...]
    for g, w in enumerate(POOL_WINDOWS):
        cols = slice(g * POOL_CG, (g + 1) * POOL_CG)
        a = ubuf[:, cols]
        shift = 1
        while shift < w:
            a = a + pltpu.roll(a, shift, axis=0)
            shift *= 2
        inv_cnt = 1.0 / jnp.minimum(pos + 1, w).astype(F32)
        r = a[POOL_PAD:, :] * inv_cnt - ubuf[POOL_PAD:, cols]
        pbuf[:, cols] = _mm(r.astype(BF16), pool_w_ref[g]) * pool_scale[:, cols]
        yield
    catbuf[:, Q_COLS:] = _rms(pbuf[...], g_pool_ref[...]).astype(BF16)

    c_i = lax.broadcasted_iota(jnp.int32, (2 * WINDOW, 2 * WINDOW), 0)
    r_i = lax.broadcasted_iota(jnp.int32, (2 * WINDOW, 2 * WINDOW), 1) % WINDOW
    d_i = c_i - r_i
    band = (d_i >= 1) & (d_i <= WINDOW)
    band0 = band & (c_i >= WINDOW - base)

    def scores(j, h_kv, par):
        rows = slice(j * WINDOW, (j + 1) * WINDOW)
        keys = slice(j * WINDOW, (j + 2) * WINDOW)
        q2 = jnp.concatenate([qbuf[rows, (2 * h_kv) * LANES:(2 * h_kv + 1) * LANES],
                              qbuf[rows, (2 * h_kv + 1) * LANES:(2 * h_kv + 2) * LANES]], axis=0)
        return lax.dot_general(kbuf[2 * h_kv + par, keys, :], q2, (((1,), (1,)), ((), ())),
                               preferred_element_type=F32)

    units = [(j, h_kv, par) for j in range(nblk) for h_kv in range(N_KV_HEADS) for par in range(2)]
    pending = []
    for n in range(min(ATTN_LOOKAHEAD, len(units))):
        pending.append(scores(*units[n]))
        yield
    heads = [None] * N_HEADS
    for n, (j, h_kv, par) in enumerate(units):
        st = pending.pop(0)
        if n + ATTN_LOOKAHEAD < len(units):
            pending.append(scores(*units[n + ATTN_LOOKAHEAD]))
        mask = band0 if j == 0 else band
        st = jnp.where(mask, st, NEG)
        sink = jnp.concatenate([jnp.full((1, WINDOW), sinks_ref[GROUP * h_kv + par], F32),
                                jnp.full((1, WINDOW), sinks_ref[GROUP * h_kv + 2 + par], F32)], axis=1)
        m = jnp.maximum(jnp.max(st, axis=0, keepdims=True), sink)
        e = jnp.exp(st - m)
        den = jnp.sum(e, axis=0, keepdims=True) + jnp.exp(sink - m)
        vt = vtbuf[h_kv * HEAD_DIM:(h_kv + 1) * HEAD_DIM, j * WINDOW:(j + 2) * WINDOW]
        ot = _mm(vt, e.astype(BF16)) / den
        heads[GROUP * h_kv + par] = ot[:, :WINDOW]
        heads[GROUP * h_kv + 2 + par] = ot[:, WINDOW:]
        if h_kv == N_KV_HEADS - 1 and par == 1:
            attn = jnp.concatenate(heads, axis=0).T
            catbuf[j * WINDOW:(j + 1) * WINDOW, :Q_COLS] = _rms(attn, g_attn_ref[...]).astype(BF16)
        yield

    mix = []
    for c in range(D_MODEL // MXU_N):
        mix.append(_mm(catbuf[...], w_out_ref[:, c * MXU_N:(c + 1) * MXU_N]))
        yield
    x_mid = x_ref[...] + _rms(jnp.concatenate(mix, axis=1), g_post_ref[...])

    unew_ref[...] = ubuf[tm + 1:tm + POOL_PAD, :]
    kbuf[:, 0:WINDOW, :] = kbuf[:, tm:tm + WINDOW, :]
    vtbuf[:, 0:WINDOW] = vtbuf[:, tm:tm + WINDOW]
    ubuf[0:POOL_PAD, :] = ubuf[tm:tm + POOL_PAD, :]
    xmid_ref[...] = x_mid
    hmid_ref[...] = _rms(x_mid, g_pre_mlp_ref[...]).astype(BF16)


def _sample_mixer_kernel(x_ref, cos_ref, sin_ref, skt_ref, svt_ref, sp_ref, g_pre_ref, w_in_hbm,
                         sinks_ref, pool_w_hbm, pool_scale_ref, g_attn_ref, g_pool_ref, w_out_hbm, g_post_ref,
                         g_pre_mlp_ref, xmid_ref, hmid_ref, knew_ref, vnew_ref, unew_ref,
                         ubuf, w_in_ref, pool_w_ref, w_out_ref):
    @pl.when(pl.program_id(0) == 0)
    def _():
        _cast_weights_to_vmem([w_in_hbm, pool_w_hbm, w_out_hbm], [w_in_ref, pool_w_ref, w_out_ref])

    rows = x_ref.shape[0]
    nseq = skt_ref.shape[0]
    t = rows // nseq
    x = x_ref[...]
    q, k, v, u = _project(x, g_pre_ref[...], w_in_ref[...], (cos_ref[...], sin_ref[...]))

    k3 = k.reshape(nseq, t, KV_COLS)
    v3 = v.reshape(nseq, t, KV_COLS)
    sk = jnp.swapaxes(skt_ref[...], 1, 2)
    sv = jnp.swapaxes(svt_ref[...], 1, 2)
    knew_ref[...] = jnp.swapaxes(jnp.concatenate([sk[:, t:, :], k3], axis=1), 1, 2)
    vnew_ref[...] = jnp.swapaxes(jnp.concatenate([sv[:, t:, :], v3], axis=1), 1, 2)

    r_i = lax.broadcasted_iota(jnp.int32, (GROUP * t, WINDOW), 0) % t
    c_i = lax.broadcasted_iota(jnp.int32, (GROUP * t, WINDOW), 1)
    mask_old = (c_i > r_i)[None]
    r_n = lax.broadcasted_iota(jnp.int32, (GROUP * t, t), 0) % t
    c_n = lax.broadcasted_iota(jnp.int32, (GROUP * t, t), 1)
    mask_new = (c_n <= r_n)[None]

    qb = q.astype(BF16)
    kb = k3.astype(BF16)
    vb = v3.astype(BF16)
    heads = [None] * N_HEADS
    scores = []
    for h in range(N_KV_HEADS):
        lanes = slice(h * HEAD_DIM, (h + 1) * HEAD_DIM)
        q4 = jnp.concatenate(
            [qb[:, (GROUP * h + g) * HEAD_DIM:(GROUP * h + g + 1) * HEAD_DIM].reshape(nseq, t, HEAD_DIM)
             for g in range(GROUP)], axis=1)
        s_old = jnp.einsum('bqd,bkd->bqk', q4, sk[:, :, lanes].astype(BF16), preferred_element_type=F32)
        s_new = jnp.einsum('bqd,bkd->bqk', q4, kb[:, :, lanes], preferred_element_type=F32)
        scores.append((s_old, s_new))
    for h, (s_old, s_new) in enumerate(scores):
        lanes = slice(h * HEAD_DIM, (h + 1) * HEAD_DIM)
        s_old = jnp.where(mask_old, s_old, NEG)
        s_new = jnp.where(mask_new, s_new, NEG)
        sink = jnp.concatenate(
            [jnp.full((1, t, 1), sinks_ref[GROUP * h + g], F32) for g in range(GROUP)], axis=1)
        (e_old, e_new), den = _softmax_parts([s_old, s_new], sink)
        o = jnp.einsum('bqk,bkd->bqd', e_old.astype(BF16), sv[:, :, lanes].astype(BF16), preferred_element_type=F32)
        o = o + jnp.einsum('bqk,bkd->bqd', e_new.astype(BF16), vb[:, :, lanes], preferred_element_type=F32)
        o = o / den
        for g in range(GROUP):
            heads[GROUP * h + g] = o[:, g * t:(g + 1) * t, :].reshape(rows, HEAD_DIM)
    attn = jnp.concatenate(heads, axis=1)

    ext = POOL_PAD + t
    per_seq = lambda j: pl.ds(j, nseq, stride=ext)
    win = []
    for g, w in enumerate(POOL_WINDOWS):
        cols = slice(g * POOL_CG, (g + 1) * POOL_CG)
        ubuf[g, per_seq(0), :] = jnp.zeros((nseq, POOL_CG), F32)
        for j in range(POOL_BUF):
            ubuf[g, per_seq(1 + j), :] = sp_ref[j, :, cols]
        for s_i in range(nseq):
            ubuf[g, s_i * ext + POOL_PAD:(s_i + 1) * ext, :] = u[s_i * t:(s_i + 1) * t, cols]
        for j in range(POOL_BUF):
            unew_ref[j, :, cols] = ubuf[g, per_seq(t + 1 + j), :]
        a = ubuf[g]
        shift = 1
        while shift < w:
            a = a + pltpu.roll(a, shift, axis=0)
            shift *= 2
        win.append(a.reshape(nseq, ext, POOL_CG)[:, POOL_PAD:, :].reshape(rows, POOL_CG))
    inv_cnt = [1.0 / w for w in POOL_WINDOWS]
    pooled = _pool_out(win, u, inv_cnt, pool_w_ref, pool_scale_ref[...])

    x_mid = _mix_out(x, attn, pooled, g_attn_ref[...], g_pool_ref[...], w_out_ref[...], g_post_ref[...])
    xmid_ref[...] = x_mid
    hmid_ref[...] = _rms(x_mid, g_pre_mlp_ref[...]).astype(BF16)


def _ffn_steps(x_ref, h_ref, load_p, w_up_ref, w_down_ref, g_post_ref, w_gate_ref, b_gate_ref, w_ple_ref,
               y_ref, actbuf):
    for c in range(D_FF // MXU_N):
        cols = slice(c * MXU_N, (c + 1) * MXU_N)
        up = _mm(h_ref[...], w_up_ref[:, cols])
        actbuf[:, cols] = jnp.square(jnp.maximum(up, 0.0)).astype(BF16)
        yield
    ff = []
    n_col, n_row = D_MODEL // MXU_N, D_FF // FF_CHUNK
    for c in range(n_col):
        cols = slice(c * MXU_N, (c + 1) * MXU_N)
        acc = None
        for r in range(n_row):
            rows = slice(r * FF_CHUNK, (r + 1) * FF_CHUNK)
            part = _mm(actbuf[:, rows], w_down_ref[rows, cols])
            acc = part if acc is None else acc + part
            if (c, r) != (n_col - 1, n_row - 1):
                yield
        ff.append(acc)
    x = x_ref[...] + _rms(jnp.concatenate(ff, axis=1), g_post_ref[...])
    xb = x.astype(BF16)
    yield
    ple = _mm(load_p().astype(BF16), w_ple_ref[...])
    yield
    for c in range(D_MODEL // MXU_N):
        cols = slice(c * MXU_N, (c + 1) * MXU_N)
        gate = jax.nn.sigmoid(_mm(xb, w_gate_ref[:, cols]) + b_gate_ref[:, cols])
        y_ref[:, cols] = x[:, cols] + gate * ple[:, cols]
        yield


def _interleave(first, second, pattern):
    gens = {'1': first, '2': second}
    for tag in pattern:
        next(gens[tag], None)
    for _ in first:
        pass
    for _ in second:
        pass


def _cast_weights_to_vmem(srcs, dsts):
    tasks = []
    for src, dst in zip(srcs, dsts):
        if len(src.shape) != 2:
            tasks.append((src, dst, tuple(src.shape)))
            continue
        rows, cols = src.shape
        col_block = cols if cols <= STAGE_MAX_COLS else D_MODEL
        for c0 in range(0, cols, col_block):
            for r0 in range(0, rows, STAGE_ROWS):
                view = (slice(r0, r0 + STAGE_ROWS), slice(c0, c0 + col_block))
                tasks.append((src.at[view], dst.at[view], (STAGE_ROWS, col_block)))
    shapes = sorted(set(shape for _, _, shape in tasks))
    counts = [sum(1 for t in tasks if t[2] == shape) for shape in shapes]
    slots = [STAGE_SLOTS if n >= 2 * STAGE_SLOTS else min(n, 2) for n in counts]

    def body(*scoped):
        *stages, sems = scoped
        uses = [0] * len(shapes)
        copies, frees = [], []
        last_in_slot = {}
        for t, (src, dst, shape) in enumerate(tasks):
            k = shapes.index(shape)
            slot = uses[k] % slots[k]
            uses[k] += 1
            stage = stages[k].at[slot]
            copies.append((pltpu.make_async_copy(src, stage, sems.at[k, slot]), stage, dst))
            frees.append(last_in_slot.get((k, slot)))
            last_in_slot[(k, slot)] = t
        started = 0
        for t, (copy, stage, dst) in enumerate(copies):
            while started < len(copies) and (frees[started] is None or frees[started] < t):
                copies[started][0].start()
                started += 1
            copy.wait()
            dst[...] = stage[...].astype(BF16)

    pl.run_scoped(body, *[pltpu.VMEM((n,) + shape, F32) for n, shape in zip(slots, shapes)],
                  pltpu.SemaphoreType.DMA((len(shapes), max(slots))))


N_TABLES = 5
MIXER_MATMUL_W = (1, 3, 7)
FFN_MATMUL_W = (1, 2, 4, 6)
STAGE_ROWS = 256
STAGE_MAX_COLS = 1280
STAGE_SLOTS = 8


N_SAMPLE_TILES = 2


def _prompt_layer_kernel(*refs):
    x_ref, *tables = refs[:1 + N_TABLES]
    (g_pre_mix, w_in_hbm, sinks, pool_w_hbm, pool_scale, g_attn, g_pool, w_out_hbm, g_post_mix, p_ref,
     g_pre_mlp, w_up_hbm, w_down_hbm, g_post_mlp, w_gate_hbm, b_gate, w_ple_hbm,
     xs_hbm, hs_hbm, ps_ref) = refs[1 + N_TABLES:21 + N_TABLES]
    y_ref, ys_hbm, knew_ref, vnew_ref, unew_ref = refs[21 + N_TABLES:26 + N_TABLES]
    (qbuf, kbuf, vtbuf, ubuf, pbuf, catbuf, actbuf, xmid, hmid, sems,
     w_in, pool_w, w_out, w_up, w_down, w_gate, w_ple) = refs[26 + N_TABLES:]
    tm = x_ref.shape[0]
    step = pl.program_id(0)
    last_tile = pl.num_programs(0) - 1 - N_SAMPLE_TILES
    on_sample = step < N_SAMPLE_TILES
    slot = step % 2

    def sample_loads(n):
        rows = pl.ds(n * tm, tm)
        return [pltpu.make_async_copy(xs_hbm.at[rows], xmid.at[n], sems.at[2 * n]),
                pltpu.make_async_copy(hs_hbm.at[rows], hmid.at[n], sems.at[2 * n + 1])]

    def sample_store(n):
        return pltpu.make_async_copy(y_ref, ys_hbm.at[pl.ds(n * tm, tm)], sems.at[2 * N_SAMPLE_TILES + n])

    @pl.when(step == 0)
    def _():
        for n in range(N_SAMPLE_TILES):
            for copy in sample_loads(n):
                copy.start()
        _cast_weights_to_vmem([w_in_hbm, pool_w_hbm, w_out_hbm, w_up_hbm, w_down_hbm, w_gate_hbm, w_ple_hbm],
                              [w_in, pool_w, w_out, w_up, w_down, w_gate, w_ple])
        for copy in sample_loads(0):
            copy.wait()

    for n in range(N_SAMPLE_TILES):
        @pl.when(step == n + 1)
        def _(n=n):
            sample_store(n).wait()

    @pl.when(step == 1)
    def _():
        kbuf[:, 0:WINDOW, :] = jnp.zeros((2 * N_KV_HEADS, WINDOW, LANES), BF16)
        vtbuf[:, 0:WINDOW] = jnp.zeros((KV_COLS, WINDOW), BF16)
        ubuf[0:POOL_PAD, :] = jnp.zeros((POOL_PAD, POOL_WIDTH), F32)
        for copy in sample_loads(1):
            copy.wait()

    def ffn_steps():
        return _ffn_steps(xmid.at[slot], hmid.at[slot], lambda: jnp.where(on_sample, ps_ref[...], p_ref[...]),
                          w_up, w_down, g_post_mlp, w_gate, b_gate, w_ple, y_ref, actbuf)

    ffn_only = (step == 0) | (step == pl.num_programs(0) - 1)

    @pl.when(ffn_only)
    def _():
        for _ in ffn_steps():
            pass

    @pl.when(jnp.logical_not(ffn_only))
    def _():
        mixer = _prompt_mixer_steps(jnp.clip(step - 1, 0, last_tile), x_ref, *tables, g_pre_mix, w_in, sinks, pool_w,
                                    pool_scale, g_attn, g_pool, w_out, g_post_mix, knew_ref, vnew_ref, unew_ref,
                                    qbuf, kbuf, vtbuf, ubuf, pbuf, catbuf, g_pre_mlp, xmid.at[1 - slot],
                                    hmid.at[1 - slot])
        _interleave(ffn_steps(), mixer, PROMPT_PATTERN)

    for n in range(N_SAMPLE_TILES):
        @pl.when(step == n)
        def _(n=n):
            sample_store(n).start()


def _const_spec(shape):
    zeros = (0,) * len(shape)
    return pl.BlockSpec(shape, lambda i: zeros, pipeline_mode=pl.Buffered(1))


def _rope_lane_tables(pos):
    half = HEAD_DIM // 2
    inv = ROPE_THETA ** (-np.arange(half, dtype=np.float64) / half)
    ang = pos[:, None] * inv[None, :]
    reps = LANES // half
    return (np.tile(np.cos(ang), (1, reps)).astype(np.float32), np.tile(np.sin(ang), (1, reps)).astype(np.float32))


def _rope_sign():
    lane = np.arange(LANES)
    return np.where((lane % HEAD_DIM) < HEAD_DIM // 2, -1.0, 1.0).astype(np.float32)[None, :]


def _params(dims):
    return pltpu.CompilerParams(dimension_semantics=dims, vmem_limit_bytes=VMEM_LIMIT)


def _mixer_weight_specs():
    return [
        _const_spec((1, D_MODEL)),
        _const_spec((D_MODEL, IN_COLS)),
        pl.BlockSpec(memory_space=pltpu.SMEM),
        _const_spec((len(POOL_WINDOWS), POOL_CG, POOL_CG)),
        _const_spec((1, POOL_WIDTH)),
        _const_spec((1, Q_COLS)),
        _const_spec((1, POOL_WIDTH)),
        _const_spec((D_MODEL, D_MODEL)),
        _const_spec((1, D_MODEL)),
    ]


def _ffn_weight_specs():
    return [
        _const_spec((1, D_MODEL)),
        _const_spec((D_MODEL, D_FF)),
        _const_spec((D_FF, D_MODEL)),
        _const_spec((1, D_MODEL)),
        _const_spec((D_MODEL, D_MODEL)),
        _const_spec((1, D_MODEL)),
        _const_spec((PLE_DIM, D_MODEL)),
    ]


def _prompt_layer(x, p, xs_mid, hs_mid, ps, mixer_w, ffn_w):
    seq = x.shape[0]
    tm = PROMPT_TILE
    tiles = seq // tm
    assert xs_mid.shape[0] == N_SAMPLE_TILES * tm
    cos_a, sin_a = _rope_lane_tables(np.arange(tiles, dtype=np.float64) * tm)
    cos_b, sin_b = _rope_lane_tables(np.arange(tm, dtype=np.float64))
    mixer_tile = lambda i: (jnp.clip(i - 1, 0, tiles - 1), 0)
    mixer_tile3 = lambda i: (jnp.clip(i - 1, 0, tiles - 1), 0, 0)
    ffn_tile = lambda i: (jnp.clip(i - N_SAMPLE_TILES, 0, tiles - 1), 0)
    sample_tile = lambda i: (jnp.minimum(i, N_SAMPLE_TILES - 1), 0)
    fixed = lambda i: (0, 0)
    in_specs = [
        pl.BlockSpec((tm, D_MODEL), mixer_tile),
        pl.BlockSpec((1, 1, LANES), mixer_tile3),
        pl.BlockSpec((1, 1, LANES), mixer_tile3),
        _const_spec((tm, LANES)),
        _const_spec((tm, LANES)),
        _const_spec((1, LANES)),
    ] + _mixer_weight_specs() + [pl.BlockSpec((tm, PLE_DIM), ffn_tile)] + _ffn_weight_specs() + [
        pl.BlockSpec(memory_space=pl.ANY),
        pl.BlockSpec(memory_space=pl.ANY),
        pl.BlockSpec((tm, PLE_DIM), sample_tile, pipeline_mode=pl.Buffered(1)),
    ]
    n_lead = 1 + N_TABLES
    matmul_w = [mixer_w[n] for n in MIXER_MATMUL_W] + [ffn_w[n] for n in FFN_MATMUL_W]
    for n in [n_lead + n for n in MIXER_MATMUL_W] + [n_lead + len(mixer_w) + 1 + n for n in FFN_MATMUL_W]:
        in_specs[n] = pl.BlockSpec(memory_space=pl.ANY)
    return pl.pallas_call(
        _prompt_layer_kernel,
        grid=(tiles + N_SAMPLE_TILES,),
        in_specs=in_specs,
        out_specs=[
            pl.BlockSpec((tm, D_MODEL), ffn_tile),
            pl.BlockSpec(memory_space=pl.ANY),
            pl.BlockSpec((WINDOW, KV_COLS), fixed),
            pl.BlockSpec((WINDOW, KV_COLS), fixed),
            pl.BlockSpec((POOL_BUF, POOL_WIDTH), fixed),
        ],
        out_shape=[
            jax.ShapeDtypeStruct((seq, D_MODEL), F32),
            jax.ShapeDtypeStruct(xs_mid.shape, F32),
            jax.ShapeDtypeStruct((WINDOW, KV_COLS), F32),
            jax.ShapeDtypeStruct((WINDOW, KV_COLS), F32),
            jax.ShapeDtypeStruct((POOL_BUF, POOL_WIDTH), F32),
        ],
        scratch_shapes=[
            pltpu.VMEM((tm, Q_COLS), BF16),
            pltpu.VMEM((2 * N_KV_HEADS, WINDOW + tm, LANES), BF16),
            pltpu.VMEM((KV_COLS, WINDOW + tm), BF16),
            pltpu.VMEM((POOL_PAD + tm, POOL_WIDTH), F32),
            pltpu.VMEM((tm, POOL_WIDTH), F32),
            pltpu.VMEM((tm, D_MODEL), BF16),
            pltpu.VMEM((tm, D_FF), BF16),
            pltpu.VMEM((2, tm, D_MODEL), F32),
            pltpu.VMEM((2, tm, D_MODEL), BF16),
            pltpu.SemaphoreType.DMA((3 * N_SAMPLE_TILES,)),
        ] + [pltpu.VMEM(w.shape, BF16) for w in matmul_w],
        compiler_params=_params(("arbitrary",)),
        name="prompt_layer",
    )(x, jnp.asarray(cos_a)[:, None, :], jnp.asarray(sin_a)[:, None, :], jnp.asarray(cos_b), jnp.asarray(sin_b),
      jnp.asarray(_rope_sign()), *mixer_w, p, *ffn_w, xs_mid, hs_mid, ps)


def _sample_mixer(x, state_kt, state_vt, state_pool_t, past_len, mixer_w, g_pre_mlp):
    nseq_all, t = state_kt.shape[0], x.shape[0] // state_kt.shape[0]
    nseq = SAMPLE_SEQS
    rows = nseq * t
    steps = nseq_all // nseq
    cos, sin = _rope_lane_tables(past_len + np.arange(rows, dtype=np.float64) % t)
    sin = sin * _rope_sign()
    row = lambda i: (i, 0)
    seq3 = lambda i: (i, 0, 0)
    pos3 = lambda i: (0, i, 0)
    mixer_specs = _mixer_weight_specs()
    for n in MIXER_MATMUL_W:
        mixer_specs[n] = pl.BlockSpec(memory_space=pl.ANY)
    return pl.pallas_call(
        _sample_mixer_kernel,
        grid=(steps,),
        in_specs=[
            pl.BlockSpec((rows, D_MODEL), row),
            _const_spec((rows, LANES)),
            _const_spec((rows, LANES)),
            pl.BlockSpec((nseq, KV_COLS, WINDOW), seq3),
            pl.BlockSpec((nseq, KV_COLS, WINDOW), seq3),
            pl.BlockSpec((POOL_BUF, nseq, POOL_WIDTH), pos3),
        ] + mixer_specs + [_const_spec((1, D_MODEL))],
        out_specs=[
            pl.BlockSpec((rows, D_MODEL), row),
            pl.BlockSpec((rows, D_MODEL), row),
            pl.BlockSpec((nseq, KV_COLS, WINDOW), seq3),
            pl.BlockSpec((nseq, KV_COLS, WINDOW), seq3),
            pl.BlockSpec((POOL_BUF, nseq, POOL_WIDTH), pos3),
        ],
        out_shape=[
            jax.ShapeDtypeStruct((nseq_all * t, D_MODEL), F32),
            jax.ShapeDtypeStruct((nseq_all * t, D_MODEL), BF16),
            jax.ShapeDtypeStruct((nseq_all, KV_COLS, WINDOW), F32),
            jax.ShapeDtypeStruct((nseq_all, KV_COLS, WINDOW), F32),
            jax.ShapeDtypeStruct((POOL_BUF, nseq_all, POOL_WIDTH), F32),
        ],
        scratch_shapes=[pltpu.VMEM((len(POOL_WINDOWS), nseq * (POOL_PAD + t), POOL_CG), F32)]
        + [pltpu.VMEM(mixer_w[n].shape, BF16) for n in MIXER_MATMUL_W],
        compiler_params=_params(("arbitrary",)),
        name="sample_mixer",
    )(x, jnp.asarray(cos), jnp.asarray(sin), state_kt, state_vt, state_pool_t, *mixer_w, g_pre_mlp)


def kernel(x_prompt, x_sample, state_k, state_v, state_pool, p_prompt, p_sample, w_in, attn_sinks, pool_w,
           pool_scale, g_attn_out, g_pool_out, w_out, g_pre_mix, g_post_mix, g_pre_mlp, g_post_mlp, w_up, w_down,
           w_ple, w_ple_gate, b_ple_gate):
    depth = w_in.shape[0]
    batch, seq, _ = x_prompt.shape
    dec_batch, dec_seq, _ = x_sample.shape
    assert depth == 1 and batch == 1

    xp = x_prompt.reshape(seq, D_MODEL)
    xs = x_sample.reshape(dec_batch * dec_seq, D_MODEL)
    i = 0
    mixer_w = [
        g_pre_mix[i][None, :], w_in[i], attn_sinks[i], pool_w[i], pool_scale[i][None, :], g_attn_out[i][None, :],
        g_pool_out[i][None, :], w_out[i], g_post_mix[i][None, :],
    ]
    ffn_w = [
        g_pre_mlp[i][None, :], w_up[i], w_down[i], g_post_mlp[i][None, :], w_ple_gate[i], b_ple_gate[i][None, :],
        w_ple[i],
    ]

    xs_mid, hs_mid, ks_t, vs_t, us_t = _sample_mixer(
        xs, state_k[i].reshape(dec_batch, WINDOW, KV_COLS).transpose(0, 2, 1),
        state_v[i].reshape(dec_batch, WINDOW, KV_COLS).transpose(0, 2, 1),
        state_pool[i].transpose(1, 0, 2), float(PAST_LEN), mixer_w, ffn_w[0])
    ks, vs, us = ks_t.transpose(0, 2, 1), vs_t.transpose(0, 2, 1), us_t.transpose(1, 0, 2)
    yp, ys, kp, vp, up = _prompt_layer(xp, p_prompt[i, 0], xs_mid, hs_mid,
                                       p_sample[i].reshape(dec_batch * dec_seq, PLE_DIM), mixer_w, ffn_w)

    return (
        yp.reshape(batch, seq, D_MODEL),
        ys.reshape(dec_batch, dec_seq, D_MODEL),
        kp.T.reshape(depth, batch, WINDOW, N_KV_HEADS, HEAD_DIM),
        vp.T.reshape(depth, batch, WINDOW, N_KV_HEADS, HEAD_DIM),
        up.reshape(depth, batch, POOL_BUF, POOL_WIDTH),
        ks.reshape(depth, dec_batch, WINDOW, N_KV_HEADS, HEAD_DIM),
        vs.reshape(depth, dec_batch, WINDOW, N_KV_HEADS, HEAD_DIM),
        us.reshape(depth, dec_batch, POOL_BUF, POOL_WIDTH),
    )
```

```python
import jax
import jax.numpy as jnp
import numpy as np
from jax import lax
from jax.experimental import pallas as pl
from jax.experimental.pallas import tpu as pltpu

D_MODEL = 1024
HEAD_DIM = 64
N_HEADS = 8
N_KV_HEADS = 2
GROUP = N_HEADS // N_KV_HEADS
WINDOW = 128
Q_COLS = N_HEADS * HEAD_DIM
KV_COLS = N_KV_HEADS * HEAD_DIM
POOL_WIDTH = 512
POOL_WINDOWS = (2, 4, 8, 16)
POOL_CG = POOL_WIDTH // len(POOL_WINDOWS)
POOL_BUF = max(POOL_WINDOWS) - 1
POOL_PAD = POOL_BUF + 1
IN_COLS = Q_COLS + 2 * KV_COLS + POOL_WIDTH
D_FF = 4 * D_MODEL
FF_CHUNK = 1024
PLE_DIM = 256
ROPE_THETA = 10000.0
PAST_LEN = 16384
EPS = 1e-6
NEG = -1e30
LANES = 128
MXU_N = 256

ATTN_LOOKAHEAD = 4
PROMPT_PATTERN = "11" + "21" * (5 + ATTN_LOOKAHEAD + 16 + 4) + "11" + "22" + "11" + "222" + "11"

PROMPT_TILE = 512
SAMPLE_SEQS = 32
VMEM_LIMIT = 58 * 1024 * 1024

BF16 = jnp.bfloat16
F32 = jnp.float32


def _rms(x, g):
    y = x * lax.rsqrt(jnp.mean(x * x, axis=-1, keepdims=True) + EPS)
    return y * g


def _mm(a, w):
    return jnp.dot(a, w, preferred_element_type=F32)


def _rope_tables(cos_a, sin_a, cos_b, sin_b, sign):
    cos_t = cos_a * cos_b - sin_a * sin_b
    sin_t = (sin_a * cos_b + cos_a * sin_b) * sign
    return cos_t, sin_t


def _rope(x, cos_t, sin_t, first_half):
    n = x.shape[1] // LANES
    width = x.shape[1]
    partner = jnp.where(first_half, pltpu.roll(x, width - HEAD_DIM // 2, axis=1), pltpu.roll(x, HEAD_DIM // 2, axis=1))
    if n > 1:
        cos_t = jnp.concatenate([cos_t] * n, axis=1)
        sin_t = jnp.concatenate([sin_t] * n, axis=1)
    return x * cos_t + partner * sin_t


def _first_half_mask(rows, width):
    lane = lax.broadcasted_iota(jnp.int32, (rows, width), 1)
    return (lane % HEAD_DIM) < (HEAD_DIM // 2)


def _project(x, g_pre, w_in, tables):
    rows = x.shape[0]
    h = _rms(x, g_pre).astype(BF16)
    proj = _mm(h, w_in)
    cos_t, sin_t = tables
    q = _rope(proj[:, :Q_COLS], cos_t, sin_t, _first_half_mask(rows, Q_COLS)) * (HEAD_DIM ** -0.5)
    k = _rope(proj[:, Q_COLS:Q_COLS + KV_COLS], cos_t, sin_t, _first_half_mask(rows, KV_COLS))
    v = proj[:, Q_COLS + KV_COLS:Q_COLS + 2 * KV_COLS]
    u = proj[:, Q_COLS + 2 * KV_COLS:]
    return q, k, v, u


def _pool_out(win, tok, inv_cnt, pool_w_ref, pool_scale):
    outs = []
    for g in range(len(POOL_WINDOWS)):
        r = win[g] * inv_cnt[g] - tok[:, g * POOL_CG:(g + 1) * POOL_CG]
        outs.append(_mm(r.astype(BF16), pool_w_ref[g]))
    return jnp.concatenate(outs, axis=1) * pool_scale


def _mix_out(x, attn, pooled, g_attn, g_pool, w_out, g_post):
    cat = jnp.concatenate([_rms(attn, g_attn), _rms(pooled, g_pool)], axis=1).astype(BF16)
    return x + _rms(_mm(cat, w_out), g_post)


def _softmax_parts(s_list, sink):
    m = sink
    for s in s_list:
        m = jnp.maximum(m, jnp.max(s, axis=-1, keepdims=True))
    e_list = [jnp.exp(s - m) for s in s_list]
    den = jnp.exp(sink - m)
    for e in e_list:
        den = den + jnp.sum(e, axis=-1, keepdims=True)
    return e_list, den


def _prompt_mixer_steps(tile, x_ref, cos_a_ref, sin_a_ref, cos_b_ref, sin_b_ref, sign_ref, g_pre_ref, w_in_ref,
                        sinks_ref, pool_w_ref, pool_scale_ref, g_attn_ref, g_pool_ref, w_out_ref, g_post_ref,
                        knew_ref, vnew_ref, unew_ref, qbuf, kbuf, vtbuf, ubuf, pbuf, catbuf,
                        g_pre_mlp_ref, xmid_ref, hmid_ref):
    tm = x_ref.shape[0]
    nblk = tm // WINDOW
    half_lanes = LANES // 2
    base = tile * tm

    h = _rms(x_ref[...], g_pre_ref[...]).astype(BF16)
    cos_t, sin_t = _rope_tables(cos_a_ref[0], sin_a_ref[0], cos_b_ref[...], sin_b_ref[...], sign_ref[...])

    kv = _mm(h, w_in_ref[:, Q_COLS:Q_COLS + 2 * KV_COLS])
    k = _rope(kv[:, :KV_COLS], cos_t, sin_t, _first_half_mask(tm, KV_COLS))
    v = kv[:, KV_COLS:]
    knew_ref[...] = k[tm - WINDOW:, :].T
    vnew_ref[...] = v[tm - WINDOW:, :].T
    low = lax.broadcasted_iota(jnp.int32, (tm, LANES), 1) < half_lanes
    k_swapped = pltpu.roll(k, half_lanes, axis=1)
    kbuf[0, WINDOW:, :] = jnp.where(low, k, 0.0).astype(BF16)
    kbuf[1, WINDOW:, :] = jnp.where(low, 0.0, k_swapped).astype(BF16)
    kbuf[2, WINDOW:, :] = jnp.where(low, k_swapped, 0.0).astype(BF16)
    kbuf[3, WINDOW:, :] = jnp.where(low, 0.0, k).astype(BF16)
    vtbuf[:, WINDOW:] = v.T.astype(BF16)
    yield

    u_col0 = Q_COLS + 2 * KV_COLS
    for c in range(Q_COLS // MXU_N):
        cols = slice(c * MXU_N, (c + 1) * MXU_N)
        q = _rope(_mm(h, w_in_ref[:, cols]), cos_t, sin_t, _first_half_mask(tm, MXU_N)) * (HEAD_DIM ** -0.5)
        qbuf[:, cols] = q.astype(BF16)
        yield
    for c in range(POOL_WIDTH // MXU_N):
        cols = slice(c * MXU_N, (c + 1) * MXU_N)
        ubuf[POOL_PAD:, cols] = _mm(h, w_in_ref[:, u_col0 + c * MXU_N:u_col0 + (c + 1) * MXU_N])
        yield

    pos = base + lax.broadcasted_iota(jnp.int32, (tm, 1), 0)
    pool_scale = pool_scale_ref[...]
    for g, w in enumerate(POOL_WINDOWS):
        cols = slice(g * POOL_CG, (g + 1) * POOL_CG)
        a = ubuf[:, cols]
        shift = 1
        while shift < w:
            a = a + pltpu.roll(a, shift, axis=0)
            shift *= 2
        inv_cnt = 1.0 / jnp.minimum(pos + 1, w).astype(F32)
        r = a[POOL_PAD:, :] * inv_cnt - ubuf[POOL_PAD:, cols]
        pbuf[:, cols] = _mm(r.astype(BF16), pool_w_ref[g]) * pool_scale[:, cols]
        yield
    catbuf[:, Q_COLS:] = _rms(pbuf[...], g_pool_ref[...]).astype(BF16)

    c_i = lax.broadcasted_iota(jnp.int32, (2 * WINDOW, 2 * WINDOW), 0)
    r_i = lax.broadcasted_iota(jnp.int32, (2 * WINDOW, 2 * WINDOW), 1) % WINDOW
    d_i = c_i - r_i
    band = (d_i >= 1) & (d_i <= WINDOW)
    band0 = band & (c_i >= WINDOW - base)

    def scores(j, h_kv, par):
        rows = slice(j * WINDOW, (j + 1) * WINDOW)
        keys = slice(j * WINDOW, (j + 2) * WINDOW)
        q2 = jnp.concatenate([qbuf[rows, (2 * h_kv) * LANES:(2 * h_kv + 1) * LANES],
                              qbuf[rows, (2 * h_kv + 1) * LANES:(2 * h_kv + 2) * LANES]], axis=0)
        return lax.dot_general(kbuf[2 * h_kv + par, keys, :], q2, (((1,), (1,)), ((), ())),
                               preferred_element_type=F32)

    units = [(j, h_kv, par) for j in range(nblk) for h_kv in range(N_KV_HEADS) for par in range(2)]
    pending = []
    for n in range(min(ATTN_LOOKAHEAD, len(units))):
        pending.append(scores(*units[n]))
        yield
    heads = [None] * N_HEADS
    for n, (j, h_kv, par) in enumerate(units):
        st = pending.pop(0)
        if n + ATTN_LOOKAHEAD < len(units):
            pending.append(scores(*units[n + ATTN_LOOKAHEAD]))
        mask = band0 if j == 0 else band
        st = jnp.where(mask, st, NEG)
        sink = jnp.concatenate([jnp.full((1, WINDOW), sinks_ref[GROUP * h_kv + par], F32),
                                jnp.full((1, WINDOW), sinks_ref[GROUP * h_kv + 2 + par], F32)], axis=1)
        m = jnp.maximum(jnp.max(st, axis=0, keepdims=True), sink)
        e = jnp.exp(st - m)
        den = jnp.sum(e, axis=0, keepdims=True) + jnp.exp(sink - m)
        vt = vtbuf[h_kv * HEAD_DIM:(h_kv + 1) * HEAD_DIM, j * WINDOW:(j + 2) * WINDOW]
        ot = _mm(vt, e.astype(BF16)) / den
        heads[GROUP * h_kv + par] = ot[:, :WINDOW]
        heads[GROUP * h_kv + 2 + par] = ot[:, WINDOW:]
        if h_kv == N_KV_HEADS - 1 and par == 1:
            attn = jnp.concatenate(heads, axis=0).T
            catbuf[j * WINDOW:(j + 1) * WINDOW, :Q_COLS] = _rms(attn, g_attn_ref[...]).astype(BF16)
        yield

    mix = []
    for c in range(D_MODEL // MXU_N):
        mix.append(_mm(catbuf[...], w_out_ref[:, c * MXU_N:(c + 1) * MXU_N]))
        yield
    x_mid = x_ref[...] + _rms(jnp.concatenate(mix, axis=1), g_post_ref[...])

    unew_ref[:, 0, :] = ubuf[tm + 1:tm + POOL_PAD, :]
    kbuf[:, 0:WINDOW, :] = kbuf[:, tm:tm + WINDOW, :]
    vtbuf[:, 0:WINDOW] = vtbuf[:, tm:tm + WINDOW]
    ubuf[0:POOL_PAD, :] = ubuf[tm:tm + POOL_PAD, :]
    xmid_ref[...] = x_mid
    hmid_ref[...] = _rms(x_mid, g_pre_mlp_ref[...]).astype(BF16)


def _sample_mixer_kernel(x_ref, cos_ref, sin_ref, skt_ref, svt_ref, sp_ref, g_pre_ref, w_in_hbm,
                         sinks_ref, pool_w_hbm, pool_scale_ref, g_attn_ref, g_pool_ref, w_out_hbm, g_post_ref,
                         g_pre_mlp_ref, xmid_ref, hmid_ref, knew_ref, vnew_ref, unew_ref,
                         ubuf, w_in_ref, pool_w_ref, w_out_ref):
    @pl.when(pl.program_id(0) == 0)
    def _():
        _cast_weights_to_vmem([w_in_hbm, pool_w_hbm, w_out_hbm], [w_in_ref, pool_w_ref, w_out_ref])

    rows = x_ref.shape[0]
    nseq = skt_ref.shape[0]
    t = rows // nseq
    x = x_ref[...]
    q, k, v, u = _project(x, g_pre_ref[...], w_in_ref[...], (cos_ref[...], sin_ref[...]))

    k3 = k.reshape(nseq, t, KV_COLS)
    v3 = v.reshape(nseq, t, KV_COLS)
    sk = jnp.swapaxes(skt_ref[...], 1, 2)
    sv = jnp.swapaxes(svt_ref[...], 1, 2)
    knew_ref[...] = jnp.swapaxes(jnp.concatenate([sk[:, t:, :], k3], axis=1), 1, 2)
    vnew_ref[...] = jnp.swapaxes(jnp.concatenate([sv[:, t:, :], v3], axis=1), 1, 2)

    r_i = lax.broadcasted_iota(jnp.int32, (GROUP * t, WINDOW), 0) % t
    c_i = lax.broadcasted_iota(jnp.int32, (GROUP * t, WINDOW), 1)
    mask_old = (c_i > r_i)[None]
    r_n = lax.broadcasted_iota(jnp.int32, (GROUP * t, t), 0) % t
    c_n = lax.broadcasted_iota(jnp.int32, (GROUP * t, t), 1)
    mask_new = (c_n <= r_n)[None]

    qb = q.astype(BF16)
    kb = k3.astype(BF16)
    vb = v3.astype(BF16)
    heads = [None] * N_HEADS
    scores = []
    for h in range(N_KV_HEADS):
        lanes = slice(h * HEAD_DIM, (h + 1) * HEAD_DIM)
        q4 = jnp.concatenate(
            [qb[:, (GROUP * h + g) * HEAD_DIM:(GROUP * h + g + 1) * HEAD_DIM].reshape(nseq, t, HEAD_DIM)
             for g in range(GROUP)], axis=1)
        s_old = jnp.einsum('bqd,bkd->bqk', q4, sk[:, :, lanes].astype(BF16), preferred_element_type=F32)
        s_new = jnp.einsum('bqd,bkd->bqk', q4, kb[:, :, lanes], preferred_element_type=F32)
        scores.append((s_old, s_new))
    for h, (s_old, s_new) in enumerate(scores):
        lanes = slice(h * HEAD_DIM, (h + 1) * HEAD_DIM)
        s_old = jnp.where(mask_old, s_old, NEG)
        s_new = jnp.where(mask_new, s_new, NEG)
        sink = jnp.concatenate(
            [jnp.full((1, t, 1), sinks_ref[GROUP * h + g], F32) for g in range(GROUP)], axis=1)
        (e_old, e_new), den = _softmax_parts([s_old, s_new], sink)
        o = jnp.einsum('bqk,bkd->bqd', e_old.astype(BF16), sv[:, :, lanes].astype(BF16), preferred_element_type=F32)
        o = o + jnp.einsum('bqk,bkd->bqd', e_new.astype(BF16), vb[:, :, lanes], preferred_element_type=F32)
        o = o / den
        for g in range(GROUP):
            heads[GROUP * h + g] = o[:, g * t:(g + 1) * t, :].reshape(rows, HEAD_DIM)
    attn = jnp.concatenate(heads, axis=1)

    ext = POOL_PAD + t
    per_seq = lambda j: pl.ds(j, nseq, stride=ext)
    win = []
    for g, w in enumerate(POOL_WINDOWS):
        cols = slice(g * POOL_CG, (g + 1) * POOL_CG)
        ubuf[g, per_seq(0), :] = jnp.zeros((nseq, POOL_CG), F32)
        for j in range(POOL_BUF):
            ubuf[g, per_seq(1 + j), :] = sp_ref[j, :, cols]
        for s_i in range(nseq):
            ubuf[g, s_i * ext + POOL_PAD:(s_i + 1) * ext, :] = u[s_i * t:(s_i + 1) * t, cols]
        for j in range(POOL_BUF):
            unew_ref[j, :, cols] = ubuf[g, per_seq(t + 1 + j), :]
        a = ubuf[g]
        shift = 1
        while shift < w:
            a = a + pltpu.roll(a, shift, axis=0)
            shift *= 2
        win.append(a.reshape(nseq, ext, POOL_CG)[:, POOL_PAD:, :].reshape(rows, POOL_CG))
    inv_cnt = [1.0 / w for w in POOL_WINDOWS]
    pooled = _pool_out(win, u, inv_cnt, pool_w_ref, pool_scale_ref[...])

    x_mid = _mix_out(x, attn, pooled, g_attn_ref[...], g_pool_ref[...], w_out_ref[...], g_post_ref[...])
    xmid_ref[...] = x_mid
    hmid_ref[...] = _rms(x_mid, g_pre_mlp_ref[...]).astype(BF16)


def _ffn_steps(x_ref, h_ref, load_p, w_up_ref, w_down_ref, g_post_ref, w_gate_ref, b_gate_ref, w_ple_ref,
               y_ref, actbuf):
    for c in range(D_FF // MXU_N):
        cols = slice(c * MXU_N, (c + 1) * MXU_N)
        up = _mm(h_ref[...], w_up_ref[:, cols])
        actbuf[:, cols] = jnp.square(jnp.maximum(up, 0.0)).astype(BF16)
        yield
    ff = []
    n_col, n_row = D_MODEL // MXU_N, D_FF // FF_CHUNK
    for c in range(n_col):
        cols = slice(c * MXU_N, (c + 1) * MXU_N)
        acc = None
        for r in range(n_row):
            rows = slice(r * FF_CHUNK, (r + 1) * FF_CHUNK)
            part = _mm(actbuf[:, rows], w_down_ref[rows, cols])
            acc = part if acc is None else acc + part
            if (c, r) != (n_col - 1, n_row - 1):
                yield
        ff.append(acc)
    x = x_ref[...] + _rms(jnp.concatenate(ff, axis=1), g_post_ref[...])
    xb = x.astype(BF16)
    yield
    ple = _mm(load_p().astype(BF16), w_ple_ref[...])
    yield
    for c in range(D_MODEL // MXU_N):
        cols = slice(c * MXU_N, (c + 1) * MXU_N)
        gate = jax.nn.sigmoid(_mm(xb, w_gate_ref[:, cols]) + b_gate_ref[:, cols])
        y_ref[:, cols] = x[:, cols] + gate * ple[:, cols]
        yield


def _interleave(first, second, pattern):
    gens = {'1': first, '2': second}
    for tag in pattern:
        next(gens[tag], None)
    for _ in first:
        pass
    for _ in second:
        pass


def _cast_weights_to_vmem(srcs, dsts):
    tasks = []
    for src, dst in zip(srcs, dsts):
        if len(src.shape) != 2:
            tasks.append((src, dst, tuple(src.shape)))
            continue
        rows, cols = src.shape
        col_block = cols if cols <= STAGE_MAX_COLS else D_MODEL
        for c0 in range(0, cols, col_block):
            for r0 in range(0, rows, STAGE_ROWS):
                view = (slice(r0, r0 + STAGE_ROWS), slice(c0, c0 + col_block))
                tasks.append((src.at[view], dst.at[view], (STAGE_ROWS, col_block)))
    shapes = sorted(set(shape for _, _, shape in tasks))
    counts = [sum(1 for t in tasks if t[2] == shape) for shape in shapes]
    slots = [STAGE_SLOTS if n >= 2 * STAGE_SLOTS else min(n, 2) for n in counts]

    def body(*scoped):
        *stages, sems = scoped
        uses = [0] * len(shapes)
        copies, frees = [], []
        last_in_slot = {}
        for t, (src, dst, shape) in enumerate(tasks):
            k = shapes.index(shape)
            slot = uses[k] % slots[k]
            uses[k] += 1
            stage = stages[k].at[slot]
            copies.append((pltpu.make_async_copy(src, stage, sems.at[k, slot]), stage, dst))
            frees.append(last_in_slot.get((k, slot)))
            last_in_slot[(k, slot)] = t
        started = 0
        for t, (copy, stage, dst) in enumerate(copies):
            while started < len(copies) and (frees[started] is None or frees[started] < t):
                copies[started][0].start()
                started += 1
            copy.wait()
            dst[...] = stage[...].astype(BF16)

    pl.run_scoped(body, *[pltpu.VMEM((n,) + shape, F32) for n, shape in zip(slots, shapes)],
                  pltpu.SemaphoreType.DMA((len(shapes), max(slots))))


N_TABLES = 5
MIXER_MATMUL_W = (1, 3, 7)
FFN_MATMUL_W = (1, 2, 4, 6)
STAGE_ROWS = 256
STAGE_MAX_COLS = 1280
STAGE_SLOTS = 8


N_SAMPLE_TILES = 2


def _prompt_layer_kernel(*refs):
    x_ref, *tables = refs[:1 + N_TABLES]
    (g_pre_mix, w_in_hbm, sinks, pool_w_hbm, pool_scale, g_attn, g_pool, w_out_hbm, g_post_mix, p_ref,
     g_pre_mlp, w_up_hbm, w_down_hbm, g_post_mlp, w_gate_hbm, b_gate, w_ple_hbm,
     xs_hbm, hs_hbm, ps_ref) = refs[1 + N_TABLES:21 + N_TABLES]
    y_ref, ys_hbm, knew_ref, vnew_ref, unew_ref = refs[21 + N_TABLES:26 + N_TABLES]
    (qbuf, kbuf, vtbuf, ubuf, pbuf, catbuf, actbuf, xmid, hmid, sems,
     w_in, pool_w, w_out, w_up, w_down, w_gate, w_ple) = refs[26 + N_TABLES:]
    tm = x_ref.shape[0]
    step = pl.program_id(0)
    last_tile = pl.num_programs(0) - 1 - N_SAMPLE_TILES
    on_sample = step < N_SAMPLE_TILES
    slot = step % 2

    def sample_loads(n):
        rows = pl.ds(n * tm, tm)
        return [pltpu.make_async_copy(xs_hbm.at[rows], xmid.at[n], sems.at[2 * n]),
                pltpu.make_async_copy(hs_hbm.at[rows], hmid.at[n], sems.at[2 * n + 1])]

    def sample_store(n):
        return pltpu.make_async_copy(y_ref, ys_hbm.at[pl.ds(n * tm, tm)], sems.at[2 * N_SAMPLE_TILES + n])

    @pl.when(step == 0)
    def _():
        for n in range(N_SAMPLE_TILES):
            for copy in sample_loads(n):
                copy.start()
        _cast_weights_to_vmem([w_in_hbm, pool_w_hbm, w_out_hbm, w_up_hbm, w_down_hbm, w_gate_hbm, w_ple_hbm],
                              [w_in, pool_w, w_out, w_up, w_down, w_gate, w_ple])
        for copy in sample_loads(0):
            copy.wait()

    for n in range(N_SAMPLE_TILES):
        @pl.when(step == n + 1)
        def _(n=n):
            sample_store(n).wait()

    @pl.when(step == 1)
    def _():
        kbuf[:, 0:WINDOW, :] = jnp.zeros((2 * N_KV_HEADS, WINDOW, LANES), BF16)
        vtbuf[:, 0:WINDOW] = jnp.zeros((KV_COLS, WINDOW), BF16)
        ubuf[0:POOL_PAD, :] = jnp.zeros((POOL_PAD, POOL_WIDTH), F32)
        for copy in sample_loads(1):
            copy.wait()

    def ffn_steps():
        return _ffn_steps(xmid.at[slot], hmid.at[slot], lambda: jnp.where(on_sample, ps_ref[...], p_ref[...]),
                          w_up, w_down, g_post_mlp, w_gate, b_gate, w_ple, y_ref, actbuf)

    ffn_only = (step == 0) | (step == pl.num_programs(0) - 1)

    @pl.when(ffn_only)
    def _():
        for _ in ffn_steps():
            pass

    @pl.when(jnp.logical_not(ffn_only))
    def _():
        mixer = _prompt_mixer_steps(jnp.clip(step - 1, 0, last_tile), x_ref, *tables, g_pre_mix, w_in, sinks, pool_w,
                                    pool_scale, g_attn, g_pool, w_out, g_post_mix, knew_ref, vnew_ref, unew_ref,
                                    qbuf, kbuf, vtbuf, ubuf, pbuf, catbuf, g_pre_mlp, xmid.at[1 - slot],
                                    hmid.at[1 - slot])
        _interleave(ffn_steps(), mixer, PROMPT_PATTERN)

    for n in range(N_SAMPLE_TILES):
        @pl.when(step == n)
        def _(n=n):
            sample_store(n).start()


def _const_spec(shape):
    zeros = (0,) * len(shape)
    return pl.BlockSpec(shape, lambda i: zeros, pipeline_mode=pl.Buffered(1))


def _rope_lane_tables(pos):
    half = HEAD_DIM // 2
    inv = ROPE_THETA ** (-np.arange(half, dtype=np.float64) / half)
    ang = pos[:, None] * inv[None, :]
    reps = LANES // half
    return (np.tile(np.cos(ang), (1, reps)).astype(np.float32), np.tile(np.sin(ang), (1, reps)).astype(np.float32))


def _rope_sign():
    lane = np.arange(LANES)
    return np.where((lane % HEAD_DIM) < HEAD_DIM // 2, -1.0, 1.0).astype(np.float32)[None, :]


def _params(dims):
    return pltpu.CompilerParams(dimension_semantics=dims, vmem_limit_bytes=VMEM_LIMIT)


def _mixer_weight_specs():
    return [
        _const_spec((1, D_MODEL)),
        _const_spec((D_MODEL, IN_COLS)),
        pl.BlockSpec(memory_space=pltpu.SMEM),
        _const_spec((len(POOL_WINDOWS), POOL_CG, POOL_CG)),
        _const_spec((1, POOL_WIDTH)),
        _const_spec((1, Q_COLS)),
        _const_spec((1, POOL_WIDTH)),
        _const_spec((D_MODEL, D_MODEL)),
        _const_spec((1, D_MODEL)),
    ]


def _ffn_weight_specs():
    return [
        _const_spec((1, D_MODEL)),
        _const_spec((D_MODEL, D_FF)),
        _const_spec((D_FF, D_MODEL)),
        _const_spec((1, D_MODEL)),
        _const_spec((D_MODEL, D_MODEL)),
        _const_spec((1, D_MODEL)),
        _const_spec((PLE_DIM, D_MODEL)),
    ]


def _prompt_layer(x, p, xs_mid, hs_mid, ps, mixer_w, ffn_w):
    seq = x.shape[0]
    tm = PROMPT_TILE
    tiles = seq // tm
    assert xs_mid.shape[0] == N_SAMPLE_TILES * tm
    cos_a, sin_a = _rope_lane_tables(np.arange(tiles, dtype=np.float64) * tm)
    cos_b, sin_b = _rope_lane_tables(np.arange(tm, dtype=np.float64))
    mixer_tile = lambda i: (jnp.clip(i - 1, 0, tiles - 1), 0)
    mixer_tile3 = lambda i: (jnp.clip(i - 1, 0, tiles - 1), 0, 0)
    ffn_tile = lambda i: (jnp.clip(i - N_SAMPLE_TILES, 0, tiles - 1), 0)
    sample_tile = lambda i: (jnp.minimum(i, N_SAMPLE_TILES - 1), 0)
    fixed = lambda i: (0, 0)
    in_specs = [
        pl.BlockSpec((tm, D_MODEL), mixer_tile),
        pl.BlockSpec((1, 1, LANES), mixer_tile3),
        pl.BlockSpec((1, 1, LANES), mixer_tile3),
        _const_spec((tm, LANES)),
        _const_spec((tm, LANES)),
        _const_spec((1, LANES)),
    ] + _mixer_weight_specs() + [pl.BlockSpec((tm, PLE_DIM), ffn_tile)] + _ffn_weight_specs() + [
        pl.BlockSpec(memory_space=pl.ANY),
        pl.BlockSpec(memory_space=pl.ANY),
        pl.BlockSpec((tm, PLE_DIM), sample_tile, pipeline_mode=pl.Buffered(1)),
    ]
    n_lead = 1 + N_TABLES
    matmul_w = [mixer_w[n] for n in MIXER_MATMUL_W] + [ffn_w[n] for n in FFN_MATMUL_W]
    for n in [n_lead + n for n in MIXER_MATMUL_W] + [n_lead + len(mixer_w) + 1 + n for n in FFN_MATMUL_W]:
        in_specs[n] = pl.BlockSpec(memory_space=pl.ANY)
    return pl.pallas_call(
        _prompt_layer_kernel,
        grid=(tiles + N_SAMPLE_TILES,),
        in_specs=in_specs,
        out_specs=[
            pl.BlockSpec((tm, D_MODEL), ffn_tile),
            pl.BlockSpec(memory_space=pl.ANY),
            pl.BlockSpec((WINDOW, KV_COLS), fixed),
            pl.BlockSpec((WINDOW, KV_COLS), fixed),
            pl.BlockSpec((POOL_BUF, 1, POOL_WIDTH), lambda i: (0, 0, 0)),
        ],
        out_shape=[
            jax.ShapeDtypeStruct((seq, D_MODEL), F32),
            jax.ShapeDtypeStruct(xs_mid.shape, F32),
            jax.ShapeDtypeStruct((WINDOW, KV_COLS), F32),
            jax.ShapeDtypeStruct((WINDOW, KV_COLS), F32),
            jax.ShapeDtypeStruct((POOL_BUF, 1, POOL_WIDTH), F32),
        ],
        scratch_shapes=[
            pltpu.VMEM((tm, Q_COLS), BF16),
            pltpu.VMEM((2 * N_KV_HEADS, WINDOW + tm, LANES), BF16),
            pltpu.VMEM((KV_COLS, WINDOW + tm), BF16),
            pltpu.VMEM((POOL_PAD + tm, POOL_WIDTH), F32),
            pltpu.VMEM((tm, POOL_WIDTH), F32),
            pltpu.VMEM((tm, D_MODEL), BF16),
            pltpu.VMEM((tm, D_FF), BF16),
            pltpu.VMEM((2, tm, D_MODEL), F32),
            pltpu.VMEM((2, tm, D_MODEL), BF16),
            pltpu.SemaphoreType.DMA((3 * N_SAMPLE_TILES,)),
        ] + [pltpu.VMEM(w.shape, BF16) for w in matmul_w],
        compiler_params=_params(("arbitrary",)),
        name="prompt_layer",
    )(x, jnp.asarray(cos_a)[:, None, :], jnp.asarray(sin_a)[:, None, :], jnp.asarray(cos_b), jnp.asarray(sin_b),
      jnp.asarray(_rope_sign()), *mixer_w, p, *ffn_w, xs_mid, hs_mid, ps)


def _sample_mixer(x, state_kt, state_vt, state_pool_t, past_len, mixer_w, g_pre_mlp):
    nseq_all, t = state_kt.shape[0], x.shape[0] // state_kt.shape[0]
    nseq = SAMPLE_SEQS
    rows = nseq * t
    steps = nseq_all // nseq
    cos, sin = _rope_lane_tables(past_len + np.arange(rows, dtype=np.float64) % t)
    sin = sin * _rope_sign()
    row = lambda i: (i, 0)
    seq3 = lambda i: (i, 0, 0)
    pos3 = lambda i: (0, i, 0)
    mixer_specs = _mixer_weight_specs()
    for n in MIXER_MATMUL_W:
        mixer_specs[n] = pl.BlockSpec(memory_space=pl.ANY)
    return pl.pallas_call(
        _sample_mixer_kernel,
        grid=(steps,),
        in_specs=[
            pl.BlockSpec((rows, D_MODEL), row),
            _const_spec((rows, LANES)),
            _const_spec((rows, LANES)),
            pl.BlockSpec((nseq, KV_COLS, WINDOW), seq3),
            pl.BlockSpec((nseq, KV_COLS, WINDOW), seq3),
            pl.BlockSpec((POOL_BUF, nseq, POOL_WIDTH), pos3),
        ] + mixer_specs + [_const_spec((1, D_MODEL))],
        out_specs=[
            pl.BlockSpec((rows, D_MODEL), row),
            pl.BlockSpec((rows, D_MODEL), row),
            pl.BlockSpec((nseq, KV_COLS, WINDOW), seq3),
            pl.BlockSpec((nseq, KV_COLS, WINDOW), seq3),
            pl.BlockSpec((POOL_BUF, nseq, POOL_WIDTH), pos3),
        ],
        out_shape=[
            jax.ShapeDtypeStruct((nseq_all * t, D_MODEL), F32),
            jax.ShapeDtypeStruct((nseq_all * t, D_MODEL), BF16),
            jax.ShapeDtypeStruct((nseq_all, KV_COLS, WINDOW), F32),
            jax.ShapeDtypeStruct((nseq_all, KV_COLS, WINDOW), F32),
            jax.ShapeDtypeStruct((POOL_BUF, nseq_all, POOL_WIDTH), F32),
        ],
        scratch_shapes=[pltpu.VMEM((len(POOL_WINDOWS), nseq * (POOL_PAD + t), POOL_CG), F32)]
        + [pltpu.VMEM(mixer_w[n].shape, BF16) for n in MIXER_MATMUL_W],
        compiler_params=_params(("arbitrary",)),
        name="sample_mixer",
    )(x, jnp.asarray(cos), jnp.asarray(sin), state_kt, state_vt, state_pool_t, *mixer_w, g_pre_mlp)


def kernel(x_prompt, x_sample, state_k, state_v, state_pool, p_prompt, p_sample, w_in, attn_sinks, pool_w,
           pool_scale, g_attn_out, g_pool_out, w_out, g_pre_mix, g_post_mix, g_pre_mlp, g_post_mlp, w_up, w_down,
           w_ple, w_ple_gate, b_ple_gate):
    depth = w_in.shape[0]
    batch, seq, _ = x_prompt.shape
    dec_batch, dec_seq, _ = x_sample.shape
    assert depth == 1 and batch == 1

    xp = x_prompt.reshape(seq, D_MODEL)
    xs = x_sample.reshape(dec_batch * dec_seq, D_MODEL)
    i = 0
    mixer_w = [
        g_pre_mix[i][None, :], w_in[i], attn_sinks[i], pool_w[i], pool_scale[i][None, :], g_attn_out[i][None, :],
        g_pool_out[i][None, :], w_out[i], g_post_mix[i][None, :],
    ]
    ffn_w = [
        g_pre_mlp[i][None, :], w_up[i], w_down[i], g_post_mlp[i][None, :], w_ple_gate[i], b_ple_gate[i][None, :],
        w_ple[i],
    ]

    xs_mid, hs_mid, ks_t, vs_t, us_t = _sample_mixer(
        xs, state_k[i].reshape(dec_batch, WINDOW, KV_COLS).transpose(0, 2, 1),
        state_v[i].reshape(dec_batch, WINDOW, KV_COLS).transpose(0, 2, 1),
        state_pool[i].transpose(1, 0, 2), float(PAST_LEN), mixer_w, ffn_w[0])
    ks, vs, us = ks_t.transpose(0, 2, 1), vs_t.transpose(0, 2, 1), us_t.transpose(1, 0, 2)
    yp, ys, kp, vp, up = _prompt_layer(xp, p_prompt[i, 0], xs_mid, hs_mid,
                                       p_sample[i].reshape(dec_batch * dec_seq, PLE_DIM), mixer_w, ffn_w)

    return (
        yp.reshape(batch, seq, D_MODEL),
        ys.reshape(dec_batch, dec_seq, D_MODEL),
        kp.T.reshape(depth, batch, WINDOW, N_KV_HEADS, HEAD_DIM),
        vp.T.reshape(depth, batch, WINDOW, N_KV_HEADS, HEAD_DIM),
        up.reshape(depth, batch, POOL_BUF, POOL_WIDTH),
        ks.reshape(depth, dec_batch, WINDOW, N_KV_HEADS, HEAD_DIM),
        vs.reshape(depth, dec_batch, WINDOW, N_KV_HEADS, HEAD_DIM),
        us.reshape(depth, dec_batch, POOL_BUF, POOL_WIDTH),
    )
```

```python
import jax
import jax.numpy as jnp
import numpy as np
from jax import lax
from jax.experimental import pallas as pl
from jax.experimental.pallas import tpu as pltpu

D_MODEL = 1024
HEAD_DIM = 64
N_HEADS = 8
N_KV_HEADS = 2
GROUP = N_HEADS // N_KV_HEADS
WINDOW = 128
Q_COLS = N_HEADS * HEAD_DIM
KV_COLS = N_KV_HEADS * HEAD_DIM
POOL_WIDTH = 512
POOL_WINDOWS = (2, 4, 8, 16)
POOL_CG = POOL_WIDTH // len(POOL_WINDOWS)
POOL_BUF = max(POOL_WINDOWS) - 1
POOL_PAD = POOL_BUF + 1
IN_COLS = Q_COLS + 2 * KV_COLS + POOL_WIDTH
D_FF = 4 * D_MODEL
FF_CHUNK = 1024
PLE_DIM = 256
ROPE_THETA = 10000.0
PAST_LEN = 16384
EPS = 1e-6
NEG = -1e30
LANES = 128
MXU_N = 256

ATTN_LOOKAHEAD = 4
PROMPT_PATTERN = "11" + "21" * (5 + ATTN_LOOKAHEAD + 16 + 4) + "11" + "22" + "11" + "222" + "11"

PROMPT_TILE = 512
SAMPLE_SEQS = 32
VMEM_LIMIT = 58 * 1024 * 1024

BF16 = jnp.bfloat16
F32 = jnp.float32


def _rms(x, g):
    y = x * lax.rsqrt(jnp.mean(x * x, axis=-1, keepdims=True) + EPS)
    return y * g


def _mm(a, w):
    return jnp.dot(a, w, preferred_element_type=F32)


def _rope_tables(cos_a, sin_a, cos_b, sin_b, sign):
    cos_t = cos_a * cos_b - sin_a * sin_b
    sin_t = (sin_a * cos_b + cos_a * sin_b) * sign
    return cos_t, sin_t


def _rope(x, cos_t, sin_t, first_half):
    n = x.shape[1] // LANES
    width = x.shape[1]
    partner = jnp.where(first_half, pltpu.roll(x, width - HEAD_DIM // 2, axis=1), pltpu.roll(x, HEAD_DIM // 2, axis=1))
    if n > 1:
        cos_t = jnp.concatenate([cos_t] * n, axis=1)
        sin_t = jnp.concatenate([sin_t] * n, axis=1)
    return x * cos_t + partner * sin_t


def _first_half_mask(rows, width):
    lane = lax.broadcasted_iota(jnp.int32, (rows, width), 1)
    return (lane % HEAD_DIM) < (HEAD_DIM // 2)


def _project(x, g_pre, w_in, tables):
    rows = x.shape[0]
    h = _rms(x, g_pre).astype(BF16)
    proj = _mm(h, w_in)
    cos_t, sin_t = tables
    q = _rope(proj[:, :Q_COLS], cos_t, sin_t, _first_half_mask(rows, Q_COLS)) * (HEAD_DIM ** -0.5)
    k = _rope(proj[:, Q_COLS:Q_COLS + KV_COLS], cos_t, sin_t, _first_half_mask(rows, KV_COLS))
    v = proj[:, Q_COLS + KV_COLS:Q_COLS + 2 * KV_COLS]
    u = proj[:, Q_COLS + 2 * KV_COLS:]
    return q, k, v, u


def _pool_out(win, tok, inv_cnt, pool_w_ref, pool_scale):
    outs = []
    for g in range(len(POOL_WINDOWS)):
        r = win[g] * inv_cnt[g] - tok[:, g * POOL_CG:(g + 1) * POOL_CG]
        outs.append(_mm(r.astype(BF16), pool_w_ref[g]))
    return jnp.concatenate(outs, axis=1) * pool_scale


def _mix_out(x, attn, pooled, g_attn, g_pool, w_out, g_post):
    cat = jnp.concatenate([_rms(attn, g_attn), _rms(pooled, g_pool)], axis=1).astype(BF16)
    return x + _rms(_mm(cat, w_out), g_post)


def _softmax_parts(s_list, sink):
    m = sink
    for s in s_list:
        m = jnp.maximum(m, jnp.max(s, axis=-1, keepdims=True))
    e_list = [jnp.exp(s - m) for s in s_list]
    den = jnp.exp(sink - m)
    for e in e_list:
        den = den + jnp.sum(e, axis=-1, keepdims=True)
    return e_list, den


def _prompt_mixer_steps(tile, x_ref, cos_a_ref, sin_a_ref, cos_b_ref, sin_b_ref, sign_ref, g_pre_ref, w_in_ref,
                        sinks_ref, pool_w_ref, pool_scale_ref, g_attn_ref, g_pool_ref, w_out_ref, g_post_ref,
                        knew_ref, vnew_ref, unew_ref, qbuf, kbuf, vtbuf, ubuf, pbuf, catbuf,
                        g_pre_mlp_ref, xmid_ref, hmid_ref):
    tm = x_ref.shape[0]
    nblk = tm // WINDOW
    half_lanes = LANES // 2
    base = tile * tm

    h = _rms(x_ref[...], g_pre_ref[...]).astype(BF16)
    cos_t, sin_t = _rope_tables(cos_a_ref[0], sin_a_ref[0], cos_b_ref[...], sin_b_ref[...], sign_ref[...])

    kv = _mm(h, w_in_ref[:, Q_COLS:Q_COLS + 2 * KV_COLS])
    k = _rope(kv[:, :KV_COLS], cos_t, sin_t, _first_half_mask(tm, KV_COLS))
    v = kv[:, KV_COLS:]
    knew_ref[...] = k[tm - WINDOW:, :].T
    vnew_ref[...] = v[tm - WINDOW:, :].T
    low = lax.broadcasted_iota(jnp.int32, (tm, LANES), 1) < half_lanes
    k_swapped = pltpu.roll(k, half_lanes, axis=1)
    kbuf[0, WINDOW:, :] = jnp.where(low, k, 0.0).astype(BF16)
    kbuf[1, WINDOW:, :] = jnp.where(low, 0.0, k_swapped).astype(BF16)
    kbuf[2, WINDOW:, :] = jnp.where(low, k_swapped, 0.0).astype(BF16)
    kbuf[3, WINDOW:, :] = jnp.where(low, 0.0, k).astype(BF16)
    vtbuf[:, WINDOW:] = v.T.astype(BF16)
    yield

    u_col0 = Q_COLS + 2 * KV_COLS
    for c in range(Q_COLS // MXU_N):
        cols = slice(c * MXU_N, (c + 1) * MXU_N)
        q = _rope(_mm(h, w_in_ref[:, cols]), cos_t, sin_t, _first_half_mask(tm, MXU_N)) * (HEAD_DIM ** -0.5)
        qbuf[:, cols] = q.astype(BF16)
        yield
    for c in range(POOL_WIDTH // MXU_N):
        cols = slice(c * MXU_N, (c + 1) * MXU_N)
        ubuf[POOL_PAD:, cols] = _mm(h, w_in_ref[:, u_col0 + c * MXU_N:u_col0 + (c + 1) * MXU_N])
        yield

    pos = base + lax.broadcasted_iota(jnp.int32, (tm, 1), 0)
    pool_scale = pool_scale_ref[...]
    for g, w in enumerate(POOL_WINDOWS):
        cols = slice(g * POOL_CG, (g + 1) * POOL_CG)
        a = ubuf[:, cols]
        shift = 1
        while shift < w:
            a = a + pltpu.roll(a, shift, axis=0)
            shift *= 2
        inv_cnt = 1.0 / jnp.minimum(pos + 1, w).astype(F32)
        r = a[POOL_PAD:, :] * inv_cnt - ubuf[POOL_PAD:, cols]
        pbuf[:, cols] = _mm(r.astype(BF16), pool_w_ref[g]) * pool_scale[:, cols]
        yield
    catbuf[:, Q_COLS:] = _rms(pbuf[...], g_pool_ref[...]).astype(BF16)

    c_i = lax.broadcasted_iota(jnp.int32, (WINDOW, 2 * WINDOW), 0)
    r_i = lax.broadcasted_iota(jnp.int32, (WINDOW, 2 * WINDOW), 1) % WINDOW
    from_prev = c_i > r_i
    no_prev = from_prev & (base == 0)

    def scores(j, h_kv, par):
        rows = slice(j * WINDOW, (j + 1) * WINDOW)
        keys = slice(j * WINDOW, (j + 2) * WINDOW)
        q2 = jnp.concatenate([qbuf[rows, (2 * h_kv) * LANES:(2 * h_kv + 1) * LANES],
                              qbuf[rows, (2 * h_kv + 1) * LANES:(2 * h_kv + 2) * LANES]], axis=0)
        return lax.dot_general(kbuf[2 * h_kv + par, keys, :], q2, (((1,), (1,)), ((), ())),
                               preferred_element_type=F32)

    units = [(j, h_kv, par) for j in range(nblk) for h_kv in range(N_KV_HEADS) for par in range(2)]
    pending = []
    for n in range(min(ATTN_LOOKAHEAD, len(units))):
        pending.append(scores(*units[n]))
        yield
    heads = [None] * N_HEADS
    for n, (j, h_kv, par) in enumerate(units):
        st = pending.pop(0)
        if n + ATTN_LOOKAHEAD < len(units):
            pending.append(scores(*units[n + ATTN_LOOKAHEAD]))
        st = jnp.where(from_prev, st[:WINDOW], st[WINDOW:])
        if j == 0:
            st = jnp.where(no_prev, NEG, st)
        sink = jnp.concatenate([jnp.full((1, WINDOW), sinks_ref[GROUP * h_kv + par], F32),
                                jnp.full((1, WINDOW), sinks_ref[GROUP * h_kv + 2 + par], F32)], axis=1)
        m = jnp.maximum(jnp.max(st, axis=0, keepdims=True), sink)
        e = jnp.exp(st - m)
        den = jnp.sum(e, axis=0, keepdims=True) + jnp.exp(sink - m)
        vt = vtbuf[h_kv * HEAD_DIM:(h_kv + 1) * HEAD_DIM, j * WINDOW:(j + 2) * WINDOW]
        e2 = jnp.concatenate([jnp.where(from_prev, e, 0.0), jnp.where(from_prev, 0.0, e)], axis=0).astype(BF16)
        ot = _mm(vt, e2) / den
        heads[GROUP * h_kv + par] = ot[:, :WINDOW]
        heads[GROUP * h_kv + 2 + par] = ot[:, WINDOW:]
        if h_kv == N_KV_HEADS - 1 and par == 1:
            attn = jnp.concatenate(heads, axis=0).T
            catbuf[j * WINDOW:(j + 1) * WINDOW, :Q_COLS] = _rms(attn, g_attn_ref[...]).astype(BF16)
        yield

    mix = []
    for c in range(D_MODEL // MXU_N):
        mix.append(_mm(catbuf[...], w_out_ref[:, c * MXU_N:(c + 1) * MXU_N]))
        yield
    x_mid = x_ref[...] + _rms(jnp.concatenate(mix, axis=1), g_post_ref[...])

    unew_ref[...] = ubuf[tm + 1:tm + POOL_PAD, :]
    kbuf[:, 0:WINDOW, :] = kbuf[:, tm:tm + WINDOW, :]
    vtbuf[:, 0:WINDOW] = vtbuf[:, tm:tm + WINDOW]
    ubuf[0:POOL_PAD, :] = ubuf[tm:tm + POOL_PAD, :]
    xmid_ref[...] = x_mid
    hmid_ref[...] = _rms(x_mid, g_pre_mlp_ref[...]).astype(BF16)


def _sample_mixer_kernel(x_ref, cos_ref, sin_ref, skt_ref, svt_ref, sp_ref, g_pre_ref, w_in_hbm,
                         sinks_ref, pool_w_hbm, pool_scale_ref, g_attn_ref, g_pool_ref, w_out_hbm, g_post_ref,
                         g_pre_mlp_ref, xmid_ref, hmid_ref, knew_ref, vnew_ref, unew_ref,
                         ubuf, w_in_ref, pool_w_ref, w_out_ref):
    @pl.when(pl.program_id(0) == 0)
    def _():
        _cast_weights_to_vmem([w_in_hbm, pool_w_hbm, w_out_hbm], [w_in_ref, pool_w_ref, w_out_ref])

    rows = x_ref.shape[0]
    nseq = skt_ref.shape[0]
    t = rows // nseq
    x = x_ref[...]
    q, k, v, u = _project(x, g_pre_ref[...], w_in_ref[...], (cos_ref[...], sin_ref[...]))

    k3 = k.reshape(nseq, t, KV_COLS)
    v3 = v.reshape(nseq, t, KV_COLS)
    sk = jnp.swapaxes(skt_ref[...], 1, 2)
    sv = jnp.swapaxes(svt_ref[...], 1, 2)
    knew_ref[...] = jnp.swapaxes(jnp.concatenate([sk[:, t:, :], k3], axis=1), 1, 2)
    vnew_ref[...] = jnp.swapaxes(jnp.concatenate([sv[:, t:, :], v3], axis=1), 1, 2)

    r_i = lax.broadcasted_iota(jnp.int32, (GROUP * t, WINDOW), 0) % t
    c_i = lax.broadcasted_iota(jnp.int32, (GROUP * t, WINDOW), 1)
    mask_old = (c_i > r_i)[None]
    r_n = lax.broadcasted_iota(jnp.int32, (GROUP * t, t), 0) % t
    c_n = lax.broadcasted_iota(jnp.int32, (GROUP * t, t), 1)
    mask_new = (c_n <= r_n)[None]

    qb = q.astype(BF16)
    kb = k3.astype(BF16)
    vb = v3.astype(BF16)
    heads = [None] * N_HEADS
    scores = []
    for h in range(N_KV_HEADS):
        lanes = slice(h * HEAD_DIM, (h + 1) * HEAD_DIM)
        q4 = jnp.concatenate(
            [qb[:, (GROUP * h + g) * HEAD_DIM:(GROUP * h + g + 1) * HEAD_DIM].reshape(nseq, t, HEAD_DIM)
             for g in range(GROUP)], axis=1)
        s_old = jnp.einsum('bqd,bkd->bqk', q4, sk[:, :, lanes].astype(BF16), preferred_element_type=F32)
        s_new = jnp.einsum('bqd,bkd->bqk', q4, kb[:, :, lanes], preferred_element_type=F32)
        scores.append((s_old, s_new))
    for h, (s_old, s_new) in enumerate(scores):
        lanes = slice(h * HEAD_DIM, (h + 1) * HEAD_DIM)
        s_old = jnp.where(mask_old, s_old, NEG)
        s_new = jnp.where(mask_new, s_new, NEG)
        sink = jnp.concatenate(
            [jnp.full((1, t, 1), sinks_ref[GROUP * h + g], F32) for g in range(GROUP)], axis=1)
        (e_old, e_new), den = _softmax_parts([s_old, s_new], sink)
        o = jnp.einsum('bqk,bkd->bqd', e_old.astype(BF16), sv[:, :, lanes].astype(BF16), preferred_element_type=F32)
        o = o + jnp.einsum('bqk,bkd->bqd', e_new.astype(BF16), vb[:, :, lanes], preferred_element_type=F32)
        o = o / den
        for g in range(GROUP):
            heads[GROUP * h + g] = o[:, g * t:(g + 1) * t, :].reshape(rows, HEAD_DIM)
    attn = jnp.concatenate(heads, axis=1)

    ext = POOL_PAD + t
    per_seq = lambda j: pl.ds(j, nseq, stride=ext)
    win = []
    for g, w in enumerate(POOL_WINDOWS):
        cols = slice(g * POOL_CG, (g + 1) * POOL_CG)
        ubuf[g, per_seq(0), :] = jnp.zeros((nseq, POOL_CG), F32)
        for j in range(POOL_BUF):
            ubuf[g, per_seq(1 + j), :] = sp_ref[j, :, cols]
        for s_i in range(nseq):
            ubuf[g, s_i * ext + POOL_PAD:(s_i + 1) * ext, :] = u[s_i * t:(s_i + 1) * t, cols]
        for j in range(POOL_BUF):
            unew_ref[j, :, cols] = ubuf[g, per_seq(t + 1 + j), :]
        a = ubuf[g]
        shift = 1
        while shift < w:
            a = a + pltpu.roll(a, shift, axis=0)
            shift *= 2
        win.append(a.reshape(nseq, ext, POOL_CG)[:, POOL_PAD:, :].reshape(rows, POOL_CG))
    inv_cnt = [1.0 / w for w in POOL_WINDOWS]
    pooled = _pool_out(win, u, inv_cnt, pool_w_ref, pool_scale_ref[...])

    x_mid = _mix_out(x, attn, pooled, g_attn_ref[...], g_pool_ref[...], w_out_ref[...], g_post_ref[...])
    xmid_ref[...] = x_mid
    hmid_ref[...] = _rms(x_mid, g_pre_mlp_ref[...]).astype(BF16)


def _ffn_steps(x_ref, h_ref, load_p, w_up_ref, w_down_ref, g_post_ref, w_gate_ref, b_gate_ref, w_ple_ref,
               y_ref, actbuf):
    for c in range(D_FF // MXU_N):
        cols = slice(c * MXU_N, (c + 1) * MXU_N)
        up = _mm(h_ref[...], w_up_ref[:, cols])
        actbuf[:, cols] = jnp.square(jnp.maximum(up, 0.0)).astype(BF16)
        yield
    ff = []
    n_col, n_row = D_MODEL // MXU_N, D_FF // FF_CHUNK
    for c in range(n_col):
        cols = slice(c * MXU_N, (c + 1) * MXU_N)
        acc = None
        for r in range(n_row):
            rows = slice(r * FF_CHUNK, (r + 1) * FF_CHUNK)
            part = _mm(actbuf[:, rows], w_down_ref[rows, cols])
            acc = part if acc is None else acc + part
            if (c, r) != (n_col - 1, n_row - 1):
                yield
        ff.append(acc)
    x = x_ref[...] + _rms(jnp.concatenate(ff, axis=1), g_post_ref[...])
    xb = x.astype(BF16)
    yield
    ple = _mm(load_p().astype(BF16), w_ple_ref[...])
    yield
    for c in range(D_MODEL // MXU_N):
        cols = slice(c * MXU_N, (c + 1) * MXU_N)
        gate = 0.5 * jnp.tanh(0.5 * (_mm(xb, w_gate_ref[:, cols]) + b_gate_ref[:, cols])) + 0.5
        y_ref[:, cols] = x[:, cols] + gate * ple[:, cols]
        yield


def _interleave(first, second, pattern):
    gens = {'1': first, '2': second}
    for tag in pattern:
        next(gens[tag], None)
    for _ in first:
        pass
    for _ in second:
        pass


def _cast_weights_to_vmem(srcs, dsts):
    tasks = []
    for src, dst in zip(srcs, dsts):
        if len(src.shape) != 2:
            tasks.append((src, dst, tuple(src.shape)))
            continue
        rows, cols = src.shape
        col_block = cols if cols <= STAGE_MAX_COLS else D_MODEL
        for c0 in range(0, cols, col_block):
            for r0 in range(0, rows, STAGE_ROWS):
                view = (slice(r0, r0 + STAGE_ROWS), slice(c0, c0 + col_block))
                tasks.append((src.at[view], dst.at[view], (STAGE_ROWS, col_block)))
    shapes = sorted(set(shape for _, _, shape in tasks))
    counts = [sum(1 for t in tasks if t[2] == shape) for shape in shapes]
    slots = [STAGE_SLOTS if n >= 2 * STAGE_SLOTS else min(n, 2) for n in counts]

    def body(*scoped):
        *stages, sems = scoped
        uses = [0] * len(shapes)
        copies, frees = [], []
        last_in_slot = {}
        for t, (src, dst, shape) in enumerate(tasks):
            k = shapes.index(shape)
            slot = uses[k] % slots[k]
            uses[k] += 1
            stage = stages[k].at[slot]
            copies.append((pltpu.make_async_copy(src, stage, sems.at[k, slot]), stage, dst))
            frees.append(last_in_slot.get((k, slot)))
            last_in_slot[(k, slot)] = t
        started = 0
        for t, (copy, stage, dst) in enumerate(copies):
            while started < len(copies) and (frees[started] is None or frees[started] < t):
                copies[started][0].start()
                started += 1
            copy.wait()
            dst[...] = stage[...].astype(BF16)

    pl.run_scoped(body, *[pltpu.VMEM((n,) + shape, F32) for n, shape in zip(slots, shapes)],
                  pltpu.SemaphoreType.DMA((len(shapes), max(slots))))


N_TABLES = 5
MIXER_MATMUL_W = (1, 3, 7)
FFN_MATMUL_W = (1, 2, 4, 6)
STAGE_ROWS = 256
STAGE_MAX_COLS = 1280
STAGE_SLOTS = 8


N_SAMPLE_TILES = 2


def _prompt_layer_kernel(*refs):
    x_ref, *tables = refs[:1 + N_TABLES]
    (g_pre_mix, w_in_hbm, sinks, pool_w_hbm, pool_scale, g_attn, g_pool, w_out_hbm, g_post_mix, p_ref,
     g_pre_mlp, w_up_hbm, w_down_hbm, g_post_mlp, w_gate_hbm, b_gate, w_ple_hbm,
     xs_hbm, hs_hbm, ps_ref) = refs[1 + N_TABLES:21 + N_TABLES]
    y_ref, ys_hbm, knew_ref, vnew_ref, unew_ref = refs[21 + N_TABLES:26 + N_TABLES]
    (qbuf, kbuf, vtbuf, ubuf, pbuf, catbuf, actbuf, xmid, hmid, sems,
     w_in, pool_w, w_out, w_up, w_down, w_gate, w_ple) = refs[26 + N_TABLES:]
    tm = x_ref.shape[0]
    step = pl.program_id(0)
    last_tile = pl.num_programs(0) - 1 - N_SAMPLE_TILES
    on_sample = step < N_SAMPLE_TILES
    slot = step % 2

    def sample_loads(n):
        rows = pl.ds(n * tm, tm)
        return [pltpu.make_async_copy(xs_hbm.at[rows], xmid.at[n], sems.at[2 * n]),
                pltpu.make_async_copy(hs_hbm.at[rows], hmid.at[n], sems.at[2 * n + 1])]

    def sample_store(n):
        return pltpu.make_async_copy(y_ref, ys_hbm.at[pl.ds(n * tm, tm)], sems.at[2 * N_SAMPLE_TILES + n])

    @pl.when(step == 0)
    def _():
        for n in range(N_SAMPLE_TILES):
            for copy in sample_loads(n):
                copy.start()
        _cast_weights_to_vmem([w_in_hbm, pool_w_hbm, w_out_hbm, w_up_hbm, w_down_hbm, w_gate_hbm, w_ple_hbm],
                              [w_in, pool_w, w_out, w_up, w_down, w_gate, w_ple])
        for copy in sample_loads(0):
            copy.wait()

    for n in range(N_SAMPLE_TILES):
        @pl.when(step == n + 1)
        def _(n=n):
            sample_store(n).wait()

    @pl.when(step == 1)
    def _():
        kbuf[:, 0:WINDOW, :] = jnp.zeros((2 * N_KV_HEADS, WINDOW, LANES), BF16)
        vtbuf[:, 0:WINDOW] = jnp.zeros((KV_COLS, WINDOW), BF16)
        ubuf[0:POOL_PAD, :] = jnp.zeros((POOL_PAD, POOL_WIDTH), F32)
        for copy in sample_loads(1):
            copy.wait()

    def ffn_steps():
        return _ffn_steps(xmid.at[slot], hmid.at[slot], lambda: jnp.where(on_sample, ps_ref[...], p_ref[...]),
                          w_up, w_down, g_post_mlp, w_gate, b_gate, w_ple, y_ref, actbuf)

    ffn_only = (step == 0) | (step == pl.num_programs(0) - 1)

    @pl.when(ffn_only)
    def _():
        for _ in ffn_steps():
            pass

    @pl.when(jnp.logical_not(ffn_only))
    def _():
        mixer = _prompt_mixer_steps(jnp.clip(step - 1, 0, last_tile), x_ref, *tables, g_pre_mix, w_in, sinks, pool_w,
                                    pool_scale, g_attn, g_pool, w_out, g_post_mix, knew_ref, vnew_ref, unew_ref,
                                    qbuf, kbuf, vtbuf, ubuf, pbuf, catbuf, g_pre_mlp, xmid.at[1 - slot],
                                    hmid.at[1 - slot])
        _interleave(ffn_steps(), mixer, PROMPT_PATTERN)

    for n in range(N_SAMPLE_TILES):
        @pl.when(step == n)
        def _(n=n):
            sample_store(n).start()


def _const_spec(shape):
    zeros = (0,) * len(shape)
    return pl.BlockSpec(shape, lambda i: zeros, pipeline_mode=pl.Buffered(1))


def _rope_lane_tables(pos):
    half = HEAD_DIM // 2
    inv = ROPE_THETA ** (-np.arange(half, dtype=np.float64) / half)
    ang = pos[:, None] * inv[None, :]
    reps = LANES // half
    return (np.tile(np.cos(ang), (1, reps)).astype(np.float32), np.tile(np.sin(ang), (1, reps)).astype(np.float32))


def _rope_sign():
    lane = np.arange(LANES)
    return np.where((lane % HEAD_DIM) < HEAD_DIM // 2, -1.0, 1.0).astype(np.float32)[None, :]


def _params(dims):
    return pltpu.CompilerParams(dimension_semantics=dims, vmem_limit_bytes=VMEM_LIMIT)


def _mixer_weight_specs():
    return [
        _const_spec((1, D_MODEL)),
        _const_spec((D_MODEL, IN_COLS)),
        pl.BlockSpec(memory_space=pltpu.SMEM),
        _const_spec((len(POOL_WINDOWS), POOL_CG, POOL_CG)),
        _const_spec((1, POOL_WIDTH)),
        _const_spec((1, Q_COLS)),
        _const_spec((1, POOL_WIDTH)),
        _const_spec((D_MODEL, D_MODEL)),
        _const_spec((1, D_MODEL)),
    ]


def _ffn_weight_specs():
    return [
        _const_spec((1, D_MODEL)),
        _const_spec((D_MODEL, D_FF)),
        _const_spec((D_FF, D_MODEL)),
        _const_spec((1, D_MODEL)),
        _const_spec((D_MODEL, D_MODEL)),
        _const_spec((1, D_MODEL)),
        _const_spec((PLE_DIM, D_MODEL)),
    ]


def _prompt_layer(x, p, xs_mid, hs_mid, ps, mixer_w, ffn_w):
    seq = x.shape[0]
    tm = PROMPT_TILE
    tiles = seq // tm
    assert xs_mid.shape[0] == N_SAMPLE_TILES * tm
    cos_a, sin_a = _rope_lane_tables(np.arange(tiles, dtype=np.float64) * tm)
    cos_b, sin_b = _rope_lane_tables(np.arange(tm, dtype=np.float64))
    mixer_tile = lambda i: (jnp.clip(i - 1, 0, tiles - 1), 0)
    mixer_tile3 = lambda i: (jnp.clip(i - 1, 0, tiles - 1), 0, 0)
    ffn_tile = lambda i: (jnp.clip(i - N_SAMPLE_TILES, 0, tiles - 1), 0)
    sample_tile = lambda i: (jnp.minimum(i, N_SAMPLE_TILES - 1), 0)
    fixed = lambda i: (0, 0)
    in_specs = [
        pl.BlockSpec((tm, D_MODEL), mixer_tile),
        pl.BlockSpec((1, 1, LANES), mixer_tile3),
        pl.BlockSpec((1, 1, LANES), mixer_tile3),
        _const_spec((tm, LANES)),
        _const_spec((tm, LANES)),
        _const_spec((1, LANES)),
    ] + _mixer_weight_specs() + [pl.BlockSpec((tm, PLE_DIM), ffn_tile)] + _ffn_weight_specs() + [
        pl.BlockSpec(memory_space=pl.ANY),
        pl.BlockSpec(memory_space=pl.ANY),
        pl.BlockSpec((tm, PLE_DIM), sample_tile, pipeline_mode=pl.Buffered(1)),
    ]
    n_lead = 1 + N_TABLES
    matmul_w = [mixer_w[n] for n in MIXER_MATMUL_W] + [ffn_w[n] for n in FFN_MATMUL_W]
    for n in [n_lead + n for n in MIXER_MATMUL_W] + [n_lead + len(mixer_w) + 1 + n for n in FFN_MATMUL_W]:
        in_specs[n] = pl.BlockSpec(memory_space=pl.ANY)
    return pl.pallas_call(
        _prompt_layer_kernel,
        grid=(tiles + N_SAMPLE_TILES,),
        in_specs=in_specs,
        out_specs=[
            pl.BlockSpec((tm, D_MODEL), ffn_tile),
            pl.BlockSpec(memory_space=pl.ANY),
            pl.BlockSpec((WINDOW, KV_COLS), fixed),
            pl.BlockSpec((WINDOW, KV_COLS), fixed),
            pl.BlockSpec((POOL_BUF, POOL_WIDTH), fixed),
        ],
        out_shape=[
            jax.ShapeDtypeStruct((seq, D_MODEL), F32),
            jax.ShapeDtypeStruct(xs_mid.shape, F32),
            jax.ShapeDtypeStruct((WINDOW, KV_COLS), F32),
            jax.ShapeDtypeStruct((WINDOW, KV_COLS), F32),
            jax.ShapeDtypeStruct((POOL_BUF, POOL_WIDTH), F32),
        ],
        scratch_shapes=[
            pltpu.VMEM((tm, Q_COLS), BF16),
            pltpu.VMEM((2 * N_KV_HEADS, WINDOW + tm, LANES), BF16),
            pltpu.VMEM((KV_COLS, WINDOW + tm), BF16),
            pltpu.VMEM((POOL_PAD + tm, POOL_WIDTH), F32),
            pltpu.VMEM((tm, POOL_WIDTH), F32),
            pltpu.VMEM((tm, D_MODEL), BF16),
            pltpu.VMEM((tm, D_FF), BF16),
            pltpu.VMEM((2, tm, D_MODEL), F32),
            pltpu.VMEM((2, tm, D_MODEL), BF16),
            pltpu.SemaphoreType.DMA((3 * N_SAMPLE_TILES,)),
        ] + [pltpu.VMEM(w.shape, BF16) for w in matmul_w],
        compiler_params=_params(("arbitrary",)),
        name="prompt_layer",
    )(x, jnp.asarray(cos_a)[:, None, :], jnp.asarray(sin_a)[:, None, :], jnp.asarray(cos_b), jnp.asarray(sin_b),
      jnp.asarray(_rope_sign()), *mixer_w, p, *ffn_w, xs_mid, hs_mid, ps)


def _sample_mixer(x, state_kt, state_vt, state_pool_t, past_len, mixer_w, g_pre_mlp):
    nseq_all, t = state_kt.shape[0], x.shape[0] // state_kt.shape[0]
    nseq = SAMPLE_SEQS
    rows = nseq * t
    steps = nseq_all // nseq
    cos, sin = _rope_lane_tables(past_len + np.arange(rows, dtype=np.float64) % t)
    sin = sin * _rope_sign()
    row = lambda i: (i, 0)
    seq3 = lambda i: (i, 0, 0)
    pos3 = lambda i: (0, i, 0)
    mixer_specs = _mixer_weight_specs()
    for n in MIXER_MATMUL_W:
        mixer_specs[n] = pl.BlockSpec(memory_space=pl.ANY)
    return pl.pallas_call(
        _sample_mixer_kernel,
        grid=(steps,),
        in_specs=[
            pl.BlockSpec((rows, D_MODEL), row),
            _const_spec((rows, LANES)),
            _const_spec((rows, LANES)),
            pl.BlockSpec((nseq, KV_COLS, WINDOW), seq3),
            pl.BlockSpec((nseq, KV_COLS, WINDOW), seq3),
            pl.BlockSpec((POOL_BUF, nseq, POOL_WIDTH), pos3),
        ] + mixer_specs + [_const_spec((1, D_MODEL))],
        out_specs=[
            pl.BlockSpec((rows, D_MODEL), row),
            pl.BlockSpec((rows, D_MODEL), row),
            pl.BlockSpec((nseq, KV_COLS, WINDOW), seq3),
            pl.BlockSpec((nseq, KV_COLS, WINDOW), seq3),
            pl.BlockSpec((POOL_BUF, nseq, POOL_WIDTH), pos3),
        ],
        out_shape=[
            jax.ShapeDtypeStruct((nseq_all * t, D_MODEL), F32),
            jax.ShapeDtypeStruct((nseq_all * t, D_MODEL), BF16),
            jax.ShapeDtypeStruct((nseq_all, KV_COLS, WINDOW), F32),
            jax.ShapeDtypeStruct((nseq_all, KV_COLS, WINDOW), F32),
            jax.ShapeDtypeStruct((POOL_BUF, nseq_all, POOL_WIDTH), F32),
        ],
        scratch_shapes=[pltpu.VMEM((len(POOL_WINDOWS), nseq * (POOL_PAD + t), POOL_CG), F32)]
        + [pltpu.VMEM(mixer_w[n].shape, BF16) for n in MIXER_MATMUL_W],
        compiler_params=_params(("arbitrary",)),
        name="sample_mixer",
    )(x, jnp.asarray(cos), jnp.asarray(sin), state_kt, state_vt, state_pool_t, *mixer_w, g_pre_mlp)


def kernel(x_prompt, x_sample, state_k, state_v, state_pool, p_prompt, p_sample, w_in, attn_sinks, pool_w,
           pool_scale, g_attn_out, g_pool_out, w_out, g_pre_mix, g_post_mix, g_pre_mlp, g_post_mlp, w_up, w_down,
           w_ple, w_ple_gate, b_ple_gate):
    depth = w_in.shape[0]
    batch, seq, _ = x_prompt.shape
    dec_batch, dec_seq, _ = x_sample.shape
    assert depth == 1 and batch == 1

    xp = x_prompt.reshape(seq, D_MODEL)
    xs = x_sample.reshape(dec_batch * dec_seq, D_MODEL)
    i = 0
    mixer_w = [
        g_pre_mix[i][None, :], w_in[i], attn_sinks[i], pool_w[i], pool_scale[i][None, :], g_attn_out[i][None, :],
        g_pool_out[i][None, :], w_out[i], g_post_mix[i][None, :],
    ]
    ffn_w = [
        g_pre_mlp[i][None, :], w_up[i], w_down[i], g_post_mlp[i][None, :], w_ple_gate[i], b_ple_gate[i][None, :],
        w_ple[i],
    ]

    xs_mid, hs_mid, ks_t, vs_t, us_t = _sample_mixer(
        xs, state_k[i].reshape(dec_batch, WINDOW, KV_COLS).transpose(0, 2, 1),
        state_v[i].reshape(dec_batch, WINDOW, KV_COLS).transpose(0, 2, 1),
        state_pool[i].transpose(1, 0, 2), float(PAST_LEN), mixer_w, ffn_w[0])
    ks, vs, us = ks_t.transpose(0, 2, 1), vs_t.transpose(0, 2, 1), us_t.transpose(1, 0, 2)
    yp, ys, kp, vp, up = _prompt_layer(xp, p_prompt[i, 0], xs_mid, hs_mid,
                                       p_sample[i].reshape(dec_batch * dec_seq, PLE_DIM), mixer_w, ffn_w)

    return (
        yp.reshape(batch, seq, D_MODEL),
        ys.reshape(dec_batch, dec_seq, D_MODEL),
        kp.T.reshape(depth, batch, WINDOW, N_KV_HEADS, HEAD_DIM),
        vp.T.reshape(depth, batch, WINDOW, N_KV_HEADS, HEAD_DIM),
        up.reshape(depth, batch, POOL_BUF, POOL_WIDTH),
        ks.reshape(depth, dec_batch, WINDOW, N_KV_HEADS, HEAD_DIM),
        vs.reshape(depth, dec_batch, WINDOW, N_KV_HEADS, HEAD_DIM),
        us.reshape(depth, dec_batch, POOL_BUF, POOL_WIDTH),
    )
```

```python
import jax
import jax.numpy as jnp
import numpy as np
from jax import lax
from jax.experimental import pallas as pl
from jax.experimental.pallas import tpu as pltpu

D_MODEL = 1024
HEAD_DIM = 64
N_HEADS = 8
N_KV_HEADS = 2
GROUP = N_HEADS // N_KV_HEADS
WINDOW = 128
Q_COLS = N_HEADS * HEAD_DIM
KV_COLS = N_KV_HEADS * HEAD_DIM
POOL_WIDTH = 512
POOL_WINDOWS = (2, 4, 8, 16)
POOL_CG = POOL_WIDTH // len(POOL_WINDOWS)
POOL_BUF = max(POOL_WINDOWS) - 1
POOL_PAD = POOL_BUF + 1
IN_COLS = Q_COLS + 2 * KV_COLS + POOL_WIDTH
D_FF = 4 * D_MODEL
FF_CHUNK = 1024
PLE_DIM = 256
ROPE_THETA = 10000.0
PAST_LEN = 16384
EPS = 1e-6
NEG = -1e30
LANES = 128
MXU_N = 256

ATTN_LOOKAHEAD = 4
PROMPT_PATTERN = "11" + "21" * (5 + ATTN_LOOKAHEAD + 16 + 4) + "11" + "22" + "11" + "222" + "11"

PROMPT_TILE = 512
SAMPLE_SEQS = 32
VMEM_LIMIT = 58 * 1024 * 1024

BF16 = jnp.bfloat16
F32 = jnp.float32


def _rms(x, g):
    y = x * lax.rsqrt(jnp.mean(x * x, axis=-1, keepdims=True) + EPS)
    return y * g


def _mm(a, w):
    return jnp.dot(a, w, preferred_element_type=F32)


def _rope_tables(cos_a, sin_a, cos_b, sin_b, sign):
    cos_t = cos_a * cos_b - sin_a * sin_b
    sin_t = (sin_a * cos_b + cos_a * sin_b) * sign
    return cos_t, sin_t


def _rope(x, cos_t, sin_t, first_half):
    n = x.shape[1] // LANES
    width = x.shape[1]
    partner = jnp.where(first_half, pltpu.roll(x, width - HEAD_DIM // 2, axis=1), pltpu.roll(x, HEAD_DIM // 2, axis=1))
    if n > 1:
        cos_t = jnp.concatenate([cos_t] * n, axis=1)
        sin_t = jnp.concatenate([sin_t] * n, axis=1)
    return x * cos_t + partner * sin_t


def _first_half_mask(rows, width):
    lane = lax.broadcasted_iota(jnp.int32, (rows, width), 1)
    return (lane % HEAD_DIM) < (HEAD_DIM // 2)


def _project(x, g_pre, w_in, tables):
    rows = x.shape[0]
    h = _rms(x, g_pre).astype(BF16)
    proj = _mm(h, w_in)
    cos_t, sin_t = tables
    q = _rope(proj[:, :Q_COLS], cos_t, sin_t, _first_half_mask(rows, Q_COLS)) * (HEAD_DIM ** -0.5)
    k = _rope(proj[:, Q_COLS:Q_COLS + KV_COLS], cos_t, sin_t, _first_half_mask(rows, KV_COLS))
    v = proj[:, Q_COLS + KV_COLS:Q_COLS + 2 * KV_COLS]
    u = proj[:, Q_COLS + 2 * KV_COLS:]
    return q, k, v, u


def _pool_out(win, tok, inv_cnt, pool_w_ref, pool_scale):
    outs = []
    for g in range(len(POOL_WINDOWS)):
        r = win[g] * inv_cnt[g] - tok[:, g * POOL_CG:(g + 1) * POOL_CG]
        outs.append(_mm(r.astype(BF16), pool_w_ref[g]))
    return jnp.concatenate(outs, axis=1) * pool_scale


def _mix_out(x, attn, pooled, g_attn, g_pool, w_out, g_post):
    cat = jnp.concatenate([_rms(attn, g_attn), _rms(pooled, g_pool)], axis=1).astype(BF16)
    return x + _rms(_mm(cat, w_out), g_post)


def _softmax_parts(s_list, sink):
    m = sink
    for s in s_list:
        m = jnp.maximum(m, jnp.max(s, axis=-1, keepdims=True))
    e_list = [jnp.exp(s - m) for s in s_list]
    den = jnp.exp(sink - m)
    for e in e_list:
        den = den + jnp.sum(e, axis=-1, keepdims=True)
    return e_list, den


def _prompt_mixer_steps(tile, x_ref, cos_a_ref, sin_a_ref, cos_b_ref, sin_b_ref, sign_ref, g_pre_ref, w_in_ref,
                        sinks_ref, pool_w_ref, pool_scale_ref, g_attn_ref, g_pool_ref, w_out_ref, g_post_ref,
                        knew_ref, vnew_ref, unew_ref, qbuf, kbuf, vtbuf, ubuf, pbuf, catbuf,
                        g_pre_mlp_ref, xmid_ref, hmid_ref):
    tm = x_ref.shape[0]
    nblk = tm // WINDOW
    half_lanes = LANES // 2
    base = tile * tm

    h = _rms(x_ref[...], g_pre_ref[...]).astype(BF16)
    cos_t, sin_t = _rope_tables(cos_a_ref[0], sin_a_ref[0], cos_b_ref[...], sin_b_ref[...], sign_ref[...])

    kv = _mm(h, w_in_ref[:, Q_COLS:Q_COLS + 2 * KV_COLS])
    k = _rope(kv[:, :KV_COLS], cos_t, sin_t, _first_half_mask(tm, KV_COLS))
    v = kv[:, KV_COLS:]
    knew_ref[...] = k[tm - WINDOW:, :].T
    vnew_ref[...] = v[tm - WINDOW:, :].T
    low = lax.broadcasted_iota(jnp.int32, (tm, LANES), 1) < half_lanes
    k_swapped = pltpu.roll(k, half_lanes, axis=1)
    kbuf[0, WINDOW:, :] = jnp.where(low, k, 0.0).astype(BF16)
    kbuf[1, WINDOW:, :] = jnp.where(low, 0.0, k_swapped).astype(BF16)
    kbuf[2, WINDOW:, :] = jnp.where(low, k_swapped, 0.0).astype(BF16)
    kbuf[3, WINDOW:, :] = jnp.where(low, 0.0, k).astype(BF16)
    vtbuf[:, WINDOW:] = v.T.astype(BF16)
    yield

    u_col0 = Q_COLS + 2 * KV_COLS
    for c in range(Q_COLS // MXU_N):
        cols = slice(c * MXU_N, (c + 1) * MXU_N)
        q = _rope(_mm(h, w_in_ref[:, cols]), cos_t, sin_t, _first_half_mask(tm, MXU_N)) * (HEAD_DIM ** -0.5)
        qbuf[:, cols] = q.astype(BF16)
        yield
    for c in range(POOL_WIDTH // MXU_N):
        cols = slice(c * MXU_N, (c + 1) * MXU_N)
        ubuf[POOL_PAD:, cols] = _mm(h, w_in_ref[:, u_col0 + c * MXU_N:u_col0 + (c + 1) * MXU_N])
        yield

    pos = base + lax.broadcasted_iota(jnp.int32, (tm, 1), 0)
    pool_scale = pool_scale_ref[...]
    for g, w in enumerate(POOL_WINDOWS):
        cols = slice(g * POOL_CG, (g + 1) * POOL_CG)
        a = ubuf[:, cols]
        shift = 1
        while shift < w:
            a = a + pltpu.roll(a, shift, axis=0)
            shift *= 2
        inv_cnt = 1.0 / jnp.minimum(pos + 1, w).astype(F32)
        r = a[POOL_PAD:, :] * inv_cnt - ubuf[POOL_PAD:, cols]
        pbuf[:, cols] = _mm(r.astype(BF16), pool_w_ref[g]) * pool_scale[:, cols]
        yield
    catbuf[:, Q_COLS:] = _rms(pbuf[...], g_pool_ref[...]).astype(BF16)

    c_i = lax.broadcasted_iota(jnp.int32, (WINDOW, 2 * WINDOW), 0)
    r_i = lax.broadcasted_iota(jnp.int32, (WINDOW, 2 * WINDOW), 1) % WINDOW
    from_prev = c_i > r_i
    no_prev = from_prev & (base == 0)

    def scores(j, h_kv, par):
        rows = slice(j * WINDOW, (j + 1) * WINDOW)
        keys = slice(j * WINDOW, (j + 2) * WINDOW)
        q2 = jnp.concatenate([qbuf[rows, (2 * h_kv) * LANES:(2 * h_kv + 1) * LANES],
                              qbuf[rows, (2 * h_kv + 1) * LANES:(2 * h_kv + 2) * LANES]], axis=0)
        return lax.dot_general(kbuf[2 * h_kv + par, keys, :], q2, (((1,), (1,)), ((), ())),
                               preferred_element_type=F32)

    units = [(j, h_kv, par) for j in range(nblk) for h_kv in range(N_KV_HEADS) for par in range(2)]
    pending = []
    for n in range(min(ATTN_LOOKAHEAD, len(units))):
        pending.append(scores(*units[n]))
        yield
    heads = [None] * N_HEADS
    for n, (j, h_kv, par) in enumerate(units):
        st = pending.pop(0)
        if n + ATTN_LOOKAHEAD < len(units):
            pending.append(scores(*units[n + ATTN_LOOKAHEAD]))
        st = jnp.where(from_prev, st[:WINDOW], st[WINDOW:])
        if j == 0:
            st = jnp.where(no_prev, NEG, st)
        sink = jnp.concatenate([jnp.full((1, WINDOW), sinks_ref[GROUP * h_kv + par], F32),
                                jnp.full((1, WINDOW), sinks_ref[GROUP * h_kv + 2 + par], F32)], axis=1)
        m = jnp.maximum(jnp.max(st, axis=0, keepdims=True), sink)
        e = jnp.exp(st - m)
        den = jnp.sum(e, axis=0, keepdims=True) + jnp.exp(sink - m)
        vt = vtbuf[h_kv * HEAD_DIM:(h_kv + 1) * HEAD_DIM, j * WINDOW:(j + 2) * WINDOW]
        e2 = jnp.concatenate([jnp.where(from_prev, e, 0.0), jnp.where(from_prev, 0.0, e)], axis=0).astype(BF16)
        ot = _mm(vt, e2) / den
        heads[GROUP * h_kv + par] = ot[:, :WINDOW]
        heads[GROUP * h_kv + 2 + par] = ot[:, WINDOW:]
        if h_kv == N_KV_HEADS - 1 and par == 1:
            attn = jnp.concatenate(heads, axis=0).T
            catbuf[j * WINDOW:(j + 1) * WINDOW, :Q_COLS] = _rms(attn, g_attn_ref[...]).astype(BF16)
        yield

    mix = []
    for c in range(D_MODEL // MXU_N):
        mix.append(_mm(catbuf[...], w_out_ref[:, c * MXU_N:(c + 1) * MXU_N]))
        yield
    x_mid = x_ref[...] + _rms(jnp.concatenate(mix, axis=1), g_post_ref[...])

    unew_ref[...] = ubuf[tm + 1:tm + POOL_PAD, :]
    kbuf[:, 0:WINDOW, :] = kbuf[:, tm:tm + WINDOW, :]
    vtbuf[:, 0:WINDOW] = vtbuf[:, tm:tm + WINDOW]
    ubuf[0:POOL_PAD, :] = ubuf[tm:tm + POOL_PAD, :]
    xmid_ref[...] = x_mid
    hmid_ref[...] = _rms(x_mid, g_pre_mlp_ref[...]).astype(BF16)


def _sample_mixer_kernel(x_ref, cos_ref, sin_ref, skt_ref, svt_ref, sp_ref, g_pre_ref, w_in_hbm,
                         sinks_ref, pool_w_hbm, pool_scale_ref, g_attn_ref, g_pool_ref, w_out_hbm, g_post_ref,
                         g_pre_mlp_ref, xmid_ref, hmid_ref, knew_ref, vnew_ref, unew_ref,
                         ubuf, w_in_ref, pool_w_ref, w_out_ref):
    @pl.when(pl.program_id(0) == 0)
    def _():
        _cast_weights_to_vmem([w_in_hbm, pool_w_hbm, w_out_hbm], [w_in_ref, pool_w_ref, w_out_ref])

    rows = x_ref.shape[0]
    nseq = skt_ref.shape[0]
    t = rows // nseq
    x = x_ref[...]
    q, k, v, u = _project(x, g_pre_ref[...], w_in_ref[...], (cos_ref[...], sin_ref[...]))

    k3 = k.reshape(nseq, t, KV_COLS)
    v3 = v.reshape(nseq, t, KV_COLS)
    sk = jnp.swapaxes(skt_ref[...], 1, 2)
    sv = jnp.swapaxes(svt_ref[...], 1, 2)
    knew_ref[...] = jnp.swapaxes(jnp.concatenate([sk[:, t:, :], k3], axis=1), 1, 2)
    vnew_ref[...] = jnp.swapaxes(jnp.concatenate([sv[:, t:, :], v3], axis=1), 1, 2)

    r_i = lax.broadcasted_iota(jnp.int32, (GROUP * t, WINDOW), 0) % t
    c_i = lax.broadcasted_iota(jnp.int32, (GROUP * t, WINDOW), 1)
    mask_old = (c_i > r_i)[None]
    r_n = lax.broadcasted_iota(jnp.int32, (GROUP * t, t), 0) % t
    c_n = lax.broadcasted_iota(jnp.int32, (GROUP * t, t), 1)
    mask_new = (c_n <= r_n)[None]

    qb = q.astype(BF16)
    kb = k3.astype(BF16)
    vb = v3.astype(BF16)
    heads = [None] * N_HEADS
    scores = []
    for h in range(N_KV_HEADS):
        lanes = slice(h * HEAD_DIM, (h + 1) * HEAD_DIM)
        q4 = jnp.concatenate(
            [qb[:, (GROUP * h + g) * HEAD_DIM:(GROUP * h + g + 1) * HEAD_DIM].reshape(nseq, t, HEAD_DIM)
             for g in range(GROUP)], axis=1)
        s_old = jnp.einsum('bqd,bkd->bqk', q4, sk[:, :, lanes].astype(BF16), preferred_element_type=F32)
        s_new = jnp.einsum('bqd,bkd->bqk', q4, kb[:, :, lanes], preferred_element_type=F32)
        scores.append((s_old, s_new))
    for h, (s_old, s_new) in enumerate(scores):
        lanes = slice(h * HEAD_DIM, (h + 1) * HEAD_DIM)
        s_old = jnp.where(mask_old, s_old, NEG)
        s_new = jnp.where(mask_new, s_new, NEG)
        sink = jnp.concatenate(
            [jnp.full((1, t, 1), sinks_ref[GROUP * h + g], F32) for g in range(GROUP)], axis=1)
        (e_old, e_new), den = _softmax_parts([s_old, s_new], sink)
        o = jnp.einsum('bqk,bkd->bqd', e_old.astype(BF16), sv[:, :, lanes].astype(BF16), preferred_element_type=F32)
        o = o + jnp.einsum('bqk,bkd->bqd', e_new.astype(BF16), vb[:, :, lanes], preferred_element_type=F32)
        o = o / den
        for g in range(GROUP):
            heads[GROUP * h + g] = o[:, g * t:(g + 1) * t, :].reshape(rows, HEAD_DIM)
    attn = jnp.concatenate(heads, axis=1)

    ext = POOL_PAD + t
    per_seq = lambda j: pl.ds(j, nseq, stride=ext)
    win = []
    for g, w in enumerate(POOL_WINDOWS):
        cols = slice(g * POOL_CG, (g + 1) * POOL_CG)
        ubuf[g, per_seq(0), :] = jnp.zeros((nseq, POOL_CG), F32)
        for j in range(POOL_BUF):
            ubuf[g, per_seq(1 + j), :] = sp_ref[j, :, cols]
        for s_i in range(nseq):
            ubuf[g, s_i * ext + POOL_PAD:(s_i + 1) * ext, :] = u[s_i * t:(s_i + 1) * t, cols]
        for j in range(POOL_BUF):
            unew_ref[j, :, cols] = ubuf[g, per_seq(t + 1 + j), :]
        a = ubuf[g]
        shift = 1
        while shift < w:
            a = a + pltpu.roll(a, shift, axis=0)
            shift *= 2
        win.append(a.reshape(nseq, ext, POOL_CG)[:, POOL_PAD:, :].reshape(rows, POOL_CG))
    inv_cnt = [1.0 / w for w in POOL_WINDOWS]
    pooled = _pool_out(win, u, inv_cnt, pool_w_ref, pool_scale_ref[...])

    x_mid = _mix_out(x, attn, pooled, g_attn_ref[...], g_pool_ref[...], w_out_ref[...], g_post_ref[...])
    xmid_ref[...] = x_mid
    hmid_ref[...] = _rms(x_mid, g_pre_mlp_ref[...]).astype(BF16)


def _ffn_steps(x_ref, h_ref, load_p, w_up_ref, w_down_ref, g_post_ref, w_gate_ref, b_gate_ref, w_ple_ref,
               y_ref, actbuf):
    for c in range(D_FF // MXU_N):
        cols = slice(c * MXU_N, (c + 1) * MXU_N)
        up = _mm(h_ref[...], w_up_ref[:, cols])
        actbuf[:, cols] = jnp.square(jnp.maximum(up, 0.0)).astype(BF16)
        yield
    ff = []
    n_col, n_row = D_MODEL // MXU_N, D_FF // FF_CHUNK
    for c in range(n_col):
        cols = slice(c * MXU_N, (c + 1) * MXU_N)
        acc = None
        for r in range(n_row):
            rows = slice(r * FF_CHUNK, (r + 1) * FF_CHUNK)
            part = _mm(actbuf[:, rows], w_down_ref[rows, cols])
            acc = part if acc is None else acc + part
            if (c, r) != (n_col - 1, n_row - 1):
                yield
        ff.append(acc)
    x = x_ref[...] + _rms(jnp.concatenate(ff, axis=1), g_post_ref[...])
    xb = x.astype(BF16)
    yield
    half_ple = 0.5 * _mm(load_p().astype(BF16), w_ple_ref[...])
    base_y = x + half_ple
    half_b = 0.5 * b_gate_ref[...]
    yield
    for c in range(D_MODEL // MXU_N):
        cols = slice(c * MXU_N, (c + 1) * MXU_N)
        t = jnp.tanh(0.5 * _mm(xb, w_gate_ref[:, cols]) + half_b[:, cols])
        y_ref[:, cols] = base_y[:, cols] + half_ple[:, cols] * t
        yield


def _interleave(first, second, pattern):
    gens = {'1': first, '2': second}
    for tag in pattern:
        next(gens[tag], None)
    for _ in first:
        pass
    for _ in second:
        pass


def _cast_weights_to_vmem(srcs, dsts):
    tasks = []
    for src, dst in zip(srcs, dsts):
        if len(src.shape) != 2:
            tasks.append((src, dst, tuple(src.shape)))
            continue
        rows, cols = src.shape
        col_block = cols if cols <= STAGE_MAX_COLS else D_MODEL
        for c0 in range(0, cols, col_block):
            for r0 in range(0, rows, STAGE_ROWS):
                view = (slice(r0, r0 + STAGE_ROWS), slice(c0, c0 + col_block))
                tasks.append((src.at[view], dst.at[view], (STAGE_ROWS, col_block)))
    shapes = sorted(set(shape for _, _, shape in tasks))
    counts = [sum(1 for t in tasks if t[2] == shape) for shape in shapes]
    slots = [STAGE_SLOTS if n >= 2 * STAGE_SLOTS else min(n, 2) for n in counts]

    def body(*scoped):
        *stages, sems = scoped
        uses = [0] * len(shapes)
        copies, frees = [], []
        last_in_slot = {}
        for t, (src, dst, shape) in enumerate(tasks):
            k = shapes.index(shape)
            slot = uses[k] % slots[k]
            uses[k] += 1
            stage = stages[k].at[slot]
            copies.append((pltpu.make_async_copy(src, stage, sems.at[k, slot]), stage, dst))
            frees.append(last_in_slot.get((k, slot)))
            last_in_slot[(k, slot)] = t
        started = 0
        for t, (copy, stage, dst) in enumerate(copies):
            while started < len(copies) and (frees[started] is None or frees[started] < t):
                copies[started][0].start()
                started += 1
            copy.wait()
            dst[...] = stage[...].astype(BF16)

    pl.run_scoped(body, *[pltpu.VMEM((n,) + shape, F32) for n, shape in zip(slots, shapes)],
                  pltpu.SemaphoreType.DMA((len(shapes), max(slots))))


N_TABLES = 5
MIXER_MATMUL_W = (1, 3, 7)
FFN_MATMUL_W = (1, 2, 4, 6)
STAGE_ROWS = 256
STAGE_MAX_COLS = 1280
STAGE_SLOTS = 8


N_SAMPLE_TILES = 2


def _prompt_layer_kernel(*refs):
    x_ref, *tables = refs[:1 + N_TABLES]
    (g_pre_mix, w_in_hbm, sinks, pool_w_hbm, pool_scale, g_attn, g_pool, w_out_hbm, g_post_mix, p_ref,
     g_pre_mlp, w_up_hbm, w_down_hbm, g_post_mlp, w_gate_hbm, b_gate, w_ple_hbm,
     xs_hbm, hs_hbm, ps_ref) = refs[1 + N_TABLES:21 + N_TABLES]
    y_ref, ys_hbm, knew_ref, vnew_ref, unew_ref = refs[21 + N_TABLES:26 + N_TABLES]
    (qbuf, kbuf, vtbuf, ubuf, pbuf, catbuf, actbuf, xmid, hmid, sems,
     w_in, pool_w, w_out, w_up, w_down, w_gate, w_ple) = refs[26 + N_TABLES:]
    tm = x_ref.shape[0]
    step = pl.program_id(0)
    last_tile = pl.num_programs(0) - 1 - N_SAMPLE_TILES
    on_sample = step < N_SAMPLE_TILES
    slot = step % 2

    def sample_loads(n):
        rows = pl.ds(n * tm, tm)
        return [pltpu.make_async_copy(xs_hbm.at[rows], xmid.at[n], sems.at[2 * n]),
                pltpu.make_async_copy(hs_hbm.at[rows], hmid.at[n], sems.at[2 * n + 1])]

    def sample_store(n):
        return pltpu.make_async_copy(y_ref, ys_hbm.at[pl.ds(n * tm, tm)], sems.at[2 * N_SAMPLE_TILES + n])

    @pl.when(step == 0)
    def _():
        for n in range(N_SAMPLE_TILES):
            for copy in sample_loads(n):
                copy.start()
        _cast_weights_to_vmem([w_in_hbm, pool_w_hbm, w_out_hbm, w_up_hbm, w_down_hbm, w_gate_hbm, w_ple_hbm],
                              [w_in, pool_w, w_out, w_up, w_down, w_gate, w_ple])
        for copy in sample_loads(0):
            copy.wait()

    for n in range(N_SAMPLE_TILES):
        @pl.when(step == n + 1)
        def _(n=n):
            sample_store(n).wait()

    @pl.when(step == 1)
    def _():
        kbuf[:, 0:WINDOW, :] = jnp.zeros((2 * N_KV_HEADS, WINDOW, LANES), BF16)
        vtbuf[:, 0:WINDOW] = jnp.zeros((KV_COLS, WINDOW), BF16)
        ubuf[0:POOL_PAD, :] = jnp.zeros((POOL_PAD, POOL_WIDTH), F32)
        for copy in sample_loads(1):
            copy.wait()

    def ffn_steps():
        return _ffn_steps(xmid.at[slot], hmid.at[slot], lambda: jnp.where(on_sample, ps_ref[...], p_ref[...]),
                          w_up, w_down, g_post_mlp, w_gate, b_gate, w_ple, y_ref, actbuf)

    ffn_only = (step == 0) | (step == pl.num_programs(0) - 1)

    @pl.when(ffn_only)
    def _():
        for _ in ffn_steps():
            pass

    @pl.when(jnp.logical_not(ffn_only))
    def _():
        mixer = _prompt_mixer_steps(jnp.clip(step - 1, 0, last_tile), x_ref, *tables, g_pre_mix, w_in, sinks, pool_w,
                                    pool_scale, g_attn, g_pool, w_out, g_post_mix, knew_ref, vnew_ref, unew_ref,
                                    qbuf, kbuf, vtbuf, ubuf, pbuf, catbuf, g_pre_mlp, xmid.at[1 - slot],
                                    hmid.at[1 - slot])
        _interleave(ffn_steps(), mixer, PROMPT_PATTERN)

    for n in range(N_SAMPLE_TILES):
        @pl.when(step == n)
        def _(n=n):
            sample_store(n).start()


def _const_spec(shape):
    zeros = (0,) * len(shape)
    return pl.BlockSpec(shape, lambda i: zeros, pipeline_mode=pl.Buffered(1))


def _rope_lane_tables(pos):
    half = HEAD_DIM // 2
    inv = ROPE_THETA ** (-np.arange(half, dtype=np.float64) / half)
    ang = pos[:, None] * inv[None, :]
    reps = LANES // half
    return (np.tile(np.cos(ang), (1, reps)).astype(np.float32), np.tile(np.sin(ang), (1, reps)).astype(np.float32))


def _rope_sign():
    lane = np.arange(LANES)
    return np.where((lane % HEAD_DIM) < HEAD_DIM // 2, -1.0, 1.0).astype(np.float32)[None, :]


def _params(dims):
    return pltpu.CompilerParams(dimension_semantics=dims, vmem_limit_bytes=VMEM_LIMIT)


def _mixer_weight_specs():
    return [
        _const_spec((1, D_MODEL)),
        _const_spec((D_MODEL, IN_COLS)),
        pl.BlockSpec(memory_space=pltpu.SMEM),
        _const_spec((len(POOL_WINDOWS), POOL_CG, POOL_CG)),
        _const_spec((1, POOL_WIDTH)),
        _const_spec((1, Q_COLS)),
        _const_spec((1, POOL_WIDTH)),
        _const_spec((D_MODEL, D_MODEL)),
        _const_spec((1, D_MODEL)),
    ]


def _ffn_weight_specs():
    return [
        _const_spec((1, D_MODEL)),
        _const_spec((D_MODEL, D_FF)),
        _const_spec((D_FF, D_MODEL)),
        _const_spec((1, D_MODEL)),
        _const_spec((D_MODEL, D_MODEL)),
        _const_spec((1, D_MODEL)),
        _const_spec((PLE_DIM, D_MODEL)),
    ]


def _prompt_layer(x, p, xs_mid, hs_mid, ps, mixer_w, ffn_w):
    seq = x.shape[0]
    tm = PROMPT_TILE
    tiles = seq // tm
    assert xs_mid.shape[0] == N_SAMPLE_TILES * tm
    cos_a, sin_a = _rope_lane_tables(np.arange(tiles, dtype=np.float64) * tm)
    cos_b, sin_b = _rope_lane_tables(np.arange(tm, dtype=np.float64))
    mixer_tile = lambda i: (jnp.clip(i - 1, 0, tiles - 1), 0)
    mixer_tile3 = lambda i: (jnp.clip(i - 1, 0, tiles - 1), 0, 0)
    ffn_tile = lambda i: (jnp.clip(i - N_SAMPLE_TILES, 0, tiles - 1), 0)
    sample_tile = lambda i: (jnp.minimum(i, N_SAMPLE_TILES - 1), 0)
    fixed = lambda i: (0, 0)
    in_specs = [
        pl.BlockSpec((tm, D_MODEL), mixer_tile),
        pl.BlockSpec((1, 1, LANES), mixer_tile3),
        pl.BlockSpec((1, 1, LANES), mixer_tile3),
        _const_spec((tm, LANES)),
        _const_spec((tm, LANES)),
        _const_spec((1, LANES)),
    ] + _mixer_weight_specs() + [pl.BlockSpec((tm, PLE_DIM), ffn_tile)] + _ffn_weight_specs() + [
        pl.BlockSpec(memory_space=pl.ANY),
        pl.BlockSpec(memory_space=pl.ANY),
        pl.BlockSpec((tm, PLE_DIM), sample_tile, pipeline_mode=pl.Buffered(1)),
    ]
    n_lead = 1 + N_TABLES
    matmul_w = [mixer_w[n] for n in MIXER_MATMUL_W] + [ffn_w[n] for n in FFN_MATMUL_W]
    for n in [n_lead + n for n in MIXER_MATMUL_W] + [n_lead + len(mixer_w) + 1 + n for n in FFN_MATMUL_W]:
        in_specs[n] = pl.BlockSpec(memory_space=pl.ANY)
    return pl.pallas_call(
        _prompt_layer_kernel,
        grid=(tiles + N_SAMPLE_TILES,),
        in_specs=in_specs,
        out_specs=[
            pl.BlockSpec((tm, D_MODEL), ffn_tile),
            pl.BlockSpec(memory_space=pl.ANY),
            pl.BlockSpec((WINDOW, KV_COLS), fixed),
            pl.BlockSpec((WINDOW, KV_COLS), fixed),
            pl.BlockSpec((POOL_BUF, POOL_WIDTH), fixed),
        ],
        out_shape=[
            jax.ShapeDtypeStruct((seq, D_MODEL), F32),
            jax.ShapeDtypeStruct(xs_mid.shape, F32),
            jax.ShapeDtypeStruct((WINDOW, KV_COLS), F32),
            jax.ShapeDtypeStruct((WINDOW, KV_COLS), F32),
            jax.ShapeDtypeStruct((POOL_BUF, POOL_WIDTH), F32),
        ],
        scratch_shapes=[
            pltpu.VMEM((tm, Q_COLS), BF16),
            pltpu.VMEM((2 * N_KV_HEADS, WINDOW + tm, LANES), BF16),
            pltpu.VMEM((KV_COLS, WINDOW + tm), BF16),
            pltpu.VMEM((POOL_PAD + tm, POOL_WIDTH), F32),
            pltpu.VMEM((tm, POOL_WIDTH), F32),
            pltpu.VMEM((tm, D_MODEL), BF16),
            pltpu.VMEM((tm, D_FF), BF16),
            pltpu.VMEM((2, tm, D_MODEL), F32),
            pltpu.VMEM((2, tm, D_MODEL), BF16),
            pltpu.SemaphoreType.DMA((3 * N_SAMPLE_TILES,)),
        ] + [pltpu.VMEM(w.shape, BF16) for w in matmul_w],
        compiler_params=_params(("arbitrary",)),
        name="prompt_layer",
    )(x, jnp.asarray(cos_a)[:, None, :], jnp.asarray(sin_a)[:, None, :], jnp.asarray(cos_b), jnp.asarray(sin_b),
      jnp.asarray(_rope_sign()), *mixer_w, p, *ffn_w, xs_mid, hs_mid, ps)


def _sample_mixer(x, state_kt, state_vt, state_pool_t, past_len, mixer_w, g_pre_mlp):
    nseq_all, t = state_kt.shape[0], x.shape[0] // state_kt.shape[0]
    nseq = SAMPLE_SEQS
    rows = nseq * t
    steps = nseq_all // nseq
    cos, sin = _rope_lane_tables(past_len + np.arange(rows, dtype=np.float64) % t)
    sin = sin * _rope_sign()
    row = lambda i: (i, 0)
    seq3 = lambda i: (i, 0, 0)
    pos3 = lambda i: (0, i, 0)
    mixer_specs = _mixer_weight_specs()
    for n in MIXER_MATMUL_W:
        mixer_specs[n] = pl.BlockSpec(memory_space=pl.ANY)
    return pl.pallas_call(
        _sample_mixer_kernel,
        grid=(steps,),
        in_specs=[
            pl.BlockSpec((rows, D_MODEL), row),
            _const_spec((rows, LANES)),
            _const_spec((rows, LANES)),
            pl.BlockSpec((nseq, KV_COLS, WINDOW), seq3),
            pl.BlockSpec((nseq, KV_COLS, WINDOW), seq3),
            pl.BlockSpec((POOL_BUF, nseq, POOL_WIDTH), pos3),
        ] + mixer_specs + [_const_spec((1, D_MODEL))],
        out_specs=[
            pl.BlockSpec((rows, D_MODEL), row),
            pl.BlockSpec((rows, D_MODEL), row),
            pl.BlockSpec((nseq, KV_COLS, WINDOW), seq3),
            pl.BlockSpec((nseq, KV_COLS, WINDOW), seq3),
            pl.BlockSpec((POOL_BUF, nseq, POOL_WIDTH), pos3),
        ],
        out_shape=[
            jax.ShapeDtypeStruct((nseq_all * t, D_MODEL), F32),
            jax.ShapeDtypeStruct((nseq_all * t, D_MODEL), BF16),
            jax.ShapeDtypeStruct((nseq_all, KV_COLS, WINDOW), F32),
            jax.ShapeDtypeStruct((nseq_all, KV_COLS, WINDOW), F32),
            jax.ShapeDtypeStruct((POOL_BUF, nseq_all, POOL_WIDTH), F32),
        ],
        scratch_shapes=[pltpu.VMEM((len(POOL_WINDOWS), nseq * (POOL_PAD + t), POOL_CG), F32)]
        + [pltpu.VMEM(mixer_w[n].shape, BF16) for n in MIXER_MATMUL_W],
        compiler_params=_params(("arbitrary",)),
        name="sample_mixer",
    )(x, jnp.asarray(cos), jnp.asarray(sin), state_kt, state_vt, state_pool_t, *mixer_w, g_pre_mlp)


def kernel(x_prompt, x_sample, state_k, state_v, state_pool, p_prompt, p_sample, w_in, attn_sinks, pool_w,
           pool_scale, g_attn_out, g_pool_out, w_out, g_pre_mix, g_post_mix, g_pre_mlp, g_post_mlp, w_up, w_down,
           w_ple, w_ple_gate, b_ple_gate):
    depth = w_in.shape[0]
    batch, seq, _ = x_prompt.shape
    dec_batch, dec_seq, _ = x_sample.shape
    assert depth == 1 and batch == 1

    xp = x_prompt.reshape(seq, D_MODEL)
    xs = x_sample.reshape(dec_batch * dec_seq, D_MODEL)
    i = 0
    mixer_w = [
        g_pre_mix[i][None, :], w_in[i], attn_sinks[i], pool_w[i], pool_scale[i][None, :], g_attn_out[i][None, :],
        g_pool_out[i][None, :], w_out[i], g_post_mix[i][None, :],
    ]
    ffn_w = [
        g_pre_mlp[i][None, :], w_up[i], w_down[i], g_post_mlp[i][None, :], w_ple_gate[i], b_ple_gate[i][None, :],
        w_ple[i],
    ]

    xs_mid, hs_mid, ks_t, vs_t, us_t = _sample_mixer(
        xs, state_k[i].reshape(dec_batch, WINDOW, KV_COLS).transpose(0, 2, 1),
        state_v[i].reshape(dec_batch, WINDOW, KV_COLS).transpose(0, 2, 1),
        state_pool[i].transpose(1, 0, 2), float(PAST_LEN), mixer_w, ffn_w[0])
    ks, vs, us = ks_t.transpose(0, 2, 1), vs_t.transpose(0, 2, 1), us_t.transpose(1, 0, 2)
    yp, ys, kp, vp, up = _prompt_layer(xp, p_prompt[i, 0], xs_mid, hs_mid,
                                       p_sample[i].reshape(dec_batch * dec_seq, PLE_DIM), mixer_w, ffn_w)

    return (
        yp.reshape(batch, seq, D_MODEL),
        ys.reshape(dec_batch, dec_seq, D_MODEL),
        kp.T.reshape(depth, batch, WINDOW, N_KV_HEADS, HEAD_DIM),
        vp.T.reshape(depth, batch, WINDOW, N_KV_HEADS, HEAD_DIM),
        up.reshape(depth, batch, POOL_BUF, POOL_WIDTH),
        ks.reshape(depth, dec_batch, WINDOW, N_KV_HEADS, HEAD_DIM),
        vs.reshape(depth, dec_batch, WINDOW, N_KV_HEADS, HEAD_DIM),
        us.reshape(depth, dec_batch, POOL_BUF, POOL_WIDTH),
    )
```

```python
import jax
import jax.numpy as jnp
import numpy as np
from jax import lax
from jax.experimental import pallas as pl
from jax.experimental.pallas import tpu as pltpu

D_MODEL = 1024
HEAD_DIM = 64
N_HEADS = 8
N_KV_HEADS = 2
GROUP = N_HEADS // N_KV_HEADS
WINDOW = 128
Q_COLS = N_HEADS * HEAD_DIM
KV_COLS = N_KV_HEADS * HEAD_DIM
POOL_WIDTH = 512
POOL_WINDOWS = (2, 4, 8, 16)
POOL_CG = POOL_WIDTH // len(POOL_WINDOWS)
POOL_BUF = max(POOL_WINDOWS) - 1
POOL_PAD = POOL_BUF + 1
IN_COLS = Q_COLS + 2 * KV_COLS + POOL_WIDTH
D_FF = 4 * D_MODEL
FF_CHUNK = 1024
PLE_DIM = 256
ROPE_THETA = 10000.0
PAST_LEN = 16384
EPS = 1e-6
NEG = -1e30
LANES = 128
MXU_N = 256

ATTN_LOOKAHEAD = 4
PROMPT_PATTERN = "11" + "21" * (5 + ATTN_LOOKAHEAD + 16 + 4) + "11" + "22" + "11" + "222" + "11"

PROMPT_TILE = 512
SAMPLE_SEQS = 16
VMEM_LIMIT = 58 * 1024 * 1024

BF16 = jnp.bfloat16
F32 = jnp.float32


def _rms(x, g):
    y = x * lax.rsqrt(jnp.mean(x * x, axis=-1, keepdims=True) + EPS)
    return y * g


def _mm(a, w):
    return jnp.dot(a, w, preferred_element_type=F32)


def _rope_tables(cos_a, sin_a, cos_b, sin_b, sign):
    cos_t = cos_a * cos_b - sin_a * sin_b
    sin_t = (sin_a * cos_b + cos_a * sin_b) * sign
    return cos_t, sin_t


def _rope(x, cos_t, sin_t, first_half):
    n = x.shape[1] // LANES
    width = x.shape[1]
    partner = jnp.where(first_half, pltpu.roll(x, width - HEAD_DIM // 2, axis=1), pltpu.roll(x, HEAD_DIM // 2, axis=1))
    if n > 1:
        cos_t = jnp.concatenate([cos_t] * n, axis=1)
        sin_t = jnp.concatenate([sin_t] * n, axis=1)
    return x * cos_t + partner * sin_t


def _first_half_mask(rows, width):
    lane = lax.broadcasted_iota(jnp.int32, (rows, width), 1)
    return (lane % HEAD_DIM) < (HEAD_DIM // 2)


def _project(x, g_pre, w_in, tables):
    rows = x.shape[0]
    h = _rms(x, g_pre).astype(BF16)
    proj = _mm(h, w_in)
    cos_t, sin_t = tables
    q = _rope(proj[:, :Q_COLS], cos_t, sin_t, _first_half_mask(rows, Q_COLS)) * (HEAD_DIM ** -0.5)
    k = _rope(proj[:, Q_COLS:Q_COLS + KV_COLS], cos_t, sin_t, _first_half_mask(rows, KV_COLS))
    v = proj[:, Q_COLS + KV_COLS:Q_COLS + 2 * KV_COLS]
    u = proj[:, Q_COLS + 2 * KV_COLS:]
    return q, k, v, u


def _pool_out(win, tok, inv_cnt, pool_w_ref, pool_scale):
    outs = []
    for g in range(len(POOL_WINDOWS)):
        r = win[g] * inv_cnt[g] - tok[:, g * POOL_CG:(g + 1) * POOL_CG]
        outs.append(_mm(r.astype(BF16), pool_w_ref[g]))
    return jnp.concatenate(outs, axis=1) * pool_scale


def _mix_out(x, attn, pooled, g_attn, g_pool, w_out, g_post):
    cat = jnp.concatenate([_rms(attn, g_attn), _rms(pooled, g_pool)], axis=1).astype(BF16)
    return x + _rms(_mm(cat, w_out), g_post)


def _softmax_parts(s_list, sink):
    m = sink
    for s in s_list:
        m = jnp.maximum(m, jnp.max(s, axis=-1, keepdims=True))
    e_list = [jnp.exp(s - m) for s in s_list]
    den = jnp.exp(sink - m)
    for e in e_list:
        den = den + jnp.sum(e, axis=-1, keepdims=True)
    return e_list, den


def _prompt_mixer_steps(tile, x_ref, cos_a_ref, sin_a_ref, cos_b_ref, sin_b_ref, sign_ref, g_pre_ref, w_in_ref,
                        sinks_ref, pool_w_ref, pool_scale_ref, g_attn_ref, g_pool_ref, w_out_ref, g_post_ref,
                        knew_ref, vnew_ref, unew_ref, qbuf, kbuf, vtbuf, ubuf, pbuf, catbuf,
                        g_pre_mlp_ref, xmid_ref, hmid_ref):
    tm = x_ref.shape[0]
    nblk = tm // WINDOW
    half_lanes = LANES // 2
    base = tile * tm

    h = _rms(x_ref[...], g_pre_ref[...]).astype(BF16)
    cos_t, sin_t = _rope_tables(cos_a_ref[0], sin_a_ref[0], cos_b_ref[...], sin_b_ref[...], sign_ref[...])

    kv = _mm(h, w_in_ref[:, Q_COLS:Q_COLS + 2 * KV_COLS])
    k = _rope(kv[:, :KV_COLS], cos_t, sin_t, _first_half_mask(tm, KV_COLS))
    v = kv[:, KV_COLS:]
    knew_ref[...] = k[tm - WINDOW:, :].T
    vnew_ref[...] = v[tm - WINDOW:, :].T
    low = lax.broadcasted_iota(jnp.int32, (tm, LANES), 1) < half_lanes
    k_swapped = pltpu.roll(k, half_lanes, axis=1)
    kbuf[0, WINDOW:, :] = jnp.where(low, k, 0.0).astype(BF16)
    kbuf[1, WINDOW:, :] = jnp.where(low, 0.0, k_swapped).astype(BF16)
    kbuf[2, WINDOW:, :] = jnp.where(low, k_swapped, 0.0).astype(BF16)
    kbuf[3, WINDOW:, :] = jnp.where(low, 0.0, k).astype(BF16)
    vtbuf[:, WINDOW:] = v.T.astype(BF16)
    yield

    u_col0 = Q_COLS + 2 * KV_COLS
    for c in range(Q_COLS // MXU_N):
        cols = slice(c * MXU_N, (c + 1) * MXU_N)
        q = _rope(_mm(h, w_in_ref[:, cols]), cos_t, sin_t, _first_half_mask(tm, MXU_N)) * (HEAD_DIM ** -0.5)
        qbuf[:, cols] = q.astype(BF16)
        yield
    for c in range(POOL_WIDTH // MXU_N):
        cols = slice(c * MXU_N, (c + 1) * MXU_N)
        ubuf[POOL_PAD:, cols] = _mm(h, w_in_ref[:, u_col0 + c * MXU_N:u_col0 + (c + 1) * MXU_N])
        yield

    pos = base + lax.broadcasted_iota(jnp.int32, (tm, 1), 0)
    pool_scale = pool_scale_ref[...]
    for g, w in enumerate(POOL_WINDOWS):
        cols = slice(g * POOL_CG, (g + 1) * POOL_CG)
        a = ubuf[:, cols]
        shift = 1
        while shift < w:
            a = a + pltpu.roll(a, shift, axis=0)
            shift *= 2
        inv_cnt = 1.0 / jnp.minimum(pos + 1, w).astype(F32)
        r = a[POOL_PAD:, :] * inv_cnt - ubuf[POOL_PAD:, cols]
        pbuf[:, cols] = _mm(r.astype(BF16), pool_w_ref[g]) * pool_scale[:, cols]
        yield
    catbuf[:, Q_COLS:] = _rms(pbuf[...], g_pool_ref[...]).astype(BF16)

    c_i = lax.broadcasted_iota(jnp.int32, (WINDOW, 2 * WINDOW), 0)
    r_i = lax.broadcasted_iota(jnp.int32, (WINDOW, 2 * WINDOW), 1) % WINDOW
    from_prev = c_i > r_i
    no_prev = from_prev & (base == 0)

    def scores(j, h_kv, par):
        rows = slice(j * WINDOW, (j + 1) * WINDOW)
        keys = slice(j * WINDOW, (j + 2) * WINDOW)
        q2 = jnp.concatenate([qbuf[rows, (2 * h_kv) * LANES:(2 * h_kv + 1) * LANES],
                              qbuf[rows, (2 * h_kv + 1) * LANES:(2 * h_kv + 2) * LANES]], axis=0)
        return lax.dot_general(kbuf[2 * h_kv + par, keys, :], q2, (((1,), (1,)), ((), ())),
                               preferred_element_type=F32)

    units = [(j, h_kv, par) for j in range(nblk) for h_kv in range(N_KV_HEADS) for par in range(2)]
    pending = []
    for n in range(min(ATTN_LOOKAHEAD, len(units))):
        pending.append(scores(*units[n]))
        yield
    heads = [None] * N_HEADS
    for n, (j, h_kv, par) in enumerate(units):
        st = pending.pop(0)
        if n + ATTN_LOOKAHEAD < len(units):
            pending.append(scores(*units[n + ATTN_LOOKAHEAD]))
        st = jnp.where(from_prev, st[:WINDOW], st[WINDOW:])
        if j == 0:
            st = jnp.where(no_prev, NEG, st)
        sink = jnp.concatenate([jnp.full((1, WINDOW), sinks_ref[GROUP * h_kv + par], F32),
                                jnp.full((1, WINDOW), sinks_ref[GROUP * h_kv + 2 + par], F32)], axis=1)
        m = jnp.maximum(jnp.max(st, axis=0, keepdims=True), sink)
        e = jnp.exp(st - m)
        den = jnp.sum(e, axis=0, keepdims=True) + jnp.exp(sink - m)
        vt = vtbuf[h_kv * HEAD_DIM:(h_kv + 1) * HEAD_DIM, j * WINDOW:(j + 2) * WINDOW]
        e2 = jnp.concatenate([jnp.where(from_prev, e, 0.0), jnp.where(from_prev, 0.0, e)], axis=0).astype(BF16)
        ot = _mm(vt, e2) / den
        heads[GROUP * h_kv + par] = ot[:, :WINDOW]
        heads[GROUP * h_kv + 2 + par] = ot[:, WINDOW:]
        if h_kv == N_KV_HEADS - 1 and par == 1:
            attn = jnp.concatenate(heads, axis=0).T
            catbuf[j * WINDOW:(j + 1) * WINDOW, :Q_COLS] = _rms(attn, g_attn_ref[...]).astype(BF16)
        yield

    mix = []
    for c in range(D_MODEL // MXU_N):
        mix.append(_mm(catbuf[...], w_out_ref[:, c * MXU_N:(c + 1) * MXU_N]))
        yield
    x_mid = x_ref[...] + _rms(jnp.concatenate(mix, axis=1), g_post_ref[...])

    unew_ref[...] = ubuf[tm + 1:tm + POOL_PAD, :]
    kbuf[:, 0:WINDOW, :] = kbuf[:, tm:tm + WINDOW, :]
    vtbuf[:, 0:WINDOW] = vtbuf[:, tm:tm + WINDOW]
    ubuf[0:POOL_PAD, :] = ubuf[tm:tm + POOL_PAD, :]
    xmid_ref[...] = x_mid
    hmid_ref[...] = _rms(x_mid, g_pre_mlp_ref[...]).astype(BF16)


def _sample_mixer_kernel(x_ref, cos_ref, sin_ref, skt_ref, svt_ref, sp_ref, g_pre_ref, w_in_hbm,
                         sinks_ref, pool_w_hbm, pool_scale_ref, g_attn_ref, g_pool_ref, w_out_hbm, g_post_ref,
                         g_pre_mlp_ref, xmid_ref, hmid_ref, knew_ref, vnew_ref, unew_ref,
                         ubuf, w_in_ref, pool_w_ref, w_out_ref):
    @pl.when(pl.program_id(0) == 0)
    def _():
        _cast_weights_to_vmem([w_in_hbm, pool_w_hbm, w_out_hbm], [w_in_ref, pool_w_ref, w_out_ref])

    rows = x_ref.shape[0]
    nseq = skt_ref.shape[0]
    t = rows // nseq
    x = x_ref[...]
    q, k, v, u = _project(x, g_pre_ref[...], w_in_ref[...], (cos_ref[...], sin_ref[...]))

    k3 = k.reshape(nseq, t, KV_COLS)
    v3 = v.reshape(nseq, t, KV_COLS)
    sk = jnp.swapaxes(skt_ref[...], 1, 2)
    sv = jnp.swapaxes(svt_ref[...], 1, 2)
    knew_ref[...] = jnp.swapaxes(jnp.concatenate([sk[:, t:, :], k3], axis=1), 1, 2)
    vnew_ref[...] = jnp.swapaxes(jnp.concatenate([sv[:, t:, :], v3], axis=1), 1, 2)

    r_i = lax.broadcasted_iota(jnp.int32, (GROUP * t, WINDOW), 0) % t
    c_i = lax.broadcasted_iota(jnp.int32, (GROUP * t, WINDOW), 1)
    mask_old = (c_i > r_i)[None]
    r_n = lax.broadcasted_iota(jnp.int32, (GROUP * t, t), 0) % t
    c_n = lax.broadcasted_iota(jnp.int32, (GROUP * t, t), 1)
    mask_new = (c_n <= r_n)[None]

    qb = q.astype(BF16)
    kb = k3.astype(BF16)
    vb = v3.astype(BF16)
    heads = [None] * N_HEADS
    scores = []
    for h in range(N_KV_HEADS):
        lanes = slice(h * HEAD_DIM, (h + 1) * HEAD_DIM)
        q4 = jnp.concatenate(
            [qb[:, (GROUP * h + g) * HEAD_DIM:(GROUP * h + g + 1) * HEAD_DIM].reshape(nseq, t, HEAD_DIM)
             for g in range(GROUP)], axis=1)
        s_old = jnp.einsum('bqd,bkd->bqk', q4, sk[:, :, lanes].astype(BF16), preferred_element_type=F32)
        s_new = jnp.einsum('bqd,bkd->bqk', q4, kb[:, :, lanes], preferred_element_type=F32)
        scores.append((s_old, s_new))
    for h, (s_old, s_new) in enumerate(scores):
        lanes = slice(h * HEAD_DIM, (h + 1) * HEAD_DIM)
        s_old = jnp.where(mask_old, s_old, NEG)
        s_new = jnp.where(mask_new, s_new, NEG)
        sink = jnp.concatenate(
            [jnp.full((1, t, 1), sinks_ref[GROUP * h + g], F32) for g in range(GROUP)], axis=1)
        (e_old, e_new), den = _softmax_parts([s_old, s_new], sink)
        o = jnp.einsum('bqk,bkd->bqd', e_old.astype(BF16), sv[:, :, lanes].astype(BF16), preferred_element_type=F32)
        o = o + jnp.einsum('bqk,bkd->bqd', e_new.astype(BF16), vb[:, :, lanes], preferred_element_type=F32)
        o = o / den
        for g in range(GROUP):
            heads[GROUP * h + g] = o[:, g * t:(g + 1) * t, :].reshape(rows, HEAD_DIM)
    attn = jnp.concatenate(heads, axis=1)

    ext = POOL_PAD + t
    per_seq = lambda j: pl.ds(j, nseq, stride=ext)
    win = []
    for g, w in enumerate(POOL_WINDOWS):
        cols = slice(g * POOL_CG, (g + 1) * POOL_CG)
        ubuf[g, per_seq(0), :] = jnp.zeros((nseq, POOL_CG), F32)
        for j in range(POOL_BUF):
            ubuf[g, per_seq(1 + j), :] = sp_ref[j, :, cols]
        for s_i in range(nseq):
            ubuf[g, s_i * ext + POOL_PAD:(s_i + 1) * ext, :] = u[s_i * t:(s_i + 1) * t, cols]
        for j in range(POOL_BUF):
            unew_ref[j, :, cols] = ubuf[g, per_seq(t + 1 + j), :]
        a = ubuf[g]
        shift = 1
        while shift < w:
            a = a + pltpu.roll(a, shift, axis=0)
            shift *= 2
        win.append(a.reshape(nseq, ext, POOL_CG)[:, POOL_PAD:, :].reshape(rows, POOL_CG))
    inv_cnt = [1.0 / w for w in POOL_WINDOWS]
    pooled = _pool_out(win, u, inv_cnt, pool_w_ref, pool_scale_ref[...])

    x_mid = _mix_out(x, attn, pooled, g_attn_ref[...], g_pool_ref[...], w_out_ref[...], g_post_ref[...])
    xmid_ref[...] = x_mid
    hmid_ref[...] = _rms(x_mid, g_pre_mlp_ref[...]).astype(BF16)


def _ffn_steps(x_ref, h_ref, load_p, w_up_ref, w_down_ref, g_post_ref, w_gate_ref, b_gate_ref, w_ple_ref,
               y_ref, actbuf):
    for c in range(D_FF // MXU_N):
        cols = slice(c * MXU_N, (c + 1) * MXU_N)
        up = _mm(h_ref[...], w_up_ref[:, cols])
        actbuf[:, cols] = jnp.square(jnp.maximum(up, 0.0)).astype(BF16)
        yield
    ff = []
    n_col, n_row = D_MODEL // MXU_N, D_FF // FF_CHUNK
    for c in range(n_col):
        cols = slice(c * MXU_N, (c + 1) * MXU_N)
        acc = None
        for r in range(n_row):
            rows = slice(r * FF_CHUNK, (r + 1) * FF_CHUNK)
            part = _mm(actbuf[:, rows], w_down_ref[rows, cols])
            acc = part if acc is None else acc + part
            if (c, r) != (n_col - 1, n_row - 1):
                yield
        ff.append(acc)
    x = x_ref[...] + _rms(jnp.concatenate(ff, axis=1), g_post_ref[...])
    xb = x.astype(BF16)
    yield
    half_ple = 0.5 * _mm(load_p().astype(BF16), w_ple_ref[...])
    base_y = x + half_ple
    half_b = 0.5 * b_gate_ref[...]
    yield
    for c in range(D_MODEL // MXU_N):
        cols = slice(c * MXU_N, (c + 1) * MXU_N)
        t = jnp.tanh(0.5 * _mm(xb, w_gate_ref[:, cols]) + half_b[:, cols])
        y_ref[:, cols] = base_y[:, cols] + half_ple[:, cols] * t
        yield


def _interleave(first, second, pattern):
    gens = {'1': first, '2': second}
    for tag in pattern:
        next(gens[tag], None)
    for _ in first:
        pass
    for _ in second:
        pass


def _cast_weights_to_vmem(srcs, dsts):
    tasks = []
    for src, dst in zip(srcs, dsts):
        if len(src.shape) != 2:
            tasks.append((src, dst, tuple(src.shape)))
            continue
        rows, cols = src.shape
        col_block = cols if cols <= STAGE_MAX_COLS else D_MODEL
        for c0 in range(0, cols, col_block):
            for r0 in range(0, rows, STAGE_ROWS):
                view = (slice(r0, r0 + STAGE_ROWS), slice(c0, c0 + col_block))
                tasks.append((src.at[view], dst.at[view], (STAGE_ROWS, col_block)))
    shapes = sorted(set(shape for _, _, shape in tasks))
    counts = [sum(1 for t in tasks if t[2] == shape) for shape in shapes]
    slots = [STAGE_SLOTS if n >= 2 * STAGE_SLOTS else min(n, 2) for n in counts]

    def body(*scoped):
        *stages, sems = scoped
        uses = [0] * len(shapes)
        copies, frees = [], []
        last_in_slot = {}
        for t, (src, dst, shape) in enumerate(tasks):
            k = shapes.index(shape)
            slot = uses[k] % slots[k]
            uses[k] += 1
            stage = stages[k].at[slot]
            copies.append((pltpu.make_async_copy(src, stage, sems.at[k, slot]), stage, dst))
            frees.append(last_in_slot.get((k, slot)))
            last_in_slot[(k, slot)] = t
        started = 0
        for t, (copy, stage, dst) in enumerate(copies):
            while started < len(copies) and (frees[started] is None or frees[started] < t):
                copies[started][0].start()
                started += 1
            copy.wait()
            dst[...] = stage[...].astype(BF16)

    pl.run_scoped(body, *[pltpu.VMEM((n,) + shape, F32) for n, shape in zip(slots, shapes)],
                  pltpu.SemaphoreType.DMA((len(shapes), max(slots))))


N_TABLES = 5
MIXER_MATMUL_W = (1, 3, 7)
FFN_MATMUL_W = (1, 2, 4, 6)
STAGE_ROWS = 256
STAGE_MAX_COLS = 1280
STAGE_SLOTS = 8


N_SAMPLE_TILES = 2


def _prompt_layer_kernel(*refs):
    x_ref, *tables = refs[:1 + N_TABLES]
    (g_pre_mix, w_in_hbm, sinks, pool_w_hbm, pool_scale, g_attn, g_pool, w_out_hbm, g_post_mix, p_ref,
     g_pre_mlp, w_up_hbm, w_down_hbm, g_post_mlp, w_gate_hbm, b_gate, w_ple_hbm,
     xs_hbm, hs_hbm, ps_ref) = refs[1 + N_TABLES:21 + N_TABLES]
    y_ref, ys_hbm, knew_ref, vnew_ref, unew_ref = refs[21 + N_TABLES:26 + N_TABLES]
    (qbuf, kbuf, vtbuf, ubuf, pbuf, catbuf, actbuf, xmid, hmid, sems,
     w_in, pool_w, w_out, w_up, w_down, w_gate, w_ple) = refs[26 + N_TABLES:]
    tm = x_ref.shape[0]
    step = pl.program_id(0)
    last_tile = pl.num_programs(0) - 1 - N_SAMPLE_TILES
    on_sample = step < N_SAMPLE_TILES
    slot = step % 2

    def sample_loads(n):
        rows = pl.ds(n * tm, tm)
        return [pltpu.make_async_copy(xs_hbm.at[rows], xmid.at[n], sems.at[2 * n]),
                pltpu.make_async_copy(hs_hbm.at[rows], hmid.at[n], sems.at[2 * n + 1])]

    def sample_store(n):
        return pltpu.make_async_copy(y_ref, ys_hbm.at[pl.ds(n * tm, tm)], sems.at[2 * N_SAMPLE_TILES + n])

    @pl.when(step == 0)
    def _():
        for n in range(N_SAMPLE_TILES):
            for copy in sample_loads(n):
                copy.start()
        _cast_weights_to_vmem([w_in_hbm, pool_w_hbm, w_out_hbm, w_up_hbm, w_down_hbm, w_gate_hbm, w_ple_hbm],
                              [w_in, pool_w, w_out, w_up, w_down, w_gate, w_ple])
        for copy in sample_loads(0):
            copy.wait()

    for n in range(N_SAMPLE_TILES):
        @pl.when(step == n + 1)
        def _(n=n):
            sample_store(n).wait()

    @pl.when(step == 1)
    def _():
        kbuf[:, 0:WINDOW, :] = jnp.zeros((2 * N_KV_HEADS, WINDOW, LANES), BF16)
        vtbuf[:, 0:WINDOW] = jnp.zeros((KV_COLS, WINDOW), BF16)
        ubuf[0:POOL_PAD, :] = jnp.zeros((POOL_PAD, POOL_WIDTH), F32)
        for copy in sample_loads(1):
            copy.wait()

    def ffn_steps():
        return _ffn_steps(xmid.at[slot], hmid.at[slot], lambda: jnp.where(on_sample, ps_ref[...], p_ref[...]),
                          w_up, w_down, g_post_mlp, w_gate, b_gate, w_ple, y_ref, actbuf)

    ffn_only = (step == 0) | (step == pl.num_programs(0) - 1)

    @pl.when(ffn_only)
    def _():
        for _ in ffn_steps():
            pass

    @pl.when(jnp.logical_not(ffn_only))
    def _():
        mixer = _prompt_mixer_steps(jnp.clip(step - 1, 0, last_tile), x_ref, *tables, g_pre_mix, w_in, sinks, pool_w,
                                    pool_scale, g_attn, g_pool, w_out, g_post_mix, knew_ref, vnew_ref, unew_ref,
                                    qbuf, kbuf, vtbuf, ubuf, pbuf, catbuf, g_pre_mlp, xmid.at[1 - slot],
                                    hmid.at[1 - slot])
        _interleave(ffn_steps(), mixer, PROMPT_PATTERN)

    for n in range(N_SAMPLE_TILES):
        @pl.when(step == n)
        def _(n=n):
            sample_store(n).start()


def _const_spec(shape):
    zeros = (0,) * len(shape)
    return pl.BlockSpec(shape, lambda i: zeros, pipeline_mode=pl.Buffered(1))


def _rope_lane_tables(pos):
    half = HEAD_DIM // 2
    inv = ROPE_THETA ** (-np.arange(half, dtype=np.float64) / half)
    ang = pos[:, None] * inv[None, :]
    reps = LANES // half
    return (np.tile(np.cos(ang), (1, reps)).astype(np.float32), np.tile(np.sin(ang), (1, reps)).astype(np.float32))


def _rope_sign():
    lane = np.arange(LANES)
    return np.where((lane % HEAD_DIM) < HEAD_DIM // 2, -1.0, 1.0).astype(np.float32)[None, :]


def _params(dims):
    return pltpu.CompilerParams(dimension_semantics=dims, vmem_limit_bytes=VMEM_LIMIT)


def _mixer_weight_specs():
    return [
        _const_spec((1, D_MODEL)),
        _const_spec((D_MODEL, IN_COLS)),
        pl.BlockSpec(memory_space=pltpu.SMEM),
        _const_spec((len(POOL_WINDOWS), POOL_CG, POOL_CG)),
        _const_spec((1, POOL_WIDTH)),
        _const_spec((1, Q_COLS)),
        _const_spec((1, POOL_WIDTH)),
        _const_spec((D_MODEL, D_MODEL)),
        _const_spec((1, D_MODEL)),
    ]


def _ffn_weight_specs():
    return [
        _const_spec((1, D_MODEL)),
        _const_spec((D_MODEL, D_FF)),
        _const_spec((D_FF, D_MODEL)),
        _const_spec((1, D_MODEL)),
        _const_spec((D_MODEL, D_MODEL)),
        _const_spec((1, D_MODEL)),
        _const_spec((PLE_DIM, D_MODEL)),
    ]


def _prompt_layer(x, p, xs_mid, hs_mid, ps, mixer_w, ffn_w):
    seq = x.shape[0]
    tm = PROMPT_TILE
    tiles = seq // tm
    assert xs_mid.shape[0] == N_SAMPLE_TILES * tm
    cos_a, sin_a = _rope_lane_tables(np.arange(tiles, dtype=np.float64) * tm)
    cos_b, sin_b = _rope_lane_tables(np.arange(tm, dtype=np.float64))
    mixer_tile = lambda i: (jnp.clip(i - 1, 0, tiles - 1), 0)
    mixer_tile3 = lambda i: (jnp.clip(i - 1, 0, tiles - 1), 0, 0)
    ffn_tile = lambda i: (jnp.clip(i - N_SAMPLE_TILES, 0, tiles - 1), 0)
    sample_tile = lambda i: (jnp.minimum(i, N_SAMPLE_TILES - 1), 0)
    fixed = lambda i: (0, 0)
    in_specs = [
        pl.BlockSpec((tm, D_MODEL), mixer_tile),
        pl.BlockSpec((1, 1, LANES), mixer_tile3),
        pl.BlockSpec((1, 1, LANES), mixer_tile3),
        _const_spec((tm, LANES)),
        _const_spec((tm, LANES)),
        _const_spec((1, LANES)),
    ] + _mixer_weight_specs() + [pl.BlockSpec((tm, PLE_DIM), ffn_tile)] + _ffn_weight_specs() + [
        pl.BlockSpec(memory_space=pl.ANY),
        pl.BlockSpec(memory_space=pl.ANY),
        pl.BlockSpec((tm, PLE_DIM), sample_tile, pipeline_mode=pl.Buffered(1)),
    ]
    n_lead = 1 + N_TABLES
    matmul_w = [mixer_w[n] for n in MIXER_MATMUL_W] + [ffn_w[n] for n in FFN_MATMUL_W]
    for n in [n_lead + n for n in MIXER_MATMUL_W] + [n_lead + len(mixer_w) + 1 + n for n in FFN_MATMUL_W]:
        in_specs[n] = pl.BlockSpec(memory_space=pl.ANY)
    return pl.pallas_call(
        _prompt_layer_kernel,
        grid=(tiles + N_SAMPLE_TILES,),
        in_specs=in_specs,
        out_specs=[
            pl.BlockSpec((tm, D_MODEL), ffn_tile),
            pl.BlockSpec(memory_space=pl.ANY),
            pl.BlockSpec((WINDOW, KV_COLS), fixed),
            pl.BlockSpec((WINDOW, KV_COLS), fixed),
            pl.BlockSpec((POOL_BUF, POOL_WIDTH), fixed),
        ],
        out_shape=[
            jax.ShapeDtypeStruct((seq, D_MODEL), F32),
            jax.ShapeDtypeStruct(xs_mid.shape, F32),
            jax.ShapeDtypeStruct((WINDOW, KV_COLS), F32),
            jax.ShapeDtypeStruct((WINDOW, KV_COLS), F32),
            jax.ShapeDtypeStruct((POOL_BUF, POOL_WIDTH), F32),
        ],
        scratch_shapes=[
            pltpu.VMEM((tm, Q_COLS), BF16),
            pltpu.VMEM((2 * N_KV_HEADS, WINDOW + tm, LANES), BF16),
            pltpu.VMEM((KV_COLS, WINDOW + tm), BF16),
            pltpu.VMEM((POOL_PAD + tm, POOL_WIDTH), F32),
            pltpu.VMEM((tm, POOL_WIDTH), F32),
            pltpu.VMEM((tm, D_MODEL), BF16),
            pltpu.VMEM((tm, D_FF), BF16),
            pltpu.VMEM((2, tm, D_MODEL), F32),
            pltpu.VMEM((2, tm, D_MODEL), BF16),
            pltpu.SemaphoreType.DMA((3 * N_SAMPLE_TILES,)),
        ] + [pltpu.VMEM(w.shape, BF16) for w in matmul_w],
        compiler_params=_params(("arbitrary",)),
        name="prompt_layer",
    )(x, jnp.asarray(cos_a)[:, None, :], jnp.asarray(sin_a)[:, None, :], jnp.asarray(cos_b), jnp.asarray(sin_b),
      jnp.asarray(_rope_sign()), *mixer_w, p, *ffn_w, xs_mid, hs_mid, ps)


def _sample_mixer(x, state_kt, state_vt, state_pool_t, past_len, mixer_w, g_pre_mlp):
    nseq_all, t = state_kt.shape[0], x.shape[0] // state_kt.shape[0]
    nseq = SAMPLE_SEQS
    rows = nseq * t
    steps = nseq_all // nseq
    cos, sin = _rope_lane_tables(past_len + np.arange(rows, dtype=np.float64) % t)
    sin = sin * _rope_sign()
    row = lambda i: (i, 0)
    seq3 = lambda i: (i, 0, 0)
    pos3 = lambda i: (0, i, 0)
    mixer_specs = _mixer_weight_specs()
    for n in MIXER_MATMUL_W:
        mixer_specs[n] = pl.BlockSpec(memory_space=pl.ANY)
    return pl.pallas_call(
        _sample_mixer_kernel,
        grid=(steps,),
        in_specs=[
            pl.BlockSpec((rows, D_MODEL), row),
            _const_spec((rows, LANES)),
            _const_spec((rows, LANES)),
            pl.BlockSpec((nseq, KV_COLS, WINDOW), seq3),
            pl.BlockSpec((nseq, KV_COLS, WINDOW), seq3),
            pl.BlockSpec((POOL_BUF, nseq, POOL_WIDTH), pos3),
        ] + mixer_specs + [_const_spec((1, D_MODEL))],
        out_specs=[
            pl.BlockSpec((rows, D_MODEL), row),
            pl.BlockSpec((rows, D_MODEL), row),
            pl.BlockSpec((nseq, KV_COLS, WINDOW), seq3),
            pl.BlockSpec((nseq, KV_COLS, WINDOW), seq3),
            pl.BlockSpec((POOL_BUF, nseq, POOL_WIDTH), pos3),
        ],
        out_shape=[
            jax.ShapeDtypeStruct((nseq_all * t, D_MODEL), F32),
            jax.ShapeDtypeStruct((nseq_all * t, D_MODEL), BF16),
            jax.ShapeDtypeStruct((nseq_all, KV_COLS, WINDOW), F32),
            jax.ShapeDtypeStruct((nseq_all, KV_COLS, WINDOW), F32),
            jax.ShapeDtypeStruct((POOL_BUF, nseq_all, POOL_WIDTH), F32),
        ],
        scratch_shapes=[pltpu.VMEM((len(POOL_WINDOWS), nseq * (POOL_PAD + t), POOL_CG), F32)]
        + [pltpu.VMEM(mixer_w[n].shape, BF16) for n in MIXER_MATMUL_W],
        compiler_params=_params(("arbitrary",)),
        name="sample_mixer",
    )(x, jnp.asarray(cos), jnp.asarray(sin), state_kt, state_vt, state_pool_t, *mixer_w, g_pre_mlp)


def kernel(x_prompt, x_sample, state_k, state_v, state_pool, p_prompt, p_sample, w_in, attn_sinks, pool_w,
           pool_scale, g_attn_out, g_pool_out, w_out, g_pre_mix, g_post_mix, g_pre_mlp, g_post_mlp, w_up, w_down,
           w_ple, w_ple_gate, b_ple_gate):
    depth = w_in.shape[0]
    batch, seq, _ = x_prompt.shape
    dec_batch, dec_seq, _ = x_sample.shape
    assert depth == 1 and batch == 1

    xp = x_prompt.reshape(seq, D_MODEL)
    xs = x_sample.reshape(dec_batch * dec_seq, D_MODEL)
    i = 0
    mixer_w = [
        g_pre_mix[i][None, :], w_in[i], attn_sinks[i], pool_w[i], pool_scale[i][None, :], g_attn_out[i][None, :],
        g_pool_out[i][None, :], w_out[i], g_post_mix[i][None, :],
    ]
    ffn_w = [
        g_pre_mlp[i][None, :], w_up[i], w_down[i], g_post_mlp[i][None, :], w_ple_gate[i], b_ple_gate[i][None, :],
        w_ple[i],
    ]

    xs_mid, hs_mid, ks_t, vs_t, us_t = _sample_mixer(
        xs, state_k[i].reshape(dec_batch, WINDOW, KV_COLS).transpose(0, 2, 1),
        state_v[i].reshape(dec_batch, WINDOW, KV_COLS).transpose(0, 2, 1),
        state_pool[i].transpose(1, 0, 2), float(PAST_LEN), mixer_w, ffn_w[0])
    ks, vs, us = ks_t.transpose(0, 2, 1), vs_t.transpose(0, 2, 1), us_t.transpose(1, 0, 2)
    yp, ys, kp, vp, up = _prompt_layer(xp, p_prompt[i, 0], xs_mid, hs_mid,
                                       p_sample[i].reshape(dec_batch * dec_seq, PLE_DIM), mixer_w, ffn_w)

    return (
        yp.reshape(batch, seq, D_MODEL),
        ys.reshape(dec_batch, dec_seq, D_MODEL),
        kp.T.reshape(depth, batch, WINDOW, N_KV_HEADS, HEAD_DIM),
        vp.T.reshape(depth, batch, WINDOW, N_KV_HEADS, HEAD_DIM),
        up.reshape(depth, batch, POOL_BUF, POOL_WIDTH),
        ks.reshape(depth, dec_batch, WINDOW, N_KV_HEADS, HEAD_DIM),
        vs.reshape(depth, dec_batch, WINDOW, N_KV_HEADS, HEAD_DIM),
        us.reshape(depth, dec_batch, POOL_BUF, POOL_WIDTH),
    )
```

```python
import jax
import jax.numpy as jnp
import numpy as np
from jax import lax
from jax.experimental import pallas as pl
from jax.experimental.pallas import tpu as pltpu

D_MODEL = 1024
HEAD_DIM = 64
N_HEADS = 8
N_KV_HEADS = 2
GROUP = N_HEADS // N_KV_HEADS
WINDOW = 128
Q_COLS = N_HEADS * HEAD_DIM
KV_COLS = N_KV_HEADS * HEAD_DIM
POOL_WIDTH = 512
POOL_WINDOWS = (2, 4, 8, 16)
POOL_CG = POOL_WIDTH // len(POOL_WINDOWS)
POOL_BUF = max(POOL_WINDOWS) - 1
POOL_PAD = POOL_BUF + 1
IN_COLS = Q_COLS + 2 * KV_COLS + POOL_WIDTH
D_FF = 4 * D_MODEL
FF_CHUNK = 1024
PLE_DIM = 256
ROPE_THETA = 10000.0
PAST_LEN = 16384
EPS = 1e-6
NEG = -1e30
LANES = 128
MXU_N = 256

ATTN_LOOKAHEAD = 4
PROMPT_PATTERN = "11" + "21" * (5 + ATTN_LOOKAHEAD + 16 + 4) + "11" + "22" + "11" + "222" + "11"

PROMPT_TILE = 512
SAMPLE_SEQS = 8
VMEM_LIMIT = 58 * 1024 * 1024

BF16 = jnp.bfloat16
F32 = jnp.float32


def _rms(x, g):
    y = x * lax.rsqrt(jnp.mean(x * x, axis=-1, keepdims=True) + EPS)
    return y * g


def _mm(a, w):
    return jnp.dot(a, w, preferred_element_type=F32)


def _rope_tables(cos_a, sin_a, cos_b, sin_b, sign):
    cos_t = cos_a * cos_b - sin_a * sin_b
    sin_t = (sin_a * cos_b + cos_a * sin_b) * sign
    return cos_t, sin_t


def _rope(x, cos_t, sin_t, first_half):
    n = x.shape[1] // LANES
    width = x.shape[1]
    partner = jnp.where(first_half, pltpu.roll(x, width - HEAD_DIM // 2, axis=1), pltpu.roll(x, HEAD_DIM // 2, axis=1))
    if n > 1:
        cos_t = jnp.concatenate([cos_t] * n, axis=1)
        sin_t = jnp.concatenate([sin_t] * n, axis=1)
    return x * cos_t + partner * sin_t


def _first_half_mask(rows, width):
    lane = lax.broadcasted_iota(jnp.int32, (rows, width), 1)
    return (lane % HEAD_DIM) < (HEAD_DIM // 2)


def _project(x, g_pre, w_in, tables):
    rows = x.shape[0]
    h = _rms(x, g_pre).astype(BF16)
    proj = _mm(h, w_in)
    cos_t, sin_t = tables
    q = _rope(proj[:, :Q_COLS], cos_t, sin_t, _first_half_mask(rows, Q_COLS)) * (HEAD_DIM ** -0.5)
    k = _rope(proj[:, Q_COLS:Q_COLS + KV_COLS], cos_t, sin_t, _first_half_mask(rows, KV_COLS))
    v = proj[:, Q_COLS + KV_COLS:Q_COLS + 2 * KV_COLS]
    u = proj[:, Q_COLS + 2 * KV_COLS:]
    return q, k, v, u


def _pool_out(win, tok, inv_cnt, pool_w_ref, pool_scale):
    outs = []
    for g in range(len(POOL_WINDOWS)):
        r = win[g] * inv_cnt[g] - tok[:, g * POOL_CG:(g + 1) * POOL_CG]
        outs.append(_mm(r.astype(BF16), pool_w_ref[g]))
    return jnp.concatenate(outs, axis=1) * pool_scale


def _mix_out(x, attn, pooled, g_attn, g_pool, w_out, g_post):
    cat = jnp.concatenate([_rms(attn, g_attn), _rms(pooled, g_pool)], axis=1).astype(BF16)
    return x + _rms(_mm(cat, w_out), g_post)


def _softmax_parts(s_list, sink):
    m = sink
    for s in s_list:
        m = jnp.maximum(m, jnp.max(s, axis=-1, keepdims=True))
    e_list = [jnp.exp(s - m) for s in s_list]
    den = jnp.exp(sink - m)
    for e in e_list:
        den = den + jnp.sum(e, axis=-1, keepdims=True)
    return e_list, den


def _prompt_mixer_steps(tile, x_ref, cos_a_ref, sin_a_ref, cos_b_ref, sin_b_ref, sign_ref, g_pre_ref, w_in_ref,
                        sinks_ref, pool_w_ref, pool_scale_ref, g_attn_ref, g_pool_ref, w_out_ref, g_post_ref,
                        knew_ref, vnew_ref, unew_ref, qbuf, kbuf, vtbuf, ubuf, pbuf, catbuf,
                        g_pre_mlp_ref, xmid_ref, hmid_ref):
    tm = x_ref.shape[0]
    nblk = tm // WINDOW
    half_lanes = LANES // 2
    base = tile * tm

    h = _rms(x_ref[...], g_pre_ref[...]).astype(BF16)
    cos_t, sin_t = _rope_tables(cos_a_ref[0], sin_a_ref[0], cos_b_ref[...], sin_b_ref[...], sign_ref[...])

    kv = _mm(h, w_in_ref[:, Q_COLS:Q_COLS + 2 * KV_COLS])
    k = _rope(kv[:, :KV_COLS], cos_t, sin_t, _first_half_mask(tm, KV_COLS))
    v = kv[:, KV_COLS:]
    knew_ref[...] = k[tm - WINDOW:, :].T
    vnew_ref[...] = v[tm - WINDOW:, :].T
    low = lax.broadcasted_iota(jnp.int32, (tm, LANES), 1) < half_lanes
    k_swapped = pltpu.roll(k, half_lanes, axis=1)
    kbuf[0, WINDOW:, :] = jnp.where(low, k, 0.0).astype(BF16)
    kbuf[1, WINDOW:, :] = jnp.where(low, 0.0, k_swapped).astype(BF16)
    kbuf[2, WINDOW:, :] = jnp.where(low, k_swapped, 0.0).astype(BF16)
    kbuf[3, WINDOW:, :] = jnp.where(low, 0.0, k).astype(BF16)
    vtbuf[:, WINDOW:] = v.T.astype(BF16)
    yield

    u_col0 = Q_COLS + 2 * KV_COLS
    for c in range(Q_COLS // MXU_N):
        cols = slice(c * MXU_N, (c + 1) * MXU_N)
        q = _rope(_mm(h, w_in_ref[:, cols]), cos_t, sin_t, _first_half_mask(tm, MXU_N)) * (HEAD_DIM ** -0.5)
        qbuf[:, cols] = q.astype(BF16)
        yield
    for c in range(POOL_WIDTH // MXU_N):
        cols = slice(c * MXU_N, (c + 1) * MXU_N)
        ubuf[POOL_PAD:, cols] = _mm(h, w_in_ref[:, u_col0 + c * MXU_N:u_col0 + (c + 1) * MXU_N])
        yield

    pos = base + lax.broadcasted_iota(jnp.int32, (tm, 1), 0)
    pool_scale = pool_scale_ref[...]
    for g, w in enumerate(POOL_WINDOWS):
        cols = slice(g * POOL_CG, (g + 1) * POOL_CG)
        a = ubuf[:, cols]
        shift = 1
        while shift < w:
            a = a + pltpu.roll(a, shift, axis=0)
            shift *= 2
        inv_cnt = 1.0 / jnp.minimum(pos + 1, w).astype(F32)
        r = a[POOL_PAD:, :] * inv_cnt - ubuf[POOL_PAD:, cols]
        pbuf[:, cols] = _mm(r.astype(BF16), pool_w_ref[g]) * pool_scale[:, cols]
        yield
    catbuf[:, Q_COLS:] = _rms(pbuf[...], g_pool_ref[...]).astype(BF16)

    c_i = lax.broadcasted_iota(jnp.int32, (WINDOW, 2 * WINDOW), 0)
    r_i = lax.broadcasted_iota(jnp.int32, (WINDOW, 2 * WINDOW), 1) % WINDOW
    from_prev = c_i > r_i
    no_prev = from_prev & (base == 0)

    def scores(j, h_kv, par):
        rows = slice(j * WINDOW, (j + 1) * WINDOW)
        keys = slice(j * WINDOW, (j + 2) * WINDOW)
        q2 = jnp.concatenate([qbuf[rows, (2 * h_kv) * LANES:(2 * h_kv + 1) * LANES],
                              qbuf[rows, (2 * h_kv + 1) * LANES:(2 * h_kv + 2) * LANES]], axis=0)
        return lax.dot_general(kbuf[2 * h_kv + par, keys, :], q2, (((1,), (1,)), ((), ())),
                               preferred_element_type=F32)

    units = [(j, h_kv, par) for j in range(nblk) for h_kv in range(N_KV_HEADS) for par in range(2)]
    pending = []
    for n in range(min(ATTN_LOOKAHEAD, len(units))):
        pending.append(scores(*units[n]))
        yield
    heads = [None] * N_HEADS
    for n, (j, h_kv, par) in enumerate(units):
        st = pending.pop(0)
        if n + ATTN_LOOKAHEAD < len(units):
            pending.append(scores(*units[n + ATTN_LOOKAHEAD]))
        st = jnp.where(from_prev, st[:WINDOW], st[WINDOW:])
        if j == 0:
            st = jnp.where(no_prev, NEG, st)
        sink = jnp.concatenate([jnp.full((1, WINDOW), sinks_ref[GROUP * h_kv + par], F32),
                                jnp.full((1, WINDOW), sinks_ref[GROUP * h_kv + 2 + par], F32)], axis=1)
        m = jnp.maximum(jnp.max(st, axis=0, keepdims=True), sink)
        e = jnp.exp(st - m)
        den = jnp.sum(e, axis=0, keepdims=True) + jnp.exp(sink - m)
        vt = vtbuf[h_kv * HEAD_DIM:(h_kv + 1) * HEAD_DIM, j * WINDOW:(j + 2) * WINDOW]
        e2 = jnp.concatenate([jnp.where(from_prev, e, 0.0), jnp.where(from_prev, 0.0, e)], axis=0).astype(BF16)
        ot = _mm(vt, e2) / den
        heads[GROUP * h_kv + par] = ot[:, :WINDOW]
        heads[GROUP * h_kv + 2 + par] = ot[:, WINDOW:]
        if h_kv == N_KV_HEADS - 1 and par == 1:
            attn = jnp.concatenate(heads, axis=0).T
            catbuf[j * WINDOW:(j + 1) * WINDOW, :Q_COLS] = _rms(attn, g_attn_ref[...]).astype(BF16)
        yield

    mix = []
    for c in range(D_MODEL // MXU_N):
        mix.append(_mm(catbuf[...], w_out_ref[:, c * MXU_N:(c + 1) * MXU_N]))
        yield
    x_mid = x_ref[...] + _rms(jnp.concatenate(mix, axis=1), g_post_ref[...])

    unew_ref[...] = ubuf[tm + 1:tm + POOL_PAD, :]
    kbuf[:, 0:WINDOW, :] = kbuf[:, tm:tm + WINDOW, :]
    vtbuf[:, 0:WINDOW] = vtbuf[:, tm:tm + WINDOW]
    ubuf[0:POOL_PAD, :] = ubuf[tm:tm + POOL_PAD, :]
    xmid_ref[...] = x_mid
    hmid_ref[...] = _rms(x_mid, g_pre_mlp_ref[...]).astype(BF16)


def _sample_mixer_kernel(x_ref, cos_ref, sin_ref, skt_ref, svt_ref, sp_ref, g_pre_ref, w_in_hbm,
                         sinks_ref, pool_w_hbm, pool_scale_ref, g_attn_ref, g_pool_ref, w_out_hbm, g_post_ref,
                         g_pre_mlp_ref, xmid_ref, hmid_ref, knew_ref, vnew_ref, unew_ref,
                         ubuf, w_in_ref, pool_w_ref, w_out_ref):
    @pl.when(pl.program_id(0) == 0)
    def _():
        _cast_weights_to_vmem([w_in_hbm, pool_w_hbm, w_out_hbm], [w_in_ref, pool_w_ref, w_out_ref])

    rows = x_ref.shape[0]
    nseq = skt_ref.shape[0]
    t = rows // nseq
    x = x_ref[...]
    q, k, v, u = _project(x, g_pre_ref[...], w_in_ref[...], (cos_ref[...], sin_ref[...]))

    k3 = k.reshape(nseq, t, KV_COLS)
    v3 = v.reshape(nseq, t, KV_COLS)
    sk = jnp.swapaxes(skt_ref[...], 1, 2)
    sv = jnp.swapaxes(svt_ref[...], 1, 2)
    knew_ref[...] = jnp.swapaxes(jnp.concatenate([sk[:, t:, :], k3], axis=1), 1, 2)
    vnew_ref[...] = jnp.swapaxes(jnp.concatenate([sv[:, t:, :], v3], axis=1), 1, 2)

    r_i = lax.broadcasted_iota(jnp.int32, (GROUP * t, WINDOW), 0) % t
    c_i = lax.broadcasted_iota(jnp.int32, (GROUP * t, WINDOW), 1)
    mask_old = (c_i > r_i)[None]
    r_n = lax.broadcasted_iota(jnp.int32, (GROUP * t, t), 0) % t
    c_n = lax.broadcasted_iota(jnp.int32, (GROUP * t, t), 1)
    mask_new = (c_n <= r_n)[None]

    qb = q.astype(BF16)
    kb = k3.astype(BF16)
    vb = v3.astype(BF16)
    heads = [None] * N_HEADS
    scores = []
    for h in range(N_KV_HEADS):
        lanes = slice(h * HEAD_DIM, (h + 1) * HEAD_DIM)
        q4 = jnp.concatenate(
            [qb[:, (GROUP * h + g) * HEAD_DIM:(GROUP * h + g + 1) * HEAD_DIM].reshape(nseq, t, HEAD_DIM)
             for g in range(GROUP)], axis=1)
        s_old = jnp.einsum('bqd,bkd->bqk', q4, sk[:, :, lanes].astype(BF16), preferred_element_type=F32)
        s_new = jnp.einsum('bqd,bkd->bqk', q4, kb[:, :, lanes], preferred_element_type=F32)
        scores.append((s_old, s_new))
    for h, (s_old, s_new) in enumerate(scores):
        lanes = slice(h * HEAD_DIM, (h + 1) * HEAD_DIM)
        s_old = jnp.where(mask_old, s_old, NEG)
        s_new = jnp.where(mask_new, s_new, NEG)
        sink = jnp.concatenate(
            [jnp.full((1, t, 1), sinks_ref[GROUP * h + g], F32) for g in range(GROUP)], axis=1)
        (e_old, e_new), den = _softmax_parts([s_old, s_new], sink)
        o = jnp.einsum('bqk,bkd->bqd', e_old.astype(BF16), sv[:, :, lanes].astype(BF16), preferred_element_type=F32)
        o = o + jnp.einsum('bqk,bkd->bqd', e_new.astype(BF16), vb[:, :, lanes], preferred_element_type=F32)
        o = o / den
        for g in range(GROUP):
            heads[GROUP * h + g] = o[:, g * t:(g + 1) * t, :].reshape(rows, HEAD_DIM)
    attn = jnp.concatenate(heads, axis=1)

    ext = POOL_PAD + t
    per_seq = lambda j: pl.ds(j, nseq, stride=ext)
    win = []
    for g, w in enumerate(POOL_WINDOWS):
        cols = slice(g * POOL_CG, (g + 1) * POOL_CG)
        ubuf[g, per_seq(0), :] = jnp.zeros((nseq, POOL_CG), F32)
        for j in range(POOL_BUF):
            ubuf[g, per_seq(1 + j), :] = sp_ref[j, :, cols]
        for s_i in range(nseq):
            ubuf[g, s_i * ext + POOL_PAD:(s_i + 1) * ext, :] = u[s_i * t:(s_i + 1) * t, cols]
        for j in range(POOL_BUF):
            unew_ref[j, :, cols] = ubuf[g, per_seq(t + 1 + j), :]
        a = ubuf[g]
        shift = 1
        while shift < w:
            a = a + pltpu.roll(a, shift, axis=0)
            shift *= 2
        win.append(a.reshape(nseq, ext, POOL_CG)[:, POOL_PAD:, :].reshape(rows, POOL_CG))
    inv_cnt = [1.0 / w for w in POOL_WINDOWS]
    pooled = _pool_out(win, u, inv_cnt, pool_w_ref, pool_scale_ref[...])

    x_mid = _mix_out(x, attn, pooled, g_attn_ref[...], g_pool_ref[...], w_out_ref[...], g_post_ref[...])
    xmid_ref[...] = x_mid
    hmid_ref[...] = _rms(x_mid, g_pre_mlp_ref[...]).astype(BF16)


def _ffn_steps(x_ref, h_ref, load_p, w_up_ref, w_down_ref, g_post_ref, w_gate_ref, b_gate_ref, w_ple_ref,
               y_ref, actbuf):
    for c in range(D_FF // MXU_N):
        cols = slice(c * MXU_N, (c + 1) * MXU_N)
        up = _mm(h_ref[...], w_up_ref[:, cols])
        actbuf[:, cols] = jnp.square(jnp.maximum(up, 0.0)).astype(BF16)
        yield
    ff = []
    n_col, n_row = D_MODEL // MXU_N, D_FF // FF_CHUNK
    for c in range(n_col):
        cols = slice(c * MXU_N, (c + 1) * MXU_N)
        acc = None
        for r in range(n_row):
            rows = slice(r * FF_CHUNK, (r + 1) * FF_CHUNK)
            part = _mm(actbuf[:, rows], w_down_ref[rows, cols])
            acc = part if acc is None else acc + part
            if (c, r) != (n_col - 1, n_row - 1):
                yield
        ff.append(acc)
    x = x_ref[...] + _rms(jnp.concatenate(ff, axis=1), g_post_ref[...])
    xb = x.astype(BF16)
    yield
    half_ple = 0.5 * _mm(load_p().astype(BF16), w_ple_ref[...])
    base_y = x + half_ple
    half_b = 0.5 * b_gate_ref[...]
    yield
    for c in range(D_MODEL // MXU_N):
        cols = slice(c * MXU_N, (c + 1) * MXU_N)
        t = jnp.tanh(0.5 * _mm(xb, w_gate_ref[:, cols]) + half_b[:, cols])
        y_ref[:, cols] = base_y[:, cols] + half_ple[:, cols] * t
        yield


def _interleave(first, second, pattern):
    gens = {'1': first, '2': second}
    for tag in pattern:
        next(gens[tag], None)
    for _ in first:
        pass
    for _ in second:
        pass


def _cast_weights_to_vmem(srcs, dsts):
    tasks = []
    for src, dst in zip(srcs, dsts):
        if len(src.shape) != 2:
            tasks.append((src, dst, tuple(src.shape)))
            continue
        rows, cols = src.shape
        col_block = cols if cols <= STAGE_MAX_COLS else D_MODEL
        for c0 in range(0, cols, col_block):
            for r0 in range(0, rows, STAGE_ROWS):
                view = (slice(r0, r0 + STAGE_ROWS), slice(c0, c0 + col_block))
                tasks.append((src.at[view], dst.at[view], (STAGE_ROWS, col_block)))
    shapes = sorted(set(shape for _, _, shape in tasks))
    counts = [sum(1 for t in tasks if t[2] == shape) for shape in shapes]
    slots = [STAGE_SLOTS if n >= 2 * STAGE_SLOTS else min(n, 2) for n in counts]

    def body(*scoped):
        *stages, sems = scoped
        uses = [0] * len(shapes)
        copies, frees = [], []
        last_in_slot = {}
        for t, (src, dst, shape) in enumerate(tasks):
            k = shapes.index(shape)
            slot = uses[k] % slots[k]
            uses[k] += 1
            stage = stages[k].at[slot]
            copies.append((pltpu.make_async_copy(src, stage, sems.at[k, slot]), stage, dst))
            frees.append(last_in_slot.get((k, slot)))
            last_in_slot[(k, slot)] = t
        started = 0
        for t, (copy, stage, dst) in enumerate(copies):
            while started < len(copies) and (frees[started] is None or frees[started] < t):
                copies[started][0].start()
                started += 1
            copy.wait()
            dst[...] = stage[...].astype(BF16)

    pl.run_scoped(body, *[pltpu.VMEM((n,) + shape, F32) for n, shape in zip(slots, shapes)],
                  pltpu.SemaphoreType.DMA((len(shapes), max(slots))))


N_TABLES = 5
MIXER_MATMUL_W = (1, 3, 7)
FFN_MATMUL_W = (1, 2, 4, 6)
STAGE_ROWS = 256
STAGE_MAX_COLS = 1280
STAGE_SLOTS = 8


N_SAMPLE_TILES = 2


def _prompt_layer_kernel(*refs):
    x_ref, *tables = refs[:1 + N_TABLES]
    (g_pre_mix, w_in_hbm, sinks, pool_w_hbm, pool_scale, g_attn, g_pool, w_out_hbm, g_post_mix, p_ref,
     g_pre_mlp, w_up_hbm, w_down_hbm, g_post_mlp, w_gate_hbm, b_gate, w_ple_hbm,
     xs_hbm, hs_hbm, ps_ref) = refs[1 + N_TABLES:21 + N_TABLES]
    y_ref, ys_hbm, knew_ref, vnew_ref, unew_ref = refs[21 + N_TABLES:26 + N_TABLES]
    (qbuf, kbuf, vtbuf, ubuf, pbuf, catbuf, actbuf, xmid, hmid, sems,
     w_in, pool_w, w_out, w_up, w_down, w_gate, w_ple) = refs[26 + N_TABLES:]
    tm = x_ref.shape[0]
    step = pl.program_id(0)
    last_tile = pl.num_programs(0) - 1 - N_SAMPLE_TILES
    on_sample = step < N_SAMPLE_TILES
    slot = step % 2

    def sample_loads(n):
        rows = pl.ds(n * tm, tm)
        return [pltpu.make_async_copy(xs_hbm.at[rows], xmid.at[n], sems.at[2 * n]),
                pltpu.make_async_copy(hs_hbm.at[rows], hmid.at[n], sems.at[2 * n + 1])]

    def sample_store(n):
        return pltpu.make_async_copy(y_ref, ys_hbm.at[pl.ds(n * tm, tm)], sems.at[2 * N_SAMPLE_TILES + n])

    @pl.when(step == 0)
    def _():
        for n in range(N_SAMPLE_TILES):
            for copy in sample_loads(n):
                copy.start()
        _cast_weights_to_vmem([w_in_hbm, pool_w_hbm, w_out_hbm, w_up_hbm, w_down_hbm, w_gate_hbm, w_ple_hbm],
                              [w_in, pool_w, w_out, w_up, w_down, w_gate, w_ple])
        for copy in sample_loads(0):
            copy.wait()

    for n in range(N_SAMPLE_TILES):
        @pl.when(step == n + 1)
        def _(n=n):
            sample_store(n).wait()

    @pl.when(step == 1)
    def _():
        kbuf[:, 0:WINDOW, :] = jnp.zeros((2 * N_KV_HEADS, WINDOW, LANES), BF16)
        vtbuf[:, 0:WINDOW] = jnp.zeros((KV_COLS, WINDOW), BF16)
        ubuf[0:POOL_PAD, :] = jnp.zeros((POOL_PAD, POOL_WIDTH), F32)
        for copy in sample_loads(1):
            copy.wait()

    def ffn_steps():
        return _ffn_steps(xmid.at[slot], hmid.at[slot], lambda: jnp.where(on_sample, ps_ref[...], p_ref[...]),
                          w_up, w_down, g_post_mlp, w_gate, b_gate, w_ple, y_ref, actbuf)

    ffn_only = (step == 0) | (step == pl.num_programs(0) - 1)

    @pl.when(ffn_only)
    def _():
        for _ in ffn_steps():
            pass

    @pl.when(jnp.logical_not(ffn_only))
    def _():
        mixer = _prompt_mixer_steps(jnp.clip(step - 1, 0, last_tile), x_ref, *tables, g_pre_mix, w_in, sinks, pool_w,
                                    pool_scale, g_attn, g_pool, w_out, g_post_mix, knew_ref, vnew_ref, unew_ref,
                                    qbuf, kbuf, vtbuf, ubuf, pbuf, catbuf, g_pre_mlp, xmid.at[1 - slot],
                                    hmid.at[1 - slot])
        _interleave(ffn_steps(), mixer, PROMPT_PATTERN)

    for n in range(N_SAMPLE_TILES):
        @pl.when(step == n)
        def _(n=n):
            sample_store(n).start()


def _const_spec(shape):
    zeros = (0,) * len(shape)
    return pl.BlockSpec(shape, lambda i: zeros, pipeline_mode=pl.Buffered(1))


def _rope_lane_tables(pos):
    half = HEAD_DIM // 2
    inv = ROPE_THETA ** (-np.arange(half, dtype=np.float64) / half)
    ang = pos[:, None] * inv[None, :]
    reps = LANES // half
    return (np.tile(np.cos(ang), (1, reps)).astype(np.float32), np.tile(np.sin(ang), (1, reps)).astype(np.float32))


def _rope_sign():
    lane = np.arange(LANES)
    return np.where((lane % HEAD_DIM) < HEAD_DIM // 2, -1.0, 1.0).astype(np.float32)[None, :]


def _params(dims):
    return pltpu.CompilerParams(dimension_semantics=dims, vmem_limit_bytes=VMEM_LIMIT)


def _mixer_weight_specs():
    return [
        _const_spec((1, D_MODEL)),
        _const_spec((D_MODEL, IN_COLS)),
        pl.BlockSpec(memory_space=pltpu.SMEM),
        _const_spec((len(POOL_WINDOWS), POOL_CG, POOL_CG)),
        _const_spec((1, POOL_WIDTH)),
        _const_spec((1, Q_COLS)),
        _const_spec((1, POOL_WIDTH)),
        _const_spec((D_MODEL, D_MODEL)),
        _const_spec((1, D_MODEL)),
    ]


def _ffn_weight_specs():
    return [
        _const_spec((1, D_MODEL)),
        _const_spec((D_MODEL, D_FF)),
        _const_spec((D_FF, D_MODEL)),
        _const_spec((1, D_MODEL)),
        _const_spec((D_MODEL, D_MODEL)),
        _const_spec((1, D_MODEL)),
        _const_spec((PLE_DIM, D_MODEL)),
    ]


def _prompt_layer(x, p, xs_mid, hs_mid, ps, mixer_w, ffn_w):
    seq = x.shape[0]
    tm = PROMPT_TILE
    tiles = seq // tm
    assert xs_mid.shape[0] == N_SAMPLE_TILES * tm
    cos_a, sin_a = _rope_lane_tables(np.arange(tiles, dtype=np.float64) * tm)
    cos_b, sin_b = _rope_lane_tables(np.arange(tm, dtype=np.float64))
    mixer_tile = lambda i: (jnp.clip(i - 1, 0, tiles - 1), 0)
    mixer_tile3 = lambda i: (jnp.clip(i - 1, 0, tiles - 1), 0, 0)
    ffn_tile = lambda i: (jnp.clip(i - N_SAMPLE_TILES, 0, tiles - 1), 0)
    sample_tile = lambda i: (jnp.minimum(i, N_SAMPLE_TILES - 1), 0)
    fixed = lambda i: (0, 0)
    in_specs = [
        pl.BlockSpec((tm, D_MODEL), mixer_tile),
        pl.BlockSpec((1, 1, LANES), mixer_tile3),
        pl.BlockSpec((1, 1, LANES), mixer_tile3),
        _const_spec((tm, LANES)),
        _const_spec((tm, LANES)),
        _const_spec((1, LANES)),
    ] + _mixer_weight_specs() + [pl.BlockSpec((tm, PLE_DIM), ffn_tile)] + _ffn_weight_specs() + [
        pl.BlockSpec(memory_space=pl.ANY),
        pl.BlockSpec(memory_space=pl.ANY),
        pl.BlockSpec((tm, PLE_DIM), sample_tile, pipeline_mode=pl.Buffered(1)),
    ]
    n_lead = 1 + N_TABLES
    matmul_w = [mixer_w[n] for n in MIXER_MATMUL_W] + [ffn_w[n] for n in FFN_MATMUL_W]
    for n in [n_lead + n for n in MIXER_MATMUL_W] + [n_lead + len(mixer_w) + 1 + n for n in FFN_MATMUL_W]:
        in_specs[n] = pl.BlockSpec(memory_space=pl.ANY)
    return pl.pallas_call(
        _prompt_layer_kernel,
        grid=(tiles + N_SAMPLE_TILES,),
        in_specs=in_specs,
        out_specs=[
            pl.BlockSpec((tm, D_MODEL), ffn_tile),
            pl.BlockSpec(memory_space=pl.ANY),
            pl.BlockSpec((WINDOW, KV_COLS), fixed),
            pl.BlockSpec((WINDOW, KV_COLS), fixed),
            pl.BlockSpec((POOL_BUF, POOL_WIDTH), fixed),
        ],
        out_shape=[
            jax.ShapeDtypeStruct((seq, D_MODEL), F32),
            jax.ShapeDtypeStruct(xs_mid.shape, F32),
            jax.ShapeDtypeStruct((WINDOW, KV_COLS), F32),
            jax.ShapeDtypeStruct((WINDOW, KV_COLS), F32),
            jax.ShapeDtypeStruct((POOL_BUF, POOL_WIDTH), F32),
        ],
        scratch_shapes=[
            pltpu.VMEM((tm, Q_COLS), BF16),
            pltpu.VMEM((2 * N_KV_HEADS, WINDOW + tm, LANES), BF16),
            pltpu.VMEM((KV_COLS, WINDOW + tm), BF16),
            pltpu.VMEM((POOL_PAD + tm, POOL_WIDTH), F32),
            pltpu.VMEM((tm, POOL_WIDTH), F32),
            pltpu.VMEM((tm, D_MODEL), BF16),
            pltpu.VMEM((tm, D_FF), BF16),
            pltpu.VMEM((2, tm, D_MODEL), F32),
            pltpu.VMEM((2, tm, D_MODEL), BF16),
            pltpu.SemaphoreType.DMA((3 * N_SAMPLE_TILES,)),
        ] + [pltpu.VMEM(w.shape, BF16) for w in matmul_w],
        compiler_params=_params(("arbitrary",)),
        name="prompt_layer",
    )(x, jnp.asarray(cos_a)[:, None, :], jnp.asarray(sin_a)[:, None, :], jnp.asarray(cos_b), jnp.asarray(sin_b),
      jnp.asarray(_rope_sign()), *mixer_w, p, *ffn_w, xs_mid, hs_mid, ps)


def _sample_mixer(x, state_kt, state_vt, state_pool_t, past_len, mixer_w, g_pre_mlp):
    nseq_all, t = state_kt.shape[0], x.shape[0] // state_kt.shape[0]
    nseq = SAMPLE_SEQS
    rows = nseq * t
    steps = nseq_all // nseq
    cos, sin = _rope_lane_tables(past_len + np.arange(rows, dtype=np.float64) % t)
    sin = sin * _rope_sign()
    row = lambda i: (i, 0)
    seq3 = lambda i: (i, 0, 0)
    pos3 = lambda i: (0, i, 0)
    mixer_specs = _mixer_weight_specs()
    for n in MIXER_MATMUL_W:
        mixer_specs[n] = pl.BlockSpec(memory_space=pl.ANY)
    return pl.pallas_call(
        _sample_mixer_kernel,
        grid=(steps,),
        in_specs=[
            pl.BlockSpec((rows, D_MODEL), row),
            _const_spec((rows, LANES)),
            _const_spec((rows, LANES)),
            pl.BlockSpec((nseq, KV_COLS, WINDOW), seq3),
            pl.BlockSpec((nseq, KV_COLS, WINDOW), seq3),
            pl.BlockSpec((POOL_BUF, nseq, POOL_WIDTH), pos3),
        ] + mixer_specs + [_const_spec((1, D_MODEL))],
        out_specs=[
            pl.BlockSpec((rows, D_MODEL), row),
            pl.BlockSpec((rows, D_MODEL), row),
            pl.BlockSpec((nseq, KV_COLS, WINDOW), seq3),
            pl.BlockSpec((nseq, KV_COLS, WINDOW), seq3),
            pl.BlockSpec((POOL_BUF, nseq, POOL_WIDTH), pos3),
        ],
        out_shape=[
            jax.ShapeDtypeStruct((nseq_all * t, D_MODEL), F32),
            jax.ShapeDtypeStruct((nseq_all * t, D_MODEL), BF16),
            jax.ShapeDtypeStruct((nseq_all, KV_COLS, WINDOW), F32),
            jax.ShapeDtypeStruct((nseq_all, KV_COLS, WINDOW), F32),
            jax.ShapeDtypeStruct((POOL_BUF, nseq_all, POOL_WIDTH), F32),
        ],
        scratch_shapes=[pltpu.VMEM((len(POOL_WINDOWS), nseq * (POOL_PAD + t), POOL_CG), F32)]
        + [pltpu.VMEM(mixer_w[n].shape, BF16) for n in MIXER_MATMUL_W],
        compiler_params=_params(("arbitrary",)),
        name="sample_mixer",
    )(x, jnp.asarray(cos), jnp.asarray(sin), state_kt, state_vt, state_pool_t, *mixer_w, g_pre_mlp)


def kernel(x_prompt, x_sample, state_k, state_v, state_pool, p_prompt, p_sample, w_in, attn_sinks, pool_w,
           pool_scale, g_attn_out, g_pool_out, w_out, g_pre_mix, g_post_mix, g_pre_mlp, g_post_mlp, w_up, w_down,
           w_ple, w_ple_gate, b_ple_gate):
    depth = w_in.shape[0]
    batch, seq, _ = x_prompt.shape
    dec_batch, dec_seq, _ = x_sample.shape
    assert depth == 1 and batch == 1

    xp = x_prompt.reshape(seq, D_MODEL)
    xs = x_sample.reshape(dec_batch * dec_seq, D_MODEL)
    i = 0
    mixer_w = [
        g_pre_mix[i][None, :], w_in[i], attn_sinks[i], pool_w[i], pool_scale[i][None, :], g_attn_out[i][None, :],
        g_pool_out[i][None, :], w_out[i], g_post_mix[i][None, :],
    ]
    ffn_w = [
        g_pre_mlp[i][None, :], w_up[i], w_down[i], g_post_mlp[i][None, :], w_ple_gate[i], b_ple_gate[i][None, :],
        w_ple[i],
    ]

    xs_mid, hs_mid, ks_t, vs_t, us_t = _sample_mixer(
        xs, state_k[i].reshape(dec_batch, WINDOW, KV_COLS).transpose(0, 2, 1),
        state_v[i].reshape(dec_batch, WINDOW, KV_COLS).transpose(0, 2, 1),
        state_pool[i].transpose(1, 0, 2), float(PAST_LEN), mixer_w, ffn_w[0])
    ks, vs, us = ks_t.transpose(0, 2, 1), vs_t.transpose(0, 2, 1), us_t.transpose(1, 0, 2)
    yp, ys, kp, vp, up = _prompt_layer(xp, p_prompt[i, 0], xs_mid, hs_mid,
                                       p_sample[i].reshape(dec_batch * dec_seq, PLE_DIM), mixer_w, ffn_w)

    return (
        yp.reshape(batch, seq, D_MODEL),
        ys.reshape(dec_batch, dec_seq, D_MODEL),
        kp.T.reshape(depth, batch, WINDOW, N_KV_HEADS, HEAD_DIM),
        vp.T.reshape(depth, batch, WINDOW, N_KV_HEADS, HEAD_DIM),
        up.reshape(depth, batch, POOL_BUF, POOL_WIDTH),
        ks.reshape(depth, dec_batch, WINDOW, N_KV_HEADS, HEAD_DIM),
        vs.reshape(depth, dec_batch, WINDOW, N_KV_HEADS, HEAD_DIM),
        us.reshape(depth, dec_batch, POOL_BUF, POOL_WIDTH),
    )
```
